```python
import math
import jax, jax.numpy as jnp
from jax import lax
import numpy as np

D_MODEL = 1024
BATCH = 4
SEQ = 8192
DEPTH = 1

D_FOURIER = D_MODEL // 2
FOURIER_GROUPS = 4
D_SSM = D_MODEL // 2
SSM_GROUP = 16
SSM_GROUPS = D_SSM // SSM_GROUP
SSM_STATE = 64
N_DIRECTIONS = 2
DT_MIN = 1e-3
DT_MAX = 1e-1
N_BRANCHES = 2
D_IN = D_FOURIER + D_SSM + N_BRANCHES * D_MODEL
N_EXPERTS = 16
D_EXPERT = 2816
EC_CAPACITY = 2
PLE_DIM = 256
RMS_EPS = 1e-6

kernel_name = 'hybrid_fourier_s5_expert_choice_block'


def rmsnorm(x, g):
    xf = x.astype(jnp.float32)
    y = xf * lax.rsqrt(jnp.mean(xf * xf, axis=-1, keepdims=True) + RMS_EPS)
    return (y * g.astype(jnp.float32)).astype(x.dtype)


def fourier_mix(u):
    b, s, _ = u.shape
    ug = u.astype(jnp.float32).reshape(b, s, FOURIER_GROUPS, D_FOURIER // FOURIER_GROUPS)
    ug = jnp.transpose(ug, (0, 2, 1, 3))
    f = jnp.fft.fft2(ug, axes=(-2, -1), norm='ortho').real
    f = jnp.transpose(f, (0, 2, 1, 3)).reshape(b, s, D_FOURIER)
    return f.astype(u.dtype)


def _linear_recurrence(c1, c2):
    a1, b1 = c1
    a2, b2 = c2
    return a1 * a2, a2 * b1 + b2


def ssm_direction(ug, a_re, a_im, log_dt, b_re, b_im, c_re, c_im, reverse):
    f32 = jnp.float32
    lam = lax.complex(a_re.astype(f32), a_im.astype(f32))
    dt = jnp.exp(log_dt.astype(f32))[:, None]
    lam_bar = jnp.exp(lam * dt)
    b_bar = ((lam_bar - 1.0) / lam)[..., None] * lax.complex(b_re.astype(f32), b_im.astype(f32))
    bu = jnp.einsum('bsgi,gni->bsgn', ug.astype(jnp.complex64), b_bar)
    a = jnp.broadcast_to(lam_bar, (1, ug.shape[1]) + lam_bar.shape)
    _, states = lax.associative_scan(_linear_recurrence, (a, bu), reverse=reverse, axis=1)
    c = lax.complex(c_re.astype(f32), c_im.astype(f32))
    return jnp.einsum('bsgn,gon->bsgo', states, c).real


def ssm_branch(u, a_re, a_im, log_dt, b_re, b_im, c_re, c_im, d, w_glu):
    bsz, s, _ = u.shape
    uf = u.astype(jnp.float32)
    ug = uf.reshape(bsz, s, SSM_GROUPS, SSM_GROUP)
    y = d.astype(jnp.float32) * uf
    for direction, rev in enumerate((False, True)):
        y = y + ssm_direction(ug, a_re[direction], a_im[direction], log_dt[direction],
                              b_re[direction], b_im[direction], c_re[direction], c_im[direction],
                              rev).reshape(bsz, s, D_SSM)
    y = jax.nn.gelu(y).astype(u.dtype)
    val, gate = jnp.split(y @ w_glu, 2, axis=-1)
    return val * jax.nn.sigmoid(gate)


def expert_choice_ffn(h, w_router, w_gate, w_up, w_down):
    bsz, s, d = h.shape
    cap = EC_CAPACITY * s // N_EXPERTS
    affinity = jax.nn.softmax((h @ w_router).astype(jnp.float32), axis=-1)
    vals, idx = lax.top_k(jnp.swapaxes(affinity, 1, 2), cap)
    bidx = jnp.arange(bsz)[:, None, None]
    xg = h[bidx, idx]
    hid = jax.nn.silu(jnp.einsum('becd,edf->becf', xg, w_gate)) * jnp.einsum('becd,edf->becf', xg, w_up)
    y = jnp.einsum('becf,efd->becd', hid, w_down) * vals[..., None].astype(h.dtype)
    flat = (bidx * s + idx).reshape(-1)
    out = jnp.zeros((bsz * s, d), h.dtype).at[flat].add(y.reshape(-1, d))
    return out.reshape(bsz, s, d)


def setup_inputs(seed: int = 0) -> dict:
    key = jax.random.key(seed)
    ks = jax.random.split(key, 24)

    def nrm(k, shape, scale):
        return jax.random.normal(k, shape, jnp.float32) * scale

    L = DEPTH
    G, N = SSM_GROUPS, SSM_STATE
    return {
        'x': nrm(ks[0], (BATCH, SEQ, D_MODEL), 1.0),
        'p': nrm(ks[1], (DEPTH, BATCH, SEQ, PLE_DIM), 1.0),
        'g_mix': 1.0 + nrm(ks[2], (L, D_MODEL), 0.01),
        'w_in': nrm(ks[3], (L, D_MODEL, D_IN), D_MODEL ** -0.5),
        'w_fourier': nrm(ks[4], (L, D_FOURIER, D_MODEL), D_FOURIER ** -0.5),
        'ssm_a_re': -0.5 + nrm(ks[5], (L, N_DIRECTIONS, G, N), 0.01),
        'ssm_a_im': math.pi * jnp.arange(N, dtype=jnp.float32) + nrm(ks[6], (L, N_DIRECTIONS, G, N), 0.01),
        'ssm_log_dt': jax.random.uniform(ks[7], (L, N_DIRECTIONS, G), jnp.float32,
                                         math.log(DT_MIN), math.log(DT_MAX)),
        'ssm_b_re': nrm(ks[8], (L, N_DIRECTIONS, G, N, SSM_GROUP), (2 * SSM_GROUP) ** -0.5),
        'ssm_b_im': nrm(ks[9], (L, N_DIRECTIONS, G, N, SSM_GROUP), (2 * SSM_GROUP) ** -0.5),
        'ssm_c_re': nrm(ks[10], (L, N_DIRECTIONS, G, SSM_GROUP, N), SSM_STATE ** -0.5),
        'ssm_c_im': nrm(ks[11], (L, N_DIRECTIONS, G, SSM_GROUP, N), SSM_STATE ** -0.5),
        'ssm_d': nrm(ks[12], (L, D_SSM), 1.0),
        'w_glu': nrm(ks[13], (L, D_SSM, 2 * D_MODEL), D_SSM ** -0.5),
        'w_out': nrm(ks[14], (L, D_MODEL, D_MODEL), D_MODEL ** -0.5),
        'g_ffn': 1.0 + nrm(ks[15], (L, D_MODEL), 0.01),
        'w_router': nrm(ks[16], (L, D_MODEL, N_EXPERTS), D_MODEL ** -0.5),
        'w_exp_gate': nrm(ks[17], (L, N_EXPERTS, D_MODEL, D_EXPERT), D_MODEL ** -0.5),
        'w_exp_up': nrm(ks[18], (L, N_EXPERTS, D_MODEL, D_EXPERT), D_MODEL ** -0.5),
        'w_exp_down': nrm(ks[19], (L, N_EXPERTS, D_EXPERT, D_MODEL), D_EXPERT ** -0.5),
        'g_ple': 1.0 + nrm(ks[20], (L, D_MODEL), 0.01),
        'w_ple_gate': nrm(ks[21], (L, D_MODEL, D_MODEL), D_MODEL ** -0.5),
        'w_ple_proj': nrm(ks[22], (L, PLE_DIM, D_MODEL), PLE_DIM ** -0.5),
        'g_final': 1.0 + nrm(ks[23], (D_MODEL,), 0.01),
    }


def reference(x, p, g_mix, w_in, w_fourier, ssm_a_re, ssm_a_im, ssm_log_dt, ssm_b_re, ssm_b_im,
              ssm_c_re, ssm_c_im, ssm_d, w_glu, w_out, g_ffn, w_router, w_exp_gate, w_exp_up,
              w_exp_down, g_ple, w_ple_gate, w_ple_proj, g_final):
    for i in range(DEPTH):
        h = rmsnorm(x, g_mix[i])
        z = h @ w_in[i]
        u_f, u_s, gates = jnp.split(z, [D_FOURIER, D_FOURIER + D_SSM], axis=-1)
        y_f = fourier_mix(u_f) @ w_fourier[i]
        y_s = ssm_branch(u_s, ssm_a_re[i], ssm_a_im[i], ssm_log_dt[i], ssm_b_re[i], ssm_b_im[i],
                         ssm_c_re[i], ssm_c_im[i], ssm_d[i], w_glu[i])
        g_f, g_s = jnp.split(jax.nn.sigmoid(gates), N_BRANCHES, axis=-1)
        x = x + (g_f * y_f + g_s * y_s) @ w_out[i]
        x = x + expert_choice_ffn(rmsnorm(x, g_ffn[i]), w_router[i], w_exp_gate[i],
                                  w_exp_up[i], w_exp_down[i])
        e = p[i] @ w_ple_proj[i]
        x = x + jax.nn.sigmoid(rmsnorm(x, g_ple[i]) @ w_ple_gate[i]) * e
    return rmsnorm(x, g_final)
```

```python
import functools
import math

import jax
import jax.numpy as jnp
from jax import lax
from jax.experimental import pallas as pl
from jax.experimental.pallas import tpu as pltpu

F32 = jnp.float32
BF16 = jnp.bfloat16
I32 = jnp.int32

RMS_EPS = 1e-6
FOURIER_GROUPS = 4
EC_CAPACITY = 2
DFT_N2 = 128
SUBLANES = 8
PACKED_ROWS = 16
SSM_CHUNK = 32
TOPK_BLOCK = 128
GATHER_TILE = 256
GATHER_WINDOW = 48
GATHER_EXPERTS = 2
COMBINE_TILE = 256
COMBINE_WINDOW = 128
VMEM_LIMIT = 56 * 1024 * 1024


def _cparams(*sem):
    return pltpu.CompilerParams(dimension_semantics=sem, vmem_limit_bytes=VMEM_LIMIT)


def _rms(x, g):
    return x * lax.rsqrt(jnp.mean(x * x, axis=-1, keepdims=True) + RMS_EPS) * g


def _dot(a, b):
    return jnp.dot(a, b, preferred_element_type=F32)


def _inproj_kernel(x_ref, g_ref, w_ref, bd_ref, pq_ref, us_ref, gate_ref, *, df, ds):
    h = _rms(x_ref[...], g_ref[...]).astype(BF16)
    z = _dot(h, w_ref[...])
    pq_ref[...] = _dot(z[:, :df].astype(BF16), bd_ref[...])
    us_ref[...] = z[:, df:df + ds].astype(BF16)
    gate_ref[...] = jax.nn.sigmoid(z[:, df + ds:]).astype(BF16)


def _inproj(x2, g, w_in, bd, df, ds, tm=512):
    t, d = x2.shape
    dg = w_in.shape[1] - df - ds
    return pl.pallas_call(
        functools.partial(_inproj_kernel, df=df, ds=ds),
        grid=(t // tm,),
        in_specs=[
            pl.BlockSpec((tm, d), lambda i: (i, 0)),
            pl.BlockSpec((1, d), lambda i: (0, 0)),
            pl.BlockSpec(w_in.shape, lambda i: (0, 0)),
            pl.BlockSpec(bd.shape, lambda i: (0, 0)),
        ],
        out_specs=[
            pl.BlockSpec((tm, 2 * df), lambda i: (i, 0)),
            pl.BlockSpec((tm, ds), lambda i: (i, 0)),
            pl.BlockSpec((tm, dg), lambda i: (i, 0)),
        ],
        out_shape=[
            jax.ShapeDtypeStruct((t, 2 * df), F32),
            jax.ShapeDtypeStruct((t, ds), BF16),
            jax.ShapeDtypeStruct((t, dg), BF16),
        ],
        compiler_params=_cparams("parallel"),
        name="inproj",
    )(x2, g, w_in, bd)


def _channel_dft_table(df):
    c = df // FOURIER_GROUPS
    k = jnp.arange(c, dtype=I32)
    ang = (2.0 * math.pi / c) * ((k[:, None] * k[None, :]) % c).astype(F32)
    eye = jnp.eye(FOURIER_GROUPS, dtype=F32)
    scale = 1.0 / math.sqrt(c)
    re = jnp.kron(eye, jnp.cos(ang)) * scale
    im = -jnp.kron(eye, jnp.sin(ang)) * scale
    return jnp.concatenate([re, im], axis=1).astype(BF16)


def _dft1_tables(n1):
    k = jnp.arange(n1, dtype=I32)
    ang = (2.0 * math.pi / n1) * ((k[:, None] * k[None, :]) % n1).astype(F32)
    eye = jnp.eye(SUBLANES, dtype=F32)
    scale = 1.0 / math.sqrt(n1)
    return (jnp.kron(jnp.cos(ang), eye) * scale).astype(BF16), (jnp.kron(jnp.sin(ang), eye) * scale).astype(BF16)


def _dft3_tables(n1):
    s = n1 * DFT_N2
    nb = n1 // SUBLANES
    k1 = jnp.arange(n1, dtype=I32)[:, None, None]
    k2 = jnp.arange(DFT_N2, dtype=I32)[None, :, None]
    n2 = jnp.arange(DFT_N2, dtype=I32)[None, None, :]
    ang = (2.0 * math.pi / s) * ((n2 * (k1 + n1 * k2)) % s).astype(F32)
    eye = jnp.eye(SUBLANES, dtype=BF16)[None, None, :, :, None]
    rows = DFT_N2 * SUBLANES

    def expand(t):
        t = (t * (1.0 / math.sqrt(DFT_N2))).astype(BF16).reshape(nb, SUBLANES, DFT_N2, DFT_N2)
        t = jnp.transpose(t, (0, 2, 1, 3))[:, :, :, None, :]
        return (t * eye).reshape(nb, rows, rows)

    return expand(jnp.cos(ang)), expand(jnp.sin(ang))


def _dft1_kernel(z_ref, ck_ref, sk_ref, a_ref):
    n1, c2 = z_ref.shape[1], z_ref.shape[3]
    c = c2 // 2
    z = z_ref[0].reshape(n1 * SUBLANES, c2).astype(BF16)
    cz = _dot(ck_ref[...], z)
    sz = _dot(sk_ref[...], z)
    a_ref[0, :, :, :c] = (cz[:, :c] + sz[:, c:]).reshape(n1, SUBLANES, c)
    a_ref[0, :, :, c:] = (cz[:, c:] - sz[:, :c]).reshape(n1, SUBLANES, c)


def _dft1(pq4, ck, sk):
    b, n1, n2, c2 = pq4.shape
    return pl.pallas_call(
        _dft1_kernel,
        grid=(b, n2 // SUBLANES),
        in_specs=[
            pl.BlockSpec((1, n1, SUBLANES, c2), lambda i, j: (i, 0, j, 0)),
            pl.BlockSpec(ck.shape, lambda i, j: (0, 0)),
            pl.BlockSpec(sk.shape, lambda i, j: (0, 0)),
        ],
        out_specs=pl.BlockSpec((1, n1, SUBLANES, c2), lambda i, j: (i, 0, j, 0)),
        out_shape=jax.ShapeDtypeStruct(pq4.shape, F32),
        compiler_params=_cparams("parallel", "parallel"),
        name="dft1",
    )(pq4, ck, sk)


def _dft3_kernel(a_ref, tr_ref, ti_ref, o_ref):
    c2 = a_ref.shape[3]
    c = c2 // 2
    a = a_ref[0].reshape(SUBLANES * DFT_N2, c2).astype(BF16)
    out = _dot(tr_ref[0], a[:, :c]) + _dot(ti_ref[0], a[:, c:])
    o_ref[0] = out.reshape(DFT_N2, SUBLANES, c)


def _dft3(a4, tr, ti):
    b, n1, n2, c2 = a4.shape
    c = c2 // 2
    rows = DFT_N2 * SUBLANES
    return pl.pallas_call(
        _dft3_kernel,
        grid=(n1 // SUBLANES, b),
        in_specs=[
            pl.BlockSpec((1, SUBLANES, n2, c2), lambda k, i: (i, k, 0, 0)),
            pl.BlockSpec((1, rows, rows), lambda k, i: (k, 0, 0)),
            pl.BlockSpec((1, rows, rows), lambda k, i: (k, 0, 0)),
        ],
        out_specs=pl.BlockSpec((1, DFT_N2, SUBLANES, c), lambda k, i: (i, 0, k, 0)),
        out_shape=jax.ShapeDtypeStruct((b, DFT_N2, n1, c), F32),
        compiler_params=_cparams("parallel", "parallel"),
        name="dft3",
    )(a4, tr, ti)


def _ssm_tables(a_re, a_im, log_dt, b_re, b_im, c_re, c_im, d):
    hp = lax.Precision.HIGHEST
    L = SSM_CHUNK
    nd, g, n = a_re.shape
    gi = b_re.shape[-1]
    dt = jnp.exp(log_dt.astype(F32))[..., None]
    ar, ai = a_re.astype(F32), a_im.astype(F32)
    tau = jnp.arange(L + 1, dtype=F32)[:, None, None, None]
    mag = jnp.exp(tau * (ar * dt)[None])
    ang = tau * (ai * dt)[None]
    pr, pi = mag * jnp.cos(ang), mag * jnp.sin(ang)
    nr, ni = pr[1] - 1.0, pi[1]
    den = ar * ar + ai * ai
    qr, qi = (nr * ar + ni * ai) / den, (ni * ar - nr * ai) / den
    br, bi = b_re.astype(F32), b_im.astype(F32)
    bbr = qr[..., None] * br - qi[..., None] * bi
    bbi = qr[..., None] * bi + qi[..., None] * br
    cr, ci = c_re.astype(F32), c_im.astype(F32)

    pbr = pr[:L, ..., None] * bbr[None] - pi[:L, ..., None] * bbi[None]
    pbi = pr[:L, ..., None] * bbi[None] + pi[:L, ..., None] * bbr[None]
    kk = (jnp.einsum('dgon,tdgni->dgtoi', cr, pbr, precision=hp)
          - jnp.einsum('dgon,tdgni->dgtoi', ci, pbi, precision=hp))
    kfull = jnp.concatenate([kk[1][:, :0:-1], kk[0][:, :1] + kk[1][:, :1], kk[0][:, 1:]], axis=1)
    j = jnp.arange(L, dtype=I32)
    tap = jnp.arange(2 * L - 1, dtype=I32)[:, None, None]
    place = (j[None, None, :] - j[None, :, None] + (L - 1) == tap).astype(F32)
    t_tab = jnp.einsum('gtoi,tab->gaibo', kfull, place, precision=hp).reshape(g, L * gi, L * gi)

    ef_r, ef_i = pbr[::-1, 0], pbi[::-1, 0]
    eb_r, eb_i = pbr[:, 1], pbi[:, 1]
    e4 = jnp.stack([ef_r, eb_r, ef_i, eb_i], axis=0)
    e_tab = jnp.transpose(e4, (2, 1, 4, 0, 3)).reshape(g, L * gi, 4 * n)

    pf_r, pf_i = pr[1:, 0], pi[1:, 0]
    pb_r, pb_i = pr[1:, 1][::-1], pi[1:, 1][::-1]

    def readout(cr_, ci_, pr_, pi_):
        mr = cr_[None] * pr_[:, :, None, :] - ci_[None] * pi_[:, :, None, :]
        mi = cr_[None] * pi_[:, :, None, :] + ci_[None] * pr_[:, :, None, :]
        return mr, -mi

    ffr, ffi = readout(cr[0], ci[0], pf_r, pf_i)
    fbr, fbi = readout(cr[1], ci[1], pb_r, pb_i)
    f4 = jnp.stack([ffr, fbr, ffi, fbi], axis=0)
    f_tab = jnp.transpose(f4, (2, 0, 4, 1, 3)).reshape(g, 4 * n, L * gi)

    al = jnp.stack([jnp.concatenate([pr[L, 0], pr[L, 1]], axis=-1),
                    jnp.concatenate([pi[L, 0], pi[L, 1]], axis=-1)], axis=1)
    dv = jnp.tile(d.astype(F32).reshape(g, 1, gi), (1, L, 1)).reshape(g, 1, L * gi)
    return t_tab.astype(BF16), e_tab.astype(BF16), f_tab.astype(BF16), al, dv


def _ssm_kernel(u_ref, t_ref, e_ref, f_ref, al_ref, dv_ref, y_ref, s_scr, h_scr, *, nc, rows, n):
    u = u_ref[0]
    y1 = _dot(u, t_ref[0]) + dv_ref[0] * u.astype(F32)
    s_scr[...] = _dot(u, e_ref[0])
    ar = al_ref[0, 0:1, :]
    ai = al_ref[0, 1:2, :]
    n2 = 2 * n
    is_fwd = lax.broadcasted_iota(I32, (rows, n2), 1) < n

    def step(i, carry):
        hr, hi = carry
        rf = pl.multiple_of(i * rows, rows)
        rb = pl.multiple_of((nc - 1 - i) * rows, rows)
        sf = s_scr[pl.ds(rf, rows), :]
        sb = s_scr[pl.ds(rb, rows), :]
        h_scr[pl.ds(rf, rows), 0:n] = hr[:, 0:n]
        h_scr[pl.ds(rf, rows), n2:n2 + n] = hi[:, 0:n]
        h_scr[pl.ds(rb, rows), n:n2] = hr[:, n:n2]
        h_scr[pl.ds(rb, rows), n2 + n:2 * n2] = hi[:, n:n2]
        sr = jnp.where(is_fwd, sf[:, :n2], sb[:, :n2])
        si = jnp.where(is_fwd, sf[:, n2:], sb[:, n2:])
        return ar * hr - ai * hi + sr, ar * hi + ai * hr + si

    zero = jnp.zeros((rows, n2), F32)
    lax.fori_loop(0, nc, step, (zero, zero))
    y = y1 + _dot(h_scr[...].astype(BF16), f_ref[0])
    y_ref[0] = jax.nn.gelu(y, approximate=True).astype(BF16)


def _ssm(ug, t_tab, e_tab, f_tab, al, dv, nc, rows):
    g, m, k = ug.shape
    n4 = e_tab.shape[2]
    return pl.pallas_call(
        functools.partial(_ssm_kernel, nc=nc, rows=rows, n=n4 // 4),
        grid=(g,),
        in_specs=[
            pl.BlockSpec((1, m, k), lambda i: (i, 0, 0)),
            pl.BlockSpec((1, k, k), lambda i: (i, 0, 0)),
            pl.BlockSpec((1, k, n4), lambda i: (i, 0, 0)),
            pl.BlockSpec((1, n4, k), lambda i: (i, 0, 0)),
            pl.BlockSpec((1, 2, n4 // 2), lambda i: (i, 0, 0)),
            pl.BlockSpec((1, 1, k), lambda i: (i, 0, 0)),
        ],
        out_specs=pl.BlockSpec((1, m, k), lambda i: (i, 0, 0)),
        out_shape=jax.ShapeDtypeStruct((g, m, k), BF16),
        scratch_shapes=[pltpu.VMEM((m, n4), F32), pltpu.VMEM((m, n4), F32)],
        compiler_params=_cparams("parallel"),
        name="ssm",
    )(ug, t_tab, e_tab, f_tab, al, dv)


def _mixout_kernel(fre_ref, ys_ref, gate_ref, x_ref, wf_ref, wglu_ref, wout_ref, gffn_ref, wr_ref,
                   x1_ref, h2_ref, aff_ref, *, d, ne):
    y_f = _dot(fre_ref[...].astype(BF16), wf_ref[...])
    vg = _dot(ys_ref[...], wglu_ref[...])
    y_s = vg[:, :d] * jax.nn.sigmoid(vg[:, d:])
    gate = gate_ref[...].astype(F32)
    m = gate[:, :d] * y_f + gate[:, d:] * y_s
    x1 = x_ref[...] + _dot(m.astype(BF16), wout_ref[...])
    x1_ref[...] = x1
    h2 = _rms(x1, gffn_ref[...])
    h2_ref[...] = h2.astype(BF16)
    hi = h2.astype(BF16)
    lo = (h2 - hi.astype(F32)).astype(BF16)
    r = _dot(jnp.concatenate([hi, lo], axis=1), wr_ref[...])
    logits = r[:, :ne] + r[:, ne:]
    logits = logits - jnp.max(logits, axis=-1, keepdims=True)
    p = jnp.exp(logits)
    aff_ref[...] = p / jnp.sum(p, axis=-1, keepdims=True)


def _mixout(fre, ys, gate, x2, wf, wglu, wout, gffn, wr2, ne, tm=512):
    t, d = x2.shape
    full = lambda a: pl.BlockSpec(a.shape, lambda i: (0,) * a.ndim)
    row = lambda a: pl.BlockSpec((tm, a.shape[1]), lambda i: (i, 0))
    return pl.pallas_call(
        functools.partial(_mixout_kernel, d=d, ne=ne),
        grid=(t // tm,),
        in_specs=[row(fre), row(ys), row(gate), row(x2), full(wf), full(wglu), full(wout), full(gffn), full(wr2)],
        out_specs=[
            pl.BlockSpec((tm, d), lambda i: (i, 0)),
            pl.BlockSpec((tm, d), lambda i: (i, 0)),
            pl.BlockSpec((tm, ne), lambda i: (i, 0)),
        ],
        out_shape=[
            jax.ShapeDtypeStruct((t, d), F32),
            jax.ShapeDtypeStruct((t, d), BF16),
            jax.ShapeDtypeStruct((t, ne), F32),
        ],
        compiler_params=_cparams("parallel"),
        name="mixout",
    )(fre, ys, gate, x2, wf, wglu, wout, gffn, wr2)


def _topk_kernel(aff_ref, pos_ref, st_ref, *, cap, blk):
    v = aff_ref[...]
    r, s = v.shape
    capf = float(cap)

    def bit_step(i, t):
        cand = t | (jnp.int32(1) << (30 - i))
        cnt = jnp.sum(jnp.where(v >= pltpu.bitcast(cand, F32), 1.0, 0.0), axis=1, keepdims=True)
        return jnp.where(cnt >= capf, cand, t)

    thr = lax.fori_loop(0, 31, bit_step, jnp.zeros((r, 1), I32))
    gt = jnp.where(v >= pltpu.bitcast(thr + 1, F32), 1.0, 0.0)
    eq = jnp.where(v >= pltpu.bitcast(thr, F32), 1.0, 0.0) - gt
    need = capf - jnp.sum(gt, axis=1, keepdims=True)
    ii = lax.broadcasted_iota(I32, (blk, blk), 0)
    jj = lax.broadcasted_iota(I32, (blk, blk), 1)
    tri = jnp.where(ii < jj, 1.0, 0.0).astype(BF16)
    run_eq = jnp.zeros((r, 1), F32)
    run = jnp.zeros((r, 1), F32)
    for k in range(s // blk):
        sl = slice(k * blk, (k + 1) * blk)
        eqb, gtb = eq[:, sl], gt[:, sl]
        rank_eq = _dot(eqb.astype(BF16), tri) + run_eq
        run_eq = run_eq + jnp.sum(eqb, axis=1, keepdims=True)
        mask = gtb + eqb * jnp.where(rank_eq < need, 1.0, 0.0)
        pos = _dot(mask.astype(BF16), tri) + run
        st_ref[:, k:k + 1] = run.astype(I32)
        run = run + jnp.sum(mask, axis=1, keepdims=True)
        pos_ref[:, sl] = jnp.where(mask > 0.0, pos, -1.0).astype(I32)
    st_ref[:, s // blk:s // blk + 1] = run.astype(I32)


def _topk(aff_rows, cap, blk=TOPK_BLOCK):
    r, s = aff_rows.shape
    nblk = s // blk
    return pl.pallas_call(
        functools.partial(_topk_kernel, cap=cap, blk=blk),
        grid=(1,),
        in_specs=[pl.BlockSpec((r, s), lambda i: (0, 0))],
        out_specs=[pl.BlockSpec((r, s), lambda i: (0, 0)), pl.BlockSpec((r, nblk + 1), lambda i: (0, 0))],
        out_shape=[jax.ShapeDtypeStruct((r, s), I32), jax.ShapeDtypeStruct((r, nblk + 1), I32)],
        compiler_params=_cparams("arbitrary"),
        name="topk",
    )(aff_rows)


def _floor_rows(x):
    return (x // PACKED_ROWS) * PACKED_ROWS


def _num_passes(starts, ends, win):
    need = ends[0] - _floor_rows(starts[0])
    for a, b in zip(starts[1:], ends[1:]):
        need = jnp.maximum(need, b - _floor_rows(a))
    return (need + win - 1) // win


def _pass_window(start, p, win, cap):
    first = _floor_rows(start) + p * win
    return first, pl.multiple_of(jnp.minimum(first, cap - win), PACKED_ROWS)


def _gather_kernel(st_ref, h_ref, pos_ref, aff_ref, x_ref, v_ref, acc_ref, vacc_ref, *, nt, tt, win, cap, ne, ng):
    row0 = pl.program_id(0) * ne + pl.program_id(1) * ng
    acc_ref[...] = jnp.zeros_like(acc_ref)
    vacc_ref[...] = jnp.zeros_like(vacc_ref)
    riota = lax.broadcasted_iota(I32, (win, tt), 0)

    def tile_body(t, carry):
        hrows = h_ref[0, pl.ds(pl.multiple_of(t * tt, tt), tt), :]
        starts = [st_ref[(row0 + g) * (nt + 1) + t] for g in range(ng)]
        ends = [st_ref[(row0 + g) * (nt + 1) + t + 1] for g in range(ng)]

        def pass_body(p, c):
            hots, wins = [], []
            for g in range(ng):
                first, ws = _pass_window(starts[g], p, win, cap)
                pos = pos_ref[g, pl.ds(t, 1), :]
                hots.append((pos - ws == riota) & (pos >= first))
                wins.append(ws)
            onehot = jnp.concatenate([jnp.where(h, 1.0, 0.0).astype(BF16) for h in hots], axis=0)
            rows = _dot(onehot, hrows)
            for g in range(ng):
                acc_ref[g, pl.ds(wins[g], win), :] += rows[g * win:(g + 1) * win]
                vals = jnp.sum(jnp.where(hots[g], aff_ref[g, pl.ds(t, 1), :], 0.0), axis=1, keepdims=True)
                vacc_ref[g, pl.ds(wins[g], win), :] += vals
            return c

        lax.fori_loop(0, _num_passes(starts, ends, win), pass_body, 0)
        return carry

    lax.fori_loop(0, nt, tile_body, 0)
    x_ref[0] = acc_ref[...].astype(BF16)
    v_ref[0] = vacc_ref[...]


def _gather(starts, h3, pos3, aff3t, cap, ne):
    b, s, d = h3.shape
    tt, win, ng = GATHER_TILE, GATHER_WINDOW, GATHER_EXPERTS
    nt = s // tt
    grid_spec = pltpu.PrefetchScalarGridSpec(
        num_scalar_prefetch=1,
        grid=(b, ne // ng),
        in_specs=[
            pl.BlockSpec((1, s, d), lambda i, e, st: (i, 0, 0), pipeline_mode=pl.Buffered(1)),
            pl.BlockSpec((ng, nt, tt), lambda i, e, st: (i * (ne // ng) + e, 0, 0)),
            pl.BlockSpec((ng, nt, tt), lambda i, e, st: (i * (ne // ng) + e, 0, 0)),
        ],
        out_specs=[
            pl.BlockSpec((1, ng, cap, d), lambda i, e, st: (i, e, 0, 0)),
            pl.BlockSpec((1, ng, cap, 1), lambda i, e, st: (i, e, 0, 0)),
        ],
        scratch_shapes=[pltpu.VMEM((ng, cap, d), F32), pltpu.VMEM((ng, cap, 1), F32)],
    )
    return pl.pallas_call(
        functools.partial(_gather_kernel, nt=nt, tt=tt, win=win, cap=cap, ne=ne, ng=ng),
        grid_spec=grid_spec,
        out_shape=[jax.ShapeDtypeStruct((b, ne, cap, d), BF16), jax.ShapeDtypeStruct((b, ne, cap, 1), F32)],
        compiler_params=_cparams("arbitrary", "arbitrary"),
        name="gather",
    )(starts, h3, pos3, aff3t)


def _ffn_kernel(x_ref, v_ref, wg_ref, wu_ref, wd_ref, y_ref, acc_ref, *, nf):
    f = pl.program_id(2)
    nb, _, cap, d = x_ref.shape
    x = x_ref[...].reshape(nb * cap, d)
    g = _dot(x, wg_ref[0].astype(BF16))
    u = _dot(x, wu_ref[0].astype(BF16))
    hid = (g * jax.nn.sigmoid(g) * u).astype(BF16)
    part = _dot(hid, wd_ref[0].astype(BF16))

    @pl.when(f == 0)
    def _():
        acc_ref[...] = part

    @pl.when(f > 0)
    def _():
        acc_ref[...] += part

    @pl.when(f == nf - 1)
    def _():
        y = acc_ref[...] * v_ref[...].reshape(nb * cap, 1)
        y_ref[...] = y.astype(BF16).reshape(nb, 1, cap, d)


def _ffn(xg, vals, wg, wu, wd, nb, tf):
    b, ne, cap, d = xg.shape
    dexp = wg.shape[2]
    nf = dexp // tf
    return pl.pallas_call(
        functools.partial(_ffn_kernel, nf=nf),
        grid=(ne, b // nb, nf),
        in_specs=[
            pl.BlockSpec((nb, 1, cap, d), lambda e, i, f: (i, e, 0, 0)),
            pl.BlockSpec((nb, 1, cap, 1), lambda e, i, f: (i, e, 0, 0)),
            pl.BlockSpec((1, d, tf), lambda e, i, f: (e, 0, f)),
            pl.BlockSpec((1, d, tf), lambda e, i, f: (e, 0, f)),
            pl.BlockSpec((1, tf, d), lambda e, i, f: (e, f, 0)),
        ],
        out_specs=pl.BlockSpec((nb, 1, cap, d), lambda e, i, f: (i, e, 0, 0)),
        out_shape=jax.ShapeDtypeStruct(xg.shape, BF16),
        scratch_shapes=[pltpu.VMEM((nb * cap, d), F32)],
        compiler_params=_cparams("parallel", "parallel", "arbitrary"),
        name="ffn",
    )(xg, vals, wg, wu, wd)


def _combine_kernel(st_ref, y_ref, pos_ref, x1_ref, p_ref, wpp_ref, wpg_ref, gple_ref, gout_ref,
                    o_ref, *, nt, tt, win, cap, ne, final):
    row0 = pl.program_id(0) * ne
    t = pl.program_id(1)
    starts = [st_ref[(row0 + e) * (nt + 1) + t] for e in range(ne)]
    ends = [st_ref[(row0 + e) * (nt + 1) + t + 1] for e in range(ne)]
    liota = lax.broadcasted_iota(I32, (tt, win), 1)
    pos = pos_ref[0]

    def pass_body(p, acc):
        hots, wins = [], []
        for e in range(ne):
            first, ws = _pass_window(starts[e], p, win, cap)
            pe = pos[:, e:e + 1]
            hots.append(jnp.where((pe - ws == liota) & (pe >= first), 1.0, 0.0).astype(BF16))
            wins.append(y_ref[0, e, pl.ds(ws, win), :])
        return acc + _dot(jnp.concatenate(hots, axis=1), jnp.concatenate(wins, axis=0))

    acc = lax.fori_loop(0, _num_passes(starts, ends, win), pass_body, x1_ref[0])
    emb = _dot(p_ref[0].astype(BF16), wpp_ref[...])
    gate = jax.nn.sigmoid(_dot(_rms(acc, gple_ref[...]).astype(BF16), wpg_ref[...]))
    x3 = acc + gate * emb
    o_ref[0] = _rms(x3, gout_ref[...]) if final else x3


def _combine(starts, yg, post, x13, p3, wpp, wpg, gple, gout, cap, final):
    b, ne, _, d = yg.shape
    s = x13.shape[1]
    tt, win = COMBINE_TILE, COMBINE_WINDOW
    nt = s // tt
    tile = lambda a: pl.BlockSpec((1, tt, a.shape[2]), lambda i, t, st: (i, t, 0))
    full = lambda a: pl.BlockSpec(a.shape, lambda i, t, st: (0,) * a.ndim)
    grid_spec = pltpu.PrefetchScalarGridSpec(
        num_scalar_prefetch=1,
        grid=(b, nt),
        in_specs=[
            pl.BlockSpec((1, ne, cap, d), lambda i, t, st: (i, 0, 0, 0), pipeline_mode=pl.Buffered(1)),
            tile(post), tile(x13), tile(p3), full(wpp), full(wpg), full(gple), full(gout),
        ],
        out_specs=pl.BlockSpec((1, tt, d), lambda i, t, st: (i, t, 0)),
    )
    return pl.pallas_call(
        functools.partial(_combine_kernel, nt=nt, tt=tt, win=win, cap=cap, ne=ne, final=final),
        grid_spec=grid_spec,
        out_shape=jax.ShapeDtypeStruct(x13.shape, F32),
        compiler_params=_cparams("arbitrary", "arbitrary"),
        name="combine",
    )(starts, yg, post, x13, p3, wpp, wpg, gple, gout)


def kernel(x, p, g_mix, w_in, w_fourier, ssm_a_re, ssm_a_im, ssm_log_dt, ssm_b_re, ssm_b_im, ssm_c_re, ssm_c_im, ssm_d, w_glu, w_out, g_ffn, w_router, w_exp_gate, w_exp_up, w_exp_down, g_ple, w_ple_gate, w_ple_proj, g_final):
    b, s, d = x.shape
    depth = p.shape[0]
    df = w_fourier.shape[1]
    ds = w_glu.shape[1]
    ne = w_router.shape[2]
    dexp = w_exp_gate.shape[3]
    groups, gi = ssm_b_re.shape[2], ssm_b_re.shape[4]
    L = SSM_CHUNK
    n1 = s // DFT_N2
    nc = s // L
    cap = EC_CAPACITY * s // ne
    rows = -(-b // SUBLANES) * SUBLANES
    assert s % (DFT_N2 * SUBLANES) == 0 and s % GATHER_TILE == 0 and cap % PACKED_ROWS == 0
    assert cap >= COMBINE_WINDOW and ne % GATHER_EXPERTS == 0 and GATHER_TILE == COMBINE_TILE
    assert ds == groups * gi and df % FOURIER_GROUPS == 0

    bd = _channel_dft_table(df)
    ck, sk = _dft1_tables(n1)
    tr, ti = _dft3_tables(n1)
    tf = 256 if dexp % 256 == 0 else dexp
    nb_ffn = 2 if b % 2 == 0 else 1

    xcur = x.reshape(b * s, d)
    for i in range(depth):
        final = i == depth - 1
        row = lambda v: v.astype(F32).reshape(1, -1)
        pq, us, gate = _inproj(xcur, row(g_mix[i]), w_in[i].astype(BF16), bd, df, ds)
        a4 = _dft1(pq.reshape(b, n1, DFT_N2, 2 * df), ck, sk)
        fre = _dft3(a4, tr, ti).reshape(b * s, df)

        tabs = _ssm_tables(ssm_a_re[i], ssm_a_im[i], ssm_log_dt[i], ssm_b_re[i], ssm_b_im[i],
                           ssm_c_re[i], ssm_c_im[i], ssm_d[i])
        ug = jnp.transpose(us.reshape(b, nc, L, groups, gi), (3, 1, 0, 2, 4))
        ug = jnp.pad(ug, ((0, 0), (0, 0), (0, rows - b), (0, 0), (0, 0))).reshape(groups, nc * rows, L * gi)
        yg_ssm = _ssm(ug, *tabs, nc=nc, rows=rows)
        ys = jnp.transpose(yg_ssm.reshape(groups, nc, rows, L, gi)[:, :, :b], (2, 1, 3, 0, 4)).reshape(b * s, ds)

        wr = w_router[i].astype(F32)
        wr_hi = wr.astype(BF16)
        wr_lo = (wr - wr_hi.astype(F32)).astype(BF16)
        wr2 = jnp.concatenate([jnp.concatenate([wr_hi, wr_lo], axis=1),
                               jnp.concatenate([wr_hi, jnp.zeros_like(wr_lo)], axis=1)], axis=0)
        x1, h2, aff = _mixout(fre, ys, gate, xcur, w_fourier[i].astype(BF16), w_glu[i].astype(BF16),
                              w_out[i].astype(BF16), row(g_ffn[i]), wr2, ne)

        aff3 = aff.reshape(b, s, ne)
        aff_rows = jnp.transpose(aff3, (0, 2, 1)).reshape(b * ne, s)
        posm, st_blk = _topk(aff_rows, cap)
        starts = st_blk[:, ::GATHER_TILE // TOPK_BLOCK].reshape(-1)
        tiles = (b * ne, s // GATHER_TILE, GATHER_TILE)
        xg, vals = _gather(starts, h2.reshape(b, s, d), posm.reshape(tiles), aff_rows.reshape(tiles), cap, ne)
        yg = _ffn(xg, vals, w_exp_gate[i], w_exp_up[i], w_exp_down[i], nb_ffn, tf)
        post = jnp.transpose(posm.reshape(b, ne, s), (0, 2, 1))
        xnext = _combine(starts, yg, post, x1.reshape(b, s, d), p[i],
                         w_ple_proj[i].astype(BF16), w_ple_gate[i].astype(BF16), row(g_ple[i]),
                         row(g_final) if final else row(g_ple[i]), cap, final)
        xcur = xnext.reshape(b * s, d)
    return xcur.reshape(b, s, d)
```

```python
import functools
import math

import jax
import jax.numpy as jnp
from jax import lax
from jax.experimental import pallas as pl
from jax.experimental.pallas import tpu as pltpu

F32 = jnp.float32
BF16 = jnp.bfloat16
I32 = jnp.int32

RMS_EPS = 1e-6
FOURIER_GROUPS = 4
EC_CAPACITY = 2
DFT_N2 = 128
SUBLANES = 8
LANES = 128
PACKED_ROWS = 16
SSM_CHUNK = 32
SSM_TILE_CHUNKS = 128
TOPK_BLOCK = 128
GATHER_TILE = 256
GATHER_WINDOW = 48
GATHER_EXPERTS = 2
GATHER_UNROLL = 4
COMBINE_TILE = 256
COMBINE_WINDOW = 48
COMBINE_SPLIT = 32
VMEM_LIMIT = 56 * 1024 * 1024


def _cparams(*sem):
    return pltpu.CompilerParams(dimension_semantics=sem, vmem_limit_bytes=VMEM_LIMIT)


def _rms(x, g):
    return x * lax.rsqrt(jnp.mean(x * x, axis=-1, keepdims=True) + RMS_EPS) * g


def _dot(a, b):
    return jnp.dot(a, b, preferred_element_type=F32)


def _inproj_kernel(x_ref, g_ref, w_ref, bd_ref, pq_ref, us_ref, gate_ref, *, df, ds):
    h = _rms(x_ref[...], g_ref[...]).astype(BF16)
    z = _dot(h, w_ref[...])
    pq_ref[...] = _dot(z[:, :df].astype(BF16), bd_ref[...])
    for q in range(ds // LANES):
        us_ref[q] = z[:, df + q * LANES:df + (q + 1) * LANES]
    gate_ref[...] = jax.nn.sigmoid(z[:, df + ds:]).astype(BF16)


def _inproj(x2, g, w_in, bd, df, ds, tm=512):
    t, d = x2.shape
    dg = w_in.shape[1] - df - ds
    return pl.pallas_call(
        functools.partial(_inproj_kernel, df=df, ds=ds),
        grid=(t // tm,),
        in_specs=[
            pl.BlockSpec((tm, d), lambda i: (i, 0)),
            pl.BlockSpec((1, d), lambda i: (0, 0)),
            pl.BlockSpec(w_in.shape, lambda i: (0, 0)),
            pl.BlockSpec(bd.shape, lambda i: (0, 0)),
        ],
        out_specs=[
            pl.BlockSpec((tm, 2 * df), lambda i: (i, 0)),
            pl.BlockSpec((ds // LANES, tm, LANES), lambda i: (0, i, 0)),
            pl.BlockSpec((tm, dg), lambda i: (i, 0)),
        ],
        out_shape=[
            jax.ShapeDtypeStruct((t, 2 * df), F32),
            jax.ShapeDtypeStruct((ds // LANES, t, LANES), F32),
            jax.ShapeDtypeStruct((t, dg), BF16),
        ],
        compiler_params=_cparams("parallel"),
        name="inproj",
    )(x2, g, w_in, bd)


def _channel_dft_table(df):
    c = df // FOURIER_GROUPS
    k = jnp.arange(c, dtype=I32)
    ang = (2.0 * math.pi / c) * ((k[:, None] * k[None, :]) % c).astype(F32)
    eye = jnp.eye(FOURIER_GROUPS, dtype=F32)
    scale = 1.0 / math.sqrt(c)
    re = jnp.kron(eye, jnp.cos(ang)) * scale
    im = -jnp.kron(eye, jnp.sin(ang)) * scale
    return jnp.concatenate([re, im], axis=1).astype(BF16)


def _dft1_tables(n1):
    k = jnp.arange(n1, dtype=I32)
    ang = (2.0 * math.pi / n1) * ((k[:, None] * k[None, :]) % n1).astype(F32)
    eye = jnp.eye(SUBLANES, dtype=F32)
    scale = 1.0 / math.sqrt(n1)
    return (jnp.kron(jnp.cos(ang), eye) * scale).astype(BF16), (jnp.kron(jnp.sin(ang), eye) * scale).astype(BF16)


def _dft3_tables(n1):
    s = n1 * DFT_N2
    nb = n1 // SUBLANES
    k1 = jnp.arange(n1, dtype=I32)[:, None, None]
    k2 = jnp.arange(DFT_N2, dtype=I32)[None, :, None]
    n2 = jnp.arange(DFT_N2, dtype=I32)[None, None, :]
    ang = (2.0 * math.pi / s) * ((n2 * (k1 + n1 * k2)) % s).astype(F32)
    eye = jnp.eye(SUBLANES, dtype=BF16)[None, None, :, :, None]
    rows = DFT_N2 * SUBLANES

    def expand(t):
        t = (t * (1.0 / math.sqrt(DFT_N2))).astype(BF16).reshape(nb, SUBLANES, DFT_N2, DFT_N2)
        t = jnp.transpose(t, (0, 2, 1, 3))[:, :, :, None, :]
        return (t * eye).reshape(nb, rows, rows)

    return expand(jnp.cos(ang)), expand(jnp.sin(ang))


def _dft1_kernel(z_ref, ck_ref, sk_ref, a_ref):
    n1, c2 = z_ref.shape[1], z_ref.shape[3]
    c = c2 // 2
    z = z_ref[0].reshape(n1 * SUBLANES, c2).astype(BF16)
    cz = _dot(ck_ref[...], z)
    sz = _dot(sk_ref[...], z)
    a_ref[0, :, :, :c] = (cz[:, :c] + sz[:, c:]).reshape(n1, SUBLANES, c)
    a_ref[0, :, :, c:] = (cz[:, c:] - sz[:, :c]).reshape(n1, SUBLANES, c)


def _dft1(pq4, ck, sk):
    b, n1, n2, c2 = pq4.shape
    return pl.pallas_call(
        _dft1_kernel,
        grid=(b, n2 // SUBLANES),
        in_specs=[
            pl.BlockSpec((1, n1, SUBLANES, c2), lambda i, j: (i, 0, j, 0)),
            pl.BlockSpec(ck.shape, lambda i, j: (0, 0)),
            pl.BlockSpec(sk.shape, lambda i, j: (0, 0)),
        ],
        out_specs=pl.BlockSpec((1, n1, SUBLANES, c2), lambda i, j: (i, 0, j, 0)),
        out_shape=jax.ShapeDtypeStruct(pq4.shape, F32),
        compiler_params=_cparams("parallel", "parallel"),
        name="dft1",
    )(pq4, ck, sk)


def _dft3_kernel(a_ref, tr_ref, ti_ref, o_ref):
    c2 = a_ref.shape[3]
    c = c2 // 2
    a = a_ref[0].reshape(SUBLANES * DFT_N2, c2).astype(BF16)
    out = _dot(tr_ref[0], a[:, :c]) + _dot(ti_ref[0], a[:, c:])
    o_ref[0] = out.reshape(DFT_N2, SUBLANES, c)


def _dft3(a4, tr, ti):
    b, n1, n2, c2 = a4.shape
    c = c2 // 2
    rows = DFT_N2 * SUBLANES
    return pl.pallas_call(
        _dft3_kernel,
        grid=(n1 // SUBLANES, b),
        in_specs=[
            pl.BlockSpec((1, SUBLANES, n2, c2), lambda k, i: (i, k, 0, 0)),
            pl.BlockSpec((1, rows, rows), lambda k, i: (k, 0, 0)),
            pl.BlockSpec((1, rows, rows), lambda k, i: (k, 0, 0)),
        ],
        out_specs=pl.BlockSpec((1, DFT_N2, SUBLANES, c), lambda k, i: (i, 0, k, 0)),
        out_shape=jax.ShapeDtypeStruct((b, DFT_N2, n1, c), F32),
        compiler_params=_cparams("parallel", "parallel"),
        name="dft3",
    )(a4, tr, ti)


def _ssm_tables(a_re, a_im, log_dt, b_re, b_im, c_re, c_im, d):
    hp = lax.Precision.HIGHEST
    L = SSM_CHUNK
    nd, g, n = a_re.shape
    gi = b_re.shape[-1]
    dt = jnp.exp(log_dt.astype(F32))[..., None]
    ar, ai = a_re.astype(F32), a_im.astype(F32)
    tau = jnp.arange(L + 1, dtype=F32)[:, None, None, None]
    mag = jnp.exp(tau * (ar * dt)[None])
    ang = tau * (ai * dt)[None]
    pr, pi = mag * jnp.cos(ang), mag * jnp.sin(ang)
    nr, ni = pr[1] - 1.0, pi[1]
    den = ar * ar + ai * ai
    qr, qi = (nr * ar + ni * ai) / den, (ni * ar - nr * ai) / den
    br, bi = b_re.astype(F32), b_im.astype(F32)
    bbr = qr[..., None] * br - qi[..., None] * bi
    bbi = qr[..., None] * bi + qi[..., None] * br
    cr, ci = c_re.astype(F32), c_im.astype(F32)

    pbr = pr[:L, ..., None] * bbr[None] - pi[:L, ..., None] * bbi[None]
    pbi = pr[:L, ..., None] * bbi[None] + pi[:L, ..., None] * bbr[None]
    kk = (jnp.einsum('dgon,tdgni->dgtoi', cr, pbr, precision=hp)
          - jnp.einsum('dgon,tdgni->dgtoi', ci, pbi, precision=hp))
    kfull = jnp.concatenate([kk[1][:, :0:-1], kk[0][:, :1] + kk[1][:, :1], kk[0][:, 1:]], axis=1)
    j = jnp.arange(L, dtype=I32)
    tap = jnp.arange(2 * L - 1, dtype=I32)[:, None, None]
    place = (j[None, None, :] - j[None, :, None] + (L - 1) == tap).astype(F32)
    t_tab = jnp.einsum('gtoi,tab->gaibo', kfull, place, precision=hp).reshape(g, L * gi, L * gi)

    ef_r, ef_i = pbr[::-1, 0], pbi[::-1, 0]
    eb_r, eb_i = pbr[:, 1], pbi[:, 1]
    e4 = jnp.stack([ef_r, eb_r, ef_i, eb_i], axis=0)
    e_tab = jnp.transpose(e4, (2, 1, 4, 0, 3)).reshape(g, L * gi, 4 * n)

    pf_r, pf_i = pr[1:, 0], pi[1:, 0]
    pb_r, pb_i = pr[1:, 1][::-1], pi[1:, 1][::-1]

    def readout(cr_, ci_, pr_, pi_):
        mr = cr_[None] * pr_[:, :, None, :] - ci_[None] * pi_[:, :, None, :]
        mi = cr_[None] * pi_[:, :, None, :] + ci_[None] * pr_[:, :, None, :]
        return mr, -mi

    ffr, ffi = readout(cr[0], ci[0], pf_r, pf_i)
    fbr, fbi = readout(cr[1], ci[1], pb_r, pb_i)
    f4 = jnp.stack([ffr, fbr, ffi, fbi], axis=0)
    f_tab = jnp.transpose(f4, (2, 0, 4, 1, 3)).reshape(g, 4 * n, L * gi)

    al = jnp.stack([jnp.concatenate([pr[L, 0], pr[L, 1]], axis=-1),
                    jnp.concatenate([pi[L, 0], pi[L, 1]], axis=-1)], axis=1)
    dv = jnp.tile(d.astype(F32).reshape(g, 1, gi), (1, L, 1)).reshape(g, L * gi, 1)
    left = lambda tab: jnp.swapaxes(tab, 1, 2).astype(BF16)
    return left(t_tab), left(e_tab), left(f_tab), al, dv


def _to_groups_kernel(u_ref, a_ref, *, L, groups, gi):
    ncl = a_ref.shape[2]
    gq = LANES // gi
    for q in range(u_ref.shape[0]):
        for j in range(L):
            zt = u_ref[q, pl.ds(j, ncl, stride=L), :].T
            a_ref[q * gq:(q + 1) * gq, pl.ds(j * gi, gi), :] = zt.reshape(gq, gi, ncl).astype(BF16)


def _to_groups(us, L, groups, gi):
    nq, t, _ = us.shape
    ncl = SSM_TILE_CHUNKS
    return pl.pallas_call(
        functools.partial(_to_groups_kernel, L=L, groups=groups, gi=gi),
        grid=(t // (ncl * L),),
        in_specs=[pl.BlockSpec((nq, ncl * L, LANES), lambda i: (0, i, 0))],
        out_specs=pl.BlockSpec((groups, L * gi, ncl), lambda i: (0, 0, i)),
        out_shape=jax.ShapeDtypeStruct((groups, L * gi, t // L), BF16),
        compiler_params=_cparams("parallel"),
        name="to_groups",
    )(us)


def _to_tokens_kernel(y_ref, o_ref, *, L, groups, gi):
    ncl = y_ref.shape[2]
    gq = LANES // gi
    for q in range(o_ref.shape[0]):
        for j in range(L):
            yj = y_ref[q * gq:(q + 1) * gq, pl.ds(j * gi, gi), :].astype(F32).reshape(LANES, ncl)
            o_ref[q, pl.ds(j, ncl, stride=L), :] = yj.T


def _to_tokens(yt, L, groups, gi):
    _, k, nchunks = yt.shape
    ncl = SSM_TILE_CHUNKS
    return pl.pallas_call(
        functools.partial(_to_tokens_kernel, L=L, groups=groups, gi=gi),
        grid=(nchunks // ncl,),
        in_specs=[pl.BlockSpec((groups, k, ncl), lambda i: (0, 0, i))],
        out_specs=pl.BlockSpec((groups * gi // LANES, ncl * L, LANES), lambda i: (0, i, 0)),
        out_shape=jax.ShapeDtypeStruct((groups * gi // LANES, nchunks * L, LANES), F32),
        compiler_params=_cparams("parallel"),
        name="to_tokens",
    )(yt)


def _ssm_kernel(a_ref, t_ref, e_ref, f_ref, al_ref, dv_ref, y_ref, s_scr, h_scr, *, nc, nb, rows, n):
    a = a_ref[0]
    y1 = _dot(t_ref[0], a) + dv_ref[0] * a.astype(F32)
    st = _dot(e_ref[0], a)
    n2 = 2 * n
    s_scr[...] = jnp.zeros_like(s_scr)
    for b in range(nb):
        sb_t = st[:, b * nc:(b + 1) * nc].T
        for q in range(2):
            s_scr[q, pl.ds(b, nc, stride=rows), :] = sb_t[:, q * n2:(q + 1) * n2]
    ar = al_ref[0, 0:1, :]
    ai = al_ref[0, 1:2, :]
    is_fwd = lax.broadcasted_iota(I32, (rows, n2), 1) < n

    def step(i, carry):
        hr, hi = carry
        rf = pl.ds(pl.multiple_of(i * rows, rows), rows)
        rb = pl.ds(pl.multiple_of((nc - 1 - i) * rows, rows), rows)
        h_scr[0, rf, 0:n] = hr[:, 0:n]
        h_scr[1, rf, 0:n] = hi[:, 0:n]
        h_scr[0, rb, n:n2] = hr[:, n:n2]
        h_scr[1, rb, n:n2] = hi[:, n:n2]
        sr = jnp.where(is_fwd, s_scr[0, rf, :], s_scr[0, rb, :])
        si = jnp.where(is_fwd, s_scr[1, rf, :], s_scr[1, rb, :])
        return ar * hr - ai * hi + sr, ar * hi + ai * hr + si

    zero = jnp.zeros((rows, n2), F32)
    lax.fori_loop(0, nc, step, (zero, zero))
    ht = jnp.concatenate(
        [jnp.concatenate([h_scr[q, pl.ds(b, nc, stride=rows), :].T for q in range(2)], axis=0) for b in range(nb)],
        axis=1)
    y = y1 + _dot(f_ref[0], ht.astype(BF16))
    y_ref[0] = jax.nn.gelu(y, approximate=True).astype(BF16)


def _ssm(ag, t_tab, e_tab, f_tab, al, dv, nc, nb):
    g, k, m = ag.shape
    n4 = e_tab.shape[1]
    rows = -(-nb // SUBLANES) * SUBLANES
    return pl.pallas_call(
        functools.partial(_ssm_kernel, nc=nc, nb=nb, rows=rows, n=n4 // 4),
        grid=(g,),
        in_specs=[
            pl.BlockSpec((1, k, m), lambda i: (i, 0, 0)),
            pl.BlockSpec((1, k, k), lambda i: (i, 0, 0)),
            pl.BlockSpec((1, n4, k), lambda i: (i, 0, 0)),
            pl.BlockSpec((1, k, n4), lambda i: (i, 0, 0)),
            pl.BlockSpec((1, 2, n4 // 2), lambda i: (i, 0, 0)),
            pl.BlockSpec((1, k, 1), lambda i: (i, 0, 0)),
        ],
        out_specs=pl.BlockSpec((1, k, m), lambda i: (i, 0, 0)),
        out_shape=jax.ShapeDtypeStruct((g, k, m), BF16),
        scratch_shapes=[pltpu.VMEM((2, nc * rows, n4 // 2), F32), pltpu.VMEM((2, nc * rows, n4 // 2), F32)],
        compiler_params=_cparams("parallel"),
        name="ssm",
    )(ag, t_tab, e_tab, f_tab, al, dv)


def _mixout_kernel(fre_ref, ys_ref, gate_ref, x_ref, wf_ref, wglu_ref, wout_ref, gffn_ref, wr_ref,
                   x1_ref, h2_ref, aff_ref, *, d, ne):
    y_f = _dot(fre_ref[...].astype(BF16), wf_ref[...])
    ys = jnp.concatenate([ys_ref[q] for q in range(ys_ref.shape[0])], axis=1)
    vg = _dot(ys.astype(BF16), wglu_ref[...])
    y_s = vg[:, :d] * jax.nn.sigmoid(vg[:, d:])
    gate = gate_ref[...].astype(F32)
    m = gate[:, :d] * y_f + gate[:, d:] * y_s
    x1 = x_ref[...] + _dot(m.astype(BF16), wout_ref[...])
    x1_ref[...] = x1
    h2 = _rms(x1, gffn_ref[...])
    h2_ref[...] = h2.astype(BF16)
    hi = h2.astype(BF16)
    lo = (h2 - hi.astype(F32)).astype(BF16)
    r = _dot(jnp.concatenate([hi, lo], axis=1), wr_ref[...])
    logits = r[:, :ne] + r[:, ne:]
    logits = logits - jnp.max(logits, axis=-1, keepdims=True)
    p = jnp.exp(logits)
    aff_ref[...] = p / jnp.sum(p, axis=-1, keepdims=True)


def _mixout(fre, ys, gate, x2, wf, wglu, wout, gffn, wr2, ne, tm=512):
    t, d = x2.shape
    full = lambda a: pl.BlockSpec(a.shape, lambda i: (0,) * a.ndim)
    row = lambda a: pl.BlockSpec((tm, a.shape[1]), lambda i: (i, 0))
    return pl.pallas_call(
        functools.partial(_mixout_kernel, d=d, ne=ne),
        grid=(t // tm,),
        in_specs=[row(fre), pl.BlockSpec((ys.shape[0], tm, LANES), lambda i: (0, i, 0)), row(gate), row(x2),
                  full(wf), full(wglu), full(wout), full(gffn), full(wr2)],
        out_specs=[
            pl.BlockSpec((tm, d), lambda i: (i, 0)),
            pl.BlockSpec((tm, d), lambda i: (i, 0)),
            pl.BlockSpec((tm, ne), lambda i: (i, 0)),
        ],
        out_shape=[
            jax.ShapeDtypeStruct((t, d), F32),
            jax.ShapeDtypeStruct((t, d), BF16),
            jax.ShapeDtypeStruct((t, ne), F32),
        ],
        compiler_params=_cparams("parallel"),
        name="mixout",
    )(fre, ys, gate, x2, wf, wglu, wout, gffn, wr2)


def _topk_kernel(aff_ref, pos_ref, st_ref, *, cap, blk):
    v = aff_ref[...]
    r, s = v.shape
    capf = float(cap)

    def bit_step(i, t):
        cand = t | (jnp.int32(1) << (30 - i))
        cnt = jnp.sum(jnp.where(v >= pltpu.bitcast(cand, F32), 1.0, 0.0), axis=1, keepdims=True)
        return jnp.where(cnt >= capf, cand, t)

    thr = lax.fori_loop(0, 31, bit_step, jnp.zeros((r, 1), I32))
    gt = jnp.where(v >= pltpu.bitcast(thr + 1, F32), 1.0, 0.0)
    eq = jnp.where(v >= pltpu.bitcast(thr, F32), 1.0, 0.0) - gt
    need = capf - jnp.sum(gt, axis=1, keepdims=True)
    ii = lax.broadcasted_iota(I32, (blk, blk), 0)
    jj = lax.broadcasted_iota(I32, (blk, blk), 1)
    tri = jnp.where(ii < jj, 1.0, 0.0).astype(BF16)
    run_eq = jnp.zeros((r, 1), F32)
    run = jnp.zeros((r, 1), F32)
    for k in range(s // blk):
        sl = slice(k * blk, (k + 1) * blk)
        eqb, gtb = eq[:, sl], gt[:, sl]
        rank_eq = _dot(eqb.astype(BF16), tri) + run_eq
        run_eq = run_eq + jnp.sum(eqb, axis=1, keepdims=True)
        mask = gtb + eqb * jnp.where(rank_eq < need, 1.0, 0.0)
        pos = _dot(mask.astype(BF16), tri) + run
        st_ref[:, k:k + 1] = run.astype(I32)
        run = run + jnp.sum(mask, axis=1, keepdims=True)
        pos_ref[:, sl] = jnp.where(mask > 0.0, pos, -1.0).astype(I32)
    st_ref[:, s // blk:s // blk + 1] = run.astype(I32)


def _topk(aff_rows, cap, blk=TOPK_BLOCK):
    r, s = aff_rows.shape
    nblk = s // blk
    return pl.pallas_call(
        functools.partial(_topk_kernel, cap=cap, blk=blk),
        grid=(1,),
        in_specs=[pl.BlockSpec((r, s), lambda i: (0, 0))],
        out_specs=[pl.BlockSpec((r, s), lambda i: (0, 0)), pl.BlockSpec((r, nblk + 1), lambda i: (0, 0))],
        out_shape=[jax.ShapeDtypeStruct((r, s), I32), jax.ShapeDtypeStruct((r, nblk + 1), I32)],
        compiler_params=_cparams("arbitrary"),
        name="topk",
    )(aff_rows)


def _floor_rows(x):
    return (x // PACKED_ROWS) * PACKED_ROWS


def _num_passes(starts, ends, win):
    need = ends[0] - _floor_rows(starts[0])
    for a, b in zip(starts[1:], ends[1:]):
        need = jnp.maximum(need, b - _floor_rows(a))
    return (need + win - 1) // win


def _pass_window(start, p, win, cap):
    first = _floor_rows(start) + p * win
    return first, pl.multiple_of(jnp.minimum(first, cap - win), PACKED_ROWS)


def _gather_kernel(st_ref, h_ref, pos_ref, aff_ref, x_ref, v_ref, acc_ref, vacc_ref, *, nt, tt, win, cap, ne, ng):
    row0 = pl.program_id(0) * ne + pl.program_id(1) * ng
    acc_ref[...] = jnp.zeros_like(acc_ref)
    vacc_ref[...] = jnp.zeros_like(vacc_ref)
    riota = lax.broadcasted_iota(I32, (win, tt), 0)

    def bounds(t):
        return ([st_ref[(row0 + g) * (nt + 1) + t] for g in range(ng)],
                [st_ref[(row0 + g) * (nt + 1) + t + 1] for g in range(ng)])

    def one_pass(t, p, starts):
        hrows = h_ref[0, pl.ds(pl.multiple_of(t * tt, tt), tt), :]
        hots, wins = [], []
        for g in range(ng):
            first, ws = _pass_window(starts[g], p, win, cap)
            pos = pos_ref[g, pl.ds(t, 1), :]
            hots.append((pos - ws == riota) & (pos >= first))
            wins.append(ws)
        onehot = jnp.concatenate([jnp.where(h, 1.0, 0.0).astype(BF16) for h in hots], axis=0)
        rows = _dot(onehot, hrows)
        for g in range(ng):
            acc_ref[g, pl.ds(wins[g], win), :] += rows[g * win:(g + 1) * win]
            vals = jnp.sum(jnp.where(hots[g], aff_ref[g, pl.ds(t, 1), :], 0.0), axis=1, keepdims=True)
            vacc_ref[g, pl.ds(wins[g], win), :] += vals

    def first_pass(t, carry):
        one_pass(t, 0, bounds(t)[0])
        return carry

    def more_passes(t, carry):
        starts, ends = bounds(t)

        def body(p, c):
            one_pass(t, p, starts)
            return c

        lax.fori_loop(1, _num_passes(starts, ends, win), body, 0)
        return carry

    lax.fori_loop(0, nt, first_pass, 0, unroll=GATHER_UNROLL)
    lax.fori_loop(0, nt, more_passes, 0)
    x_ref[0] = acc_ref[...].astype(BF16)
    v_ref[0] = vacc_ref[...]


def _gather(starts, h3, pos3, aff3t, cap, ne):
    b, s, d = h3.shape
    tt, win, ng = GATHER_TILE, GATHER_WINDOW, GATHER_EXPERTS
    nt = s // tt
    grid_spec = pltpu.PrefetchScalarGridSpec(
        num_scalar_prefetch=1,
        grid=(b, ne // ng),
        in_specs=[
            pl.BlockSpec((1, s, d), lambda i, e, st: (i, 0, 0), pipeline_mode=pl.Buffered(1)),
            pl.BlockSpec((ng, nt, tt), lambda i, e, st: (i * (ne // ng) + e, 0, 0)),
            pl.BlockSpec((ng, nt, tt), lambda i, e, st: (i * (ne // ng) + e, 0, 0)),
        ],
        out_specs=[
            pl.BlockSpec((1, ng, cap, d), lambda i, e, st: (i, e, 0, 0)),
            pl.BlockSpec((1, ng, cap, 1), lambda i, e, st: (i, e, 0, 0)),
        ],
        scratch_shapes=[pltpu.VMEM((ng, cap, d), F32), pltpu.VMEM((ng, cap, 1), F32)],
    )
    return pl.pallas_call(
        functools.partial(_gather_kernel, nt=nt, tt=tt, win=win, cap=cap, ne=ne, ng=ng),
        grid_spec=grid_spec,
        out_shape=[jax.ShapeDtypeStruct((b, ne, cap, d), BF16), jax.ShapeDtypeStruct((b, ne, cap, 1), F32)],
        compiler_params=_cparams("arbitrary", "arbitrary"),
        name="gather",
    )(starts, h3, pos3, aff3t)


def _ffn_kernel(x_ref, v_ref, wg_ref, wu_ref, wd_ref, y_ref, acc_ref, *, nf):
    f = pl.program_id(2)
    nb, _, cap, d = x_ref.shape

    @pl.when(f == 0)
    def _():
        acc_ref[...] = jnp.zeros_like(acc_ref)

    x = x_ref[...].reshape(nb * cap, d)
    g = _dot(x, wg_ref[0].astype(BF16))
    u = _dot(x, wu_ref[0].astype(BF16))
    hid = (g * jax.nn.sigmoid(g) * u).astype(BF16)
    acc_ref[...] += _dot(hid, wd_ref[0].astype(BF16))

    @pl.when(f == nf - 1)
    def _():
        y = acc_ref[...] * v_ref[...].reshape(nb * cap, 1)
        y_ref[...] = y.astype(BF16).reshape(nb, 1, cap, d)


def _ffn(xg, vals, wg, wu, wd, nb, tf):
    b, ne, cap, d = xg.shape
    dexp = wg.shape[2]
    nf = dexp // tf
    return pl.pallas_call(
        functools.partial(_ffn_kernel, nf=nf),
        grid=(ne, b // nb, nf),
        in_specs=[
            pl.BlockSpec((nb, 1, cap, d), lambda e, i, f: (i, e, 0, 0)),
            pl.BlockSpec((nb, 1, cap, 1), lambda e, i, f: (i, e, 0, 0)),
            pl.BlockSpec((1, d, tf), lambda e, i, f: (e, 0, f)),
            pl.BlockSpec((1, d, tf), lambda e, i, f: (e, 0, f)),
            pl.BlockSpec((1, tf, d), lambda e, i, f: (e, f, 0)),
        ],
        out_specs=pl.BlockSpec((nb, 1, cap, d), lambda e, i, f: (i, e, 0, 0)),
        out_shape=jax.ShapeDtypeStruct(xg.shape, BF16),
        scratch_shapes=[pltpu.VMEM((nb * cap, d), F32)],
        compiler_params=_cparams("parallel", "parallel", "arbitrary"),
        name="ffn",
    )(xg, vals, wg, wu, wd)


def _combine_kernel(st_ref, y_ref, pos_ref, x1_ref, p_ref, wpp_ref, wpg_ref, gple_ref, gout_ref,
                    o_ref, *, nt, tt, win, cap, ne, final):
    row0 = pl.program_id(0) * ne
    t = pl.program_id(1)
    starts = [st_ref[(row0 + e) * (nt + 1) + t] for e in range(ne)]
    ends = [st_ref[(row0 + e) * (nt + 1) + t + 1] for e in range(ne)]
    k = ne * win
    lane = lax.broadcasted_iota(I32, (1, k), 1)
    lane_e = jnp.zeros((1, k), I32)
    for e in range(1, ne):
        lane_e = lane_e + jnp.where(lane >= e * win, 1, 0)
    lane_r = (lane - lane_e * win).astype(F32)
    part = lax.broadcasted_iota(I32, (2 * ne, k), 0)
    spread = jnp.where(part == lane_e, float(COMBINE_SPLIT), jnp.where(part == lane_e + ne, 1.0, 0.0)).astype(BF16)
    rank = _dot(pos_ref[0], spread)

    def pass_body(p, acc):
        firsts = jnp.zeros((1, k), F32)
        offs = jnp.zeros((1, k), F32)
        wins = []
        for e in range(ne):
            first, ws = _pass_window(starts[e], p, win, cap)
            firsts = jnp.where(lane_e == e, first.astype(F32), firsts)
            offs = jnp.where(lane_e == e, ws.astype(F32), offs)
            wins.append(y_ref[0, e, pl.ds(ws, win), :])
        onehot = jnp.where((rank - offs == lane_r) & (rank >= firsts), 1.0, 0.0).astype(BF16)
        return acc + _dot(onehot, jnp.concatenate(wins, axis=0))

    acc = lax.fori_loop(0, _num_passes(starts, ends, win), pass_body, x1_ref[0])
    emb = _dot(p_ref[0].astype(BF16), wpp_ref[...])
    gate = jax.nn.sigmoid(_dot(_rms(acc, gple_ref[...]).astype(BF16), wpg_ref[...]))
    x3 = acc + gate * emb
    o_ref[0] = _rms(x3, gout_ref[...]) if final else x3


def _combine(starts, yg, post, x13, p3, wpp, wpg, gple, gout, cap, final):
    b, ne, _, d = yg.shape
    s = x13.shape[1]
    tt, win = COMBINE_TILE, COMBINE_WINDOW
    nt = s // tt
    tile = lambda a: pl.BlockSpec((1, tt, a.shape[2]), lambda i, t, st: (i, t, 0))
    full = lambda a: pl.BlockSpec(a.shape, lambda i, t, st: (0,) * a.ndim)
    grid_spec = pltpu.PrefetchScalarGridSpec(
        num_scalar_prefetch=1,
        grid=(b, nt),
        in_specs=[
            pl.BlockSpec((1, ne, cap, d), lambda i, t, st: (i, 0, 0, 0), pipeline_mode=pl.Buffered(1)),
            tile(post), tile(x13), tile(p3), full(wpp), full(wpg), full(gple), full(gout),
        ],
        out_specs=pl.BlockSpec((1, tt, d), lambda i, t, st: (i, t, 0)),
    )
    return pl.pallas_call(
        functools.partial(_combine_kernel, nt=nt, tt=tt, win=win, cap=cap, ne=ne, final=final),
        grid_spec=grid_spec,
        out_shape=jax.ShapeDtypeStruct(x13.shape, F32),
        compiler_params=_cparams("arbitrary", "arbitrary"),
        name="combine",
    )(starts, yg, post, x13, p3, wpp, wpg, gple, gout)


def kernel(x, p, g_mix, w_in, w_fourier, ssm_a_re, ssm_a_im, ssm_log_dt, ssm_b_re, ssm_b_im, ssm_c_re, ssm_c_im, ssm_d, w_glu, w_out, g_ffn, w_router, w_exp_gate, w_exp_up, w_exp_down, g_ple, w_ple_gate, w_ple_proj, g_final):
    b, s, d = x.shape
    depth = p.shape[0]
    df = w_fourier.shape[1]
    ds = w_glu.shape[1]
    ne = w_router.shape[2]
    dexp = w_exp_gate.shape[3]
    groups, gi = ssm_b_re.shape[2], ssm_b_re.shape[4]
    L = SSM_CHUNK
    n1 = s // DFT_N2
    nc = s // L
    cap = EC_CAPACITY * s // ne
    assert s % (DFT_N2 * SUBLANES) == 0 and s % GATHER_TILE == 0 and cap % PACKED_ROWS == 0
    assert nc % SSM_TILE_CHUNKS == 0 and gi == PACKED_ROWS and cap // COMBINE_SPLIT < 256
    assert 2 * ssm_a_re.shape[3] == LANES and ds % LANES == 0
    assert cap >= COMBINE_WINDOW and ne % GATHER_EXPERTS == 0 and GATHER_TILE == COMBINE_TILE
    assert ds == groups * gi and df % FOURIER_GROUPS == 0

    bd = _channel_dft_table(df)
    ck, sk = _dft1_tables(n1)
    tr, ti = _dft3_tables(n1)
    tf = 256 if dexp % 256 == 0 else dexp
    nb_ffn = 2 if b % 2 == 0 else 1

    xcur = x.reshape(b * s, d)
    for i in range(depth):
        final = i == depth - 1
        row = lambda v: v.astype(F32).reshape(1, -1)
        pq, us, gate = _inproj(xcur, row(g_mix[i]), w_in[i].astype(BF16), bd, df, ds)
        a4 = _dft1(pq.reshape(b, n1, DFT_N2, 2 * df), ck, sk)
        fre = _dft3(a4, tr, ti).reshape(b * s, df)

        tabs = _ssm_tables(ssm_a_re[i], ssm_a_im[i], ssm_log_dt[i], ssm_b_re[i], ssm_b_im[i],
                           ssm_c_re[i], ssm_c_im[i], ssm_d[i])
        ys = _to_tokens(_ssm(_to_groups(us, L, groups, gi), *tabs, nc=nc, nb=b), L, groups, gi)

        wr = w_router[i].astype(F32)
        wr_hi = wr.astype(BF16)
        wr_lo = (wr - wr_hi.astype(F32)).astype(BF16)
        wr2 = jnp.concatenate([jnp.concatenate([wr_hi, wr_lo], axis=1),
                               jnp.concatenate([wr_hi, jnp.zeros_like(wr_lo)], axis=1)], axis=0)
        x1, h2, aff = _mixout(fre, ys, gate, xcur, w_fourier[i].astype(BF16), w_glu[i].astype(BF16),
                              w_out[i].astype(BF16), row(g_ffn[i]), wr2, ne)

        aff3 = aff.reshape(b, s, ne)
        aff_rows = jnp.transpose(aff3, (0, 2, 1)).reshape(b * ne, s)
        posm, st_blk = _topk(aff_rows, cap)
        starts = st_blk[:, ::GATHER_TILE // TOPK_BLOCK].reshape(-1)
        tiles = (b * ne, s // GATHER_TILE, GATHER_TILE)
        xg, vals = _gather(starts, h2.reshape(b, s, d), posm.reshape(tiles), aff_rows.reshape(tiles), cap, ne)
        yg = _ffn(xg, vals, w_exp_gate[i], w_exp_up[i], w_exp_down[i], nb_ffn, tf)
        post = jnp.transpose(posm.reshape(b, ne, s), (0, 2, 1))
        post = jnp.concatenate([jnp.where(post < 0, 0, post // COMBINE_SPLIT),
                                jnp.where(post < 0, -1, post % COMBINE_SPLIT)], axis=-1).astype(BF16)
        xnext = _combine(starts, yg, post, x1.reshape(b, s, d), p[i],
                         w_ple_proj[i].astype(BF16), w_ple_gate[i].astype(BF16), row(g_ple[i]),
                         row(g_final) if final else row(g_ple[i]), cap, final)
        xcur = xnext.reshape(b * s, d)
    return xcur.reshape(b, s, d)
```

```python
import functools
import math

import jax
import jax.numpy as jnp
from jax import lax
from jax.experimental import pallas as pl
from jax.experimental.pallas import tpu as pltpu

F32 = jnp.float32
BF16 = jnp.bfloat16
I32 = jnp.int32

RMS_EPS = 1e-6
FOURIER_GROUPS = 4
EC_CAPACITY = 2
DFT_N2 = 128
SUBLANES = 8
LANES = 128
PACKED_ROWS = 16
SSM_CHUNK = 32
SSM_TILE_CHUNKS = 128
TOPK_BLOCK = 128
GATHER_TILE = 256
GATHER_WINDOW = 48
GATHER_EXPERTS = 2
GATHER_UNROLL = 4
COMBINE_TILE = 256
COMBINE_WINDOW = 48
COMBINE_SPLIT = 32
VMEM_LIMIT = 56 * 1024 * 1024


def _cparams(*sem):
    return pltpu.CompilerParams(dimension_semantics=sem, vmem_limit_bytes=VMEM_LIMIT)


def _rms(x, g):
    return x * lax.rsqrt(jnp.mean(x * x, axis=-1, keepdims=True) + RMS_EPS) * g


def _dot(a, b):
    return jnp.dot(a, b, preferred_element_type=F32)


def _inproj_kernel(x_ref, g_ref, w_ref, bd_ref, pq_ref, us_ref, gate_ref, *, df, ds):
    h = _rms(x_ref[...], g_ref[...]).astype(BF16)
    z = _dot(h, w_ref[...])
    pq_ref[...] = _dot(z[:, :df].astype(BF16), bd_ref[...])
    for q in range(ds // LANES):
        us_ref[q] = z[:, df + q * LANES:df + (q + 1) * LANES]
    gate_ref[...] = jax.nn.sigmoid(z[:, df + ds:]).astype(BF16)


def _inproj(x2, g, w_in, bd, df, ds, tm=512):
    t, d = x2.shape
    dg = w_in.shape[1] - df - ds
    return pl.pallas_call(
        functools.partial(_inproj_kernel, df=df, ds=ds),
        grid=(t // tm,),
        in_specs=[
            pl.BlockSpec((tm, d), lambda i: (i, 0)),
            pl.BlockSpec((1, d), lambda i: (0, 0)),
            pl.BlockSpec(w_in.shape, lambda i: (0, 0)),
            pl.BlockSpec(bd.shape, lambda i: (0, 0)),
        ],
        out_specs=[
            pl.BlockSpec((tm, 2 * df), lambda i: (i, 0)),
            pl.BlockSpec((ds // LANES, tm, LANES), lambda i: (0, i, 0)),
            pl.BlockSpec((tm, dg), lambda i: (i, 0)),
        ],
        out_shape=[
            jax.ShapeDtypeStruct((t, 2 * df), F32),
            jax.ShapeDtypeStruct((ds // LANES, t, LANES), F32),
            jax.ShapeDtypeStruct((t, dg), BF16),
        ],
        compiler_params=_cparams("parallel"),
        name="inproj",
    )(x2, g, w_in, bd)


def _channel_dft_table(df):
    c = df // FOURIER_GROUPS
    k = jnp.arange(c, dtype=I32)
    ang = (2.0 * math.pi / c) * ((k[:, None] * k[None, :]) % c).astype(F32)
    eye = jnp.eye(FOURIER_GROUPS, dtype=F32)
    scale = 1.0 / math.sqrt(c)
    re = jnp.kron(eye, jnp.cos(ang)) * scale
    im = -jnp.kron(eye, jnp.sin(ang)) * scale
    return jnp.concatenate([re, im], axis=1).astype(BF16)


def _dft1_tables(n1):
    k = jnp.arange(n1, dtype=I32)
    ang = (2.0 * math.pi / n1) * ((k[:, None] * k[None, :]) % n1).astype(F32)
    eye = jnp.eye(SUBLANES, dtype=F32)
    scale = 1.0 / math.sqrt(n1)
    return (jnp.kron(jnp.cos(ang), eye) * scale).astype(BF16), (jnp.kron(jnp.sin(ang), eye) * scale).astype(BF16)


def _dft3_tables(n1):
    s = n1 * DFT_N2
    nb = n1 // SUBLANES
    k1 = jnp.arange(n1, dtype=I32)[:, None, None]
    k2 = jnp.arange(DFT_N2, dtype=I32)[None, :, None]
    n2 = jnp.arange(DFT_N2, dtype=I32)[None, None, :]
    ang = (2.0 * math.pi / s) * ((n2 * (k1 + n1 * k2)) % s).astype(F32)
    eye = jnp.eye(SUBLANES, dtype=BF16)[None, None, :, :, None]
    rows = DFT_N2 * SUBLANES

    def expand(t):
        t = (t * (1.0 / math.sqrt(DFT_N2))).astype(BF16).reshape(nb, SUBLANES, DFT_N2, DFT_N2)
        t = jnp.transpose(t, (0, 2, 1, 3))[:, :, :, None, :]
        return (t * eye).reshape(nb, rows, rows)

    return expand(jnp.cos(ang)), expand(jnp.sin(ang))


def _dft1_kernel(z_ref, ck_ref, sk_ref, a_ref):
    n1, c2 = z_ref.shape[1], z_ref.shape[3]
    c = c2 // 2
    z = z_ref[0].reshape(n1 * SUBLANES, c2).astype(BF16)
    cz = _dot(ck_ref[...], z)
    sz = _dot(sk_ref[...], z)
    a_ref[0, :, :, :c] = (cz[:, :c] + sz[:, c:]).reshape(n1, SUBLANES, c)
    a_ref[0, :, :, c:] = (cz[:, c:] - sz[:, :c]).reshape(n1, SUBLANES, c)


def _dft1(pq4, ck, sk):
    b, n1, n2, c2 = pq4.shape
    return pl.pallas_call(
        _dft1_kernel,
        grid=(b, n2 // SUBLANES),
        in_specs=[
            pl.BlockSpec((1, n1, SUBLANES, c2), lambda i, j: (i, 0, j, 0)),
            pl.BlockSpec(ck.shape, lambda i, j: (0, 0)),
            pl.BlockSpec(sk.shape, lambda i, j: (0, 0)),
        ],
        out_specs=pl.BlockSpec((1, n1, SUBLANES, c2), lambda i, j: (i, 0, j, 0)),
        out_shape=jax.ShapeDtypeStruct(pq4.shape, F32),
        compiler_params=_cparams("parallel", "parallel"),
        name="dft1",
    )(pq4, ck, sk)


def _dft3_kernel(a_ref, tr_ref, ti_ref, o_ref):
    c2 = a_ref.shape[3]
    c = c2 // 2
    a = a_ref[0].reshape(SUBLANES * DFT_N2, c2).astype(BF16)
    out = _dot(tr_ref[0], a[:, :c]) + _dot(ti_ref[0], a[:, c:])
    o_ref[0] = out.reshape(DFT_N2, SUBLANES, c)


def _dft3(a4, tr, ti):
    b, n1, n2, c2 = a4.shape
    c = c2 // 2
    rows = DFT_N2 * SUBLANES
    return pl.pallas_call(
        _dft3_kernel,
        grid=(n1 // SUBLANES, b),
        in_specs=[
            pl.BlockSpec((1, SUBLANES, n2, c2), lambda k, i: (i, k, 0, 0)),
            pl.BlockSpec((1, rows, rows), lambda k, i: (k, 0, 0)),
            pl.BlockSpec((1, rows, rows), lambda k, i: (k, 0, 0)),
        ],
        out_specs=pl.BlockSpec((1, DFT_N2, SUBLANES, c), lambda k, i: (i, 0, k, 0)),
        out_shape=jax.ShapeDtypeStruct((b, DFT_N2, n1, c), F32),
        compiler_params=_cparams("parallel", "parallel"),
        name="dft3",
    )(a4, tr, ti)


def _ssm_tables(a_re, a_im, log_dt, b_re, b_im, c_re, c_im, d):
    hp = lax.Precision.HIGHEST
    L = SSM_CHUNK
    nd, g, n = a_re.shape
    gi = b_re.shape[-1]
    dt = jnp.exp(log_dt.astype(F32))[..., None]
    ar, ai = a_re.astype(F32), a_im.astype(F32)
    tau = jnp.arange(L + 1, dtype=F32)[:, None, None, None]
    mag = jnp.exp(tau * (ar * dt)[None])
    ang = tau * (ai * dt)[None]
    pr, pi = mag * jnp.cos(ang), mag * jnp.sin(ang)
    nr, ni = pr[1] - 1.0, pi[1]
    den = ar * ar + ai * ai
    qr, qi = (nr * ar + ni * ai) / den, (ni * ar - nr * ai) / den
    br, bi = b_re.astype(F32), b_im.astype(F32)
    bbr = qr[..., None] * br - qi[..., None] * bi
    bbi = qr[..., None] * bi + qi[..., None] * br
    cr, ci = c_re.astype(F32), c_im.astype(F32)

    pbr = pr[:L, ..., None] * bbr[None] - pi[:L, ..., None] * bbi[None]
    pbi = pr[:L, ..., None] * bbi[None] + pi[:L, ..., None] * bbr[None]
    kk = (jnp.einsum('dgon,tdgni->dgtoi', cr, pbr, precision=hp)
          - jnp.einsum('dgon,tdgni->dgtoi', ci, pbi, precision=hp))
    kfull = jnp.concatenate([kk[1][:, :0:-1], kk[0][:, :1] + kk[1][:, :1], kk[0][:, 1:]], axis=1)
    j = jnp.arange(L, dtype=I32)
    tap = jnp.arange(2 * L - 1, dtype=I32)[:, None, None]
    place = (j[None, None, :] - j[None, :, None] + (L - 1) == tap).astype(F32)
    t_tab = jnp.einsum('gtoi,tab->gaibo', kfull, place, precision=hp).reshape(g, L * gi, L * gi)

    ef_r, ef_i = pbr[::-1, 0], pbi[::-1, 0]
    eb_r, eb_i = pbr[:, 1], pbi[:, 1]
    e4 = jnp.stack([ef_r, eb_r, ef_i, eb_i], axis=0)
    e_tab = jnp.transpose(e4, (2, 1, 4, 0, 3)).reshape(g, L * gi, 4 * n)

    pf_r, pf_i = pr[1:, 0], pi[1:, 0]
    pb_r, pb_i = pr[1:, 1][::-1], pi[1:, 1][::-1]

    def readout(cr_, ci_, pr_, pi_):
        mr = cr_[None] * pr_[:, :, None, :] - ci_[None] * pi_[:, :, None, :]
        mi = cr_[None] * pi_[:, :, None, :] + ci_[None] * pr_[:, :, None, :]
        return mr, -mi

    ffr, ffi = readout(cr[0], ci[0], pf_r, pf_i)
    fbr, fbi = readout(cr[1], ci[1], pb_r, pb_i)
    f4 = jnp.stack([ffr, fbr, ffi, fbi], axis=0)
    f_tab = jnp.transpose(f4, (2, 0, 4, 1, 3)).reshape(g, 4 * n, L * gi)

    al = jnp.stack([jnp.concatenate([pr[L, 0], pr[L, 1]], axis=-1),
                    jnp.concatenate([pi[L, 0], pi[L, 1]], axis=-1)], axis=1)
    dv = jnp.tile(d.astype(F32).reshape(g, 1, gi), (1, L, 1)).reshape(g, L * gi, 1)
    left = lambda tab: jnp.swapaxes(tab, 1, 2).astype(BF16)
    return left(t_tab), left(e_tab), left(f_tab), al, dv


def _to_groups_kernel(u_ref, a_ref, *, L, groups, gi):
    ncl = a_ref.shape[2]
    gq = LANES // gi
    for q in range(u_ref.shape[0]):
        for j in range(L):
            zt = u_ref[q, pl.ds(j, ncl, stride=L), :].T
            a_ref[q * gq:(q + 1) * gq, pl.ds(j * gi, gi), :] = zt.reshape(gq, gi, ncl).astype(BF16)


def _to_groups(us, L, groups, gi):
    nq, t, _ = us.shape
    ncl = SSM_TILE_CHUNKS
    return pl.pallas_call(
        functools.partial(_to_groups_kernel, L=L, groups=groups, gi=gi),
        grid=(t // (ncl * L),),
        in_specs=[pl.BlockSpec((nq, ncl * L, LANES), lambda i: (0, i, 0))],
        out_specs=pl.BlockSpec((groups, L * gi, ncl), lambda i: (0, 0, i)),
        out_shape=jax.ShapeDtypeStruct((groups, L * gi, t // L), BF16),
        compiler_params=_cparams("parallel"),
        name="to_groups",
    )(us)


def _to_tokens_kernel(y_ref, o_ref, *, L, groups, gi):
    ncl = y_ref.shape[2]
    gq = LANES // gi
    for q in range(o_ref.shape[0]):
        for j in range(L):
            yj = y_ref[q * gq:(q + 1) * gq, pl.ds(j * gi, gi), :].astype(F32).reshape(LANES, ncl)
            o_ref[q, pl.ds(j, ncl, stride=L), :] = yj.T


def _to_tokens(yt, L, groups, gi):
    _, k, nchunks = yt.shape
    ncl = SSM_TILE_CHUNKS
    return pl.pallas_call(
        functools.partial(_to_tokens_kernel, L=L, groups=groups, gi=gi),
        grid=(nchunks // ncl,),
        in_specs=[pl.BlockSpec((groups, k, ncl), lambda i: (0, 0, i))],
        out_specs=pl.BlockSpec((groups * gi // LANES, ncl * L, LANES), lambda i: (0, i, 0)),
        out_shape=jax.ShapeDtypeStruct((groups * gi // LANES, nchunks * L, LANES), F32),
        compiler_params=_cparams("parallel"),
        name="to_tokens",
    )(yt)


def _ssm_kernel(a_ref, t_ref, e_ref, f_ref, al_ref, dv_ref, y_ref, s_scr, h_scr, *, nc, nb, rows, n):
    a = a_ref[0]
    y1 = _dot(t_ref[0], a) + dv_ref[0] * a.astype(F32)
    st = _dot(e_ref[0], a)
    n2 = 2 * n
    s_scr[...] = jnp.zeros_like(s_scr)
    for b in range(nb):
        sb_t = st[:, b * nc:(b + 1) * nc].T
        for q in range(2):
            s_scr[q, pl.ds(b, nc, stride=rows), :] = sb_t[:, q * n2:(q + 1) * n2]
    ar = al_ref[0, 0:1, :]
    ai = al_ref[0, 1:2, :]
    is_fwd = lax.broadcasted_iota(I32, (rows, n2), 1) < n

    def step(i, carry):
        hr, hi = carry
        rf = pl.ds(pl.multiple_of(i * rows, rows), rows)
        rb = pl.ds(pl.multiple_of((nc - 1 - i) * rows, rows), rows)
        h_scr[0, rf, 0:n] = hr[:, 0:n]
        h_scr[1, rf, 0:n] = hi[:, 0:n]
        h_scr[0, rb, n:n2] = hr[:, n:n2]
        h_scr[1, rb, n:n2] = hi[:, n:n2]
        sr = jnp.where(is_fwd, s_scr[0, rf, :], s_scr[0, rb, :])
        si = jnp.where(is_fwd, s_scr[1, rf, :], s_scr[1, rb, :])
        return ar * hr - ai * hi + sr, ar * hi + ai * hr + si

    zero = jnp.zeros((rows, n2), F32)
    lax.fori_loop(0, nc, step, (zero, zero))
    ht = jnp.concatenate(
        [jnp.concatenate([h_scr[q, pl.ds(b, nc, stride=rows), :].T for q in range(2)], axis=0) for b in range(nb)],
        axis=1)
    y = y1 + _dot(f_ref[0], ht.astype(BF16))
    y_ref[0] = jax.nn.gelu(y, approximate=True).astype(BF16)


def _ssm(ag, t_tab, e_tab, f_tab, al, dv, nc, nb):
    g, k, m = ag.shape
    n4 = e_tab.shape[1]
    rows = -(-nb // SUBLANES) * SUBLANES
    return pl.pallas_call(
        functools.partial(_ssm_kernel, nc=nc, nb=nb, rows=rows, n=n4 // 4),
        grid=(g,),
        in_specs=[
            pl.BlockSpec((1, k, m), lambda i: (i, 0, 0)),
            pl.BlockSpec((1, k, k), lambda i: (i, 0, 0)),
            pl.BlockSpec((1, n4, k), lambda i: (i, 0, 0)),
            pl.BlockSpec((1, k, n4), lambda i: (i, 0, 0)),
            pl.BlockSpec((1, 2, n4 // 2), lambda i: (i, 0, 0)),
            pl.BlockSpec((1, k, 1), lambda i: (i, 0, 0)),
        ],
        out_specs=pl.BlockSpec((1, k, m), lambda i: (i, 0, 0)),
        out_shape=jax.ShapeDtypeStruct((g, k, m), BF16),
        scratch_shapes=[pltpu.VMEM((2, nc * rows, n4 // 2), F32), pltpu.VMEM((2, nc * rows, n4 // 2), F32)],
        compiler_params=_cparams("parallel"),
        name="ssm",
    )(ag, t_tab, e_tab, f_tab, al, dv)


def _mixout_kernel(fre_ref, ys_ref, gate_ref, x_ref, wf_ref, wglu_ref, wout_ref, gffn_ref, wr_ref,
                   x1_ref, h2_ref, aff_ref, *, d, ne):
    y_f = _dot(fre_ref[...].astype(BF16), wf_ref[...])
    ys = jnp.concatenate([ys_ref[q] for q in range(ys_ref.shape[0])], axis=1)
    vg = _dot(ys.astype(BF16), wglu_ref[...])
    y_s = vg[:, :d] * jax.nn.sigmoid(vg[:, d:])
    gate = gate_ref[...].astype(F32)
    m = gate[:, :d] * y_f + gate[:, d:] * y_s
    x1 = x_ref[...] + _dot(m.astype(BF16), wout_ref[...])
    x1_ref[...] = x1
    h2 = _rms(x1, gffn_ref[...])
    h2_ref[...] = h2.astype(BF16)
    hi = h2.astype(BF16)
    lo = (h2 - hi.astype(F32)).astype(BF16)
    r = _dot(jnp.concatenate([hi, lo], axis=1), wr_ref[...])
    logits = r[:, :LANES] + r[:, LANES:]
    logits = jnp.where(lax.broadcasted_iota(I32, logits.shape, 1) < ne, logits, -1e30)
    logits = logits - jnp.max(logits, axis=-1, keepdims=True)
    p = jnp.exp(logits)
    aff = p / jnp.sum(p, axis=-1, keepdims=True)
    aff_ref[...] = aff.T[:ne, :]


def _mixout(fre, ys, gate, x2, wf, wglu, wout, gffn, wr2, ne, tm=512):
    t, d = x2.shape
    full = lambda a: pl.BlockSpec(a.shape, lambda i: (0,) * a.ndim)
    row = lambda a: pl.BlockSpec((tm, a.shape[1]), lambda i: (i, 0))
    return pl.pallas_call(
        functools.partial(_mixout_kernel, d=d, ne=ne),
        grid=(t // tm,),
        in_specs=[row(fre), pl.BlockSpec((ys.shape[0], tm, LANES), lambda i: (0, i, 0)), row(gate), row(x2),
                  full(wf), full(wglu), full(wout), full(gffn), full(wr2)],
        out_specs=[
            pl.BlockSpec((tm, d), lambda i: (i, 0)),
            pl.BlockSpec((tm, d), lambda i: (i, 0)),
            pl.BlockSpec((ne, tm), lambda i: (0, i)),
        ],
        out_shape=[
            jax.ShapeDtypeStruct((t, d), F32),
            jax.ShapeDtypeStruct((t, d), BF16),
            jax.ShapeDtypeStruct((ne, t), F32),
        ],
        compiler_params=_cparams("parallel"),
        name="mixout",
    )(fre, ys, gate, x2, wf, wglu, wout, gffn, wr2)


def _topk_kernel(aff_ref, pos_ref, st_ref, *, cap, blk):
    v = aff_ref[...]
    r, s = v.shape
    capf = float(cap)

    def bit_step(i, t):
        cand = t | (jnp.int32(1) << (30 - i))
        cnt = jnp.sum(jnp.where(v >= pltpu.bitcast(cand, F32), 1.0, 0.0), axis=1, keepdims=True)
        return jnp.where(cnt >= capf, cand, t)

    thr = lax.fori_loop(0, 31, bit_step, jnp.zeros((r, 1), I32))
    gt = jnp.where(v >= pltpu.bitcast(thr + 1, F32), 1.0, 0.0)
    eq = jnp.where(v >= pltpu.bitcast(thr, F32), 1.0, 0.0) - gt
    need = capf - jnp.sum(gt, axis=1, keepdims=True)
    ii = lax.broadcasted_iota(I32, (blk, blk), 0)
    jj = lax.broadcasted_iota(I32, (blk, blk), 1)
    tri = jnp.where(ii < jj, 1.0, 0.0).astype(BF16)
    run_eq = jnp.zeros((r, 1), F32)
    run = jnp.zeros((r, 1), F32)
    for k in range(s // blk):
        sl = slice(k * blk, (k + 1) * blk)
        eqb, gtb = eq[:, sl], gt[:, sl]
        rank_eq = _dot(eqb.astype(BF16), tri) + run_eq
        run_eq = run_eq + jnp.sum(eqb, axis=1, keepdims=True)
        mask = gtb + eqb * jnp.where(rank_eq < need, 1.0, 0.0)
        pos = _dot(mask.astype(BF16), tri) + run
        st_ref[:, k:k + 1] = run.astype(I32)
        run = run + jnp.sum(mask, axis=1, keepdims=True)
        pos_ref[:, sl] = jnp.where(mask > 0.0, pos, -1.0).astype(I32)
    st_ref[:, s // blk:s // blk + 1] = run.astype(I32)


def _topk(aff_rows, cap, blk=TOPK_BLOCK):
    r, s = aff_rows.shape
    nblk = s // blk
    return pl.pallas_call(
        functools.partial(_topk_kernel, cap=cap, blk=blk),
        grid=(1,),
        in_specs=[pl.BlockSpec((r, s), lambda i: (0, 0))],
        out_specs=[pl.BlockSpec((r, s), lambda i: (0, 0)), pl.BlockSpec((r, nblk + 1), lambda i: (0, 0))],
        out_shape=[jax.ShapeDtypeStruct((r, s), I32), jax.ShapeDtypeStruct((r, nblk + 1), I32)],
        compiler_params=_cparams("arbitrary"),
        name="topk",
    )(aff_rows)


def _floor_rows(x):
    return (x // PACKED_ROWS) * PACKED_ROWS


def _num_passes(starts, ends, win):
    need = ends[0] - _floor_rows(starts[0])
    for a, b in zip(starts[1:], ends[1:]):
        need = jnp.maximum(need, b - _floor_rows(a))
    return (need + win - 1) // win


def _pass_window(start, p, win, cap):
    first = _floor_rows(start) + p * win
    return first, pl.multiple_of(jnp.minimum(first, cap - win), PACKED_ROWS)


def _gather_kernel(st_ref, h_ref, pos_ref, aff_ref, x_ref, v_ref, acc_ref, vacc_ref, *, nt, tt, win, cap, nbatch, ng):
    rows = [((pl.program_id(1) * ng + g) * nbatch + pl.program_id(0)) * (nt + 1) for g in range(ng)]
    acc_ref[...] = jnp.zeros_like(acc_ref)
    vacc_ref[...] = jnp.zeros_like(vacc_ref)
    riota = lax.broadcasted_iota(I32, (win, tt), 0)

    def bounds(t):
        return [st_ref[r + t] for r in rows], [st_ref[r + t + 1] for r in rows]

    def one_pass(t, p, starts):
        hrows = h_ref[0, pl.ds(pl.multiple_of(t * tt, tt), tt), :]
        hots, wins = [], []
        for g in range(ng):
            first, ws = _pass_window(starts[g], p, win, cap)
            pos = pos_ref[g, pl.ds(t, 1), :]
            hots.append((pos - ws == riota) & (pos >= first))
            wins.append(ws)
        onehot = jnp.concatenate([jnp.where(h, 1.0, 0.0).astype(BF16) for h in hots], axis=0)
        rows = _dot(onehot, hrows)
        for g in range(ng):
            acc_ref[g, pl.ds(wins[g], win), :] += rows[g * win:(g + 1) * win]
            vals = jnp.sum(jnp.where(hots[g], aff_ref[g, pl.ds(t, 1), :], 0.0), axis=1, keepdims=True)
            vacc_ref[g, pl.ds(wins[g], win), :] += vals

    def first_pass(t, carry):
        one_pass(t, 0, bounds(t)[0])
        return carry

    def more_passes(t, carry):
        starts, ends = bounds(t)

        def body(p, c):
            one_pass(t, p, starts)
            return c

        lax.fori_loop(1, _num_passes(starts, ends, win), body, 0)
        return carry

    lax.fori_loop(0, nt, first_pass, 0, unroll=GATHER_UNROLL)
    lax.fori_loop(0, nt, more_passes, 0)
    x_ref[0] = acc_ref[...].astype(BF16)
    v_ref[0] = vacc_ref[...]


def _gather(starts, h3, pos3, aff3t, cap, ne):
    b, s, d = h3.shape
    tt, win, ng = GATHER_TILE, GATHER_WINDOW, GATHER_EXPERTS
    nt = s // tt
    grid_spec = pltpu.PrefetchScalarGridSpec(
        num_scalar_prefetch=1,
        grid=(b, ne // ng),
        in_specs=[
            pl.BlockSpec((1, s, d), lambda i, e, st: (i, 0, 0), pipeline_mode=pl.Buffered(1)),
            pl.BlockSpec((ng, nt, tt), lambda i, e, st: (e, i, 0)),
            pl.BlockSpec((ng, nt, tt), lambda i, e, st: (e, i, 0)),
        ],
        out_specs=[
            pl.BlockSpec((1, ng, cap, d), lambda i, e, st: (i, e, 0, 0)),
            pl.BlockSpec((1, ng, cap, 1), lambda i, e, st: (i, e, 0, 0)),
        ],
        scratch_shapes=[pltpu.VMEM((ng, cap, d), F32), pltpu.VMEM((ng, cap, 1), F32)],
    )
    return pl.pallas_call(
        functools.partial(_gather_kernel, nt=nt, tt=tt, win=win, cap=cap, nbatch=b, ng=ng),
        grid_spec=grid_spec,
        out_shape=[jax.ShapeDtypeStruct((b, ne, cap, d), BF16), jax.ShapeDtypeStruct((b, ne, cap, 1), F32)],
        compiler_params=_cparams("arbitrary", "arbitrary"),
        name="gather",
    )(starts, h3, pos3, aff3t)


def _ffn_kernel(x_ref, v_ref, wg_ref, wu_ref, wd_ref, y_ref, acc_ref, *, nf):
    f = pl.program_id(2)
    nb, _, cap, d = x_ref.shape

    @pl.when(f == 0)
    def _():
        acc_ref[...] = jnp.zeros_like(acc_ref)

    x = x_ref[...].reshape(nb * cap, d)
    g = _dot(x, wg_ref[0].astype(BF16))
    u = _dot(x, wu_ref[0].astype(BF16))
    hid = (g * jax.nn.sigmoid(g) * u).astype(BF16)
    acc_ref[...] += _dot(hid, wd_ref[0].astype(BF16))

    @pl.when(f == nf - 1)
    def _():
        y = acc_ref[...] * v_ref[...].reshape(nb * cap, 1)
        y_ref[...] = y.astype(BF16).reshape(nb, 1, cap, d)


def _ffn(xg, vals, wg, wu, wd, nb, tf):
    b, ne, cap, d = xg.shape
    dexp = wg.shape[2]
    nf = dexp // tf
    return pl.pallas_call(
        functools.partial(_ffn_kernel, nf=nf),
        grid=(ne, b // nb, nf),
        in_specs=[
            pl.BlockSpec((nb, 1, cap, d), lambda e, i, f: (i, e, 0, 0)),
            pl.BlockSpec((nb, 1, cap, 1), lambda e, i, f: (i, e, 0, 0)),
            pl.BlockSpec((1, d, tf), lambda e, i, f: (e, 0, f)),
            pl.BlockSpec((1, d, tf), lambda e, i, f: (e, 0, f)),
            pl.BlockSpec((1, tf, d), lambda e, i, f: (e, f, 0)),
        ],
        out_specs=pl.BlockSpec((nb, 1, cap, d), lambda e, i, f: (i, e, 0, 0)),
        out_shape=jax.ShapeDtypeStruct(xg.shape, BF16),
        scratch_shapes=[pltpu.VMEM((nb * cap, d), F32)],
        compiler_params=_cparams("parallel", "parallel", "arbitrary"),
        name="ffn",
    )(xg, vals, wg, wu, wd)


def _combine_kernel(st_ref, y_ref, pos_ref, x1_ref, p_ref, wpp_ref, wpg_ref, gple_ref, gout_ref,
                    o_ref, *, nt, tt, win, cap, ne, nbatch, final):
    t = pl.program_id(1)
    rows = [(e * nbatch + pl.program_id(0)) * (nt + 1) + t for e in range(ne)]
    starts = [st_ref[r] for r in rows]
    ends = [st_ref[r + 1] for r in rows]
    k = ne * win
    lane = lax.broadcasted_iota(I32, (1, k), 1)
    lane_e = jnp.zeros((1, k), I32)
    for e in range(1, ne):
        lane_e = lane_e + jnp.where(lane >= e * win, 1, 0)
    lane_r = (lane - lane_e * win).astype(F32)
    part = lax.broadcasted_iota(I32, (2 * LANES, k), 0)
    spread = jnp.where(part == lane_e, float(COMBINE_SPLIT), jnp.where(part == lane_e + LANES, 1.0, 0.0)).astype(BF16)
    pos = pos_ref[...].astype(F32)
    pos = jnp.concatenate([pos, jnp.zeros((LANES - ne, tt), F32)], axis=0).T
    hi = jnp.floor(pos * (1.0 / COMBINE_SPLIT))
    lo = pos - hi * COMBINE_SPLIT
    rank = _dot(jnp.concatenate([hi, lo], axis=1).astype(BF16), spread)

    def pass_body(p, acc):
        firsts = jnp.zeros((1, k), F32)
        offs = jnp.zeros((1, k), F32)
        wins = []
        for e in range(ne):
            first, ws = _pass_window(starts[e], p, win, cap)
            firsts = jnp.where(lane_e == e, first.astype(F32), firsts)
            offs = jnp.where(lane_e == e, ws.astype(F32), offs)
            wins.append(y_ref[0, e, pl.ds(ws, win), :])
        onehot = jnp.where((rank - offs == lane_r) & (rank >= firsts), 1.0, 0.0).astype(BF16)
        return acc + _dot(onehot, jnp.concatenate(wins, axis=0))

    acc = lax.fori_loop(0, _num_passes(starts, ends, win), pass_body, x1_ref[0])
    emb = _dot(p_ref[0].astype(BF16), wpp_ref[...])
    gate = jax.nn.sigmoid(_dot(_rms(acc, gple_ref[...]).astype(BF16), wpg_ref[...]))
    x3 = acc + gate * emb
    o_ref[0] = _rms(x3, gout_ref[...]) if final else x3


def _combine(starts, yg, post, x13, p3, wpp, wpg, gple, gout, cap, final):
    b, ne, _, d = yg.shape
    s = x13.shape[1]
    tt, win = COMBINE_TILE, COMBINE_WINDOW
    nt = s // tt
    tile = lambda a: pl.BlockSpec((1, tt, a.shape[2]), lambda i, t, st: (i, t, 0))
    full = lambda a: pl.BlockSpec(a.shape, lambda i, t, st: (0,) * a.ndim)
    grid_spec = pltpu.PrefetchScalarGridSpec(
        num_scalar_prefetch=1,
        grid=(b, nt),
        in_specs=[
            pl.BlockSpec((1, ne, cap, d), lambda i, t, st: (i, 0, 0, 0), pipeline_mode=pl.Buffered(1)),
            pl.BlockSpec((ne, tt), lambda i, t, st: (0, i * nt + t)),
            tile(x13), tile(p3), full(wpp), full(wpg), full(gple), full(gout),
        ],
        out_specs=pl.BlockSpec((1, tt, d), lambda i, t, st: (i, t, 0)),
    )
    return pl.pallas_call(
        functools.partial(_combine_kernel, nt=nt, tt=tt, win=win, cap=cap, ne=ne, nbatch=b, final=final),
        grid_spec=grid_spec,
        out_shape=jax.ShapeDtypeStruct(x13.shape, F32),
        compiler_params=_cparams("arbitrary", "arbitrary"),
        name="combine",
    )(starts, yg, post, x13, p3, wpp, wpg, gple, gout)


def kernel(x, p, g_mix, w_in, w_fourier, ssm_a_re, ssm_a_im, ssm_log_dt, ssm_b_re, ssm_b_im, ssm_c_re, ssm_c_im, ssm_d, w_glu, w_out, g_ffn, w_router, w_exp_gate, w_exp_up, w_exp_down, g_ple, w_ple_gate, w_ple_proj, g_final):
    b, s, d = x.shape
    depth = p.shape[0]
    df = w_fourier.shape[1]
    ds = w_glu.shape[1]
    ne = w_router.shape[2]
    dexp = w_exp_gate.shape[3]
    groups, gi = ssm_b_re.shape[2], ssm_b_re.shape[4]
    L = SSM_CHUNK
    n1 = s // DFT_N2
    nc = s // L
    cap = EC_CAPACITY * s // ne
    assert s % (DFT_N2 * SUBLANES) == 0 and s % GATHER_TILE == 0 and cap % PACKED_ROWS == 0
    assert nc % SSM_TILE_CHUNKS == 0 and gi == PACKED_ROWS and cap // COMBINE_SPLIT < 256
    assert 2 * ssm_a_re.shape[3] == LANES and ds % LANES == 0
    assert cap >= COMBINE_WINDOW and ne % GATHER_EXPERTS == 0 and GATHER_TILE == COMBINE_TILE
    assert ds == groups * gi and df % FOURIER_GROUPS == 0

    bd = _channel_dft_table(df)
    ck, sk = _dft1_tables(n1)
    tr, ti = _dft3_tables(n1)
    tf = 256 if dexp % 256 == 0 else dexp
    nb_ffn = 2 if b % 2 == 0 else 1

    xcur = x.reshape(b * s, d)
    for i in range(depth):
        final = i == depth - 1
        row = lambda v: v.astype(F32).reshape(1, -1)
        pq, us, gate = _inproj(xcur, row(g_mix[i]), w_in[i].astype(BF16), bd, df, ds)
        a4 = _dft1(pq.reshape(b, n1, DFT_N2, 2 * df), ck, sk)
        fre = _dft3(a4, tr, ti).reshape(b * s, df)

        tabs = _ssm_tables(ssm_a_re[i], ssm_a_im[i], ssm_log_dt[i], ssm_b_re[i], ssm_b_im[i],
                           ssm_c_re[i], ssm_c_im[i], ssm_d[i])
        ys = _to_tokens(_ssm(_to_groups(us, L, groups, gi), *tabs, nc=nc, nb=b), L, groups, gi)

        wr = w_router[i].astype(F32)
        wr_hi = wr.astype(BF16)
        wr_lo = (wr - wr_hi.astype(F32)).astype(BF16)
        lane_pad = lambda w: jnp.pad(w, ((0, 0), (0, LANES - ne)))
        wr2 = jnp.concatenate([jnp.concatenate([lane_pad(wr_hi), lane_pad(wr_lo)], axis=1),
                               jnp.concatenate([lane_pad(wr_hi), jnp.zeros((d, LANES), BF16)], axis=1)], axis=0)
        x1, h2, aff = _mixout(fre, ys, gate, xcur, w_fourier[i].astype(BF16), w_glu[i].astype(BF16),
                              w_out[i].astype(BF16), row(g_ffn[i]), wr2, ne)

        aff_rows = aff.reshape(ne * b, s)
        posm, st_blk = _topk(aff_rows, cap)
        starts = st_blk[:, ::GATHER_TILE // TOPK_BLOCK].reshape(-1)
        tiles = (ne, b * (s // GATHER_TILE), GATHER_TILE)
        xg, vals = _gather(starts, h2.reshape(b, s, d), posm.reshape(tiles), aff_rows.reshape(tiles), cap, ne)
        yg = _ffn(xg, vals, w_exp_gate[i], w_exp_up[i], w_exp_down[i], nb_ffn, tf)
        xnext = _combine(starts, yg, posm.reshape(ne, b * s), x1.reshape(b, s, d), p[i],
                         w_ple_proj[i].astype(BF16), w_ple_gate[i].astype(BF16), row(g_ple[i]),
                         row(g_final) if final else row(g_ple[i]), cap, final)
        xcur = xnext.reshape(b * s, d)
    return xcur.reshape(b, s, d)
```

```python
import functools
import math

import jax
import jax.numpy as jnp
from jax import lax
from jax.experimental import pallas as pl
from jax.experimental.pallas import tpu as pltpu

F32 = jnp.float32
BF16 = jnp.bfloat16
I32 = jnp.int32

RMS_EPS = 1e-6
FOURIER_GROUPS = 4
EC_CAPACITY = 2
DFT_N2 = 128
SUBLANES = 8
LANES = 128
PACKED_ROWS = 16
SSM_CHUNK = 32
SSM_TILE_CHUNKS = 128
TOPK_BLOCK = 128
GATHER_TILE = 256
MIX_ROWS = 256
FFN_ROWS = 512
GATHER_WINDOW = 64
GATHER_EXPERTS = 2
GATHER_UNROLL = 4
COMBINE_TILE = 256
COMBINE_WINDOW = 64
COMBINE_SPLIT = 32
VMEM_LIMIT = 56 * 1024 * 1024


def _cparams(*sem):
    return pltpu.CompilerParams(dimension_semantics=sem, vmem_limit_bytes=VMEM_LIMIT)


def _rms(x, g):
    return x * lax.rsqrt(jnp.mean(x * x, axis=-1, keepdims=True) + RMS_EPS) * g


def _dot(a, b):
    return jnp.dot(a, b, preferred_element_type=F32)


def _inproj_kernel(x_ref, g_ref, w_ref, bd_ref, pq_ref, us_ref, gate_ref, *, df, ds):
    h = _rms(x_ref[...], g_ref[...]).astype(BF16)
    z = _dot(h, w_ref[...])
    pq_ref[...] = _dot(z[:, :df].astype(BF16), bd_ref[...])
    for q in range(ds // LANES):
        us_ref[q] = z[:, df + q * LANES:df + (q + 1) * LANES]
    gate_ref[...] = jax.nn.sigmoid(z[:, df + ds:]).astype(BF16)


def _inproj(x2, g, w_in, bd, df, ds, tm=512):
    t, d = x2.shape
    dg = w_in.shape[1] - df - ds
    return pl.pallas_call(
        functools.partial(_inproj_kernel, df=df, ds=ds),
        grid=(t // tm,),
        in_specs=[
            pl.BlockSpec((tm, d), lambda i: (i, 0)),
            pl.BlockSpec((1, d), lambda i: (0, 0)),
            pl.BlockSpec(w_in.shape, lambda i: (0, 0)),
            pl.BlockSpec(bd.shape, lambda i: (0, 0)),
        ],
        out_specs=[
            pl.BlockSpec((tm, 2 * df), lambda i: (i, 0)),
            pl.BlockSpec((ds // LANES, tm, LANES), lambda i: (0, i, 0)),
            pl.BlockSpec((tm, dg), lambda i: (i, 0)),
        ],
        out_shape=[
            jax.ShapeDtypeStruct((t, 2 * df), F32),
            jax.ShapeDtypeStruct((ds // LANES, t, LANES), F32),
            jax.ShapeDtypeStruct((t, dg), BF16),
        ],
        compiler_params=_cparams("parallel"),
        name="inproj",
    )(x2, g, w_in, bd)


def _channel_dft_table(df):
    c = df // FOURIER_GROUPS
    k = jnp.arange(c, dtype=I32)
    ang = (2.0 * math.pi / c) * ((k[:, None] * k[None, :]) % c).astype(F32)
    eye = jnp.eye(FOURIER_GROUPS, dtype=F32)
    scale = 1.0 / math.sqrt(c)
    re = jnp.kron(eye, jnp.cos(ang)) * scale
    im = -jnp.kron(eye, jnp.sin(ang)) * scale
    return jnp.concatenate([re, im], axis=1).astype(BF16)


def _dft1_tables(n1):
    k = jnp.arange(n1, dtype=I32)
    ang = (2.0 * math.pi / n1) * ((k[:, None] * k[None, :]) % n1).astype(F32)
    eye = jnp.eye(SUBLANES, dtype=F32)
    scale = 1.0 / math.sqrt(n1)
    return (jnp.kron(jnp.cos(ang), eye) * scale).astype(BF16), (jnp.kron(jnp.sin(ang), eye) * scale).astype(BF16)


def _dft3_tables(n1):
    s = n1 * DFT_N2
    nb = n1 // SUBLANES
    k1 = jnp.arange(n1, dtype=I32)[:, None, None]
    k2 = jnp.arange(DFT_N2, dtype=I32)[None, :, None]
    n2 = jnp.arange(DFT_N2, dtype=I32)[None, None, :]
    ang = (2.0 * math.pi / s) * ((n2 * (k1 + n1 * k2)) % s).astype(F32)
    eye = jnp.eye(SUBLANES, dtype=BF16)[None, None, :, :, None]
    rows = DFT_N2 * SUBLANES

    def expand(t):
        t = (t * (1.0 / math.sqrt(DFT_N2))).astype(BF16).reshape(nb, SUBLANES, DFT_N2, DFT_N2)
        t = jnp.transpose(t, (0, 2, 1, 3))[:, :, :, None, :]
        return (t * eye).reshape(nb, rows, rows)

    return expand(jnp.cos(ang)), expand(jnp.sin(ang))


def _dft1_kernel(z_ref, ck_ref, sk_ref, a_ref):
    n1, c2 = z_ref.shape[1], z_ref.shape[3]
    c = c2 // 2
    z = z_ref[0].reshape(n1 * SUBLANES, c2).astype(BF16)
    cz = _dot(ck_ref[...], z)
    sz = _dot(sk_ref[...], z)
    a_ref[0, :, :, :c] = (cz[:, :c] + sz[:, c:]).reshape(n1, SUBLANES, c)
    a_ref[0, :, :, c:] = (cz[:, c:] - sz[:, :c]).reshape(n1, SUBLANES, c)


def _dft1(pq4, ck, sk):
    b, n1, n2, c2 = pq4.shape
    return pl.pallas_call(
        _dft1_kernel,
        grid=(b, n2 // SUBLANES),
        in_specs=[
            pl.BlockSpec((1, n1, SUBLANES, c2), lambda i, j: (i, 0, j, 0)),
            pl.BlockSpec(ck.shape, lambda i, j: (0, 0)),
            pl.BlockSpec(sk.shape, lambda i, j: (0, 0)),
        ],
        out_specs=pl.BlockSpec((1, n1, SUBLANES, c2), lambda i, j: (i, 0, j, 0)),
        out_shape=jax.ShapeDtypeStruct(pq4.shape, F32),
        compiler_params=_cparams("parallel", "parallel"),
        name="dft1",
    )(pq4, ck, sk)


def _dft3_kernel(a_ref, tr_ref, ti_ref, o_ref):
    c2 = a_ref.shape[3]
    c = c2 // 2
    a = a_ref[0].reshape(SUBLANES * DFT_N2, c2).astype(BF16)
    out = _dot(tr_ref[0], a[:, :c]) + _dot(ti_ref[0], a[:, c:])
    o_ref[0] = out.reshape(DFT_N2, SUBLANES, c)


def _dft3(a4, tr, ti):
    b, n1, n2, c2 = a4.shape
    c = c2 // 2
    rows = DFT_N2 * SUBLANES
    return pl.pallas_call(
        _dft3_kernel,
        grid=(n1 // SUBLANES, b),
        in_specs=[
            pl.BlockSpec((1, SUBLANES, n2, c2), lambda k, i: (i, k, 0, 0)),
            pl.BlockSpec((1, rows, rows), lambda k, i: (k, 0, 0)),
            pl.BlockSpec((1, rows, rows), lambda k, i: (k, 0, 0)),
        ],
        out_specs=pl.BlockSpec((1, DFT_N2, SUBLANES, c), lambda k, i: (i, 0, k, 0)),
        out_shape=jax.ShapeDtypeStruct((b, DFT_N2, n1, c), F32),
        compiler_params=_cparams("parallel", "parallel"),
        name="dft3",
    )(a4, tr, ti)


def _ssm_tables(a_re, a_im, log_dt, b_re, b_im, c_re, c_im, d):
    hp = lax.Precision.HIGHEST
    L = SSM_CHUNK
    nd, g, n = a_re.shape
    gi = b_re.shape[-1]
    dt = jnp.exp(log_dt.astype(F32))[..., None]
    ar, ai = a_re.astype(F32), a_im.astype(F32)
    tau = jnp.arange(L + 1, dtype=F32)[:, None, None, None]
    mag = jnp.exp(tau * (ar * dt)[None])
    ang = tau * (ai * dt)[None]
    pr, pi = mag * jnp.cos(ang), mag * jnp.sin(ang)
    nr, ni = pr[1] - 1.0, pi[1]
    den = ar * ar + ai * ai
    qr, qi = (nr * ar + ni * ai) / den, (ni * ar - nr * ai) / den
    br, bi = b_re.astype(F32), b_im.astype(F32)
    bbr = qr[..., None] * br - qi[..., None] * bi
    bbi = qr[..., None] * bi + qi[..., None] * br
    cr, ci = c_re.astype(F32), c_im.astype(F32)

    pbr = pr[:L, ..., None] * bbr[None] - pi[:L, ..., None] * bbi[None]
    pbi = pr[:L, ..., None] * bbi[None] + pi[:L, ..., None] * bbr[None]
    kk = (jnp.einsum('dgon,tdgni->dgtoi', cr, pbr, precision=hp)
          - jnp.einsum('dgon,tdgni->dgtoi', ci, pbi, precision=hp))
    kfull = jnp.concatenate([kk[1][:, :0:-1], kk[0][:, :1] + kk[1][:, :1], kk[0][:, 1:]], axis=1)
    j = jnp.arange(L, dtype=I32)
    tap = jnp.arange(2 * L - 1, dtype=I32)[:, None, None]
    place = (j[None, None, :] - j[None, :, None] + (L - 1) == tap).astype(F32)
    t_tab = jnp.einsum('gtoi,tab->gaibo', kfull, place, precision=hp).reshape(g, L * gi, L * gi)

    ef_r, ef_i = pbr[::-1, 0], pbi[::-1, 0]
    eb_r, eb_i = pbr[:, 1], pbi[:, 1]
    e4 = jnp.stack([ef_r, eb_r, ef_i, eb_i], axis=0)
    e_tab = jnp.transpose(e4, (2, 1, 4, 0, 3)).reshape(g, L * gi, 4 * n)

    pf_r, pf_i = pr[1:, 0], pi[1:, 0]
    pb_r, pb_i = pr[1:, 1][::-1], pi[1:, 1][::-1]

    def readout(cr_, ci_, pr_, pi_):
        mr = cr_[None] * pr_[:, :, None, :] - ci_[None] * pi_[:, :, None, :]
        mi = cr_[None] * pi_[:, :, None, :] + ci_[None] * pr_[:, :, None, :]
        return mr, -mi

    ffr, ffi = readout(cr[0], ci[0], pf_r, pf_i)
    fbr, fbi = readout(cr[1], ci[1], pb_r, pb_i)
    f4 = jnp.stack([ffr, fbr, ffi, fbi], axis=0)
    f_tab = jnp.transpose(f4, (2, 0, 4, 1, 3)).reshape(g, 4 * n, L * gi)

    al = jnp.stack([jnp.concatenate([pr[L, 0], pr[L, 1]], axis=-1),
                    jnp.concatenate([pi[L, 0], pi[L, 1]], axis=-1)], axis=1)
    dv = jnp.tile(d.astype(F32).reshape(g, 1, gi), (1, L, 1)).reshape(g, L * gi, 1)
    left = lambda tab: jnp.swapaxes(tab, 1, 2).astype(BF16)
    return left(t_tab), left(e_tab), left(f_tab), al, dv


def _to_groups_kernel(u_ref, a_ref, *, L, groups, gi):
    ncl = a_ref.shape[2]
    gq = LANES // gi
    for q in range(u_ref.shape[0]):
        for j in range(L):
            zt = u_ref[q, pl.ds(j, ncl, stride=L), :].T
            a_ref[q * gq:(q + 1) * gq, pl.ds(j * gi, gi), :] = zt.reshape(gq, gi, ncl).astype(BF16)


def _to_groups(us, L, groups, gi):
    nq, t, _ = us.shape
    ncl = SSM_TILE_CHUNKS
    return pl.pallas_call(
        functools.partial(_to_groups_kernel, L=L, groups=groups, gi=gi),
        grid=(t // (ncl * L),),
        in_specs=[pl.BlockSpec((nq, ncl * L, LANES), lambda i: (0, i, 0))],
        out_specs=pl.BlockSpec((groups, L * gi, ncl), lambda i: (0, 0, i)),
        out_shape=jax.ShapeDtypeStruct((groups, L * gi, t // L), BF16),
        compiler_params=_cparams("parallel"),
        name="to_groups",
    )(us)


def _to_tokens_kernel(y_ref, o_ref, *, L, groups, gi):
    ncl = y_ref.shape[2]
    gq = LANES // gi
    for q in range(o_ref.shape[0]):
        for j in range(L):
            yj = y_ref[q * gq:(q + 1) * gq, pl.ds(j * gi, gi), :].astype(F32).reshape(LANES, ncl)
            o_ref[q, pl.ds(j, ncl, stride=L), :] = yj.T


def _to_tokens(yt, L, groups, gi):
    _, k, nchunks = yt.shape
    ncl = SSM_TILE_CHUNKS
    return pl.pallas_call(
        functools.partial(_to_tokens_kernel, L=L, groups=groups, gi=gi),
        grid=(nchunks // ncl,),
        in_specs=[pl.BlockSpec((groups, k, ncl), lambda i: (0, 0, i))],
        out_specs=pl.BlockSpec((groups * gi // LANES, ncl * L, LANES), lambda i: (0, i, 0)),
        out_shape=jax.ShapeDtypeStruct((groups * gi // LANES, nchunks * L, LANES), F32),
        compiler_params=_cparams("parallel"),
        name="to_tokens",
    )(yt)


def _ssm_kernel(a_ref, t_ref, e_ref, f_ref, al_ref, dv_ref, y_ref, s_scr, h_scr, *, nc, nb, rows, n):
    a = a_ref[0]
    y1 = _dot(t_ref[0], a) + dv_ref[0] * a.astype(F32)
    st = _dot(e_ref[0], a)
    n2 = 2 * n
    s_scr[...] = jnp.zeros_like(s_scr)
    for b in range(nb):
        sb_t = st[:, b * nc:(b + 1) * nc].T
        for q in range(2):
            s_scr[q, pl.ds(b, nc, stride=rows), :] = sb_t[:, q * n2:(q + 1) * n2]
    ar = al_ref[0, 0:1, :]
    ai = al_ref[0, 1:2, :]
    is_fwd = lax.broadcasted_iota(I32, (rows, n2), 1) < n

    def step(i, carry):
        hr, hi = carry
        rf = pl.ds(pl.multiple_of(i * rows, rows), rows)
        rb = pl.ds(pl.multiple_of((nc - 1 - i) * rows, rows), rows)
        h_scr[0, rf, 0:n] = hr[:, 0:n]
        h_scr[1, rf, 0:n] = hi[:, 0:n]
        h_scr[0, rb, n:n2] = hr[:, n:n2]
        h_scr[1, rb, n:n2] = hi[:, n:n2]
        sr = jnp.where(is_fwd, s_scr[0, rf, :], s_scr[0, rb, :])
        si = jnp.where(is_fwd, s_scr[1, rf, :], s_scr[1, rb, :])
        return ar * hr - ai * hi + sr, ar * hi + ai * hr + si

    zero = jnp.zeros((rows, n2), F32)
    lax.fori_loop(0, nc, step, (zero, zero))
    ht = jnp.concatenate(
        [jnp.concatenate([h_scr[q, pl.ds(b, nc, stride=rows), :].T for q in range(2)], axis=0) for b in range(nb)],
        axis=1)
    y = y1 + _dot(f_ref[0], ht.astype(BF16))
    y_ref[0] = jax.nn.gelu(y, approximate=True).astype(BF16)


def _ssm(ag, t_tab, e_tab, f_tab, al, dv, nc, nb):
    g, k, m = ag.shape
    n4 = e_tab.shape[1]
    rows = -(-nb // SUBLANES) * SUBLANES
    return pl.pallas_call(
        functools.partial(_ssm_kernel, nc=nc, nb=nb, rows=rows, n=n4 // 4),
        grid=(g,),
        in_specs=[
            pl.BlockSpec((1, k, m), lambda i: (i, 0, 0)),
            pl.BlockSpec((1, k, k), lambda i: (i, 0, 0)),
            pl.BlockSpec((1, n4, k), lambda i: (i, 0, 0)),
            pl.BlockSpec((1, k, n4), lambda i: (i, 0, 0)),
            pl.BlockSpec((1, 2, n4 // 2), lambda i: (i, 0, 0)),
            pl.BlockSpec((1, k, 1), lambda i: (i, 0, 0)),
        ],
        out_specs=pl.BlockSpec((1, k, m), lambda i: (i, 0, 0)),
        out_shape=jax.ShapeDtypeStruct((g, k, m), BF16),
        scratch_shapes=[pltpu.VMEM((2, nc * rows, n4 // 2), F32), pltpu.VMEM((2, nc * rows, n4 // 2), F32)],
        compiler_params=_cparams("parallel"),
        name="ssm",
    )(ag, t_tab, e_tab, f_tab, al, dv)


def _mixout_kernel(fre_ref, ys_ref, gate_ref, x_ref, wf_ref, wglu_ref, wout_ref, gffn_ref, wr_ref,
                   x1_ref, h2_ref, aff_ref, *, d, ne):
    for r in range(fre_ref.shape[0] // MIX_ROWS):
        rows = pl.ds(r * MIX_ROWS, MIX_ROWS)
        y_f = _dot(fre_ref[rows, :].astype(BF16), wf_ref[...])
        ys = jnp.concatenate([ys_ref[q, rows, :] for q in range(ys_ref.shape[0])], axis=1)
        vg = _dot(ys.astype(BF16), wglu_ref[...])
        y_s = vg[:, :d] * jax.nn.sigmoid(vg[:, d:])
        gate = gate_ref[rows, :].astype(F32)
        m = gate[:, :d] * y_f + gate[:, d:] * y_s
        x1 = x_ref[rows, :] + _dot(m.astype(BF16), wout_ref[...])
        x1_ref[rows, :] = x1
        h2 = _rms(x1, gffn_ref[...])
        h2_ref[rows, :] = h2.astype(BF16)
        hi = h2.astype(BF16)
        lo = (h2 - hi.astype(F32)).astype(BF16)
        rl = _dot(jnp.concatenate([hi, lo], axis=1), wr_ref[...])
        logits = rl[:, :LANES] + rl[:, LANES:]
        logits = jnp.where(lax.broadcasted_iota(I32, logits.shape, 1) < ne, logits, -1e30)
        logits = logits - jnp.max(logits, axis=-1, keepdims=True)
        p = jnp.exp(logits)
        aff = p / jnp.sum(p, axis=-1, keepdims=True)
        aff_ref[:, rows] = aff.T[:ne, :]


def _mixout(fre, ys, gate, x2, wf, wglu, wout, gffn, wr2, ne, tm=512):
    t, d = x2.shape
    full = lambda a: pl.BlockSpec(a.shape, lambda i: (0,) * a.ndim)
    row = lambda a: pl.BlockSpec((tm, a.shape[1]), lambda i: (i, 0))
    return pl.pallas_call(
        functools.partial(_mixout_kernel, d=d, ne=ne),
        grid=(t // tm,),
        in_specs=[row(fre), pl.BlockSpec((ys.shape[0], tm, LANES), lambda i: (0, i, 0)), row(gate), row(x2),
                  full(wf), full(wglu), full(wout), full(gffn), full(wr2)],
        out_specs=[
            pl.BlockSpec((tm, d), lambda i: (i, 0)),
            pl.BlockSpec((tm, d), lambda i: (i, 0)),
            pl.BlockSpec((ne, tm), lambda i: (0, i)),
        ],
        out_shape=[
            jax.ShapeDtypeStruct((t, d), F32),
            jax.ShapeDtypeStruct((t, d), BF16),
            jax.ShapeDtypeStruct((ne, t), F32),
        ],
        compiler_params=_cparams("parallel"),
        name="mixout",
    )(fre, ys, gate, x2, wf, wglu, wout, gffn, wr2)


def _topk_kernel(aff_ref, pos_ref, st_ref, *, cap, blk):
    v = aff_ref[...]
    r, s = v.shape
    capf = float(cap)

    def bit_step(i, t):
        cand = t | (jnp.int32(1) << (30 - i))
        cnt = jnp.sum(jnp.where(v >= pltpu.bitcast(cand, F32), 1.0, 0.0), axis=1, keepdims=True)
        return jnp.where(cnt >= capf, cand, t)

    thr = lax.fori_loop(0, 31, bit_step, jnp.zeros((r, 1), I32))
    gt = jnp.where(v >= pltpu.bitcast(thr + 1, F32), 1.0, 0.0)
    eq = jnp.where(v >= pltpu.bitcast(thr, F32), 1.0, 0.0) - gt
    need = capf - jnp.sum(gt, axis=1, keepdims=True)
    ii = lax.broadcasted_iota(I32, (blk, blk), 0)
    jj = lax.broadcasted_iota(I32, (blk, blk), 1)
    tri = jnp.where(ii < jj, 1.0, 0.0).astype(BF16)
    run_eq = jnp.zeros((r, 1), F32)
    run = jnp.zeros((r, 1), F32)
    for k in range(s // blk):
        sl = slice(k * blk, (k + 1) * blk)
        eqb, gtb = eq[:, sl], gt[:, sl]
        rank_eq = _dot(eqb.astype(BF16), tri) + run_eq
        run_eq = run_eq + jnp.sum(eqb, axis=1, keepdims=True)
        mask = gtb + eqb * jnp.where(rank_eq < need, 1.0, 0.0)
        pos = _dot(mask.astype(BF16), tri) + run
        st_ref[:, k:k + 1] = run.astype(I32)
        run = run + jnp.sum(mask, axis=1, keepdims=True)
        pos_ref[:, sl] = jnp.where(mask > 0.0, pos, -1.0).astype(I32)
    st_ref[:, s // blk:s // blk + 1] = run.astype(I32)


def _topk(aff_rows, cap, blk=TOPK_BLOCK):
    r, s = aff_rows.shape
    nblk = s // blk
    return pl.pallas_call(
        functools.partial(_topk_kernel, cap=cap, blk=blk),
        grid=(1,),
        in_specs=[pl.BlockSpec((r, s), lambda i: (0, 0))],
        out_specs=[pl.BlockSpec((r, s), lambda i: (0, 0)), pl.BlockSpec((r, nblk + 1), lambda i: (0, 0))],
        out_shape=[jax.ShapeDtypeStruct((r, s), I32), jax.ShapeDtypeStruct((r, nblk + 1), I32)],
        compiler_params=_cparams("arbitrary"),
        name="topk",
    )(aff_rows)


def _floor_rows(x):
    return (x // PACKED_ROWS) * PACKED_ROWS


def _num_passes(starts, ends, win):
    need = ends[0] - _floor_rows(starts[0])
    for a, b in zip(starts[1:], ends[1:]):
        need = jnp.maximum(need, b - _floor_rows(a))
    return (need + win - 1) // win


def _pass_window(start, p, win, cap):
    first = _floor_rows(start) + p * win
    return first, pl.multiple_of(jnp.minimum(first, cap - win), PACKED_ROWS)


def _gather_kernel(st_ref, h_ref, pos_ref, aff_ref, x_ref, v_ref, acc_ref, vacc_ref, *, nt, tt, win, cap, nbatch, ng):
    rows = [((pl.program_id(1) * ng + g) * nbatch + pl.program_id(0)) * (nt + 1) for g in range(ng)]
    acc_ref[...] = jnp.zeros_like(acc_ref)
    vacc_ref[...] = jnp.zeros_like(vacc_ref)
    riota = lax.broadcasted_iota(I32, (win, tt), 0)

    def bounds(t):
        return [st_ref[r + t] for r in rows], [st_ref[r + t + 1] for r in rows]

    def one_pass(t, p, starts):
        hrows = h_ref[0, pl.ds(pl.multiple_of(t * tt, tt), tt), :]
        hots, wins = [], []
        for g in range(ng):
            first, ws = _pass_window(starts[g], p, win, cap)
            pos = pos_ref[g, pl.ds(t, 1), :]
            hots.append((pos - ws == riota) & (pos >= first))
            wins.append(ws)
        onehot = jnp.concatenate([jnp.where(h, 1.0, 0.0).astype(BF16) for h in hots], axis=0)
        rows = _dot(onehot, hrows)
        for g in range(ng):
            acc_ref[g, pl.ds(wins[g], win), :] += rows[g * win:(g + 1) * win]
            vals = jnp.sum(jnp.where(hots[g], aff_ref[g, pl.ds(t, 1), :], 0.0), axis=1, keepdims=True)
            vacc_ref[g, pl.ds(wins[g], win), :] += vals

    def first_pass(t, carry):
        one_pass(t, 0, bounds(t)[0])
        return carry

    def more_passes(t, carry):
        starts, ends = bounds(t)

        def body(p, c):
            one_pass(t, p, starts)
            return c

        lax.fori_loop(1, _num_passes(starts, ends, win), body, 0)
        return carry

    lax.fori_loop(0, nt, first_pass, 0, unroll=GATHER_UNROLL)
    lax.fori_loop(0, nt, more_passes, 0)
    x_ref[0] = acc_ref[...].astype(BF16)
    v_ref[0] = vacc_ref[...]


def _gather(starts, h3, pos3, aff3t, cap, ne):
    b, s, d = h3.shape
    tt, win, ng = GATHER_TILE, GATHER_WINDOW, GATHER_EXPERTS
    nt = s // tt
    grid_spec = pltpu.PrefetchScalarGridSpec(
        num_scalar_prefetch=1,
        grid=(b, ne // ng),
        in_specs=[
            pl.BlockSpec((1, s, d), lambda i, e, st: (i, 0, 0), pipeline_mode=pl.Buffered(1)),
            pl.BlockSpec((ng, nt, tt), lambda i, e, st: (e, i, 0)),
            pl.BlockSpec((ng, nt, tt), lambda i, e, st: (e, i, 0)),
        ],
        out_specs=[
            pl.BlockSpec((1, ng, cap, d), lambda i, e, st: (i, e, 0, 0)),
            pl.BlockSpec((1, ng, cap, 1), lambda i, e, st: (i, e, 0, 0)),
        ],
        scratch_shapes=[pltpu.VMEM((ng, cap, d), F32), pltpu.VMEM((ng, cap, 1), F32)],
    )
    return pl.pallas_call(
        functools.partial(_gather_kernel, nt=nt, tt=tt, win=win, cap=cap, nbatch=b, ng=ng),
        grid_spec=grid_spec,
        out_shape=[jax.ShapeDtypeStruct((b, ne, cap, d), BF16), jax.ShapeDtypeStruct((b, ne, cap, 1), F32)],
        compiler_params=_cparams("arbitrary", "arbitrary"),
        name="gather",
    )(starts, h3, pos3, aff3t)


def _ffn_kernel(x_ref, v_ref, wg_ref, wu_ref, wd_ref, y_ref, acc_ref, *, nf):
    f = pl.program_id(2)
    nb, _, cap, d = x_ref.shape

    @pl.when(f == 0)
    def _():
        acc_ref[...] = jnp.zeros_like(acc_ref)

    wg = wg_ref[0].astype(BF16)
    wu = wu_ref[0].astype(BF16)
    wd = wd_ref[0].astype(BF16)
    mr = min(FFN_ROWS, cap)
    for r in range(nb * cap // mr):
        x = x_ref[r * mr // cap, 0, pl.ds(r * mr % cap, mr), :]
        g = _dot(x, wg)
        u = _dot(x, wu)
        hid = (g * jax.nn.sigmoid(g) * u).astype(BF16)
        acc_ref[pl.ds(r * mr, mr), :] += _dot(hid, wd)

    @pl.when(f == nf - 1)
    def _():
        y = acc_ref[...] * v_ref[...].reshape(nb * cap, 1)
        y_ref[...] = y.astype(BF16).reshape(nb, 1, cap, d)


def _ffn(xg, vals, wg, wu, wd, nb, tf):
    b, ne, cap, d = xg.shape
    dexp = wg.shape[2]
    nf = dexp // tf
    return pl.pallas_call(
        functools.partial(_ffn_kernel, nf=nf),
        grid=(ne, b // nb, nf),
        in_specs=[
            pl.BlockSpec((nb, 1, cap, d), lambda e, i, f: (i, e, 0, 0)),
            pl.BlockSpec((nb, 1, cap, 1), lambda e, i, f: (i, e, 0, 0)),
            pl.BlockSpec((1, d, tf), lambda e, i, f: (e, 0, f)),
            pl.BlockSpec((1, d, tf), lambda e, i, f: (e, 0, f)),
            pl.BlockSpec((1, tf, d), lambda e, i, f: (e, f, 0)),
        ],
        out_specs=pl.BlockSpec((nb, 1, cap, d), lambda e, i, f: (i, e, 0, 0)),
        out_shape=jax.ShapeDtypeStruct(xg.shape, BF16),
        scratch_shapes=[pltpu.VMEM((nb * cap, d), F32)],
        compiler_params=_cparams("parallel", "parallel", "arbitrary"),
        name="ffn",
    )(xg, vals, wg, wu, wd)


def _combine_kernel(st_ref, y_ref, pos_ref, x1_ref, p_ref, wpp_ref, wpg_ref, gple_ref, gout_ref,
                    o_ref, *, nt, tt, win, cap, ne, nbatch, final):
    t = pl.program_id(1)
    rows = [(e * nbatch + pl.program_id(0)) * (nt + 1) + t for e in range(ne)]
    starts = [st_ref[r] for r in rows]
    ends = [st_ref[r + 1] for r in rows]
    k = ne * win
    lane = lax.broadcasted_iota(I32, (1, k), 1)
    lane_e = jnp.zeros((1, k), I32)
    for e in range(1, ne):
        lane_e = lane_e + jnp.where(lane >= e * win, 1, 0)
    lane_r = (lane - lane_e * win).astype(F32)
    part = lax.broadcasted_iota(I32, (2 * LANES, k), 0)
    spread = jnp.where(part == lane_e, float(COMBINE_SPLIT), jnp.where(part == lane_e + LANES, 1.0, 0.0)).astype(BF16)
    pos = pos_ref[...].astype(F32)
    pos = jnp.concatenate([pos, jnp.zeros((LANES - ne, tt), F32)], axis=0).T
    hi = jnp.floor(pos * (1.0 / COMBINE_SPLIT))
    lo = pos - hi * COMBINE_SPLIT
    rank = _dot(jnp.concatenate([hi, lo], axis=1).astype(BF16), spread)

    def pass_body(p, acc):
        firsts = jnp.zeros((1, k), F32)
        offs = jnp.zeros((1, k), F32)
        wins = []
        for e in range(ne):
            first, ws = _pass_window(starts[e], p, win, cap)
            firsts = jnp.where(lane_e == e, first.astype(F32), firsts)
            offs = jnp.where(lane_e == e, ws.astype(F32), offs)
            wins.append(y_ref[0, e, pl.ds(ws, win), :])
        onehot = jnp.where((rank - offs == lane_r) & (rank >= firsts), 1.0, 0.0).astype(BF16)
        return acc + _dot(onehot, jnp.concatenate(wins, axis=0))

    acc = lax.fori_loop(0, _num_passes(starts, ends, win), pass_body, x1_ref[0])
    for r in range(tt // LANES):
        rows = pl.ds(r * LANES, LANES)
        x2 = acc[r * LANES:(r + 1) * LANES]
        emb = _dot(p_ref[0, rows, :].astype(BF16), wpp_ref[...])
        gate = jax.nn.sigmoid(_dot(_rms(x2, gple_ref[...]).astype(BF16), wpg_ref[...]))
        x3 = x2 + gate * emb
        o_ref[0, rows, :] = _rms(x3, gout_ref[...]) if final else x3


def _combine(starts, yg, post, x13, p3, wpp, wpg, gple, gout, cap, final):
    b, ne, _, d = yg.shape
    s = x13.shape[1]
    tt, win = COMBINE_TILE, COMBINE_WINDOW
    nt = s // tt
    tile = lambda a: pl.BlockSpec((1, tt, a.shape[2]), lambda i, t, st: (i, t, 0))
    full = lambda a: pl.BlockSpec(a.shape, lambda i, t, st: (0,) * a.ndim)
    grid_spec = pltpu.PrefetchScalarGridSpec(
        num_scalar_prefetch=1,
        grid=(b, nt),
        in_specs=[
            pl.BlockSpec((1, ne, cap, d), lambda i, t, st: (i, 0, 0, 0), pipeline_mode=pl.Buffered(1)),
            pl.BlockSpec((ne, tt), lambda i, t, st: (0, i * nt + t)),
            tile(x13), tile(p3), full(wpp), full(wpg), full(gple), full(gout),
        ],
        out_specs=pl.BlockSpec((1, tt, d), lambda i, t, st: (i, t, 0)),
    )
    return pl.pallas_call(
        functools.partial(_combine_kernel, nt=nt, tt=tt, win=win, cap=cap, ne=ne, nbatch=b, final=final),
        grid_spec=grid_spec,
        out_shape=jax.ShapeDtypeStruct(x13.shape, F32),
        compiler_params=_cparams("arbitrary", "arbitrary"),
        name="combine",
    )(starts, yg, post, x13, p3, wpp, wpg, gple, gout)


def kernel(x, p, g_mix, w_in, w_fourier, ssm_a_re, ssm_a_im, ssm_log_dt, ssm_b_re, ssm_b_im, ssm_c_re, ssm_c_im, ssm_d, w_glu, w_out, g_ffn, w_router, w_exp_gate, w_exp_up, w_exp_down, g_ple, w_ple_gate, w_ple_proj, g_final):
    b, s, d = x.shape
    depth = p.shape[0]
    df = w_fourier.shape[1]
    ds = w_glu.shape[1]
    ne = w_router.shape[2]
    dexp = w_exp_gate.shape[3]
    groups, gi = ssm_b_re.shape[2], ssm_b_re.shape[4]
    L = SSM_CHUNK
    n1 = s // DFT_N2
    nc = s // L
    cap = EC_CAPACITY * s // ne
    assert s % (DFT_N2 * SUBLANES) == 0 and s % GATHER_TILE == 0 and cap % PACKED_ROWS == 0
    assert nc % SSM_TILE_CHUNKS == 0 and gi == PACKED_ROWS and cap // COMBINE_SPLIT < 256
    assert 2 * ssm_a_re.shape[3] == LANES and ds % LANES == 0
    assert cap >= COMBINE_WINDOW and ne % GATHER_EXPERTS == 0 and GATHER_TILE == COMBINE_TILE
    assert ds == groups * gi and df % FOURIER_GROUPS == 0

    bd = _channel_dft_table(df)
    ck, sk = _dft1_tables(n1)
    tr, ti = _dft3_tables(n1)
    tf = 256 if dexp % 256 == 0 else dexp
    nb_ffn = 2 if b % 2 == 0 else 1

    xcur = x.reshape(b * s, d)
    for i in range(depth):
        final = i == depth - 1
        row = lambda v: v.astype(F32).reshape(1, -1)
        pq, us, gate = _inproj(xcur, row(g_mix[i]), w_in[i].astype(BF16), bd, df, ds)
        a4 = _dft1(pq.reshape(b, n1, DFT_N2, 2 * df), ck, sk)
        fre = _dft3(a4, tr, ti).reshape(b * s, df)

        tabs = _ssm_tables(ssm_a_re[i], ssm_a_im[i], ssm_log_dt[i], ssm_b_re[i], ssm_b_im[i],
                           ssm_c_re[i], ssm_c_im[i], ssm_d[i])
        ys = _to_tokens(_ssm(_to_groups(us, L, groups, gi), *tabs, nc=nc, nb=b), L, groups, gi)

        wr = w_router[i].astype(F32)
        wr_hi = wr.astype(BF16)
        wr_lo = (wr - wr_hi.astype(F32)).astype(BF16)
        lane_pad = lambda w: jnp.pad(w, ((0, 0), (0, LANES - ne)))
        wr2 = jnp.concatenate([jnp.concatenate([lane_pad(wr_hi), lane_pad(wr_lo)], axis=1),
                               jnp.concatenate([lane_pad(wr_hi), jnp.zeros((d, LANES), BF16)], axis=1)], axis=0)
        x1, h2, aff = _mixout(fre, ys, gate, xcur, w_fourier[i].astype(BF16), w_glu[i].astype(BF16),
                              w_out[i].astype(BF16), row(g_ffn[i]), wr2, ne)

        aff_rows = aff.reshape(ne * b, s)
        posm, st_blk = _topk(aff_rows, cap)
        starts = st_blk[:, ::GATHER_TILE // TOPK_BLOCK].reshape(-1)
        tiles = (ne, b * (s // GATHER_TILE), GATHER_TILE)
        xg, vals = _gather(starts, h2.reshape(b, s, d), posm.reshape(tiles), aff_rows.reshape(tiles), cap, ne)
        yg = _ffn(xg, vals, w_exp_gate[i], w_exp_up[i], w_exp_down[i], nb_ffn, tf)
        xnext = _combine(starts, yg, posm.reshape(ne, b * s), x1.reshape(b, s, d), p[i],
                         w_ple_proj[i].astype(BF16), w_ple_gate[i].astype(BF16), row(g_ple[i]),
                         row(g_final) if final else row(g_ple[i]), cap, final)
        xcur = xnext.reshape(b * s, d)
    return xcur.reshape(b, s, d)
```

```python
import functools
import math

import jax
import jax.numpy as jnp
from jax import lax
from jax.experimental import pallas as pl
from jax.experimental.pallas import tpu as pltpu

F32 = jnp.float32
BF16 = jnp.bfloat16
I32 = jnp.int32

RMS_EPS = 1e-6
FOURIER_GROUPS = 4
EC_CAPACITY = 2
DFT_N2 = 128
SUBLANES = 8
LANES = 128
PACKED_ROWS = 16
SSM_CHUNK = 32
SSM_TILE_CHUNKS = 128
TOPK_BLOCK = 128
GATHER_TILE = 256
MIX_ROWS = 256
FFN_ROWS = 512
GATHER_WINDOW = 64
GATHER_EXPERTS = 4
GATHER_UNROLL = 4
COMBINE_TILE = 256
COMBINE_WINDOW = 64
COMBINE_SPLIT = 32
VMEM_LIMIT = 56 * 1024 * 1024


def _cparams(*sem):
    return pltpu.CompilerParams(dimension_semantics=sem, vmem_limit_bytes=VMEM_LIMIT)


def _rms(x, g):
    return x * lax.rsqrt(jnp.mean(x * x, axis=-1, keepdims=True) + RMS_EPS) * g


def _dot(a, b):
    return jnp.dot(a, b, preferred_element_type=F32)


def _inproj_kernel(x_ref, g_ref, w_ref, bd_ref, pq_ref, us_ref, gate_ref, *, df, ds):
    h = _rms(x_ref[...], g_ref[...]).astype(BF16)
    z = _dot(h, w_ref[...])
    pq_ref[...] = _dot(z[:, :df].astype(BF16), bd_ref[...])
    for q in range(ds // LANES):
        us_ref[q] = z[:, df + q * LANES:df + (q + 1) * LANES]
    gate_ref[...] = jax.nn.sigmoid(z[:, df + ds:]).astype(BF16)


def _inproj(x2, g, w_in, bd, df, ds, tm=512):
    t, d = x2.shape
    dg = w_in.shape[1] - df - ds
    return pl.pallas_call(
        functools.partial(_inproj_kernel, df=df, ds=ds),
        grid=(t // tm,),
        in_specs=[
            pl.BlockSpec((tm, d), lambda i: (i, 0)),
            pl.BlockSpec((1, d), lambda i: (0, 0)),
            pl.BlockSpec(w_in.shape, lambda i: (0, 0)),
            pl.BlockSpec(bd.shape, lambda i: (0, 0)),
        ],
        out_specs=[
            pl.BlockSpec((tm, 2 * df), lambda i: (i, 0)),
            pl.BlockSpec((ds // LANES, tm, LANES), lambda i: (0, i, 0)),
            pl.BlockSpec((tm, dg), lambda i: (i, 0)),
        ],
        out_shape=[
            jax.ShapeDtypeStruct((t, 2 * df), F32),
            jax.ShapeDtypeStruct((ds // LANES, t, LANES), F32),
            jax.ShapeDtypeStruct((t, dg), BF16),
        ],
        compiler_params=_cparams("parallel"),
        name="inproj",
    )(x2, g, w_in, bd)


def _channel_dft_table(df):
    c = df // FOURIER_GROUPS
    k = jnp.arange(c, dtype=I32)
    ang = (2.0 * math.pi / c) * ((k[:, None] * k[None, :]) % c).astype(F32)
    eye = jnp.eye(FOURIER_GROUPS, dtype=F32)
    scale = 1.0 / math.sqrt(c)
    re = jnp.kron(eye, jnp.cos(ang)) * scale
    im = -jnp.kron(eye, jnp.sin(ang)) * scale
    return jnp.concatenate([re, im], axis=1).astype(BF16)


def _dft1_tables(n1):
    k = jnp.arange(n1, dtype=I32)
    ang = (2.0 * math.pi / n1) * ((k[:, None] * k[None, :]) % n1).astype(F32)
    eye = jnp.eye(SUBLANES, dtype=F32)
    scale = 1.0 / math.sqrt(n1)
    return (jnp.kron(jnp.cos(ang), eye) * scale).astype(BF16), (jnp.kron(jnp.sin(ang), eye) * scale).astype(BF16)


def _dft3_tables(n1):
    s = n1 * DFT_N2
    nb = n1 // SUBLANES
    k1 = jnp.arange(n1, dtype=I32)[:, None, None]
    k2 = jnp.arange(DFT_N2, dtype=I32)[None, :, None]
    n2 = jnp.arange(DFT_N2, dtype=I32)[None, None, :]
    ang = (2.0 * math.pi / s) * ((n2 * (k1 + n1 * k2)) % s).astype(F32)
    eye = jnp.eye(SUBLANES, dtype=BF16)[None, None, :, :, None]
    rows = DFT_N2 * SUBLANES

    def expand(t):
        t = (t * (1.0 / math.sqrt(DFT_N2))).astype(BF16).reshape(nb, SUBLANES, DFT_N2, DFT_N2)
        t = jnp.transpose(t, (0, 2, 1, 3))[:, :, :, None, :]
        return (t * eye).reshape(nb, rows, rows)

    return expand(jnp.cos(ang)), expand(jnp.sin(ang))


def _dft1_kernel(z_ref, ck_ref, sk_ref, a_ref):
    n1, c2 = z_ref.shape[1], z_ref.shape[3]
    c = c2 // 2
    z = z_ref[0].reshape(n1 * SUBLANES, c2).astype(BF16)
    cz = _dot(ck_ref[...], z)
    sz = _dot(sk_ref[...], z)
    a_ref[0, :, :, :c] = (cz[:, :c] + sz[:, c:]).reshape(n1, SUBLANES, c)
    a_ref[0, :, :, c:] = (cz[:, c:] - sz[:, :c]).reshape(n1, SUBLANES, c)


def _dft1(pq4, ck, sk):
    b, n1, n2, c2 = pq4.shape
    return pl.pallas_call(
        _dft1_kernel,
        grid=(b, n2 // SUBLANES),
        in_specs=[
            pl.BlockSpec((1, n1, SUBLANES, c2), lambda i, j: (i, 0, j, 0)),
            pl.BlockSpec(ck.shape, lambda i, j: (0, 0)),
            pl.BlockSpec(sk.shape, lambda i, j: (0, 0)),
        ],
        out_specs=pl.BlockSpec((1, n1, SUBLANES, c2), lambda i, j: (i, 0, j, 0)),
        out_shape=jax.ShapeDtypeStruct(pq4.shape, F32),
        compiler_params=_cparams("parallel", "parallel"),
        name="dft1",
    )(pq4, ck, sk)


def _dft3_kernel(a_ref, tr_ref, ti_ref, o_ref):
    c2 = a_ref.shape[3]
    c = c2 // 2
    a = a_ref[0].reshape(SUBLANES * DFT_N2, c2).astype(BF16)
    out = _dot(tr_ref[0], a[:, :c]) + _dot(ti_ref[0], a[:, c:])
    o_ref[0] = out.reshape(DFT_N2, SUBLANES, c)


def _dft3(a4, tr, ti):
    b, n1, n2, c2 = a4.shape
    c = c2 // 2
    rows = DFT_N2 * SUBLANES
    return pl.pallas_call(
        _dft3_kernel,
        grid=(n1 // SUBLANES, b),
        in_specs=[
            pl.BlockSpec((1, SUBLANES, n2, c2), lambda k, i: (i, k, 0, 0)),
            pl.BlockSpec((1, rows, rows), lambda k, i: (k, 0, 0)),
            pl.BlockSpec((1, rows, rows), lambda k, i: (k, 0, 0)),
        ],
        out_specs=pl.BlockSpec((1, DFT_N2, SUBLANES, c), lambda k, i: (i, 0, k, 0)),
        out_shape=jax.ShapeDtypeStruct((b, DFT_N2, n1, c), F32),
        compiler_params=_cparams("parallel", "parallel"),
        name="dft3",
    )(a4, tr, ti)


def _ssm_tables(a_re, a_im, log_dt, b_re, b_im, c_re, c_im, d):
    hp = lax.Precision.HIGHEST
    L = SSM_CHUNK
    _, g, n = a_re.shape
    gi = b_re.shape[-1]
    dt = jnp.exp(log_dt.astype(F32))[..., None]
    ar, ai = a_re.astype(F32), a_im.astype(F32)
    tau = jnp.arange(L + 1, dtype=F32)[:, None, None, None]
    mag = jnp.exp(tau * (ar * dt)[None])
    ang = tau * (ai * dt)[None]
    pr, pi = mag * jnp.cos(ang), mag * jnp.sin(ang)
    nr, ni = pr[1] - 1.0, pi[1]
    den = ar * ar + ai * ai
    qr, qi = (nr * ar + ni * ai) / den, (ni * ar - nr * ai) / den
    br, bi = b_re.astype(F32), b_im.astype(F32)
    bbr = qr[..., None] * br - qi[..., None] * bi
    bbi = qr[..., None] * bi + qi[..., None] * br
    cr, ci = c_re.astype(F32), c_im.astype(F32)

    lane = jnp.arange(L * gi, dtype=I32)
    rep_j = (lane[None, :] // gi == jnp.arange(L, dtype=I32)[:, None]).astype(F32)
    rep_i = (lane[None, :] % gi == jnp.arange(gi, dtype=I32)[:, None]).astype(F32)
    rows_of = lambda small, rep: jnp.einsum('ma,gak->gmk', rep.T, small, precision=hp)
    lanes_of = lambda small, rep: jnp.einsum('gka,am->gkm', small, rep, precision=hp)
    steps_first = lambda p: jnp.transpose(p, (1, 0, 2))
    steps_last = lambda p: jnp.transpose(p, (1, 2, 0))

    def readout_rows(c_r, c_i, p_r, p_i):
        ca = jnp.concatenate([c_r, c_r], axis=-1)
        cb = jnp.concatenate([-c_i, -c_i], axis=-1)
        pa = steps_first(jnp.concatenate([p_r, -p_i], axis=-1))
        pb = steps_first(jnp.concatenate([p_i, p_r], axis=-1))
        return rows_of(ca, rep_i) * rows_of(pa, rep_j) + rows_of(cb, rep_i) * rows_of(pb, rep_j)

    ff = readout_rows(cr[0], ci[0], pr[1:, 0], pi[1:, 0])
    fb = readout_rows(cr[1], ci[1], pr[1:, 1][::-1], pi[1:, 1][::-1])
    f_tab = jnp.concatenate([ff[..., :n], fb[..., :n], ff[..., n:], fb[..., n:]], axis=-1)

    def summary_rows(b_r, b_i, p_r, p_i):
        ba = jnp.concatenate([b_r, b_i], axis=1)
        bb = jnp.concatenate([-b_i, b_r], axis=1)
        pa = steps_last(jnp.concatenate([p_r, p_r], axis=-1))
        pb = steps_last(jnp.concatenate([p_i, p_i], axis=-1))
        return lanes_of(ba, rep_i) * lanes_of(pa, rep_j) + lanes_of(bb, rep_i) * lanes_of(pb, rep_j)

    ef = summary_rows(bbr[0], bbi[0], pr[:L, 0][::-1], pi[:L, 0][::-1])
    eb = summary_rows(bbr[1], bbi[1], pr[:L, 1], pi[:L, 1])
    e_tab = jnp.concatenate([ef[:, :n], eb[:, :n], ef[:, n:], eb[:, n:]], axis=1)

    def taps(dr):
        a = readout_rows(cr[dr], ci[dr], pr[:L, dr], pi[:L, dr])
        bst = jnp.concatenate([bbr[dr], bbi[dr]], axis=1)
        return jnp.einsum('gmk,gki->gmi', a, bst, precision=hp).reshape(g, L, gi, gi)

    kf, kb = taps(0), taps(1)
    kw = jnp.concatenate([kf[:, :0:-1], kf[:, :1] + kb[:, :1], kb[:, 1:]], axis=1)
    kw = jnp.transpose(kw, (0, 2, 1, 3)).reshape(g, gi, (2 * L - 1) * gi)
    t_tab = jnp.stack([kw[:, :, (L - 1 - jo) * gi:(2 * L - 1 - jo) * gi] for jo in range(L)], axis=1)
    t_tab = t_tab.reshape(g, L * gi, L * gi)

    al = jnp.stack([jnp.concatenate([pr[L, 0], pr[L, 1]], axis=-1),
                    jnp.concatenate([pi[L, 0], pi[L, 1]], axis=-1)], axis=1)
    dv = jnp.tile(d.astype(F32).reshape(g, 1, gi), (1, L, 1)).reshape(g, L * gi, 1)
    return t_tab.astype(BF16), e_tab.astype(BF16), f_tab.astype(BF16), al, dv


def _to_groups_kernel(u_ref, a_ref, *, L, groups, gi):
    ncl = a_ref.shape[2]
    gq = LANES // gi
    for q in range(u_ref.shape[0]):
        for j in range(L):
            zt = u_ref[q, pl.ds(j, ncl, stride=L), :].T
            a_ref[q * gq:(q + 1) * gq, pl.ds(j * gi, gi), :] = zt.reshape(gq, gi, ncl).astype(BF16)


def _to_groups(us, L, groups, gi):
    nq, t, _ = us.shape
    ncl = SSM_TILE_CHUNKS
    return pl.pallas_call(
        functools.partial(_to_groups_kernel, L=L, groups=groups, gi=gi),
        grid=(t // (ncl * L),),
        in_specs=[pl.BlockSpec((nq, ncl * L, LANES), lambda i: (0, i, 0))],
        out_specs=pl.BlockSpec((groups, L * gi, ncl), lambda i: (0, 0, i)),
        out_shape=jax.ShapeDtypeStruct((groups, L * gi, t // L), BF16),
        compiler_params=_cparams("parallel"),
        name="to_groups",
    )(us)


def _to_tokens_kernel(y_ref, o_ref, *, L, groups, gi):
    ncl = y_ref.shape[2]
    gq = LANES // gi
    for q in range(o_ref.shape[0]):
        for j in range(L):
            yj = y_ref[q * gq:(q + 1) * gq, pl.ds(j * gi, gi), :].astype(F32).reshape(LANES, ncl)
            o_ref[q, pl.ds(j, ncl, stride=L), :] = yj.T


def _to_tokens(yt, L, groups, gi):
    _, k, nchunks = yt.shape
    ncl = SSM_TILE_CHUNKS
    return pl.pallas_call(
        functools.partial(_to_tokens_kernel, L=L, groups=groups, gi=gi),
        grid=(nchunks // ncl,),
        in_specs=[pl.BlockSpec((groups, k, ncl), lambda i: (0, 0, i))],
        out_specs=pl.BlockSpec((groups * gi // LANES, ncl * L, LANES), lambda i: (0, i, 0)),
        out_shape=jax.ShapeDtypeStruct((groups * gi // LANES, nchunks * L, LANES), F32),
        compiler_params=_cparams("parallel"),
        name="to_tokens",
    )(yt)


def _ssm_kernel(a_ref, t_ref, e_ref, f_ref, al_ref, dv_ref, y_ref, s_scr, h_scr, *, nc, nb, rows, n):
    a = a_ref[0]
    y1 = _dot(t_ref[0], a) + dv_ref[0] * a.astype(F32)
    st = _dot(e_ref[0], a)
    n2 = 2 * n
    s_scr[...] = jnp.zeros_like(s_scr)
    for b in range(nb):
        sb_t = st[:, b * nc:(b + 1) * nc].T
        for q in range(2):
            s_scr[q, pl.ds(b, nc, stride=rows), :] = sb_t[:, q * n2:(q + 1) * n2]
    ar = al_ref[0, 0:1, :]
    ai = al_ref[0, 1:2, :]
    is_fwd = lax.broadcasted_iota(I32, (rows, n2), 1) < n

    def step(i, carry):
        hr, hi = carry
        rf = pl.ds(pl.multiple_of(i * rows, rows), rows)
        rb = pl.ds(pl.multiple_of((nc - 1 - i) * rows, rows), rows)
        h_scr[0, rf, 0:n] = hr[:, 0:n]
        h_scr[1, rf, 0:n] = hi[:, 0:n]
        h_scr[0, rb, n:n2] = hr[:, n:n2]
        h_scr[1, rb, n:n2] = hi[:, n:n2]
        sr = jnp.where(is_fwd, s_scr[0, rf, :], s_scr[0, rb, :])
        si = jnp.where(is_fwd, s_scr[1, rf, :], s_scr[1, rb, :])
        return ar * hr - ai * hi + sr, ar * hi + ai * hr + si

    zero = jnp.zeros((rows, n2), F32)
    lax.fori_loop(0, nc, step, (zero, zero))
    ht = jnp.concatenate(
        [jnp.concatenate([h_scr[q, pl.ds(b, nc, stride=rows), :].T for q in range(2)], axis=0) for b in range(nb)],
        axis=1)
    y = y1 + _dot(f_ref[0], ht.astype(BF16))
    y_ref[0] = jax.nn.gelu(y, approximate=True).astype(BF16)


def _ssm(ag, t_tab, e_tab, f_tab, al, dv, nc, nb):
    g, k, m = ag.shape
    n4 = e_tab.shape[1]
    rows = -(-nb // SUBLANES) * SUBLANES
    return pl.pallas_call(
        functools.partial(_ssm_kernel, nc=nc, nb=nb, rows=rows, n=n4 // 4),
        grid=(g,),
        in_specs=[
            pl.BlockSpec((1, k, m), lambda i: (i, 0, 0)),
            pl.BlockSpec((1, k, k), lambda i: (i, 0, 0)),
            pl.BlockSpec((1, n4, k), lambda i: (i, 0, 0)),
            pl.BlockSpec((1, k, n4), lambda i: (i, 0, 0)),
            pl.BlockSpec((1, 2, n4 // 2), lambda i: (i, 0, 0)),
            pl.BlockSpec((1, k, 1), lambda i: (i, 0, 0)),
        ],
        out_specs=pl.BlockSpec((1, k, m), lambda i: (i, 0, 0)),
        out_shape=jax.ShapeDtypeStruct((g, k, m), BF16),
        scratch_shapes=[pltpu.VMEM((2, nc * rows, n4 // 2), F32), pltpu.VMEM((2, nc * rows, n4 // 2), F32)],
        compiler_params=_cparams("parallel"),
        name="ssm",
    )(ag, t_tab, e_tab, f_tab, al, dv)


def _mixout_kernel(fre_ref, ys_ref, gate_ref, x_ref, wf_ref, wglu_ref, wout_ref, gffn_ref, wr_ref,
                   x1_ref, h2_ref, aff_ref, *, d, ne):
    for r in range(fre_ref.shape[0] // MIX_ROWS):
        rows = pl.ds(r * MIX_ROWS, MIX_ROWS)
        y_f = _dot(fre_ref[rows, :].astype(BF16), wf_ref[...])
        ys = jnp.concatenate([ys_ref[q, rows, :] for q in range(ys_ref.shape[0])], axis=1)
        vg = _dot(ys.astype(BF16), wglu_ref[...])
        y_s = vg[:, :d] * jax.nn.sigmoid(vg[:, d:])
        gate = gate_ref[rows, :].astype(F32)
        m = gate[:, :d] * y_f + gate[:, d:] * y_s
        x1 = x_ref[rows, :] + _dot(m.astype(BF16), wout_ref[...])
        x1_ref[rows, :] = x1
        h2 = _rms(x1, gffn_ref[...])
        h2_ref[rows, :] = h2.astype(BF16)
        hi = h2.astype(BF16)
        lo = (h2 - hi.astype(F32)).astype(BF16)
        rl = _dot(jnp.concatenate([hi, lo], axis=1), wr_ref[...])
        logits = rl[:, :LANES] + rl[:, LANES:]
        logits = jnp.where(lax.broadcasted_iota(I32, logits.shape, 1) < ne, logits, -1e30)
        logits = logits - jnp.max(logits, axis=-1, keepdims=True)
        p = jnp.exp(logits)
        aff = p / jnp.sum(p, axis=-1, keepdims=True)
        aff_ref[:, rows] = aff.T[:ne, :]


def _mixout(fre, ys, gate, x2, wf, wglu, wout, gffn, wr2, ne, tm=512):
    t, d = x2.shape
    full = lambda a: pl.BlockSpec(a.shape, lambda i: (0,) * a.ndim)
    row = lambda a: pl.BlockSpec((tm, a.shape[1]), lambda i: (i, 0))
    return pl.pallas_call(
        functools.partial(_mixout_kernel, d=d, ne=ne),
        grid=(t // tm,),
        in_specs=[row(fre), pl.BlockSpec((ys.shape[0], tm, LANES), lambda i: (0, i, 0)), row(gate), row(x2),
                  full(wf), full(wglu), full(wout), full(gffn), full(wr2)],
        out_specs=[
            pl.BlockSpec((tm, d), lambda i: (i, 0)),
            pl.BlockSpec((tm, d), lambda i: (i, 0)),
            pl.BlockSpec((ne, tm), lambda i: (0, i)),
        ],
        out_shape=[
            jax.ShapeDtypeStruct((t, d), F32),
            jax.ShapeDtypeStruct((t, d), BF16),
            jax.ShapeDtypeStruct((ne, t), F32),
        ],
        compiler_params=_cparams("parallel"),
        name="mixout",
    )(fre, ys, gate, x2, wf, wglu, wout, gffn, wr2)


def _topk_kernel(aff_ref, pos_ref, st_ref, *, cap, blk):
    v = aff_ref[...]
    r, s = v.shape
    capf = float(cap)

    def bit_step(i, t):
        cand = t | (jnp.int32(1) << (30 - i))
        cnt = jnp.sum(jnp.where(v >= pltpu.bitcast(cand, F32), 1.0, 0.0), axis=1, keepdims=True)
        return jnp.where(cnt >= capf, cand, t)

    thr = lax.fori_loop(0, 31, bit_step, jnp.zeros((r, 1), I32))
    gt = jnp.where(v >= pltpu.bitcast(thr + 1, F32), 1.0, 0.0)
    eq = jnp.where(v >= pltpu.bitcast(thr, F32), 1.0, 0.0) - gt
    need = capf - jnp.sum(gt, axis=1, keepdims=True)
    ii = lax.broadcasted_iota(I32, (blk, blk), 0)
    jj = lax.broadcasted_iota(I32, (blk, blk), 1)
    tri = jnp.where(ii < jj, 1.0, 0.0).astype(BF16)
    run_eq = jnp.zeros((r, 1), F32)
    run = jnp.zeros((r, 1), F32)
    for k in range(s // blk):
        sl = slice(k * blk, (k + 1) * blk)
        eqb, gtb = eq[:, sl], gt[:, sl]
        rank_eq = _dot(eqb.astype(BF16), tri) + run_eq
        run_eq = run_eq + jnp.sum(eqb, axis=1, keepdims=True)
        mask = gtb + eqb * jnp.where(rank_eq < need, 1.0, 0.0)
        pos = _dot(mask.astype(BF16), tri) + run
        st_ref[:, k:k + 1] = run.astype(I32)
        run = run + jnp.sum(mask, axis=1, keepdims=True)
        pos_ref[:, sl] = jnp.where(mask > 0.0, pos, -1.0).astype(I32)
    st_ref[:, s // blk:s // blk + 1] = run.astype(I32)


def _topk(aff_rows, cap, blk=TOPK_BLOCK):
    r, s = aff_rows.shape
    nblk = s // blk
    return pl.pallas_call(
        functools.partial(_topk_kernel, cap=cap, blk=blk),
        grid=(1,),
        in_specs=[pl.BlockSpec((r, s), lambda i: (0, 0))],
        out_specs=[pl.BlockSpec((r, s), lambda i: (0, 0)), pl.BlockSpec((r, nblk + 1), lambda i: (0, 0))],
        out_shape=[jax.ShapeDtypeStruct((r, s), I32), jax.ShapeDtypeStruct((r, nblk + 1), I32)],
        compiler_params=_cparams("arbitrary"),
        name="topk",
    )(aff_rows)


def _floor_rows(x):
    return (x // PACKED_ROWS) * PACKED_ROWS


def _num_passes(starts, ends, win):
    need = ends[0] - _floor_rows(starts[0])
    for a, b in zip(starts[1:], ends[1:]):
        need = jnp.maximum(need, b - _floor_rows(a))
    return (need + win - 1) // win


def _pass_window(start, p, win, cap):
    first = _floor_rows(start) + p * win
    return first, pl.multiple_of(jnp.minimum(first, cap - win), PACKED_ROWS)


def _gather_kernel(st_ref, h_ref, pos_ref, aff_ref, x_ref, v_ref, *, nt, tt, win, cap, nbatch, ng):
    rows = [((pl.program_id(1) * ng + g) * nbatch + pl.program_id(0)) * (nt + 1) for g in range(ng)]
    x_ref[...] = jnp.zeros_like(x_ref)
    v_ref[...] = jnp.zeros_like(v_ref)
    riota = lax.broadcasted_iota(I32, (win, tt), 0)

    def bounds(t):
        return [st_ref[r + t] for r in rows], [st_ref[r + t + 1] for r in rows]

    def one_pass(t, p, starts):
        hrows = h_ref[0, pl.ds(pl.multiple_of(t * tt, tt), tt), :]
        hots, wins = [], []
        for g in range(ng):
            first, ws = _pass_window(starts[g], p, win, cap)
            pos = pos_ref[g, pl.ds(t, 1), :]
            hots.append((pos - ws == riota) & (pos >= first))
            wins.append(ws)
        onehot = jnp.concatenate([jnp.where(h, 1.0, 0.0).astype(BF16) for h in hots], axis=0)
        rows = _dot(onehot, hrows)
        for g in range(ng):
            x_ref[0, g, pl.ds(wins[g], win), :] += rows[g * win:(g + 1) * win].astype(BF16)
            vals = jnp.sum(jnp.where(hots[g], aff_ref[g, pl.ds(t, 1), :], 0.0), axis=1, keepdims=True)
            v_ref[0, g, pl.ds(wins[g], win), :] += vals

    def first_pass(t, most):
        starts, ends = bounds(t)
        one_pass(t, 0, starts)
        return jnp.maximum(most, _num_passes(starts, ends, win))

    def more_passes(t, carry):
        starts, ends = bounds(t)

        def body(p, c):
            one_pass(t, p, starts)
            return c

        lax.fori_loop(1, _num_passes(starts, ends, win), body, 0)
        return carry

    most = lax.fori_loop(0, nt, first_pass, jnp.int32(0), unroll=GATHER_UNROLL)

    @pl.when(most > 1)
    def _():
        lax.fori_loop(0, nt, more_passes, 0)


def _gather(starts, h3, pos3, aff3t, cap, ne):
    b, s, d = h3.shape
    tt, win, ng = GATHER_TILE, GATHER_WINDOW, GATHER_EXPERTS
    nt = s // tt
    grid_spec = pltpu.PrefetchScalarGridSpec(
        num_scalar_prefetch=1,
        grid=(b, ne // ng),
        in_specs=[
            pl.BlockSpec((1, s, d), lambda i, e, st: (i, 0, 0), pipeline_mode=pl.Buffered(1)),
            pl.BlockSpec((ng, nt, tt), lambda i, e, st: (e, i, 0)),
            pl.BlockSpec((ng, nt, tt), lambda i, e, st: (e, i, 0)),
        ],
        out_specs=[
            pl.BlockSpec((1, ng, cap, d), lambda i, e, st: (i, e, 0, 0)),
            pl.BlockSpec((1, ng, cap, 1), lambda i, e, st: (i, e, 0, 0)),
        ],
    )
    return pl.pallas_call(
        functools.partial(_gather_kernel, nt=nt, tt=tt, win=win, cap=cap, nbatch=b, ng=ng),
        grid_spec=grid_spec,
        out_shape=[jax.ShapeDtypeStruct((b, ne, cap, d), BF16), jax.ShapeDtypeStruct((b, ne, cap, 1), F32)],
        compiler_params=_cparams("arbitrary", "arbitrary"),
        name="gather",
    )(starts, h3, pos3, aff3t)


def _ffn_kernel(x_ref, v_ref, wg_ref, wu_ref, wd_ref, y_ref, acc_ref, *, nf):
    f = pl.program_id(2)
    nb, _, cap, d = x_ref.shape

    @pl.when(f == 0)
    def _():
        acc_ref[...] = jnp.zeros_like(acc_ref)

    wg = wg_ref[0].astype(BF16)
    wu = wu_ref[0].astype(BF16)
    wd = wd_ref[0].astype(BF16)
    mr = min(FFN_ROWS, cap)
    for r in range(nb * cap // mr):
        x = x_ref[r * mr // cap, 0, pl.ds(r * mr % cap, mr), :]
        g = _dot(x, wg)
        u = _dot(x, wu)
        hid = (g * jax.nn.sigmoid(g) * u).astype(BF16)
        acc_ref[pl.ds(r * mr, mr), :] += _dot(hid, wd)

    @pl.when(f == nf - 1)
    def _():
        y = acc_ref[...] * v_ref[...].reshape(nb * cap, 1)
        y_ref[...] = y.astype(BF16).reshape(nb, 1, cap, d)


def _ffn(xg, vals, wg, wu, wd, nb, tf):
    b, ne, cap, d = xg.shape
    dexp = wg.shape[2]
    nf = dexp // tf
    return pl.pallas_call(
        functools.partial(_ffn_kernel, nf=nf),
        grid=(ne, b // nb, nf),
        in_specs=[
            pl.BlockSpec((nb, 1, cap, d), lambda e, i, f: (i, e, 0, 0)),
            pl.BlockSpec((nb, 1, cap, 1), lambda e, i, f: (i, e, 0, 0)),
            pl.BlockSpec((1, d, tf), lambda e, i, f: (e, 0, f)),
            pl.BlockSpec((1, d, tf), lambda e, i, f: (e, 0, f)),
            pl.BlockSpec((1, tf, d), lambda e, i, f: (e, f, 0)),
        ],
        out_specs=pl.BlockSpec((nb, 1, cap, d), lambda e, i, f: (i, e, 0, 0)),
        out_shape=jax.ShapeDtypeStruct(xg.shape, BF16),
        scratch_shapes=[pltpu.VMEM((nb * cap, d), F32)],
        compiler_params=_cparams("parallel", "parallel", "arbitrary"),
        name="ffn",
    )(xg, vals, wg, wu, wd)


def _combine_kernel(st_ref, y_ref, pos_ref, x1_ref, p_ref, wpp_ref, wpg_ref, gple_ref, gout_ref,
                    o_ref, *, nt, tt, win, cap, ne, nbatch, final):
    t = pl.program_id(1)
    rows = [(e * nbatch + pl.program_id(0)) * (nt + 1) + t for e in range(ne)]
    starts = [st_ref[r] for r in rows]
    ends = [st_ref[r + 1] for r in rows]
    k = ne * win
    lane = lax.broadcasted_iota(I32, (1, k), 1)
    lane_e = jnp.zeros((1, k), I32)
    for e in range(1, ne):
        lane_e = lane_e + jnp.where(lane >= e * win, 1, 0)
    lane_r = (lane - lane_e * win).astype(F32)
    part = lax.broadcasted_iota(I32, (2 * LANES, k), 0)
    spread = jnp.where(part == lane_e, float(COMBINE_SPLIT), jnp.where(part == lane_e + LANES, 1.0, 0.0)).astype(BF16)
    pos = pos_ref[...].astype(F32)
    pos = jnp.concatenate([pos, jnp.zeros((LANES - ne, tt), F32)], axis=0).T
    hi = jnp.floor(pos * (1.0 / COMBINE_SPLIT))
    lo = pos - hi * COMBINE_SPLIT
    rank = _dot(jnp.concatenate([hi, lo], axis=1).astype(BF16), spread)

    def pass_body(p, acc):
        firsts = jnp.zeros((1, k), F32)
        offs = jnp.zeros((1, k), F32)
        wins = []
        for e in range(ne):
            first, ws = _pass_window(starts[e], p, win, cap)
            firsts = jnp.where(lane_e == e, first.astype(F32), firsts)
            offs = jnp.where(lane_e == e, ws.astype(F32), offs)
            wins.append(y_ref[0, e, pl.ds(ws, win), :])
        onehot = jnp.where((rank - offs == lane_r) & (rank >= firsts), 1.0, 0.0).astype(BF16)
        return acc + _dot(onehot, jnp.concatenate(wins, axis=0))

    acc = lax.fori_loop(0, _num_passes(starts, ends, win), pass_body, x1_ref[0])
    for r in range(tt // LANES):
        rows = pl.ds(r * LANES, LANES)
        x2 = acc[r * LANES:(r + 1) * LANES]
        emb = _dot(p_ref[0, rows, :].astype(BF16), wpp_ref[...])
        gate = jax.nn.sigmoid(_dot(_rms(x2, gple_ref[...]).astype(BF16), wpg_ref[...]))
        x3 = x2 + gate * emb
        o_ref[0, rows, :] = _rms(x3, gout_ref[...]) if final else x3


def _combine(starts, yg, post, x13, p3, wpp, wpg, gple, gout, cap, final):
    b, ne, _, d = yg.shape
    s = x13.shape[1]
    tt, win = COMBINE_TILE, COMBINE_WINDOW
    nt = s // tt
    tile = lambda a: pl.BlockSpec((1, tt, a.shape[2]), lambda i, t, st: (i, t, 0))
    full = lambda a: pl.BlockSpec(a.shape, lambda i, t, st: (0,) * a.ndim)
    grid_spec = pltpu.PrefetchScalarGridSpec(
        num_scalar_prefetch=1,
        grid=(b, nt),
        in_specs=[
            pl.BlockSpec((1, ne, cap, d), lambda i, t, st: (i, 0, 0, 0), pipeline_mode=pl.Buffered(1)),
            pl.BlockSpec((ne, tt), lambda i, t, st: (0, i * nt + t)),
            tile(x13), tile(p3), full(wpp), full(wpg), full(gple), full(gout),
        ],
        out_specs=pl.BlockSpec((1, tt, d), lambda i, t, st: (i, t, 0)),
    )
    return pl.pallas_call(
        functools.partial(_combine_kernel, nt=nt, tt=tt, win=win, cap=cap, ne=ne, nbatch=b, final=final),
        grid_spec=grid_spec,
        out_shape=jax.ShapeDtypeStruct(x13.shape, F32),
        compiler_params=_cparams("arbitrary", "arbitrary"),
        name="combine",
    )(starts, yg, post, x13, p3, wpp, wpg, gple, gout)


def kernel(x, p, g_mix, w_in, w_fourier, ssm_a_re, ssm_a_im, ssm_log_dt, ssm_b_re, ssm_b_im, ssm_c_re, ssm_c_im, ssm_d, w_glu, w_out, g_ffn, w_router, w_exp_gate, w_exp_up, w_exp_down, g_ple, w_ple_gate, w_ple_proj, g_final):
    b, s, d = x.shape
    depth = p.shape[0]
    df = w_fourier.shape[1]
    ds = w_glu.shape[1]
    ne = w_router.shape[2]
    dexp = w_exp_gate.shape[3]
    groups, gi = ssm_b_re.shape[2], ssm_b_re.shape[4]
    L = SSM_CHUNK
    n1 = s // DFT_N2
    nc = s // L
    cap = EC_CAPACITY * s // ne
    assert s % (DFT_N2 * SUBLANES) == 0 and s % GATHER_TILE == 0 and cap % PACKED_ROWS == 0
    assert nc % SSM_TILE_CHUNKS == 0 and gi == PACKED_ROWS and cap // COMBINE_SPLIT < 256
    assert 2 * ssm_a_re.shape[3] == LANES and ds % LANES == 0
    assert cap >= COMBINE_WINDOW and ne % GATHER_EXPERTS == 0 and GATHER_TILE == COMBINE_TILE
    assert ds == groups * gi and df % FOURIER_GROUPS == 0

    bd = _channel_dft_table(df)
    ck, sk = _dft1_tables(n1)
    tr, ti = _dft3_tables(n1)
    tf = 256 if dexp % 256 == 0 else dexp
    nb_ffn = 2 if b % 2 == 0 else 1

    xcur = x.reshape(b * s, d)
    for i in range(depth):
        final = i == depth - 1
        row = lambda v: v.astype(F32).reshape(1, -1)
        pq, us, gate = _inproj(xcur, row(g_mix[i]), w_in[i].astype(BF16), bd, df, ds)
        a4 = _dft1(pq.reshape(b, n1, DFT_N2, 2 * df), ck, sk)
        fre = _dft3(a4, tr, ti).reshape(b * s, df)

        tabs = _ssm_tables(ssm_a_re[i], ssm_a_im[i], ssm_log_dt[i], ssm_b_re[i], ssm_b_im[i],
                           ssm_c_re[i], ssm_c_im[i], ssm_d[i])
        ys = _to_tokens(_ssm(_to_groups(us, L, groups, gi), *tabs, nc=nc, nb=b), L, groups, gi)

        wr = w_router[i].astype(F32)
        wr_hi = wr.astype(BF16)
        wr_lo = (wr - wr_hi.astype(F32)).astype(BF16)
        lane_pad = lambda w: jnp.pad(w, ((0, 0), (0, LANES - ne)))
        wr2 = jnp.concatenate([jnp.concatenate([lane_pad(wr_hi), lane_pad(wr_lo)], axis=1),
                               jnp.concatenate([lane_pad(wr_hi), jnp.zeros((d, LANES), BF16)], axis=1)], axis=0)
        x1, h2, aff = _mixout(fre, ys, gate, xcur, w_fourier[i].astype(BF16), w_glu[i].astype(BF16),
                              w_out[i].astype(BF16), row(g_ffn[i]), wr2, ne)

        aff_rows = aff.reshape(ne * b, s)
        posm, st_blk = _topk(aff_rows, cap)
        starts = st_blk[:, ::GATHER_TILE // TOPK_BLOCK].reshape(-1)
        tiles = (ne, b * (s // GATHER_TILE), GATHER_TILE)
        xg, vals = _gather(starts, h2.reshape(b, s, d), posm.reshape(tiles), aff_rows.reshape(tiles), cap, ne)
        yg = _ffn(xg, vals, w_exp_gate[i], w_exp_up[i], w_exp_down[i], nb_ffn, tf)
        xnext = _combine(starts, yg, posm.reshape(ne, b * s), x1.reshape(b, s, d), p[i],
                         w_ple_proj[i].astype(BF16), w_ple_gate[i].astype(BF16), row(g_ple[i]),
                         row(g_final) if final else row(g_ple[i]), cap, final)
        xcur = xnext.reshape(b * s, d)
    return xcur.reshape(b, s, d)
```

```python
import functools
import math

import jax
import jax.numpy as jnp
from jax import lax
from jax.experimental import pallas as pl
from jax.experimental.pallas import tpu as pltpu

F32 = jnp.float32
BF16 = jnp.bfloat16
I32 = jnp.int32

RMS_EPS = 1e-6
FOURIER_GROUPS = 4
EC_CAPACITY = 2
DFT_N2 = 128
SUBLANES = 8
LANES = 128
PACKED_ROWS = 16
SSM_CHUNK = 32
SSM_TILE_CHUNKS = 128
TOPK_BLOCK = 128
GATHER_TILE = 256
MIX_ROWS = 256
FFN_ROWS = 512
GATHER_WINDOW = 64
GATHER_EXPERTS = 4
GATHER_UNROLL = 4
COMBINE_TILE = 256
COMBINE_WINDOW = 64
COMBINE_SPLIT = 32
VMEM_LIMIT = 56 * 1024 * 1024


def _cparams(*sem):
    return pltpu.CompilerParams(dimension_semantics=sem, vmem_limit_bytes=VMEM_LIMIT)


def _rms(x, g):
    return x * lax.rsqrt(jnp.mean(x * x, axis=-1, keepdims=True) + RMS_EPS) * g


def _dot(a, b):
    return jnp.dot(a, b, preferred_element_type=F32)


def _inproj_kernel(x_ref, g_ref, w_ref, bd_ref, pq_ref, us_ref, gate_ref, *, df, ds):
    h = _rms(x_ref[...], g_ref[...]).astype(BF16)
    z = _dot(h, w_ref[...])
    pq_ref[...] = _dot(z[:, :df].astype(BF16), bd_ref[...])
    for q in range(ds // LANES):
        us_ref[q] = z[:, df + q * LANES:df + (q + 1) * LANES]
    gate_ref[...] = jax.nn.sigmoid(z[:, df + ds:]).astype(BF16)


def _inproj(x2, g, w_in, bd, df, ds, tm=512):
    t, d = x2.shape
    dg = w_in.shape[1] - df - ds
    return pl.pallas_call(
        functools.partial(_inproj_kernel, df=df, ds=ds),
        grid=(t // tm,),
        in_specs=[
            pl.BlockSpec((tm, d), lambda i: (i, 0)),
            pl.BlockSpec((1, d), lambda i: (0, 0)),
            pl.BlockSpec(w_in.shape, lambda i: (0, 0)),
            pl.BlockSpec(bd.shape, lambda i: (0, 0)),
        ],
        out_specs=[
            pl.BlockSpec((tm, 2 * df), lambda i: (i, 0)),
            pl.BlockSpec((ds // LANES, tm, LANES), lambda i: (0, i, 0)),
            pl.BlockSpec((tm, dg), lambda i: (i, 0)),
        ],
        out_shape=[
            jax.ShapeDtypeStruct((t, 2 * df), F32),
            jax.ShapeDtypeStruct((ds // LANES, t, LANES), F32),
            jax.ShapeDtypeStruct((t, dg), BF16),
        ],
        compiler_params=_cparams("parallel"),
        name="inproj",
    )(x2, g, w_in, bd)


def _channel_dft_table(df):
    c = df // FOURIER_GROUPS
    k = jnp.arange(c, dtype=I32)
    ang = (2.0 * math.pi / c) * ((k[:, None] * k[None, :]) % c).astype(F32)
    eye = jnp.eye(FOURIER_GROUPS, dtype=F32)
    scale = 1.0 / math.sqrt(c)
    re = jnp.kron(eye, jnp.cos(ang)) * scale
    im = -jnp.kron(eye, jnp.sin(ang)) * scale
    return jnp.concatenate([re, im], axis=1).astype(BF16)


def _dft1_tables(n1):
    k = jnp.arange(n1, dtype=I32)
    ang = (2.0 * math.pi / n1) * ((k[:, None] * k[None, :]) % n1).astype(F32)
    eye = jnp.eye(SUBLANES, dtype=F32)
    scale = 1.0 / math.sqrt(n1)
    return (jnp.kron(jnp.cos(ang), eye) * scale).astype(BF16), (jnp.kron(jnp.sin(ang), eye) * scale).astype(BF16)


def _dft3_tables(n1):
    s = n1 * DFT_N2
    nb = n1 // SUBLANES
    k1 = jnp.arange(n1, dtype=I32)[:, None, None]
    k2 = jnp.arange(DFT_N2, dtype=I32)[None, :, None]
    n2 = jnp.arange(DFT_N2, dtype=I32)[None, None, :]
    ang = (2.0 * math.pi / s) * ((n2 * (k1 + n1 * k2)) % s).astype(F32)
    eye = jnp.eye(SUBLANES, dtype=BF16)[None, None, :, :, None]
    rows = DFT_N2 * SUBLANES

    def expand(t):
        t = (t * (1.0 / math.sqrt(DFT_N2))).astype(BF16).reshape(nb, SUBLANES, DFT_N2, DFT_N2)
        t = jnp.transpose(t, (0, 2, 1, 3))[:, :, :, None, :]
        return (t * eye).reshape(nb, rows, rows)

    return expand(jnp.cos(ang)), expand(jnp.sin(ang))


def _dft1_kernel(z_ref, ck_ref, sk_ref, a_ref):
    n1, c2 = z_ref.shape[1], z_ref.shape[3]
    c = c2 // 2
    z = z_ref[0].reshape(n1 * SUBLANES, c2).astype(BF16)
    cz = _dot(ck_ref[...], z)
    sz = _dot(sk_ref[...], z)
    a_ref[0, :, :, :c] = (cz[:, :c] + sz[:, c:]).reshape(n1, SUBLANES, c)
    a_ref[0, :, :, c:] = (cz[:, c:] - sz[:, :c]).reshape(n1, SUBLANES, c)


def _dft1(pq4, ck, sk):
    b, n1, n2, c2 = pq4.shape
    return pl.pallas_call(
        _dft1_kernel,
        grid=(b, n2 // SUBLANES),
        in_specs=[
            pl.BlockSpec((1, n1, SUBLANES, c2), lambda i, j: (i, 0, j, 0)),
            pl.BlockSpec(ck.shape, lambda i, j: (0, 0)),
            pl.BlockSpec(sk.shape, lambda i, j: (0, 0)),
        ],
        out_specs=pl.BlockSpec((1, n1, SUBLANES, c2), lambda i, j: (i, 0, j, 0)),
        out_shape=jax.ShapeDtypeStruct(pq4.shape, F32),
        compiler_params=_cparams("parallel", "parallel"),
        name="dft1",
    )(pq4, ck, sk)


def _dft3_kernel(a_ref, tr_ref, ti_ref, o_ref):
    c2 = a_ref.shape[3]
    c = c2 // 2
    a = a_ref[0].reshape(SUBLANES * DFT_N2, c2).astype(BF16)
    out = _dot(tr_ref[0], a[:, :c]) + _dot(ti_ref[0], a[:, c:])
    o_ref[0] = out.reshape(DFT_N2, SUBLANES, c)


def _dft3(a4, tr, ti):
    b, n1, n2, c2 = a4.shape
    c = c2 // 2
    rows = DFT_N2 * SUBLANES
    return pl.pallas_call(
        _dft3_kernel,
        grid=(n1 // SUBLANES, b),
        in_specs=[
            pl.BlockSpec((1, SUBLANES, n2, c2), lambda k, i: (i, k, 0, 0)),
            pl.BlockSpec((1, rows, rows), lambda k, i: (k, 0, 0)),
            pl.BlockSpec((1, rows, rows), lambda k, i: (k, 0, 0)),
        ],
        out_specs=pl.BlockSpec((1, DFT_N2, SUBLANES, c), lambda k, i: (i, 0, k, 0)),
        out_shape=jax.ShapeDtypeStruct((b, DFT_N2, n1, c), F32),
        compiler_params=_cparams("parallel", "parallel"),
        name="dft3",
    )(a4, tr, ti)


def _ssm_tables(a_re, a_im, log_dt, b_re, b_im, c_re, c_im, d):
    hp = lax.Precision.HIGHEST
    L = SSM_CHUNK
    _, g, n = a_re.shape
    gi = b_re.shape[-1]
    dt = jnp.exp(log_dt.astype(F32))[..., None]
    ar, ai = a_re.astype(F32), a_im.astype(F32)
    tau = jnp.arange(L + 1, dtype=F32)[:, None, None, None]
    mag = jnp.exp(tau * (ar * dt)[None])
    ang = tau * (ai * dt)[None]
    pr, pi = mag * jnp.cos(ang), mag * jnp.sin(ang)
    nr, ni = pr[1] - 1.0, pi[1]
    den = ar * ar + ai * ai
    qr, qi = (nr * ar + ni * ai) / den, (ni * ar - nr * ai) / den
    br, bi = b_re.astype(F32), b_im.astype(F32)
    bbr = qr[..., None] * br - qi[..., None] * bi
    bbi = qr[..., None] * bi + qi[..., None] * br
    cr, ci = c_re.astype(F32), c_im.astype(F32)

    lane = jnp.arange(L * gi, dtype=I32)
    rep_j = (lane[None, :] // gi == jnp.arange(L, dtype=I32)[:, None]).astype(F32)
    rep_i = (lane[None, :] % gi == jnp.arange(gi, dtype=I32)[:, None]).astype(F32)
    rows_of = lambda small, rep: jnp.einsum('ma,gak->gmk', rep.T, small, precision=hp)
    lanes_of = lambda small, rep: jnp.einsum('gka,am->gkm', small, rep, precision=hp)
    steps_first = lambda p: jnp.transpose(p, (1, 0, 2))
    steps_last = lambda p: jnp.transpose(p, (1, 2, 0))

    def readout_rows(c_r, c_i, p_r, p_i):
        ca = jnp.concatenate([c_r, c_r], axis=-1)
        cb = jnp.concatenate([-c_i, -c_i], axis=-1)
        pa = steps_first(jnp.concatenate([p_r, -p_i], axis=-1))
        pb = steps_first(jnp.concatenate([p_i, p_r], axis=-1))
        return rows_of(ca, rep_i) * rows_of(pa, rep_j) + rows_of(cb, rep_i) * rows_of(pb, rep_j)

    ff = readout_rows(cr[0], ci[0], pr[1:, 0], pi[1:, 0])
    fb = readout_rows(cr[1], ci[1], pr[1:, 1][::-1], pi[1:, 1][::-1])
    f_tab = jnp.concatenate([ff[..., :n], fb[..., :n], ff[..., n:], fb[..., n:]], axis=-1)

    def summary_rows(b_r, b_i, p_r, p_i):
        ba = jnp.concatenate([b_r, b_i], axis=1)
        bb = jnp.concatenate([-b_i, b_r], axis=1)
        pa = steps_last(jnp.concatenate([p_r, p_r], axis=-1))
        pb = steps_last(jnp.concatenate([p_i, p_i], axis=-1))
        return lanes_of(ba, rep_i) * lanes_of(pa, rep_j) + lanes_of(bb, rep_i) * lanes_of(pb, rep_j)

    ef = summary_rows(bbr[0], bbi[0], pr[:L, 0][::-1], pi[:L, 0][::-1])
    eb = summary_rows(bbr[1], bbi[1], pr[:L, 1], pi[:L, 1])
    e_tab = jnp.concatenate([ef[:, :n], eb[:, :n], ef[:, n:], eb[:, n:]], axis=1)

    def taps(dr):
        a = readout_rows(cr[dr], ci[dr], pr[:L, dr], pi[:L, dr])
        bst = jnp.concatenate([bbr[dr], bbi[dr]], axis=1)
        return jnp.einsum('gmk,gki->gmi', a, bst, precision=hp).reshape(g, L, gi, gi)

    kf, kb = taps(0), taps(1)
    kw = jnp.concatenate([kf[:, :0:-1], kf[:, :1] + kb[:, :1], kb[:, 1:]], axis=1)
    kw = jnp.transpose(kw, (0, 2, 1, 3)).reshape(g, gi, (2 * L - 1) * gi)
    t_tab = jnp.stack([kw[:, :, (L - 1 - jo) * gi:(2 * L - 1 - jo) * gi] for jo in range(L)], axis=1)
    t_tab = t_tab.reshape(g, L * gi, L * gi)

    al = jnp.stack([jnp.concatenate([pr[L, 0], pr[L, 1]], axis=-1),
                    jnp.concatenate([pi[L, 0], pi[L, 1]], axis=-1)], axis=1)
    dv = jnp.tile(d.astype(F32).reshape(g, 1, gi), (1, L, 1)).reshape(g, L * gi, 1)
    return t_tab.astype(BF16), e_tab.astype(BF16), f_tab.astype(BF16), al, dv


def _to_groups_kernel(u_ref, a_ref, *, L, groups, gi):
    ncl = a_ref.shape[2]
    gq = LANES // gi
    for q in range(u_ref.shape[0]):
        for j in range(L):
            zt = u_ref[q, pl.ds(j, ncl, stride=L), :].T
            a_ref[q * gq:(q + 1) * gq, pl.ds(j * gi, gi), :] = zt.reshape(gq, gi, ncl).astype(BF16)


def _to_groups(us, L, groups, gi):
    nq, t, _ = us.shape
    ncl = SSM_TILE_CHUNKS
    return pl.pallas_call(
        functools.partial(_to_groups_kernel, L=L, groups=groups, gi=gi),
        grid=(t // (ncl * L),),
        in_specs=[pl.BlockSpec((nq, ncl * L, LANES), lambda i: (0, i, 0))],
        out_specs=pl.BlockSpec((groups, L * gi, ncl), lambda i: (0, 0, i)),
        out_shape=jax.ShapeDtypeStruct((groups, L * gi, t // L), BF16),
        compiler_params=_cparams("parallel"),
        name="to_groups",
    )(us)


def _to_tokens_kernel(y_ref, o_ref, *, L, groups, gi):
    ncl = y_ref.shape[2]
    gq = LANES // gi
    for q in range(o_ref.shape[0]):
        for j in range(L):
            yj = y_ref[q * gq:(q + 1) * gq, pl.ds(j * gi, gi), :].astype(F32).reshape(LANES, ncl)
            o_ref[q, pl.ds(j, ncl, stride=L), :] = yj.T


def _to_tokens(yt, L, groups, gi):
    _, k, nchunks = yt.shape
    ncl = SSM_TILE_CHUNKS
    return pl.pallas_call(
        functools.partial(_to_tokens_kernel, L=L, groups=groups, gi=gi),
        grid=(nchunks // ncl,),
        in_specs=[pl.BlockSpec((groups, k, ncl), lambda i: (0, 0, i))],
        out_specs=pl.BlockSpec((groups * gi // LANES, ncl * L, LANES), lambda i: (0, i, 0)),
        out_shape=jax.ShapeDtypeStruct((groups * gi // LANES, nchunks * L, LANES), F32),
        compiler_params=_cparams("parallel"),
        name="to_tokens",
    )(yt)


def _ssm_kernel(a_ref, t_ref, e_ref, f_ref, al_ref, dv_ref, y_ref, s_scr, h_scr, *, nc, nb, rows, n):
    a = a_ref[0]
    y1 = _dot(t_ref[0], a) + dv_ref[0] * a.astype(F32)
    st = _dot(e_ref[0], a)
    n2 = 2 * n
    s_scr[...] = jnp.zeros_like(s_scr)
    for b in range(nb):
        sb_t = st[:, b * nc:(b + 1) * nc].T
        for q in range(2):
            s_scr[q, pl.ds(b, nc, stride=rows), :] = sb_t[:, q * n2:(q + 1) * n2]
    ar = al_ref[0, 0:1, :]
    ai = al_ref[0, 1:2, :]
    is_fwd = lax.broadcasted_iota(I32, (rows, n2), 1) < n

    def step(i, carry):
        hr, hi = carry
        rf = pl.ds(pl.multiple_of(i * rows, rows), rows)
        rb = pl.ds(pl.multiple_of((nc - 1 - i) * rows, rows), rows)
        h_scr[0, rf, 0:n] = hr[:, 0:n]
        h_scr[1, rf, 0:n] = hi[:, 0:n]
        h_scr[0, rb, n:n2] = hr[:, n:n2]
        h_scr[1, rb, n:n2] = hi[:, n:n2]
        sr = jnp.where(is_fwd, s_scr[0, rf, :], s_scr[0, rb, :])
        si = jnp.where(is_fwd, s_scr[1, rf, :], s_scr[1, rb, :])
        return ar * hr - ai * hi + sr, ar * hi + ai * hr + si

    zero = jnp.zeros((rows, n2), F32)
    lax.fori_loop(0, nc, step, (zero, zero))
    ht = jnp.concatenate(
        [jnp.concatenate([h_scr[q, pl.ds(b, nc, stride=rows), :].T for q in range(2)], axis=0) for b in range(nb)],
        axis=1)
    y = y1 + _dot(f_ref[0], ht.astype(BF16))
    y_ref[0] = jax.nn.gelu(y, approximate=True).astype(BF16)


def _ssm(ag, t_tab, e_tab, f_tab, al, dv, nc, nb):
    g, k, m = ag.shape
    n4 = e_tab.shape[1]
    rows = -(-nb // SUBLANES) * SUBLANES
    return pl.pallas_call(
        functools.partial(_ssm_kernel, nc=nc, nb=nb, rows=rows, n=n4 // 4),
        grid=(g,),
        in_specs=[
            pl.BlockSpec((1, k, m), lambda i: (i, 0, 0)),
            pl.BlockSpec((1, k, k), lambda i: (i, 0, 0)),
            pl.BlockSpec((1, n4, k), lambda i: (i, 0, 0)),
            pl.BlockSpec((1, k, n4), lambda i: (i, 0, 0)),
            pl.BlockSpec((1, 2, n4 // 2), lambda i: (i, 0, 0)),
            pl.BlockSpec((1, k, 1), lambda i: (i, 0, 0)),
        ],
        out_specs=pl.BlockSpec((1, k, m), lambda i: (i, 0, 0)),
        out_shape=jax.ShapeDtypeStruct((g, k, m), BF16),
        scratch_shapes=[pltpu.VMEM((2, nc * rows, n4 // 2), F32), pltpu.VMEM((2, nc * rows, n4 // 2), F32)],
        compiler_params=_cparams("parallel"),
        name="ssm",
    )(ag, t_tab, e_tab, f_tab, al, dv)


def _mixout_kernel(fre_ref, ys_ref, gate_ref, x_ref, wf_ref, wglu_ref, wout_ref, gffn_ref, wr_ref,
                   x1_ref, h2_ref, aff_ref, *, d, ne):
    for r in range(fre_ref.shape[0] // MIX_ROWS):
        rows = pl.ds(r * MIX_ROWS, MIX_ROWS)
        y_f = _dot(fre_ref[rows, :].astype(BF16), wf_ref[...])
        ys = jnp.concatenate([ys_ref[q, rows, :] for q in range(ys_ref.shape[0])], axis=1)
        vg = _dot(ys.astype(BF16), wglu_ref[...])
        y_s = vg[:, :d] * jax.nn.sigmoid(vg[:, d:])
        gate = gate_ref[rows, :].astype(F32)
        m = gate[:, :d] * y_f + gate[:, d:] * y_s
        x1 = x_ref[rows, :] + _dot(m.astype(BF16), wout_ref[...])
        x1_ref[rows, :] = x1
        h2 = _rms(x1, gffn_ref[...])
        h2_ref[rows, :] = h2.astype(BF16)
        hi = h2.astype(BF16)
        lo = (h2 - hi.astype(F32)).astype(BF16)
        rl = _dot(jnp.concatenate([hi, lo], axis=1), wr_ref[...])
        logits = rl[:, :LANES] + rl[:, LANES:]
        logits = jnp.where(lax.broadcasted_iota(I32, logits.shape, 1) < ne, logits, -1e30)
        logits = logits - jnp.max(logits, axis=-1, keepdims=True)
        p = jnp.exp(logits)
        aff = p / jnp.sum(p, axis=-1, keepdims=True)
        aff_ref[:, rows] = aff.T[:ne, :]


def _mixout(fre, ys, gate, x2, wf, wglu, wout, gffn, wr2, ne, tm=512):
    t, d = x2.shape
    full = lambda a: pl.BlockSpec(a.shape, lambda i: (0,) * a.ndim)
    row = lambda a: pl.BlockSpec((tm, a.shape[1]), lambda i: (i, 0))
    return pl.pallas_call(
        functools.partial(_mixout_kernel, d=d, ne=ne),
        grid=(t // tm,),
        in_specs=[row(fre), pl.BlockSpec((ys.shape[0], tm, LANES), lambda i: (0, i, 0)), row(gate), row(x2),
                  full(wf), full(wglu), full(wout), full(gffn), full(wr2)],
        out_specs=[
            pl.BlockSpec((tm, d), lambda i: (i, 0)),
            pl.BlockSpec((tm, d), lambda i: (i, 0)),
            pl.BlockSpec((ne, tm), lambda i: (0, i)),
        ],
        out_shape=[
            jax.ShapeDtypeStruct((t, d), F32),
            jax.ShapeDtypeStruct((t, d), BF16),
            jax.ShapeDtypeStruct((ne, t), F32),
        ],
        compiler_params=_cparams("parallel"),
        name="mixout",
    )(fre, ys, gate, x2, wf, wglu, wout, gffn, wr2)


def _topk_kernel(aff_ref, pos_ref, st_ref, *, cap, blk):
    v = aff_ref[...]
    r, s = v.shape
    capf = float(cap)

    def bit_step(i, t):
        cand = t | (jnp.int32(1) << (30 - i))
        cnt = jnp.sum(jnp.where(v >= pltpu.bitcast(cand, F32), 1.0, 0.0), axis=1, keepdims=True)
        return jnp.where(cnt >= capf, cand, t)

    thr = lax.fori_loop(0, 31, bit_step, jnp.zeros((r, 1), I32))
    gt = jnp.where(v >= pltpu.bitcast(thr + 1, F32), 1.0, 0.0)
    eq = jnp.where(v >= pltpu.bitcast(thr, F32), 1.0, 0.0) - gt
    need = capf - jnp.sum(gt, axis=1, keepdims=True)
    ii = lax.broadcasted_iota(I32, (blk, blk), 0)
    jj = lax.broadcasted_iota(I32, (blk, blk), 1)
    tri = jnp.where(ii < jj, 1.0, 0.0).astype(BF16)
    run_eq = jnp.zeros((r, 1), F32)
    run = jnp.zeros((r, 1), F32)
    for k in range(s // blk):
        sl = slice(k * blk, (k + 1) * blk)
        eqb, gtb = eq[:, sl], gt[:, sl]
        rank_eq = _dot(eqb.astype(BF16), tri) + run_eq
        run_eq = run_eq + jnp.sum(eqb, axis=1, keepdims=True)
        mask = gtb + eqb * jnp.where(rank_eq < need, 1.0, 0.0)
        pos = _dot(mask.astype(BF16), tri) + run
        st_ref[:, k:k + 1] = run.astype(I32)
        run = run + jnp.sum(mask, axis=1, keepdims=True)
        pos_ref[:, sl] = jnp.where(mask > 0.0, pos, -1.0).astype(I32)
    st_ref[:, s // blk:s // blk + 1] = run.astype(I32)


def _topk(aff_rows, cap, blk=TOPK_BLOCK):
    r, s = aff_rows.shape
    nblk = s // blk
    return pl.pallas_call(
        functools.partial(_topk_kernel, cap=cap, blk=blk),
        grid=(1,),
        in_specs=[pl.BlockSpec((r, s), lambda i: (0, 0))],
        out_specs=[pl.BlockSpec((r, s), lambda i: (0, 0)), pl.BlockSpec((r, nblk + 1), lambda i: (0, 0))],
        out_shape=[jax.ShapeDtypeStruct((r, s), I32), jax.ShapeDtypeStruct((r, nblk + 1), I32)],
        compiler_params=_cparams("arbitrary"),
        name="topk",
    )(aff_rows)


def _floor_rows(x):
    return (x // PACKED_ROWS) * PACKED_ROWS


def _num_passes(starts, ends, win):
    need = ends[0] - _floor_rows(starts[0])
    for a, b in zip(starts[1:], ends[1:]):
        need = jnp.maximum(need, b - _floor_rows(a))
    return (need + win - 1) // win


def _pass_window(start, p, win, cap):
    first = _floor_rows(start) + p * win
    return first, pl.multiple_of(jnp.minimum(first, cap - win), PACKED_ROWS)


def _gather_kernel(st_ref, h_ref, pos_ref, aff_ref, x_ref, v_ref, *, nt, tt, win, cap, nbatch, ng):
    rows = [((pl.program_id(1) * ng + g) * nbatch + pl.program_id(0)) * (nt + 1) for g in range(ng)]
    x_ref[...] = jnp.zeros_like(x_ref)
    v_ref[...] = jnp.zeros_like(v_ref)
    riota = lax.broadcasted_iota(I32, (win, tt), 0)

    def bounds(t):
        return [st_ref[r + t] for r in rows], [st_ref[r + t + 1] for r in rows]

    def one_pass(t, p, starts):
        hrows = h_ref[0, pl.ds(pl.multiple_of(t * tt, tt), tt), :]
        hots, wins = [], []
        for g in range(ng):
            first, ws = _pass_window(starts[g], p, win, cap)
            pos = pos_ref[g, pl.ds(t, 1), :]
            hots.append((pos - ws == riota) & (pos >= first))
            wins.append(ws)
        onehot = jnp.concatenate([jnp.where(h, 1.0, 0.0).astype(BF16) for h in hots], axis=0)
        rows = _dot(onehot, hrows)
        for g in range(ng):
            x_ref[0, g, pl.ds(wins[g], win), :] += rows[g * win:(g + 1) * win].astype(BF16)
            vals = jnp.sum(jnp.where(hots[g], aff_ref[g, pl.ds(t, 1), :], 0.0), axis=1, keepdims=True)
            v_ref[0, g, pl.ds(wins[g], win), :] += vals

    def first_pass(t, most):
        starts, ends = bounds(t)
        one_pass(t, 0, starts)
        return jnp.maximum(most, _num_passes(starts, ends, win))

    def more_passes(t, carry):
        starts, ends = bounds(t)

        def body(p, c):
            one_pass(t, p, starts)
            return c

        lax.fori_loop(1, _num_passes(starts, ends, win), body, 0)
        return carry

    most = lax.fori_loop(0, nt, first_pass, jnp.int32(0), unroll=GATHER_UNROLL)

    @pl.when(most > 1)
    def _():
        lax.fori_loop(0, nt, more_passes, 0)


def _gather(starts, h3, pos3, aff3t, cap, ne):
    b, s, d = h3.shape
    tt, win, ng = GATHER_TILE, GATHER_WINDOW, GATHER_EXPERTS
    nt = s // tt
    grid_spec = pltpu.PrefetchScalarGridSpec(
        num_scalar_prefetch=1,
        grid=(b, ne // ng),
        in_specs=[
            pl.BlockSpec((1, s, d), lambda i, e, st: (i, 0, 0), pipeline_mode=pl.Buffered(1)),
            pl.BlockSpec((ng, nt, tt), lambda i, e, st: (e, i, 0)),
            pl.BlockSpec((ng, nt, tt), lambda i, e, st: (e, i, 0)),
        ],
        out_specs=[
            pl.BlockSpec((1, ng, cap, d), lambda i, e, st: (i, e, 0, 0)),
            pl.BlockSpec((1, ng, cap, 1), lambda i, e, st: (i, e, 0, 0)),
        ],
    )
    return pl.pallas_call(
        functools.partial(_gather_kernel, nt=nt, tt=tt, win=win, cap=cap, nbatch=b, ng=ng),
        grid_spec=grid_spec,
        out_shape=[jax.ShapeDtypeStruct((b, ne, cap, d), BF16), jax.ShapeDtypeStruct((b, ne, cap, 1), F32)],
        compiler_params=_cparams("arbitrary", "arbitrary"),
        name="gather",
    )(starts, h3, pos3, aff3t)


def _ffn_kernel(x_ref, v_ref, wg_ref, wu_ref, wd_ref, y_ref, acc_ref, *, nf):
    f = pl.program_id(2)
    nb, _, cap, d = x_ref.shape

    @pl.when(f == 0)
    def _():
        acc_ref[...] = jnp.zeros_like(acc_ref)

    wg = wg_ref[0].astype(BF16)
    wu = wu_ref[0].astype(BF16)
    wd = wd_ref[0].astype(BF16)
    mr = min(FFN_ROWS, cap)
    for r in range(nb * cap // mr):
        x = x_ref[r * mr // cap, 0, pl.ds(r * mr % cap, mr), :]
        g = _dot(x, wg)
        u = _dot(x, wu)
        hid = (g * jax.nn.sigmoid(g) * u).astype(BF16)
        acc_ref[pl.ds(r * mr, mr), :] += _dot(hid, wd)

    @pl.when(f == nf - 1)
    def _():
        y = acc_ref[...] * v_ref[...].reshape(nb * cap, 1)
        y_ref[...] = y.astype(BF16).reshape(nb, 1, cap, d)


def _ffn(xg, vals, wg, wu, wd, nb, tf):
    b, ne, cap, d = xg.shape
    dexp = wg.shape[2]
    nf = dexp // tf
    return pl.pallas_call(
        functools.partial(_ffn_kernel, nf=nf),
        grid=(ne, b // nb, nf),
        in_specs=[
            pl.BlockSpec((nb, 1, cap, d), lambda e, i, f: (i, e, 0, 0)),
            pl.BlockSpec((nb, 1, cap, 1), lambda e, i, f: (i, e, 0, 0)),
            pl.BlockSpec((1, d, tf), lambda e, i, f: (e, 0, f)),
            pl.BlockSpec((1, d, tf), lambda e, i, f: (e, 0, f)),
            pl.BlockSpec((1, tf, d), lambda e, i, f: (e, f, 0)),
        ],
        out_specs=pl.BlockSpec((nb, 1, cap, d), lambda e, i, f: (i, e, 0, 0)),
        out_shape=jax.ShapeDtypeStruct(xg.shape, BF16),
        scratch_shapes=[pltpu.VMEM((nb * cap, d), F32)],
        compiler_params=_cparams("parallel", "parallel", "arbitrary"),
        name="ffn",
    )(xg, vals, wg, wu, wd)


def _combine_kernel(st_ref, y_ref, pos_ref, spread_ref, lanes_ref, x1_ref, p_ref, wpp_ref, wpg_ref, gple_ref,
                    gout_ref, o_ref, *, nt, tt, win, cap, ne, nbatch, final):
    t = pl.program_id(1)
    rows = [(e * nbatch + pl.program_id(0)) * (nt + 1) + t for e in range(ne)]
    starts = [st_ref[r] for r in rows]
    ends = [st_ref[r + 1] for r in rows]
    k = ne * win
    lane_e = lanes_ref[0:1, :]
    lane_r = lanes_ref[1:2, :]
    pos = pos_ref[...].astype(F32)
    pos = jnp.concatenate([pos, jnp.zeros((LANES - ne, tt), F32)], axis=0).T
    hi = jnp.floor(pos * (1.0 / COMBINE_SPLIT))
    lo = pos - hi * COMBINE_SPLIT
    rank = _dot(jnp.concatenate([hi, lo], axis=1).astype(BF16), spread_ref[...])

    def expert_rows(p):
        firsts = jnp.zeros((1, k), F32)
        offs = jnp.zeros((1, k), F32)
        wins = []
        for e in range(ne):
            first, ws = _pass_window(starts[e], p, win, cap)
            firsts = jnp.where(lane_e == e, first.astype(F32), firsts)
            offs = jnp.where(lane_e == e, ws.astype(F32), offs)
            wins.append(y_ref[0, e, pl.ds(ws, win), :])
        onehot = jnp.where((rank - offs == lane_r) & (rank >= firsts), 1.0, 0.0).astype(BF16)
        return _dot(onehot, jnp.concatenate(wins, axis=0))

    def finish(acc):
        for r in range(tt // LANES):
            rows = pl.ds(r * LANES, LANES)
            x2 = acc[r * LANES:(r + 1) * LANES]
            emb = _dot(p_ref[0, rows, :].astype(BF16), wpp_ref[...])
            gate = jax.nn.sigmoid(_dot(_rms(x2, gple_ref[...]).astype(BF16), wpg_ref[...]))
            x3 = x2 + gate * emb
            o_ref[0, rows, :] = _rms(x3, gout_ref[...]) if final else x3

    first_pass = x1_ref[0] + expert_rows(0)
    finish(first_pass)
    npass = _num_passes(starts, ends, win)

    @pl.when(npass > 1)
    def _():
        finish(lax.fori_loop(1, npass, lambda p, acc: acc + expert_rows(p), x1_ref[0] + expert_rows(0)))


def _combine(starts, yg, post, x13, p3, wpp, wpg, gple, gout, cap, final):
    b, ne, _, d = yg.shape
    s = x13.shape[1]
    tt, win = COMBINE_TILE, COMBINE_WINDOW
    nt = s // tt
    lane = jnp.arange(ne * win, dtype=I32)
    part = jnp.arange(2 * LANES, dtype=I32)[:, None]
    spread = jnp.where(part == lane // win, float(COMBINE_SPLIT), jnp.where(part == lane // win + LANES, 1.0, 0.0))
    spread = spread.astype(BF16)
    lanes = jnp.stack([lane // win, lane % win]).astype(F32)
    tile = lambda a: pl.BlockSpec((1, tt, a.shape[2]), lambda i, t, st: (i, t, 0))
    full = lambda a: pl.BlockSpec(a.shape, lambda i, t, st: (0,) * a.ndim)
    grid_spec = pltpu.PrefetchScalarGridSpec(
        num_scalar_prefetch=1,
        grid=(b, nt),
        in_specs=[
            pl.BlockSpec((1, ne, cap, d), lambda i, t, st: (i, 0, 0, 0), pipeline_mode=pl.Buffered(1)),
            pl.BlockSpec((ne, tt), lambda i, t, st: (0, i * nt + t)),
            full(spread), full(lanes), tile(x13), tile(p3), full(wpp), full(wpg), full(gple), full(gout),
        ],
        out_specs=pl.BlockSpec((1, tt, d), lambda i, t, st: (i, t, 0)),
    )
    return pl.pallas_call(
        functools.partial(_combine_kernel, nt=nt, tt=tt, win=win, cap=cap, ne=ne, nbatch=b, final=final),
        grid_spec=grid_spec,
        out_shape=jax.ShapeDtypeStruct(x13.shape, F32),
        compiler_params=_cparams("arbitrary", "arbitrary"),
        name="combine",
    )(starts, yg, post, spread, lanes, x13, p3, wpp, wpg, gple, gout)


def kernel(x, p, g_mix, w_in, w_fourier, ssm_a_re, ssm_a_im, ssm_log_dt, ssm_b_re, ssm_b_im, ssm_c_re, ssm_c_im, ssm_d, w_glu, w_out, g_ffn, w_router, w_exp_gate, w_exp_up, w_exp_down, g_ple, w_ple_gate, w_ple_proj, g_final):
    b, s, d = x.shape
    depth = p.shape[0]
    df = w_fourier.shape[1]
    ds = w_glu.shape[1]
    ne = w_router.shape[2]
    dexp = w_exp_gate.shape[3]
    groups, gi = ssm_b_re.shape[2], ssm_b_re.shape[4]
    L = SSM_CHUNK
    n1 = s // DFT_N2
    nc = s // L
    cap = EC_CAPACITY * s // ne
    assert s % (DFT_N2 * SUBLANES) == 0 and s % GATHER_TILE == 0 and cap % PACKED_ROWS == 0
    assert nc % SSM_TILE_CHUNKS == 0 and gi == PACKED_ROWS and cap // COMBINE_SPLIT < 256
    assert 2 * ssm_a_re.shape[3] == LANES and ds % LANES == 0
    assert cap >= COMBINE_WINDOW and ne % GATHER_EXPERTS == 0 and GATHER_TILE == COMBINE_TILE
    assert ds == groups * gi and df % FOURIER_GROUPS == 0

    bd = _channel_dft_table(df)
    ck, sk = _dft1_tables(n1)
    tr, ti = _dft3_tables(n1)
    tf = 256 if dexp % 256 == 0 else dexp
    nb_ffn = 2 if b % 2 == 0 else 1

    xcur = x.reshape(b * s, d)
    for i in range(depth):
        final = i == depth - 1
        row = lambda v: v.astype(F32).reshape(1, -1)
        pq, us, gate = _inproj(xcur, row(g_mix[i]), w_in[i].astype(BF16), bd, df, ds)
        a4 = _dft1(pq.reshape(b, n1, DFT_N2, 2 * df), ck, sk)
        fre = _dft3(a4, tr, ti).reshape(b * s, df)

        tabs = _ssm_tables(ssm_a_re[i], ssm_a_im[i], ssm_log_dt[i], ssm_b_re[i], ssm_b_im[i],
                           ssm_c_re[i], ssm_c_im[i], ssm_d[i])
        ys = _to_tokens(_ssm(_to_groups(us, L, groups, gi), *tabs, nc=nc, nb=b), L, groups, gi)

        wr = w_router[i].astype(F32)
        wr_hi = wr.astype(BF16)
        wr_lo = (wr - wr_hi.astype(F32)).astype(BF16)
        lane_pad = lambda w: jnp.pad(w, ((0, 0), (0, LANES - ne)))
        wr2 = jnp.concatenate([jnp.concatenate([lane_pad(wr_hi), lane_pad(wr_lo)], axis=1),
                               jnp.concatenate([lane_pad(wr_hi), jnp.zeros((d, LANES), BF16)], axis=1)], axis=0)
        x1, h2, aff = _mixout(fre, ys, gate, xcur, w_fourier[i].astype(BF16), w_glu[i].astype(BF16),
                              w_out[i].astype(BF16), row(g_ffn[i]), wr2, ne)

        aff_rows = aff.reshape(ne * b, s)
        posm, st_blk = _topk(aff_rows, cap)
        starts = st_blk[:, ::GATHER_TILE // TOPK_BLOCK].reshape(-1)
        tiles = (ne, b * (s // GATHER_TILE), GATHER_TILE)
        xg, vals = _gather(starts, h2.reshape(b, s, d), posm.reshape(tiles), aff_rows.reshape(tiles), cap, ne)
        yg = _ffn(xg, vals, w_exp_gate[i], w_exp_up[i], w_exp_down[i], nb_ffn, tf)
        xnext = _combine(starts, yg, posm.reshape(ne, b * s), x1.reshape(b, s, d), p[i],
                         w_ple_proj[i].astype(BF16), w_ple_gate[i].astype(BF16), row(g_ple[i]),
                         row(g_final) if final else row(g_ple[i]), cap, final)
        xcur = xnext.reshape(b * s, d)
    return xcur.reshape(b, s, d)
```

```python
import functools
import math

import jax
import jax.numpy as jnp
from jax import lax
from jax.experimental import pallas as pl
from jax.experimental.pallas import tpu as pltpu

F32 = jnp.float32
BF16 = jnp.bfloat16
I32 = jnp.int32

RMS_EPS = 1e-6
FOURIER_GROUPS = 4
EC_CAPACITY = 2
DFT_N2 = 128
SUBLANES = 8
LANES = 128
PACKED_ROWS = 16
SSM_CHUNK = 32
SSM_TILE_CHUNKS = 128
TOPK_BLOCK = 128
GATHER_TILE = 256
MIX_ROWS = 256
FFN_ROWS = 512
GATHER_WINDOW = 64
GATHER_EXPERTS = 4
GATHER_UNROLL = 4
COMBINE_TILE = 256
COMBINE_WINDOW = 64
COMBINE_SPLIT = 32
VMEM_LIMIT = 56 * 1024 * 1024


def _cparams(*sem):
    return pltpu.CompilerParams(dimension_semantics=sem, vmem_limit_bytes=VMEM_LIMIT)


def _rms(x, g):
    return x * lax.rsqrt(jnp.mean(x * x, axis=-1, keepdims=True) + RMS_EPS) * g


def _dot(a, b):
    return jnp.dot(a, b, preferred_element_type=F32)


def _inproj_kernel(x_ref, g_ref, w_ref, bd_ref, pq_ref, us_ref, gate_ref, *, df, ds):
    h = _rms(x_ref[...], g_ref[...]).astype(BF16)
    z = _dot(h, w_ref[...])
    pq_ref[...] = _dot(z[:, :df].astype(BF16), bd_ref[...])
    for q in range(ds // LANES):
        us_ref[q] = z[:, df + q * LANES:df + (q + 1) * LANES]
    gate_ref[...] = jax.nn.sigmoid(z[:, df + ds:]).astype(BF16)


def _inproj(x2, g, w_in, bd, df, ds, tm=512):
    t, d = x2.shape
    dg = w_in.shape[1] - df - ds
    return pl.pallas_call(
        functools.partial(_inproj_kernel, df=df, ds=ds),
        grid=(t // tm,),
        in_specs=[
            pl.BlockSpec((tm, d), lambda i: (i, 0)),
            pl.BlockSpec((1, d), lambda i: (0, 0)),
            pl.BlockSpec(w_in.shape, lambda i: (0, 0)),
            pl.BlockSpec(bd.shape, lambda i: (0, 0)),
        ],
        out_specs=[
            pl.BlockSpec((tm, 2 * df), lambda i: (i, 0)),
            pl.BlockSpec((ds // LANES, tm, LANES), lambda i: (0, i, 0)),
            pl.BlockSpec((tm, dg), lambda i: (i, 0)),
        ],
        out_shape=[
            jax.ShapeDtypeStruct((t, 2 * df), F32),
            jax.ShapeDtypeStruct((ds // LANES, t, LANES), F32),
            jax.ShapeDtypeStruct((t, dg), BF16),
        ],
        compiler_params=_cparams("parallel"),
        name="inproj",
    )(x2, g, w_in, bd)


def _channel_dft_table(df):
    c = df // FOURIER_GROUPS
    k = jnp.arange(c, dtype=I32)
    ang = (2.0 * math.pi / c) * ((k[:, None] * k[None, :]) % c).astype(F32)
    eye = jnp.eye(FOURIER_GROUPS, dtype=F32)
    scale = 1.0 / math.sqrt(c)
    re = jnp.kron(eye, jnp.cos(ang)) * scale
    im = -jnp.kron(eye, jnp.sin(ang)) * scale
    return jnp.concatenate([re, im], axis=1).astype(BF16)


def _dft1_tables(n1):
    k = jnp.arange(n1, dtype=I32)
    ang = (2.0 * math.pi / n1) * ((k[:, None] * k[None, :]) % n1).astype(F32)
    eye = jnp.eye(SUBLANES, dtype=F32)
    scale = 1.0 / math.sqrt(n1)
    return (jnp.kron(jnp.cos(ang), eye) * scale).astype(BF16), (jnp.kron(jnp.sin(ang), eye) * scale).astype(BF16)


def _dft3_tables(n1):
    s = n1 * DFT_N2
    nb = n1 // SUBLANES
    k1 = jnp.arange(n1, dtype=I32)[:, None, None]
    k2 = jnp.arange(DFT_N2, dtype=I32)[None, :, None]
    n2 = jnp.arange(DFT_N2, dtype=I32)[None, None, :]
    ang = (2.0 * math.pi / s) * ((n2 * (k1 + n1 * k2)) % s).astype(F32)
    eye = jnp.eye(SUBLANES, dtype=BF16)[None, None, :, :, None]
    rows = DFT_N2 * SUBLANES

    def expand(t):
        t = (t * (1.0 / math.sqrt(DFT_N2))).astype(BF16).reshape(nb, SUBLANES, DFT_N2, DFT_N2)
        t = jnp.transpose(t, (0, 2, 1, 3))[:, :, :, None, :]
        return (t * eye).reshape(nb, rows, rows)

    return expand(jnp.cos(ang)), expand(jnp.sin(ang))


def _dft1_kernel(z_ref, ck_ref, sk_ref, a_ref):
    n1, c2 = z_ref.shape[1], z_ref.shape[3]
    c = c2 // 2
    z = z_ref[0].reshape(n1 * SUBLANES, c2).astype(BF16)
    cz = _dot(ck_ref[...], z)
    sz = _dot(sk_ref[...], z)
    a_ref[0, :, :, :c] = (cz[:, :c] + sz[:, c:]).reshape(n1, SUBLANES, c)
    a_ref[0, :, :, c:] = (cz[:, c:] - sz[:, :c]).reshape(n1, SUBLANES, c)


def _dft1(pq4, ck, sk):
    b, n1, n2, c2 = pq4.shape
    return pl.pallas_call(
        _dft1_kernel,
        grid=(b, n2 // SUBLANES),
        in_specs=[
            pl.BlockSpec((1, n1, SUBLANES, c2), lambda i, j: (i, 0, j, 0)),
            pl.BlockSpec(ck.shape, lambda i, j: (0, 0)),
            pl.BlockSpec(sk.shape, lambda i, j: (0, 0)),
        ],
        out_specs=pl.BlockSpec((1, n1, SUBLANES, c2), lambda i, j: (i, 0, j, 0)),
        out_shape=jax.ShapeDtypeStruct(pq4.shape, F32),
        compiler_params=_cparams("parallel", "parallel"),
        name="dft1",
    )(pq4, ck, sk)


def _dft3_kernel(a_ref, tr_ref, ti_ref, o_ref):
    c2 = a_ref.shape[3]
    c = c2 // 2
    a = a_ref[0].reshape(SUBLANES * DFT_N2, c2).astype(BF16)
    out = _dot(tr_ref[0], a[:, :c]) + _dot(ti_ref[0], a[:, c:])
    o_ref[0] = out.reshape(DFT_N2, SUBLANES, c)


def _dft3(a4, tr, ti):
    b, n1, n2, c2 = a4.shape
    c = c2 // 2
    rows = DFT_N2 * SUBLANES
    return pl.pallas_call(
        _dft3_kernel,
        grid=(n1 // SUBLANES, b),
        in_specs=[
            pl.BlockSpec((1, SUBLANES, n2, c2), lambda k, i: (i, k, 0, 0)),
            pl.BlockSpec((1, rows, rows), lambda k, i: (k, 0, 0)),
            pl.BlockSpec((1, rows, rows), lambda k, i: (k, 0, 0)),
        ],
        out_specs=pl.BlockSpec((1, DFT_N2, SUBLANES, c), lambda k, i: (i, 0, k, 0)),
        out_shape=jax.ShapeDtypeStruct((b, DFT_N2, n1, c), F32),
        compiler_params=_cparams("parallel", "parallel"),
        name="dft3",
    )(a4, tr, ti)


def _ssm_tables(a_re, a_im, log_dt, b_re, b_im, c_re, c_im, d):
    L = SSM_CHUNK
    _, g, n = a_re.shape
    gi = b_re.shape[-1]
    dt = jnp.exp(log_dt.astype(F32))[..., None]
    ar, ai = a_re.astype(F32), a_im.astype(F32)
    tau = jnp.arange(L + 1, dtype=F32)[:, None, None, None]
    mag = jnp.exp(tau * (ar * dt)[None])
    ang = tau * (ai * dt)[None]
    pr, pi = mag * jnp.cos(ang), mag * jnp.sin(ang)
    nr, ni = pr[1] - 1.0, pi[1]
    den = ar * ar + ai * ai
    qr, qi = (nr * ar + ni * ai) / den, (ni * ar - nr * ai) / den
    br, bi = b_re.astype(F32), b_im.astype(F32)
    bbr = qr[..., None] * br - qi[..., None] * bi
    bbi = qr[..., None] * bi + qi[..., None] * br
    cr, ci = c_re.astype(F32), c_im.astype(F32)

    cat = jnp.concatenate
    steps = lambda p: jnp.transpose(p, (1, 0, 2))
    pf_r, pf_i, pb_r, pb_i = steps(pr[1:, 0]), steps(pi[1:, 0]), steps(pr[1:, 1][::-1]), steps(pi[1:, 1][::-1])
    f_small = jnp.stack([cat([cr[0], cr[1], cr[0], cr[1]], -1), cat([-ci[0], -ci[1], -ci[0], -ci[1]], -1)], 1)
    f_steps = jnp.stack([cat([pf_r, pb_r, -pf_i, -pb_i], -1), cat([pf_i, pb_i, pf_r, pb_r], -1)], 1)
    last = lambda p: jnp.transpose(p, (1, 2, 0))
    ef_r, ef_i, eb_r, eb_i = last(pr[:L, 0][::-1]), last(pi[:L, 0][::-1]), last(pr[:L, 1]), last(pi[:L, 1])
    e_small = jnp.stack([cat([bbr[0], bbr[1], bbi[0], bbi[1]], 1), cat([-bbi[0], -bbi[1], bbr[0], bbr[1]], 1)], 1)
    e_steps = jnp.stack([cat([ef_r, eb_r, ef_r, eb_r], 1), cat([ef_i, eb_i, ef_i, eb_i], 1)], 1)
    zpad = lambda p, before: jnp.pad(p, ((0, 0), (0, 0), (before, L - before)))
    wf_r, wf_i, wb_r, wb_i = zpad(ef_r, 0), zpad(ef_i, 0), zpad(eb_r, L - 1), zpad(eb_i, L - 1)
    t_small = jnp.stack([cat([bbr[0], bbi[0], bbr[1], bbi[1]], 1), cat([-bbi[0], bbr[0], -bbi[1], bbr[1]], 1)], 1)
    t_steps = jnp.stack([cat([wf_r, wf_r, wb_r, wb_r], 1), cat([wf_i, wf_i, wb_i, wb_i], 1)], 1)
    t_left = cat([cr[0], -ci[0], cr[1], -ci[1]], -1)

    al = jnp.stack([cat([pr[L, 0], pr[L, 1]], -1), cat([pi[L, 0], pi[L, 1]], -1)], axis=1)
    dv = jnp.tile(d.astype(F32).reshape(g, 1, gi), (1, L, 1)).reshape(g, L * gi, 1)
    t_tab, e_tab, f_tab = _ssm_table_call(f_small, f_steps, e_small, e_steps, t_small, t_steps, t_left, L, gi)
    return t_tab, e_tab, f_tab, al, dv


def _ssm_table_kernel(fs_ref, fp_ref, es_ref, ep_ref, ts_ref, tp_ref, tl_ref, ri_ref, rj_ref, rit_ref, rjt_ref,
                      ri2_ref, rs_ref, t_ref, e_ref, f_ref, *, L, gi):
    def split(a):
        hi = a.astype(BF16)
        return hi, (a - hi.astype(F32)).astype(BF16)

    def spread_lanes(a, rep):
        hi, lo = split(a)
        return _dot(hi, rep) + _dot(lo, rep)

    def spread_rows(rep, a):
        hi, lo = split(a)
        return _dot(rep, hi) + _dot(rep, lo)

    ri, rj, rit, rjt, ri2, rs = ri_ref[...], rj_ref[...], rit_ref[...], rjt_ref[...], ri2_ref[...], rs_ref[...]
    f_ref[0] = (spread_rows(rit, fs_ref[0, 0]) * spread_rows(rjt, fp_ref[0, 0])
                + spread_rows(rit, fs_ref[0, 1]) * spread_rows(rjt, fp_ref[0, 1])).astype(BF16)
    e_ref[0] = (spread_lanes(es_ref[0, 0], ri) * spread_lanes(ep_ref[0, 0], rj)
                + spread_lanes(es_ref[0, 1], ri) * spread_lanes(ep_ref[0, 1], rj)).astype(BF16)
    ew = (spread_lanes(ts_ref[0, 0], ri2) * spread_lanes(tp_ref[0, 0], rs)
          + spread_lanes(ts_ref[0, 1], ri2) * spread_lanes(tp_ref[0, 1], rs))
    c_hi, c_lo = split(tl_ref[0])
    e_hi, e_lo = split(ew)
    kw = _dot(c_hi, e_hi) + _dot(c_hi, e_lo) + _dot(c_lo, e_hi)
    per = LANES // gi
    for r in range(per):
        shifted = kw if r == 0 else pltpu.roll(kw, 2 * L * gi - r * gi, axis=1)
        for a in range(L // per):
            jo = L - 1 - (a * per + r)
            t_ref[0, pl.ds(jo * gi, gi), :] = shifted[:, a * LANES:a * LANES + L * gi].astype(BF16)


def _ssm_table_call(f_small, f_steps, e_small, e_steps, t_small, t_steps, t_left, L, gi):
    g = f_small.shape[0]
    n4 = f_small.shape[3]
    k = L * gi
    lane = jnp.arange(2 * k, dtype=I32)
    ri2 = (lane[None, :] % gi == jnp.arange(gi, dtype=I32)[:, None]).astype(BF16)
    rs = (lane[None, :] // gi == jnp.arange(2 * L, dtype=I32)[:, None]).astype(BF16)
    ri, rj = ri2[:, :k], rs[:L, :k]
    per_group = lambda a: pl.BlockSpec((1,) + a.shape[1:], lambda i: (i,) + (0,) * (a.ndim - 1))
    full = lambda a: pl.BlockSpec(a.shape, lambda i: (0,) * a.ndim)
    ins = (f_small, f_steps, e_small, e_steps, t_small, t_steps, t_left)
    reps = (ri, rj, ri.T, rj.T, ri2, rs)
    return pl.pallas_call(
        functools.partial(_ssm_table_kernel, L=L, gi=gi),
        grid=(g,),
        in_specs=[per_group(a) for a in ins] + [full(a) for a in reps],
        out_specs=[pl.BlockSpec((1, k, k), lambda i: (i, 0, 0)), pl.BlockSpec((1, n4, k), lambda i: (i, 0, 0)),
                   pl.BlockSpec((1, k, n4), lambda i: (i, 0, 0))],
        out_shape=[jax.ShapeDtypeStruct((g, k, k), BF16), jax.ShapeDtypeStruct((g, n4, k), BF16),
                   jax.ShapeDtypeStruct((g, k, n4), BF16)],
        compiler_params=_cparams("parallel"),
        name="ssm_tables",
    )(*ins, *reps)


def _to_groups_kernel(u_ref, a_ref, *, L, groups, gi):
    ncl = a_ref.shape[2]
    gq = LANES // gi
    for q in range(u_ref.shape[0]):
        for j in range(L):
            zt = u_ref[q, pl.ds(j, ncl, stride=L), :].T
            a_ref[q * gq:(q + 1) * gq, pl.ds(j * gi, gi), :] = zt.reshape(gq, gi, ncl).astype(BF16)


def _to_groups(us, L, groups, gi):
    nq, t, _ = us.shape
    ncl = SSM_TILE_CHUNKS
    return pl.pallas_call(
        functools.partial(_to_groups_kernel, L=L, groups=groups, gi=gi),
        grid=(t // (ncl * L),),
        in_specs=[pl.BlockSpec((nq, ncl * L, LANES), lambda i: (0, i, 0))],
        out_specs=pl.BlockSpec((groups, L * gi, ncl), lambda i: (0, 0, i)),
        out_shape=jax.ShapeDtypeStruct((groups, L * gi, t // L), BF16),
        compiler_params=_cparams("parallel"),
        name="to_groups",
    )(us)


def _to_tokens_kernel(y_ref, o_ref, *, L, groups, gi):
    ncl = y_ref.shape[2]
    gq = LANES // gi
    for q in range(o_ref.shape[0]):
        for j in range(L):
            yj = y_ref[q * gq:(q + 1) * gq, pl.ds(j * gi, gi), :].astype(F32).reshape(LANES, ncl)
            o_ref[q, pl.ds(j, ncl, stride=L), :] = yj.T


def _to_tokens(yt, L, groups, gi):
    _, k, nchunks = yt.shape
    ncl = SSM_TILE_CHUNKS
    return pl.pallas_call(
        functools.partial(_to_tokens_kernel, L=L, groups=groups, gi=gi),
        grid=(nchunks // ncl,),
        in_specs=[pl.BlockSpec((groups, k, ncl), lambda i: (0, 0, i))],
        out_specs=pl.BlockSpec((groups * gi // LANES, ncl * L, LANES), lambda i: (0, i, 0)),
        out_shape=jax.ShapeDtypeStruct((groups * gi // LANES, nchunks * L, LANES), F32),
        compiler_params=_cparams("parallel"),
        name="to_tokens",
    )(yt)


def _ssm_kernel(a_ref, t_ref, e_ref, f_ref, al_ref, dv_ref, y_ref, s_scr, h_scr, *, nc, nb, rows, n):
    a = a_ref[0]
    y1 = _dot(t_ref[0], a) + dv_ref[0] * a.astype(F32)
    st = _dot(e_ref[0], a)
    n2 = 2 * n
    s_scr[...] = jnp.zeros_like(s_scr)
    for b in range(nb):
        sb_t = st[:, b * nc:(b + 1) * nc].T
        for q in range(2):
            s_scr[q, pl.ds(b, nc, stride=rows), :] = sb_t[:, q * n2:(q + 1) * n2]
    ar = al_ref[0, 0:1, :]
    ai = al_ref[0, 1:2, :]
    is_fwd = lax.broadcasted_iota(I32, (rows, n2), 1) < n

    def step(i, carry):
        hr, hi = carry
        rf = pl.ds(pl.multiple_of(i * rows, rows), rows)
        rb = pl.ds(pl.multiple_of((nc - 1 - i) * rows, rows), rows)
        h_scr[0, rf, 0:n] = hr[:, 0:n]
        h_scr[1, rf, 0:n] = hi[:, 0:n]
        h_scr[0, rb, n:n2] = hr[:, n:n2]
        h_scr[1, rb, n:n2] = hi[:, n:n2]
        sr = jnp.where(is_fwd, s_scr[0, rf, :], s_scr[0, rb, :])
        si = jnp.where(is_fwd, s_scr[1, rf, :], s_scr[1, rb, :])
        return ar * hr - ai * hi + sr, ar * hi + ai * hr + si

    zero = jnp.zeros((rows, n2), F32)
    lax.fori_loop(0, nc, step, (zero, zero))
    ht = jnp.concatenate(
        [jnp.concatenate([h_scr[q, pl.ds(b, nc, stride=rows), :].T for q in range(2)], axis=0) for b in range(nb)],
        axis=1)
    y = y1 + _dot(f_ref[0], ht.astype(BF16))
    y_ref[0] = jax.nn.gelu(y, approximate=True).astype(BF16)


def _ssm(ag, t_tab, e_tab, f_tab, al, dv, nc, nb):
    g, k, m = ag.shape
    n4 = e_tab.shape[1]
    rows = -(-nb // SUBLANES) * SUBLANES
    return pl.pallas_call(
        functools.partial(_ssm_kernel, nc=nc, nb=nb, rows=rows, n=n4 // 4),
        grid=(g,),
        in_specs=[
            pl.BlockSpec((1, k, m), lambda i: (i, 0, 0)),
            pl.BlockSpec((1, k, k), lambda i: (i, 0, 0)),
            pl.BlockSpec((1, n4, k), lambda i: (i, 0, 0)),
            pl.BlockSpec((1, k, n4), lambda i: (i, 0, 0)),
            pl.BlockSpec((1, 2, n4 // 2), lambda i: (i, 0, 0)),
            pl.BlockSpec((1, k, 1), lambda i: (i, 0, 0)),
        ],
        out_specs=pl.BlockSpec((1, k, m), lambda i: (i, 0, 0)),
        out_shape=jax.ShapeDtypeStruct((g, k, m), BF16),
        scratch_shapes=[pltpu.VMEM((2, nc * rows, n4 // 2), F32), pltpu.VMEM((2, nc * rows, n4 // 2), F32)],
        compiler_params=_cparams("parallel"),
        name="ssm",
    )(ag, t_tab, e_tab, f_tab, al, dv)


def _mixout_kernel(fre_ref, ys_ref, gate_ref, x_ref, wf_ref, wglu_ref, wout_ref, gffn_ref, wr_ref,
                   x1_ref, h2_ref, aff_ref, *, d, ne):
    for r in range(fre_ref.shape[0] // MIX_ROWS):
        rows = pl.ds(r * MIX_ROWS, MIX_ROWS)
        y_f = _dot(fre_ref[rows, :].astype(BF16), wf_ref[...])
        ys = jnp.concatenate([ys_ref[q, rows, :] for q in range(ys_ref.shape[0])], axis=1)
        vg = _dot(ys.astype(BF16), wglu_ref[...])
        y_s = vg[:, :d] * jax.nn.sigmoid(vg[:, d:])
        gate = gate_ref[rows, :].astype(F32)
        m = gate[:, :d] * y_f + gate[:, d:] * y_s
        x1 = x_ref[rows, :] + _dot(m.astype(BF16), wout_ref[...])
        x1_ref[rows, :] = x1
        h2 = _rms(x1, gffn_ref[...])
        h2_ref[rows, :] = h2.astype(BF16)
        hi = h2.astype(BF16)
        lo = (h2 - hi.astype(F32)).astype(BF16)
        rl = _dot(jnp.concatenate([hi, lo], axis=1), wr_ref[...])
        logits = rl[:, :LANES] + rl[:, LANES:]
        logits = jnp.where(lax.broadcasted_iota(I32, logits.shape, 1) < ne, logits, -1e30)
        logits = logits - jnp.max(logits, axis=-1, keepdims=True)
        p = jnp.exp(logits)
        aff = p / jnp.sum(p, axis=-1, keepdims=True)
        aff_ref[:, rows] = aff.T[:ne, :]


def _mixout(fre, ys, gate, x2, wf, wglu, wout, gffn, wr2, ne, tm=512):
    t, d = x2.shape
    full = lambda a: pl.BlockSpec(a.shape, lambda i: (0,) * a.ndim)
    row = lambda a: pl.BlockSpec((tm, a.shape[1]), lambda i: (i, 0))
    return pl.pallas_call(
        functools.partial(_mixout_kernel, d=d, ne=ne),
        grid=(t // tm,),
        in_specs=[row(fre), pl.BlockSpec((ys.shape[0], tm, LANES), lambda i: (0, i, 0)), row(gate), row(x2),
                  full(wf), full(wglu), full(wout), full(gffn), full(wr2)],
        out_specs=[
            pl.BlockSpec((tm, d), lambda i: (i, 0)),
            pl.BlockSpec((tm, d), lambda i: (i, 0)),
            pl.BlockSpec((ne, tm), lambda i: (0, i)),
        ],
        out_shape=[
            jax.ShapeDtypeStruct((t, d), F32),
            jax.ShapeDtypeStruct((t, d), BF16),
            jax.ShapeDtypeStruct((ne, t), F32),
        ],
        compiler_params=_cparams("parallel"),
        name="mixout",
    )(fre, ys, gate, x2, wf, wglu, wout, gffn, wr2)


def _topk_kernel(aff_ref, pos_ref, st_ref, *, cap, blk):
    v = aff_ref[...]
    r, s = v.shape
    capf = float(cap)

    def bit_step(i, t):
        cand = t | (jnp.int32(1) << (30 - i))
        cnt = jnp.sum(jnp.where(v >= pltpu.bitcast(cand, F32), 1.0, 0.0), axis=1, keepdims=True)
        return jnp.where(cnt >= capf, cand, t)

    thr = lax.fori_loop(0, 31, bit_step, jnp.zeros((r, 1), I32))
    gt = jnp.where(v >= pltpu.bitcast(thr + 1, F32), 1.0, 0.0)
    eq = jnp.where(v >= pltpu.bitcast(thr, F32), 1.0, 0.0) - gt
    need = capf - jnp.sum(gt, axis=1, keepdims=True)
    ii = lax.broadcasted_iota(I32, (blk, blk), 0)
    jj = lax.broadcasted_iota(I32, (blk, blk), 1)
    tri = jnp.where(ii < jj, 1.0, 0.0).astype(BF16)
    run_eq = jnp.zeros((r, 1), F32)
    run = jnp.zeros((r, 1), F32)
    for k in range(s // blk):
        sl = slice(k * blk, (k + 1) * blk)
        eqb, gtb = eq[:, sl], gt[:, sl]
        rank_eq = _dot(eqb.astype(BF16), tri) + run_eq
        run_eq = run_eq + jnp.sum(eqb, axis=1, keepdims=True)
        mask = gtb + eqb * jnp.where(rank_eq < need, 1.0, 0.0)
        pos = _dot(mask.astype(BF16), tri) + run
        st_ref[:, k:k + 1] = run.astype(I32)
        run = run + jnp.sum(mask, axis=1, keepdims=True)
        pos_ref[:, sl] = jnp.where(mask > 0.0, pos, -1.0).astype(I32)
    st_ref[:, s // blk:s // blk + 1] = run.astype(I32)


def _topk(aff_rows, cap, blk=TOPK_BLOCK):
    r, s = aff_rows.shape
    nblk = s // blk
    return pl.pallas_call(
        functools.partial(_topk_kernel, cap=cap, blk=blk),
        grid=(1,),
        in_specs=[pl.BlockSpec((r, s), lambda i: (0, 0))],
        out_specs=[pl.BlockSpec((r, s), lambda i: (0, 0)), pl.BlockSpec((r, nblk + 1), lambda i: (0, 0))],
        out_shape=[jax.ShapeDtypeStruct((r, s), I32), jax.ShapeDtypeStruct((r, nblk + 1), I32)],
        compiler_params=_cparams("arbitrary"),
        name="topk",
    )(aff_rows)


def _floor_rows(x):
    return (x // PACKED_ROWS) * PACKED_ROWS


def _num_passes(starts, ends, win):
    need = ends[0] - _floor_rows(starts[0])
    for a, b in zip(starts[1:], ends[1:]):
        need = jnp.maximum(need, b - _floor_rows(a))
    return (need + win - 1) // win


def _pass_window(start, p, win, cap):
    first = _floor_rows(start) + p * win
    return first, pl.multiple_of(jnp.minimum(first, cap - win), PACKED_ROWS)


def _gather_kernel(st_ref, h_ref, pos_ref, aff_ref, x_ref, v_ref, *, nt, tt, win, cap, nbatch, ng):
    rows = [((pl.program_id(1) * ng + g) * nbatch + pl.program_id(0)) * (nt + 1) for g in range(ng)]
    x_ref[...] = jnp.zeros_like(x_ref)
    v_ref[...] = jnp.zeros_like(v_ref)
    riota = lax.broadcasted_iota(I32, (win, tt), 0)

    def bounds(t):
        return [st_ref[r + t] for r in rows], [st_ref[r + t + 1] for r in rows]

    def one_pass(t, p, starts):
        hrows = h_ref[0, pl.ds(pl.multiple_of(t * tt, tt), tt), :]
        hots, wins = [], []
        for g in range(ng):
            first, ws = _pass_window(starts[g], p, win, cap)
            pos = pos_ref[g, pl.ds(t, 1), :]
            hots.append((pos - ws == riota) & (pos >= first))
            wins.append(ws)
        onehot = jnp.concatenate([jnp.where(h, 1.0, 0.0).astype(BF16) for h in hots], axis=0)
        rows = _dot(onehot, hrows)
        for g in range(ng):
            x_ref[0, g, pl.ds(wins[g], win), :] += rows[g * win:(g + 1) * win].astype(BF16)
            vals = jnp.sum(jnp.where(hots[g], aff_ref[g, pl.ds(t, 1), :], 0.0), axis=1, keepdims=True)
            v_ref[0, g, pl.ds(wins[g], win), :] += vals

    def first_pass(t, most):
        starts, ends = bounds(t)
        one_pass(t, 0, starts)
        return jnp.maximum(most, _num_passes(starts, ends, win))

    def more_passes(t, carry):
        starts, ends = bounds(t)

        def body(p, c):
            one_pass(t, p, starts)
            return c

        lax.fori_loop(1, _num_passes(starts, ends, win), body, 0)
        return carry

    most = lax.fori_loop(0, nt, first_pass, jnp.int32(0), unroll=GATHER_UNROLL)

    @pl.when(most > 1)
    def _():
        lax.fori_loop(0, nt, more_passes, 0)


def _gather(starts, h3, pos3, aff3t, cap, ne):
    b, s, d = h3.shape
    tt, win, ng = GATHER_TILE, GATHER_WINDOW, GATHER_EXPERTS
    nt = s // tt
    grid_spec = pltpu.PrefetchScalarGridSpec(
        num_scalar_prefetch=1,
        grid=(b, ne // ng),
        in_specs=[
            pl.BlockSpec((1, s, d), lambda i, e, st: (i, 0, 0), pipeline_mode=pl.Buffered(1)),
            pl.BlockSpec((ng, nt, tt), lambda i, e, st: (e, i, 0)),
            pl.BlockSpec((ng, nt, tt), lambda i, e, st: (e, i, 0)),
        ],
        out_specs=[
            pl.BlockSpec((1, ng, cap, d), lambda i, e, st: (i, e, 0, 0)),
            pl.BlockSpec((1, ng, cap, 1), lambda i, e, st: (i, e, 0, 0)),
        ],
    )
    return pl.pallas_call(
        functools.partial(_gather_kernel, nt=nt, tt=tt, win=win, cap=cap, nbatch=b, ng=ng),
        grid_spec=grid_spec,
        out_shape=[jax.ShapeDtypeStruct((b, ne, cap, d), BF16), jax.ShapeDtypeStruct((b, ne, cap, 1), F32)],
        compiler_params=_cparams("arbitrary", "arbitrary"),
        name="gather",
    )(starts, h3, pos3, aff3t)


def _ffn_kernel(x_ref, v_ref, wg_ref, wu_ref, wd_ref, y_ref, acc_ref, *, nf):
    f = pl.program_id(2)
    nb, _, cap, d = x_ref.shape

    @pl.when(f == 0)
    def _():
        acc_ref[...] = jnp.zeros_like(acc_ref)

    wg = wg_ref[0].astype(BF16)
    wu = wu_ref[0].astype(BF16)
    wd = wd_ref[0].astype(BF16)
    mr = min(FFN_ROWS, cap)
    for r in range(nb * cap // mr):
        x = x_ref[r * mr // cap, 0, pl.ds(r * mr % cap, mr), :]
        g = _dot(x, wg)
        u = _dot(x, wu)
        hid = (g * jax.nn.sigmoid(g) * u).astype(BF16)
        acc_ref[pl.ds(r * mr, mr), :] += _dot(hid, wd)

    @pl.when(f == nf - 1)
    def _():
        y = acc_ref[...] * v_ref[...].reshape(nb * cap, 1)
        y_ref[...] = y.astype(BF16).reshape(nb, 1, cap, d)


def _ffn(xg, vals, wg, wu, wd, nb, tf):
    b, ne, cap, d = xg.shape
    dexp = wg.shape[2]
    nf = dexp // tf
    return pl.pallas_call(
        functools.partial(_ffn_kernel, nf=nf),
        grid=(ne, b // nb, nf),
        in_specs=[
            pl.BlockSpec((nb, 1, cap, d), lambda e, i, f: (i, e, 0, 0)),
            pl.BlockSpec((nb, 1, cap, 1), lambda e, i, f: (i, e, 0, 0)),
            pl.BlockSpec((1, d, tf), lambda e, i, f: (e, 0, f)),
            pl.BlockSpec((1, d, tf), lambda e, i, f: (e, 0, f)),
            pl.BlockSpec((1, tf, d), lambda e, i, f: (e, f, 0)),
        ],
        out_specs=pl.BlockSpec((nb, 1, cap, d), lambda e, i, f: (i, e, 0, 0)),
        out_shape=jax.ShapeDtypeStruct(xg.shape, BF16),
        scratch_shapes=[pltpu.VMEM((nb * cap, d), F32)],
        compiler_params=_cparams("parallel", "parallel", "arbitrary"),
        name="ffn",
    )(xg, vals, wg, wu, wd)


def _combine_kernel(st_ref, y_ref, pos_ref, spread_ref, lanes_ref, x1_ref, p_ref, wpp_ref, wpg_ref, gple_ref,
                    gout_ref, o_ref, *, nt, tt, win, cap, ne, nbatch, final):
    t = pl.program_id(1)
    rows = [(e * nbatch + pl.program_id(0)) * (nt + 1) + t for e in range(ne)]
    starts = [st_ref[r] for r in rows]
    ends = [st_ref[r + 1] for r in rows]
    k = ne * win
    lane_e = lanes_ref[0:1, :]
    lane_r = lanes_ref[1:2, :]
    pos = pos_ref[...].astype(F32)
    pos = jnp.concatenate([pos, jnp.zeros((LANES - ne, tt), F32)], axis=0).T
    hi = jnp.floor(pos * (1.0 / COMBINE_SPLIT))
    lo = pos - hi * COMBINE_SPLIT
    rank = _dot(jnp.concatenate([hi, lo], axis=1).astype(BF16), spread_ref[...])

    def expert_rows(p):
        firsts = jnp.zeros((1, k), F32)
        offs = jnp.zeros((1, k), F32)
        wins = []
        for e in range(ne):
            first, ws = _pass_window(starts[e], p, win, cap)
            firsts = jnp.where(lane_e == e, first.astype(F32), firsts)
            offs = jnp.where(lane_e == e, ws.astype(F32), offs)
            wins.append(y_ref[0, e, pl.ds(ws, win), :])
        onehot = jnp.where((rank - offs == lane_r) & (rank >= firsts), 1.0, 0.0).astype(BF16)
        return _dot(onehot, jnp.concatenate(wins, axis=0))

    def finish(acc):
        for r in range(tt // LANES):
            rows = pl.ds(r * LANES, LANES)
            x2 = acc[r * LANES:(r + 1) * LANES]
            emb = _dot(p_ref[0, rows, :].astype(BF16), wpp_ref[...])
            gate = jax.nn.sigmoid(_dot(_rms(x2, gple_ref[...]).astype(BF16), wpg_ref[...]))
            x3 = x2 + gate * emb
            o_ref[0, rows, :] = _rms(x3, gout_ref[...]) if final else x3

    first_pass = x1_ref[0] + expert_rows(0)
    finish(first_pass)
    npass = _num_passes(starts, ends, win)

    @pl.when(npass > 1)
    def _():
        finish(lax.fori_loop(1, npass, lambda p, acc: acc + expert_rows(p), x1_ref[0] + expert_rows(0)))


def _combine(starts, yg, post, x13, p3, wpp, wpg, gple, gout, cap, final):
    b, ne, _, d = yg.shape
    s = x13.shape[1]
    tt, win = COMBINE_TILE, COMBINE_WINDOW
    nt = s // tt
    lane = jnp.arange(ne * win, dtype=I32)
    part = jnp.arange(2 * LANES, dtype=I32)[:, None]
    spread = jnp.where(part == lane // win, float(COMBINE_SPLIT), jnp.where(part == lane // win + LANES, 1.0, 0.0))
    spread = spread.astype(BF16)
    lanes = jnp.stack([lane // win, lane % win]).astype(F32)
    tile = lambda a: pl.BlockSpec((1, tt, a.shape[2]), lambda i, t, st: (i, t, 0))
    full = lambda a: pl.BlockSpec(a.shape, lambda i, t, st: (0,) * a.ndim)
    grid_spec = pltpu.PrefetchScalarGridSpec(
        num_scalar_prefetch=1,
        grid=(b, nt),
        in_specs=[
            pl.BlockSpec((1, ne, cap, d), lambda i, t, st: (i, 0, 0, 0), pipeline_mode=pl.Buffered(1)),
            pl.BlockSpec((ne, tt), lambda i, t, st: (0, i * nt + t)),
            full(spread), full(lanes), tile(x13), tile(p3), full(wpp), full(wpg), full(gple), full(gout),
        ],
        out_specs=pl.BlockSpec((1, tt, d), lambda i, t, st: (i, t, 0)),
    )
    return pl.pallas_call(
        functools.partial(_combine_kernel, nt=nt, tt=tt, win=win, cap=cap, ne=ne, nbatch=b, final=final),
        grid_spec=grid_spec,
        out_shape=jax.ShapeDtypeStruct(x13.shape, F32),
        compiler_params=_cparams("arbitrary", "arbitrary"),
        name="combine",
    )(starts, yg, post, spread, lanes, x13, p3, wpp, wpg, gple, gout)


def kernel(x, p, g_mix, w_in, w_fourier, ssm_a_re, ssm_a_im, ssm_log_dt, ssm_b_re, ssm_b_im, ssm_c_re, ssm_c_im, ssm_d, w_glu, w_out, g_ffn, w_router, w_exp_gate, w_exp_up, w_exp_down, g_ple, w_ple_gate, w_ple_proj, g_final):
    b, s, d = x.shape
    depth = p.shape[0]
    df = w_fourier.shape[1]
    ds = w_glu.shape[1]
    ne = w_router.shape[2]
    dexp = w_exp_gate.shape[3]
    groups, gi = ssm_b_re.shape[2], ssm_b_re.shape[4]
    L = SSM_CHUNK
    n1 = s // DFT_N2
    nc = s // L
    cap = EC_CAPACITY * s // ne
    assert s % (DFT_N2 * SUBLANES) == 0 and s % GATHER_TILE == 0 and cap % PACKED_ROWS == 0
    assert nc % SSM_TILE_CHUNKS == 0 and gi == PACKED_ROWS and cap // COMBINE_SPLIT < 256
    assert 2 * ssm_a_re.shape[3] == LANES and ds % LANES == 0
    assert cap >= COMBINE_WINDOW and ne % GATHER_EXPERTS == 0 and GATHER_TILE == COMBINE_TILE
    assert ds == groups * gi and df % FOURIER_GROUPS == 0

    bd = _channel_dft_table(df)
    ck, sk = _dft1_tables(n1)
    tr, ti = _dft3_tables(n1)
    tf = 256 if dexp % 256 == 0 else dexp
    nb_ffn = 2 if b % 2 == 0 else 1

    xcur = x.reshape(b * s, d)
    for i in range(depth):
        final = i == depth - 1
        row = lambda v: v.astype(F32).reshape(1, -1)
        pq, us, gate = _inproj(xcur, row(g_mix[i]), w_in[i].astype(BF16), bd, df, ds)
        a4 = _dft1(pq.reshape(b, n1, DFT_N2, 2 * df), ck, sk)
        fre = _dft3(a4, tr, ti).reshape(b * s, df)

        tabs = _ssm_tables(ssm_a_re[i], ssm_a_im[i], ssm_log_dt[i], ssm_b_re[i], ssm_b_im[i],
                           ssm_c_re[i], ssm_c_im[i], ssm_d[i])
        ys = _to_tokens(_ssm(_to_groups(us, L, groups, gi), *tabs, nc=nc, nb=b), L, groups, gi)

        wr = w_router[i].astype(F32)
        wr_hi = wr.astype(BF16)
        wr_lo = (wr - wr_hi.astype(F32)).astype(BF16)
        lane_pad = lambda w: jnp.pad(w, ((0, 0), (0, LANES - ne)))
        wr2 = jnp.concatenate([jnp.concatenate([lane_pad(wr_hi), lane_pad(wr_lo)], axis=1),
                               jnp.concatenate([lane_pad(wr_hi), jnp.zeros((d, LANES), BF16)], axis=1)], axis=0)
        x1, h2, aff = _mixout(fre, ys, gate, xcur, w_fourier[i].astype(BF16), w_glu[i].astype(BF16),
                              w_out[i].astype(BF16), row(g_ffn[i]), wr2, ne)

        aff_rows = aff.reshape(ne * b, s)
        posm, st_blk = _topk(aff_rows, cap)
        starts = st_blk[:, ::GATHER_TILE // TOPK_BLOCK].reshape(-1)
        tiles = (ne, b * (s // GATHER_TILE), GATHER_TILE)
        xg, vals = _gather(starts, h2.reshape(b, s, d), posm.reshape(tiles), aff_rows.reshape(tiles), cap, ne)
        yg = _ffn(xg, vals, w_exp_gate[i], w_exp_up[i], w_exp_down[i], nb_ffn, tf)
        xnext = _combine(starts, yg, posm.reshape(ne, b * s), x1.reshape(b, s, d), p[i],
                         w_ple_proj[i].astype(BF16), w_ple_gate[i].astype(BF16), row(g_ple[i]),
                         row(g_final) if final else row(g_ple[i]), cap, final)
        xcur = xnext.reshape(b * s, d)
    return xcur.reshape(b, s, d)
```

```python
import functools
import math

import jax
import jax.numpy as jnp
from jax import lax
from jax.experimental import pallas as pl
from jax.experimental.pallas import tpu as pltpu

F32 = jnp.float32
BF16 = jnp.bfloat16
I32 = jnp.int32

RMS_EPS = 1e-6
FOURIER_GROUPS = 4
EC_CAPACITY = 2
DFT_N2 = 128
SUBLANES = 8
LANES = 128
PACKED_ROWS = 16
SSM_CHUNK = 32
SSM_TILE_CHUNKS = 128
TOPK_BLOCK = 128
GATHER_TILE = 256
MIX_ROWS = 128
FFN_ROWS = 512
GATHER_WINDOW = 64
GATHER_EXPERTS = 4
GATHER_UNROLL = 4
COMBINE_TILE = 256
COMBINE_WINDOW = 64
COMBINE_SPLIT = 32
VMEM_LIMIT = 56 * 1024 * 1024


def _cparams(*sem):
    return pltpu.CompilerParams(dimension_semantics=sem, vmem_limit_bytes=VMEM_LIMIT)


def _rms(x, g):
    return x * lax.rsqrt(jnp.mean(x * x, axis=-1, keepdims=True) + RMS_EPS) * g


def _dot(a, b):
    return jnp.dot(a, b, preferred_element_type=F32)


def _inproj_kernel(x_ref, g_ref, w_ref, bd_ref, pq_ref, us_ref, gate_ref, *, df, ds):
    h = _rms(x_ref[...], g_ref[...]).astype(BF16)
    z = _dot(h, w_ref[...])
    pq_ref[...] = _dot(z[:, :df].astype(BF16), bd_ref[...]).astype(BF16)
    for q in range(ds // LANES):
        us_ref[q] = z[:, df + q * LANES:df + (q + 1) * LANES]
    gate_ref[...] = jax.nn.sigmoid(z[:, df + ds:]).astype(BF16)


def _inproj(x2, g, w_in, bd, df, ds, tm=512):
    t, d = x2.shape
    dg = w_in.shape[1] - df - ds
    return pl.pallas_call(
        functools.partial(_inproj_kernel, df=df, ds=ds),
        grid=(t // tm,),
        in_specs=[
            pl.BlockSpec((tm, d), lambda i: (i, 0)),
            pl.BlockSpec((1, d), lambda i: (0, 0)),
            pl.BlockSpec(w_in.shape, lambda i: (0, 0)),
            pl.BlockSpec(bd.shape, lambda i: (0, 0)),
        ],
        out_specs=[
            pl.BlockSpec((tm, 2 * df), lambda i: (i, 0)),
            pl.BlockSpec((ds // LANES, tm, LANES), lambda i: (0, i, 0)),
            pl.BlockSpec((tm, dg), lambda i: (i, 0)),
        ],
        out_shape=[
            jax.ShapeDtypeStruct((t, 2 * df), BF16),
            jax.ShapeDtypeStruct((ds // LANES, t, LANES), F32),
            jax.ShapeDtypeStruct((t, dg), BF16),
        ],
        compiler_params=_cparams("parallel"),
        name="inproj",
    )(x2, g, w_in, bd)


def _channel_dft_table(df):
    c = df // FOURIER_GROUPS
    k = jnp.arange(c, dtype=I32)
    ang = (2.0 * math.pi / c) * ((k[:, None] * k[None, :]) % c).astype(F32)
    eye = jnp.eye(FOURIER_GROUPS, dtype=F32)
    scale = 1.0 / math.sqrt(c)
    re = jnp.kron(eye, jnp.cos(ang)) * scale
    im = -jnp.kron(eye, jnp.sin(ang)) * scale
    return jnp.concatenate([re, im], axis=1).astype(BF16)


def _dft1_tables(n1):
    k = jnp.arange(n1, dtype=I32)
    ang = (2.0 * math.pi / n1) * ((k[:, None] * k[None, :]) % n1).astype(F32)
    eye = jnp.eye(SUBLANES, dtype=F32)
    scale = 1.0 / math.sqrt(n1)
    return (jnp.kron(jnp.cos(ang), eye) * scale).astype(BF16), (jnp.kron(jnp.sin(ang), eye) * scale).astype(BF16)


def _dft3_tables(n1):
    s = n1 * DFT_N2
    nb = n1 // SUBLANES
    k1 = jnp.arange(n1, dtype=I32)[:, None, None]
    k2 = jnp.arange(DFT_N2, dtype=I32)[None, :, None]
    n2 = jnp.arange(DFT_N2, dtype=I32)[None, None, :]
    ang = (2.0 * math.pi / s) * ((n2 * (k1 + n1 * k2)) % s).astype(F32)
    eye = jnp.eye(SUBLANES, dtype=BF16)[None, None, :, :, None]
    rows = DFT_N2 * SUBLANES

    def expand(t):
        t = (t * (1.0 / math.sqrt(DFT_N2))).astype(BF16).reshape(nb, SUBLANES, DFT_N2, DFT_N2)
        t = jnp.transpose(t, (0, 2, 1, 3))[:, :, :, None, :]
        return (t * eye).reshape(nb, rows, rows)

    return expand(jnp.cos(ang)), expand(jnp.sin(ang))


def _dft1_kernel(z_ref, ck_ref, sk_ref, a_ref):
    n1, slab, c2 = z_ref.shape[1], z_ref.shape[2], z_ref.shape[3]
    c = c2 // 2
    z = z_ref[0].astype(F32)
    halves = []
    for h in range(slab // SUBLANES):
        zh = z[:, h * SUBLANES:(h + 1) * SUBLANES, :].reshape(n1 * SUBLANES, c2).astype(BF16)
        cz = _dot(ck_ref[...], zh)
        sz = _dot(sk_ref[...], zh)
        a = jnp.concatenate([cz[:, :c] + sz[:, c:], cz[:, c:] - sz[:, :c]], axis=1)
        halves.append(a.reshape(n1, SUBLANES, c2))
    a_ref[0] = jnp.concatenate(halves, axis=1).astype(BF16)


def _dft1(pq4, ck, sk):
    b, n1, n2, c2 = pq4.shape
    slab = PACKED_ROWS
    return pl.pallas_call(
        _dft1_kernel,
        grid=(b, n2 // slab),
        in_specs=[
            pl.BlockSpec((1, n1, slab, c2), lambda i, j: (i, 0, j, 0)),
            pl.BlockSpec(ck.shape, lambda i, j: (0, 0)),
            pl.BlockSpec(sk.shape, lambda i, j: (0, 0)),
        ],
        out_specs=pl.BlockSpec((1, n1, slab, c2), lambda i, j: (i, 0, j, 0)),
        out_shape=jax.ShapeDtypeStruct(pq4.shape, BF16),
        compiler_params=_cparams("parallel", "parallel"),
        name="dft1",
    )(pq4, ck, sk)


def _dft3_kernel(a_ref, tr_ref, ti_ref, o_ref):
    nblk, c2 = tr_ref.shape[0], a_ref.shape[3]
    c = c2 // 2
    outs = []
    for h in range(nblk):
        a = a_ref[0, h * SUBLANES:(h + 1) * SUBLANES].reshape(SUBLANES * DFT_N2, c2)
        out = _dot(tr_ref[h], a[:, :c]) + _dot(ti_ref[h], a[:, c:])
        outs.append(out.reshape(DFT_N2, SUBLANES, c))
    o_ref[0] = jnp.concatenate(outs, axis=1).astype(BF16)


def _dft3(a4, tr, ti):
    b, n1, n2, c2 = a4.shape
    c = c2 // 2
    rows = DFT_N2 * SUBLANES
    nblk = PACKED_ROWS // SUBLANES
    return pl.pallas_call(
        _dft3_kernel,
        grid=(n1 // PACKED_ROWS, b),
        in_specs=[
            pl.BlockSpec((1, PACKED_ROWS, n2, c2), lambda k, i: (i, k, 0, 0)),
            pl.BlockSpec((nblk, rows, rows), lambda k, i: (k, 0, 0)),
            pl.BlockSpec((nblk, rows, rows), lambda k, i: (k, 0, 0)),
        ],
        out_specs=pl.BlockSpec((1, DFT_N2, PACKED_ROWS, c), lambda k, i: (i, 0, k, 0)),
        out_shape=jax.ShapeDtypeStruct((b, DFT_N2, n1, c), BF16),
        compiler_params=_cparams("parallel", "parallel"),
        name="dft3",
    )(a4, tr, ti)


def _ssm_tables(a_re, a_im, log_dt, b_re, b_im, c_re, c_im, d):
    L = SSM_CHUNK
    _, g, n = a_re.shape
    gi = b_re.shape[-1]
    dt = jnp.exp(log_dt.astype(F32))[..., None]
    ar, ai = a_re.astype(F32), a_im.astype(F32)
    tau = jnp.arange(L + 1, dtype=F32)[:, None, None, None]
    mag = jnp.exp(tau * (ar * dt)[None])
    ang = tau * (ai * dt)[None]
    pr, pi = mag * jnp.cos(ang), mag * jnp.sin(ang)
    nr, ni = pr[1] - 1.0, pi[1]
    den = ar * ar + ai * ai
    qr, qi = (nr * ar + ni * ai) / den, (ni * ar - nr * ai) / den
    br, bi = b_re.astype(F32), b_im.astype(F32)
    bbr = qr[..., None] * br - qi[..., None] * bi
    bbi = qr[..., None] * bi + qi[..., None] * br
    cr, ci = c_re.astype(F32), c_im.astype(F32)

    cat = jnp.concatenate
    steps = lambda p: jnp.transpose(p, (1, 0, 2))
    pf_r, pf_i, pb_r, pb_i = steps(pr[1:, 0]), steps(pi[1:, 0]), steps(pr[1:, 1][::-1]), steps(pi[1:, 1][::-1])
    f_small = jnp.stack([cat([cr[0], cr[1], cr[0], cr[1]], -1), cat([-ci[0], -ci[1], -ci[0], -ci[1]], -1)], 1)
    f_steps = jnp.stack([cat([pf_r, pb_r, -pf_i, -pb_i], -1), cat([pf_i, pb_i, pf_r, pb_r], -1)], 1)
    last = lambda p: jnp.transpose(p, (1, 2, 0))
    ef_r, ef_i, eb_r, eb_i = last(pr[:L, 0][::-1]), last(pi[:L, 0][::-1]), last(pr[:L, 1]), last(pi[:L, 1])
    e_small = jnp.stack([cat([bbr[0], bbr[1], bbi[0], bbi[1]], 1), cat([-bbi[0], -bbi[1], bbr[0], bbr[1]], 1)], 1)
    e_steps = jnp.stack([cat([ef_r, eb_r, ef_r, eb_r], 1), cat([ef_i, eb_i, ef_i, eb_i], 1)], 1)
    zpad = lambda p, before: jnp.pad(p, ((0, 0), (0, 0), (before, L - before)))
    wf_r, wf_i, wb_r, wb_i = zpad(ef_r, 0), zpad(ef_i, 0), zpad(eb_r, L - 1), zpad(eb_i, L - 1)
    t_small = jnp.stack([cat([bbr[0], bbi[0], bbr[1], bbi[1]], 1), cat([-bbi[0], bbr[0], -bbi[1], bbr[1]], 1)], 1)
    t_steps = jnp.stack([cat([wf_r, wf_r, wb_r, wb_r], 1), cat([wf_i, wf_i, wb_i, wb_i], 1)], 1)
    t_left = cat([cr[0], -ci[0], cr[1], -ci[1]], -1)

    al = jnp.stack([cat([pr[L, 0], pr[L, 1]], -1), cat([pi[L, 0], pi[L, 1]], -1)], axis=1)
    dv = jnp.tile(d.astype(F32).reshape(g, 1, gi), (1, L, 1)).reshape(g, L * gi, 1)
    t_tab, e_tab, f_tab = _ssm_table_call(f_small, f_steps, e_small, e_steps, t_small, t_steps, t_left, L, gi)
    return t_tab, e_tab, f_tab, al, dv


def _ssm_table_kernel(fs_ref, fp_ref, es_ref, ep_ref, ts_ref, tp_ref, tl_ref, ri_ref, rj_ref, rit_ref, rjt_ref,
                      ri2_ref, rs_ref, t_ref, e_ref, f_ref, *, L, gi):
    def split(a):
        hi = a.astype(BF16)
        return hi, (a - hi.astype(F32)).astype(BF16)

    def spread_lanes(a, rep):
        hi, lo = split(a)
        return _dot(hi, rep) + _dot(lo, rep)

    def spread_rows(rep, a):
        hi, lo = split(a)
        return _dot(rep, hi) + _dot(rep, lo)

    ri, rj, rit, rjt, ri2, rs = ri_ref[...], rj_ref[...], rit_ref[...], rjt_ref[...], ri2_ref[...], rs_ref[...]
    f_ref[0] = (spread_rows(rit, fs_ref[0, 0]) * spread_rows(rjt, fp_ref[0, 0])
                + spread_rows(rit, fs_ref[0, 1]) * spread_rows(rjt, fp_ref[0, 1])).astype(BF16)
    e_ref[0] = (spread_lanes(es_ref[0, 0], ri) * spread_lanes(ep_ref[0, 0], rj)
                + spread_lanes(es_ref[0, 1], ri) * spread_lanes(ep_ref[0, 1], rj)).astype(BF16)
    ew = (spread_lanes(ts_ref[0, 0], ri2) * spread_lanes(tp_ref[0, 0], rs)
          + spread_lanes(ts_ref[0, 1], ri2) * spread_lanes(tp_ref[0, 1], rs))
    c_hi, c_lo = split(tl_ref[0])
    e_hi, e_lo = split(ew)
    kw = _dot(c_hi, e_hi) + _dot(c_hi, e_lo) + _dot(c_lo, e_hi)
    per = LANES // gi
    for r in range(per):
        shifted = kw if r == 0 else pltpu.roll(kw, 2 * L * gi - r * gi, axis=1)
        for a in range(L // per):
            jo = L - 1 - (a * per + r)
            t_ref[0, pl.ds(jo * gi, gi), :] = shifted[:, a * LANES:a * LANES + L * gi].astype(BF16)


def _ssm_table_call(f_small, f_steps, e_small, e_steps, t_small, t_steps, t_left, L, gi):
    g = f_small.shape[0]
    n4 = f_small.shape[3]
    k = L * gi
    lane = jnp.arange(2 * k, dtype=I32)
    ri2 = (lane[None, :] % gi == jnp.arange(gi, dtype=I32)[:, None]).astype(BF16)
    rs = (lane[None, :] // gi == jnp.arange(2 * L, dtype=I32)[:, None]).astype(BF16)
    ri, rj = ri2[:, :k], rs[:L, :k]
    per_group = lambda a: pl.BlockSpec((1,) + a.shape[1:], lambda i: (i,) + (0,) * (a.ndim - 1))
    full = lambda a: pl.BlockSpec(a.shape, lambda i: (0,) * a.ndim)
    ins = (f_small, f_steps, e_small, e_steps, t_small, t_steps, t_left)
    reps = (ri, rj, ri.T, rj.T, ri2, rs)
    return pl.pallas_call(
        functools.partial(_ssm_table_kernel, L=L, gi=gi),
        grid=(g,),
        in_specs=[per_group(a) for a in ins] + [full(a) for a in reps],
        out_specs=[pl.BlockSpec((1, k, k), lambda i: (i, 0, 0)), pl.BlockSpec((1, n4, k), lambda i: (i, 0, 0)),
                   pl.BlockSpec((1, k, n4), lambda i: (i, 0, 0))],
        out_shape=[jax.ShapeDtypeStruct((g, k, k), BF16), jax.ShapeDtypeStruct((g, n4, k), BF16),
                   jax.ShapeDtypeStruct((g, k, n4), BF16)],
        compiler_params=_cparams("parallel"),
        name="ssm_tables",
    )(*ins, *reps)


def _to_groups_kernel(u_ref, a_ref, *, L, groups, gi):
    ncl = a_ref.shape[2]
    gq = LANES // gi
    for q in range(u_ref.shape[0]):
        for j in range(L):
            zt = u_ref[q, pl.ds(j, ncl, stride=L), :].T
            a_ref[q * gq:(q + 1) * gq, pl.ds(j * gi, gi), :] = zt.reshape(gq, gi, ncl).astype(BF16)


def _to_groups(us, L, groups, gi):
    nq, t, _ = us.shape
    ncl = SSM_TILE_CHUNKS
    return pl.pallas_call(
        functools.partial(_to_groups_kernel, L=L, groups=groups, gi=gi),
        grid=(t // (ncl * L),),
        in_specs=[pl.BlockSpec((nq, ncl * L, LANES), lambda i: (0, i, 0))],
        out_specs=pl.BlockSpec((groups, L * gi, ncl), lambda i: (0, 0, i)),
        out_shape=jax.ShapeDtypeStruct((groups, L * gi, t // L), BF16),
        compiler_params=_cparams("parallel"),
        name="to_groups",
    )(us)


def _to_tokens_kernel(y_ref, o_ref, *, L, groups, gi):
    ncl = y_ref.shape[2]
    gq = LANES // gi
    for q in range(o_ref.shape[0]):
        for j in range(L):
            yj = y_ref[q * gq:(q + 1) * gq, pl.ds(j * gi, gi), :].astype(F32).reshape(LANES, ncl)
            o_ref[q, pl.ds(j, ncl, stride=L), :] = yj.T


def _to_tokens(yt, L, groups, gi):
    _, k, nchunks = yt.shape
    ncl = SSM_TILE_CHUNKS
    return pl.pallas_call(
        functools.partial(_to_tokens_kernel, L=L, groups=groups, gi=gi),
        grid=(nchunks // ncl,),
        in_specs=[pl.BlockSpec((groups, k, ncl), lambda i: (0, 0, i))],
        out_specs=pl.BlockSpec((groups * gi // LANES, ncl * L, LANES), lambda i: (0, i, 0)),
        out_shape=jax.ShapeDtypeStruct((groups * gi // LANES, nchunks * L, LANES), F32),
        compiler_params=_cparams("parallel"),
        name="to_tokens",
    )(yt)


def _ssm_kernel(a_ref, t_ref, e_ref, f_ref, al_ref, dv_ref, y_ref, s_scr, h_scr, *, nc, nb, rows, n):
    a = a_ref[0]
    y1 = _dot(t_ref[0], a) + dv_ref[0] * a.astype(F32)
    st = _dot(e_ref[0], a)
    n2 = 2 * n
    s_scr[...] = jnp.zeros_like(s_scr)
    for b in range(nb):
        sb_t = st[:, b * nc:(b + 1) * nc].T
        for q in range(2):
            s_scr[q, pl.ds(b, nc, stride=rows), :] = sb_t[:, q * n2:(q + 1) * n2]
    ar = al_ref[0, 0:1, :]
    ai = al_ref[0, 1:2, :]
    is_fwd = lax.broadcasted_iota(I32, (rows, n2), 1) < n

    def step(i, carry):
        hr, hi = carry
        rf = pl.ds(pl.multiple_of(i * rows, rows), rows)
        rb = pl.ds(pl.multiple_of((nc - 1 - i) * rows, rows), rows)
        h_scr[0, rf, 0:n] = hr[:, 0:n]
        h_scr[1, rf, 0:n] = hi[:, 0:n]
        h_scr[0, rb, n:n2] = hr[:, n:n2]
        h_scr[1, rb, n:n2] = hi[:, n:n2]
        sr = jnp.where(is_fwd, s_scr[0, rf, :], s_scr[0, rb, :])
        si = jnp.where(is_fwd, s_scr[1, rf, :], s_scr[1, rb, :])
        return ar * hr - ai * hi + sr, ar * hi + ai * hr + si

    zero = jnp.zeros((rows, n2), F32)
    lax.fori_loop(0, nc, step, (zero, zero))
    ht = jnp.concatenate(
        [jnp.concatenate([h_scr[q, pl.ds(b, nc, stride=rows), :].T for q in range(2)], axis=0) for b in range(nb)],
        axis=1)
    y = y1 + _dot(f_ref[0], ht.astype(BF16))
    y_ref[0] = jax.nn.gelu(y, approximate=True).astype(BF16)


def _ssm(ag, t_tab, e_tab, f_tab, al, dv, nc, nb):
    g, k, m = ag.shape
    n4 = e_tab.shape[1]
    rows = -(-nb // SUBLANES) * SUBLANES
    return pl.pallas_call(
        functools.partial(_ssm_kernel, nc=nc, nb=nb, rows=rows, n=n4 // 4),
        grid=(g,),
        in_specs=[
            pl.BlockSpec((1, k, m), lambda i: (i, 0, 0)),
            pl.BlockSpec((1, k, k), lambda i: (i, 0, 0)),
            pl.BlockSpec((1, n4, k), lambda i: (i, 0, 0)),
            pl.BlockSpec((1, k, n4), lambda i: (i, 0, 0)),
            pl.BlockSpec((1, 2, n4 // 2), lambda i: (i, 0, 0)),
            pl.BlockSpec((1, k, 1), lambda i: (i, 0, 0)),
        ],
        out_specs=pl.BlockSpec((1, k, m), lambda i: (i, 0, 0)),
        out_shape=jax.ShapeDtypeStruct((g, k, m), BF16),
        scratch_shapes=[pltpu.VMEM((2, nc * rows, n4 // 2), F32), pltpu.VMEM((2, nc * rows, n4 // 2), F32)],
        compiler_params=_cparams("parallel"),
        name="ssm",
    )(ag, t_tab, e_tab, f_tab, al, dv)


def _mixout_kernel(fre_ref, ys_ref, gate_ref, x_ref, wf_ref, wglu_ref, wout_ref, gffn_ref, wr_ref,
                   x1_ref, h2_ref, aff_ref, *, d, ne):
    for r in range(fre_ref.shape[0] // MIX_ROWS):
        rows = pl.ds(r * MIX_ROWS, MIX_ROWS)
        y_f = _dot(fre_ref[rows, :], wf_ref[...])
        ys = jnp.concatenate([ys_ref[q, rows, :] for q in range(ys_ref.shape[0])], axis=1)
        vg = _dot(ys.astype(BF16), wglu_ref[...])
        y_s = vg[:, :d] * jax.nn.sigmoid(vg[:, d:])
        gate = gate_ref[rows, :].astype(F32)
        m = gate[:, :d] * y_f + gate[:, d:] * y_s
        x1 = x_ref[rows, :] + _dot(m.astype(BF16), wout_ref[...])
        x1_ref[rows, :] = x1
        h2 = _rms(x1, gffn_ref[...])
        h2_ref[rows, :] = h2.astype(BF16)
        hi = h2.astype(BF16)
        lo = (h2 - hi.astype(F32)).astype(BF16)
        rl = _dot(jnp.concatenate([hi, lo], axis=1), wr_ref[...])
        logits = rl[:, :LANES] + rl[:, LANES:]
        logits = jnp.where(lax.broadcasted_iota(I32, logits.shape, 1) < ne, logits, -1e30)
        logits = logits - jnp.max(logits, axis=-1, keepdims=True)
        p = jnp.exp(logits)
        aff = p / jnp.sum(p, axis=-1, keepdims=True)
        aff_ref[:, rows] = aff.T[:ne, :]


def _mixout(fre, ys, gate, x2, wf, wglu, wout, gffn, wr2, ne, tm=512):
    t, d = x2.shape
    full = lambda a: pl.BlockSpec(a.shape, lambda i: (0,) * a.ndim)
    row = lambda a: pl.BlockSpec((tm, a.shape[1]), lambda i: (i, 0))
    return pl.pallas_call(
        functools.partial(_mixout_kernel, d=d, ne=ne),
        grid=(t // tm,),
        in_specs=[row(fre), pl.BlockSpec((ys.shape[0], tm, LANES), lambda i: (0, i, 0)), row(gate), row(x2),
                  full(wf), full(wglu), full(wout), full(gffn), full(wr2)],
        out_specs=[
            pl.BlockSpec((tm, d), lambda i: (i, 0)),
            pl.BlockSpec((tm, d), lambda i: (i, 0)),
            pl.BlockSpec((ne, tm), lambda i: (0, i)),
        ],
        out_shape=[
            jax.ShapeDtypeStruct((t, d), F32),
            jax.ShapeDtypeStruct((t, d), BF16),
            jax.ShapeDtypeStruct((ne, t), F32),
        ],
        compiler_params=_cparams("parallel"),
        name="mixout",
    )(fre, ys, gate, x2, wf, wglu, wout, gffn, wr2)


def _topk_kernel(aff_ref, pos_ref, st_ref, *, cap, blk):
    v = aff_ref[...]
    r, s = v.shape
    capf = float(cap)

    def bit_step(i, t):
        cand = t | (jnp.int32(1) << (30 - i))
        cnt = jnp.sum(jnp.where(v >= pltpu.bitcast(cand, F32), 1.0, 0.0), axis=1, keepdims=True)
        return jnp.where(cnt >= capf, cand, t)

    thr = lax.fori_loop(0, 31, bit_step, jnp.zeros((r, 1), I32))
    gt = jnp.where(v >= pltpu.bitcast(thr + 1, F32), 1.0, 0.0)
    eq = jnp.where(v >= pltpu.bitcast(thr, F32), 1.0, 0.0) - gt
    need = capf - jnp.sum(gt, axis=1, keepdims=True)
    ii = lax.broadcasted_iota(I32, (blk, blk), 0)
    jj = lax.broadcasted_iota(I32, (blk, blk), 1)
    tri = jnp.where(ii < jj, 1.0, 0.0).astype(BF16)
    run_eq = jnp.zeros((r, 1), F32)
    run = jnp.zeros((r, 1), F32)
    for k in range(s // blk):
        sl = slice(k * blk, (k + 1) * blk)
        eqb, gtb = eq[:, sl], gt[:, sl]
        rank_eq = _dot(eqb.astype(BF16), tri) + run_eq
        run_eq = run_eq + jnp.sum(eqb, axis=1, keepdims=True)
        mask = gtb + eqb * jnp.where(rank_eq < need, 1.0, 0.0)
        pos = _dot(mask.astype(BF16), tri) + run
        st_ref[:, k:k + 1] = run.astype(I32)
        run = run + jnp.sum(mask, axis=1, keepdims=True)
        pos_ref[:, sl] = jnp.where(mask > 0.0, pos, -1.0).astype(I32)
    st_ref[:, s // blk:s // blk + 1] = run.astype(I32)


def _topk(aff_rows, cap, blk=TOPK_BLOCK):
    r, s = aff_rows.shape
    nblk = s // blk
    return pl.pallas_call(
        functools.partial(_topk_kernel, cap=cap, blk=blk),
        grid=(1,),
        in_specs=[pl.BlockSpec((r, s), lambda i: (0, 0))],
        out_specs=[pl.BlockSpec((r, s), lambda i: (0, 0)), pl.BlockSpec((r, nblk + 1), lambda i: (0, 0))],
        out_shape=[jax.ShapeDtypeStruct((r, s), I32), jax.ShapeDtypeStruct((r, nblk + 1), I32)],
        compiler_params=_cparams("arbitrary"),
        name="topk",
    )(aff_rows)


def _floor_rows(x):
    return (x // PACKED_ROWS) * PACKED_ROWS


def _num_passes(starts, ends, win):
    need = ends[0] - _floor_rows(starts[0])
    for a, b in zip(starts[1:], ends[1:]):
        need = jnp.maximum(need, b - _floor_rows(a))
    return (need + win - 1) // win


def _pass_window(start, p, win, cap):
    first = _floor_rows(start) + p * win
    return first, pl.multiple_of(jnp.minimum(first, cap - win), PACKED_ROWS)


def _gather_kernel(st_ref, h_ref, pos_ref, aff_ref, x_ref, v_ref, *, nt, tt, win, cap, nbatch, ng):
    rows = [((pl.program_id(1) * ng + g) * nbatch + pl.program_id(0)) * (nt + 1) for g in range(ng)]
    x_ref[...] = jnp.zeros_like(x_ref)
    v_ref[...] = jnp.zeros_like(v_ref)
    riota = lax.broadcasted_iota(I32, (win, tt), 0)

    def bounds(t):
        return [st_ref[r + t] for r in rows], [st_ref[r + t + 1] for r in rows]

    def one_pass(t, p, starts):
        hrows = h_ref[0, pl.ds(pl.multiple_of(t * tt, tt), tt), :]
        hots, wins = [], []
        for g in range(ng):
            first, ws = _pass_window(starts[g], p, win, cap)
            pos = pos_ref[g, pl.ds(t, 1), :]
            hots.append((pos - ws == riota) & (pos >= first))
            wins.append(ws)
        onehot = jnp.concatenate([jnp.where(h, 1.0, 0.0).astype(BF16) for h in hots], axis=0)
        rows = _dot(onehot, hrows)
        for g in range(ng):
            x_ref[0, g, pl.ds(wins[g], win), :] += rows[g * win:(g + 1) * win].astype(BF16)
            vals = jnp.sum(jnp.where(hots[g], aff_ref[g, pl.ds(t, 1), :], 0.0), axis=1, keepdims=True)
            v_ref[0, g, pl.ds(wins[g], win), :] += vals

    def first_pass(t, most):
        starts, ends = bounds(t)
        one_pass(t, 0, starts)
        return jnp.maximum(most, _num_passes(starts, ends, win))

    def more_passes(t, carry):
        starts, ends = bounds(t)

        def body(p, c):
            one_pass(t, p, starts)
            return c

        lax.fori_loop(1, _num_passes(starts, ends, win), body, 0)
        return carry

    most = lax.fori_loop(0, nt, first_pass, jnp.int32(0), unroll=GATHER_UNROLL)

    @pl.when(most > 1)
    def _():
        lax.fori_loop(0, nt, more_passes, 0)


def _gather(starts, h3, pos3, aff3t, cap, ne):
    b, s, d = h3.shape
    tt, win, ng = GATHER_TILE, GATHER_WINDOW, GATHER_EXPERTS
    nt = s // tt
    grid_spec = pltpu.PrefetchScalarGridSpec(
        num_scalar_prefetch=1,
        grid=(b, ne // ng),
        in_specs=[
            pl.BlockSpec((1, s, d), lambda i, e, st: (i, 0, 0), pipeline_mode=pl.Buffered(1)),
            pl.BlockSpec((ng, nt, tt), lambda i, e, st: (e, i, 0)),
            pl.BlockSpec((ng, nt, tt), lambda i, e, st: (e, i, 0)),
        ],
        out_specs=[
            pl.BlockSpec((1, ng, cap, d), lambda i, e, st: (i, e, 0, 0)),
            pl.BlockSpec((1, ng, cap, 1), lambda i, e, st: (i, e, 0, 0)),
        ],
    )
    return pl.pallas_call(
        functools.partial(_gather_kernel, nt=nt, tt=tt, win=win, cap=cap, nbatch=b, ng=ng),
        grid_spec=grid_spec,
        out_shape=[jax.ShapeDtypeStruct((b, ne, cap, d), BF16), jax.ShapeDtypeStruct((b, ne, cap, 1), F32)],
        compiler_params=_cparams("arbitrary", "arbitrary"),
        name="gather",
    )(starts, h3, pos3, aff3t)


def _ffn_kernel(x_ref, v_ref, wg_ref, wu_ref, wd_ref, y_ref, acc_ref, *, nf):
    f = pl.program_id(2)
    nb, _, cap, d = x_ref.shape

    @pl.when(f == 0)
    def _():
        acc_ref[...] = jnp.zeros_like(acc_ref)

    wg = wg_ref[0].astype(BF16)
    wu = wu_ref[0].astype(BF16)
    wd = wd_ref[0].astype(BF16)
    mr = min(FFN_ROWS, cap)
    for r in range(nb * cap // mr):
        x = x_ref[r * mr // cap, 0, pl.ds(r * mr % cap, mr), :]
        g = _dot(x, wg)
        u = _dot(x, wu)
        hid = (g * jax.nn.sigmoid(g) * u).astype(BF16)
        acc_ref[pl.ds(r * mr, mr), :] += _dot(hid, wd)

    @pl.when(f == nf - 1)
    def _():
        y = acc_ref[...] * v_ref[...].reshape(nb * cap, 1)
        y_ref[...] = y.astype(BF16).reshape(nb, 1, cap, d)


def _ffn(xg, vals, wg, wu, wd, nb, tf):
    b, ne, cap, d = xg.shape
    dexp = wg.shape[2]
    nf = dexp // tf
    return pl.pallas_call(
        functools.partial(_ffn_kernel, nf=nf),
        grid=(ne, b // nb, nf),
        in_specs=[
            pl.BlockSpec((nb, 1, cap, d), lambda e, i, f: (i, e, 0, 0)),
            pl.BlockSpec((nb, 1, cap, 1), lambda e, i, f: (i, e, 0, 0)),
            pl.BlockSpec((1, d, tf), lambda e, i, f: (e, 0, f)),
            pl.BlockSpec((1, d, tf), lambda e, i, f: (e, 0, f)),
            pl.BlockSpec((1, tf, d), lambda e, i, f: (e, f, 0)),
        ],
        out_specs=pl.BlockSpec((nb, 1, cap, d), lambda e, i, f: (i, e, 0, 0)),
        out_shape=jax.ShapeDtypeStruct(xg.shape, BF16),
        scratch_shapes=[pltpu.VMEM((nb * cap, d), F32)],
        compiler_params=_cparams("parallel", "parallel", "arbitrary"),
        name="ffn",
    )(xg, vals, wg, wu, wd)


def _combine_kernel(st_ref, y_ref, pos_ref, spread_ref, lanes_ref, x1_ref, p_ref, wpp_ref, wpg_ref, gple_ref,
                    gout_ref, o_ref, *, nt, tt, win, cap, ne, nbatch, final):
    t = pl.program_id(1)
    rows = [(e * nbatch + pl.program_id(0)) * (nt + 1) + t for e in range(ne)]
    starts = [st_ref[r] for r in rows]
    ends = [st_ref[r + 1] for r in rows]
    k = ne * win
    lane_e = lanes_ref[0:1, :]
    lane_r = lanes_ref[1:2, :]
    pos = pos_ref[...].astype(F32)
    pos = jnp.concatenate([pos, jnp.zeros((LANES - ne, tt), F32)], axis=0).T
    hi = jnp.floor(pos * (1.0 / COMBINE_SPLIT))
    lo = pos - hi * COMBINE_SPLIT
    rank = _dot(jnp.concatenate([hi, lo], axis=1).astype(BF16), spread_ref[...])

    def expert_rows(p):
        firsts = jnp.zeros((1, k), F32)
        offs = jnp.zeros((1, k), F32)
        wins = []
        for e in range(ne):
            first, ws = _pass_window(starts[e], p, win, cap)
            firsts = jnp.where(lane_e == e, first.astype(F32), firsts)
            offs = jnp.where(lane_e == e, ws.astype(F32), offs)
            wins.append(y_ref[0, e, pl.ds(ws, win), :])
        onehot = jnp.where((rank - offs == lane_r) & (rank >= firsts), 1.0, 0.0).astype(BF16)
        return _dot(onehot, jnp.concatenate(wins, axis=0))

    def finish(acc):
        for r in range(tt // LANES):
            rows = pl.ds(r * LANES, LANES)
            x2 = acc[r * LANES:(r + 1) * LANES]
            emb = _dot(p_ref[0, rows, :].astype(BF16), wpp_ref[...])
            gate = jax.nn.sigmoid(_dot(_rms(x2, gple_ref[...]).astype(BF16), wpg_ref[...]))
            x3 = x2 + gate * emb
            o_ref[0, rows, :] = _rms(x3, gout_ref[...]) if final else x3

    first_pass = x1_ref[0] + expert_rows(0)
    finish(first_pass)
    npass = _num_passes(starts, ends, win)

    @pl.when(npass > 1)
    def _():
        finish(lax.fori_loop(1, npass, lambda p, acc: acc + expert_rows(p), x1_ref[0] + expert_rows(0)))


def _combine(starts, yg, post, x13, p3, wpp, wpg, gple, gout, cap, final):
    b, ne, _, d = yg.shape
    s = x13.shape[1]
    tt, win = COMBINE_TILE, COMBINE_WINDOW
    nt = s // tt
    lane = jnp.arange(ne * win, dtype=I32)
    part = jnp.arange(2 * LANES, dtype=I32)[:, None]
    spread = jnp.where(part == lane // win, float(COMBINE_SPLIT), jnp.where(part == lane // win + LANES, 1.0, 0.0))
    spread = spread.astype(BF16)
    lanes = jnp.stack([lane // win, lane % win]).astype(F32)
    tile = lambda a: pl.BlockSpec((1, tt, a.shape[2]), lambda i, t, st: (i, t, 0))
    full = lambda a: pl.BlockSpec(a.shape, lambda i, t, st: (0,) * a.ndim)
    grid_spec = pltpu.PrefetchScalarGridSpec(
        num_scalar_prefetch=1,
        grid=(b, nt),
        in_specs=[
            pl.BlockSpec((1, ne, cap, d), lambda i, t, st: (i, 0, 0, 0), pipeline_mode=pl.Buffered(1)),
            pl.BlockSpec((ne, tt), lambda i, t, st: (0, i * nt + t)),
            full(spread), full(lanes), tile(x13), tile(p3), full(wpp), full(wpg), full(gple), full(gout),
        ],
        out_specs=pl.BlockSpec((1, tt, d), lambda i, t, st: (i, t, 0)),
    )
    return pl.pallas_call(
        functools.partial(_combine_kernel, nt=nt, tt=tt, win=win, cap=cap, ne=ne, nbatch=b, final=final),
        grid_spec=grid_spec,
        out_shape=jax.ShapeDtypeStruct(x13.shape, F32),
        compiler_params=_cparams("arbitrary", "arbitrary"),
        name="combine",
    )(starts, yg, post, spread, lanes, x13, p3, wpp, wpg, gple, gout)


def kernel(x, p, g_mix, w_in, w_fourier, ssm_a_re, ssm_a_im, ssm_log_dt, ssm_b_re, ssm_b_im, ssm_c_re, ssm_c_im, ssm_d, w_glu, w_out, g_ffn, w_router, w_exp_gate, w_exp_up, w_exp_down, g_ple, w_ple_gate, w_ple_proj, g_final):
    b, s, d = x.shape
    depth = p.shape[0]
    df = w_fourier.shape[1]
    ds = w_glu.shape[1]
    ne = w_router.shape[2]
    dexp = w_exp_gate.shape[3]
    groups, gi = ssm_b_re.shape[2], ssm_b_re.shape[4]
    L = SSM_CHUNK
    n1 = s // DFT_N2
    nc = s // L
    cap = EC_CAPACITY * s // ne
    assert s % (DFT_N2 * PACKED_ROWS) == 0 and s % GATHER_TILE == 0 and cap % PACKED_ROWS == 0
    assert nc % SSM_TILE_CHUNKS == 0 and gi == PACKED_ROWS and cap // COMBINE_SPLIT < 256
    assert 2 * ssm_a_re.shape[3] == LANES and ds % LANES == 0
    assert cap >= COMBINE_WINDOW and ne % GATHER_EXPERTS == 0 and GATHER_TILE == COMBINE_TILE
    assert ds == groups * gi and df % FOURIER_GROUPS == 0

    bd = _channel_dft_table(df)
    ck, sk = _dft1_tables(n1)
    tr, ti = _dft3_tables(n1)
    tf = 256 if dexp % 256 == 0 else dexp
    nb_ffn = 2 if b % 2 == 0 else 1

    xcur = x.reshape(b * s, d)
    for i in range(depth):
        final = i == depth - 1
        row = lambda v: v.astype(F32).reshape(1, -1)
        pq, us, gate = _inproj(xcur, row(g_mix[i]), w_in[i].astype(BF16), bd, df, ds)
        a4 = _dft1(pq.reshape(b, n1, DFT_N2, 2 * df), ck, sk)
        fre = _dft3(a4, tr, ti).reshape(b * s, df)

        tabs = _ssm_tables(ssm_a_re[i], ssm_a_im[i], ssm_log_dt[i], ssm_b_re[i], ssm_b_im[i],
                           ssm_c_re[i], ssm_c_im[i], ssm_d[i])
        ys = _to_tokens(_ssm(_to_groups(us, L, groups, gi), *tabs, nc=nc, nb=b), L, groups, gi)

        wr = w_router[i].astype(F32)
        wr_hi = wr.astype(BF16)
        wr_lo = (wr - wr_hi.astype(F32)).astype(BF16)
        lane_pad = lambda w: jnp.pad(w, ((0, 0), (0, LANES - ne)))
        wr2 = jnp.concatenate([jnp.concatenate([lane_pad(wr_hi), lane_pad(wr_lo)], axis=1),
                               jnp.concatenate([lane_pad(wr_hi), jnp.zeros((d, LANES), BF16)], axis=1)], axis=0)
        x1, h2, aff = _mixout(fre, ys, gate, xcur, w_fourier[i].astype(BF16), w_glu[i].astype(BF16),
                              w_out[i].astype(BF16), row(g_ffn[i]), wr2, ne)

        aff_rows = aff.reshape(ne * b, s)
        posm, st_blk = _topk(aff_rows, cap)
        starts = st_blk[:, ::GATHER_TILE // TOPK_BLOCK].reshape(-1)
        tiles = (ne, b * (s // GATHER_TILE), GATHER_TILE)
        xg, vals = _gather(starts, h2.reshape(b, s, d), posm.reshape(tiles), aff_rows.reshape(tiles), cap, ne)
        yg = _ffn(xg, vals, w_exp_gate[i], w_exp_up[i], w_exp_down[i], nb_ffn, tf)
        xnext = _combine(starts, yg, posm.reshape(ne, b * s), x1.reshape(b, s, d), p[i],
                         w_ple_proj[i].astype(BF16), w_ple_gate[i].astype(BF16), row(g_ple[i]),
                         row(g_final) if final else row(g_ple[i]), cap, final)
        xcur = xnext.reshape(b * s, d)
    return xcur.reshape(b, s, d)
```

```python
import functools
import math

import jax
import jax.numpy as jnp
from jax import lax
from jax.experimental import pallas as pl
from jax.experimental.pallas import tpu as pltpu

F32 = jnp.float32
BF16 = jnp.bfloat16
I32 = jnp.int32

RMS_EPS = 1e-6
FOURIER_GROUPS = 4
EC_CAPACITY = 2
DFT_N2 = 128
SUBLANES = 8
LANES = 128
PACKED_ROWS = 16
SSM_CHUNK = 32
SSM_TILE_CHUNKS = 128
TOPK_BLOCK = 128
GATHER_TILE = 256
MIX_ROWS = 256
FFN_ROWS = 512
GATHER_WINDOW = 64
GATHER_EXPERTS = 4
GATHER_UNROLL = 4
COMBINE_TILE = 256
COMBINE_WINDOW = 64
COMBINE_SPLIT = 32
VMEM_LIMIT = 56 * 1024 * 1024


def _cparams(*sem):
    return pltpu.CompilerParams(dimension_semantics=sem, vmem_limit_bytes=VMEM_LIMIT)


def _rms(x, g):
    return x * lax.rsqrt(jnp.mean(x * x, axis=-1, keepdims=True) + RMS_EPS) * g


def _dot(a, b):
    return jnp.dot(a, b, preferred_element_type=F32)


def _inproj_kernel(x_ref, g_ref, w_ref, bd_ref, pq_ref, us_ref, gate_ref, *, df, ds):
    h = _rms(x_ref[...], g_ref[...]).astype(BF16)
    z = _dot(h, w_ref[...])
    pq_ref[...] = _dot(z[:, :df].astype(BF16), bd_ref[...]).astype(BF16)
    for q in range(ds // LANES):
        us_ref[q] = z[:, df + q * LANES:df + (q + 1) * LANES]
    gate_ref[...] = jax.nn.sigmoid(z[:, df + ds:]).astype(BF16)


def _inproj(x2, g, w_in, bd, df, ds, tm=512):
    t, d = x2.shape
    dg = w_in.shape[1] - df - ds
    return pl.pallas_call(
        functools.partial(_inproj_kernel, df=df, ds=ds),
        grid=(t // tm,),
        in_specs=[
            pl.BlockSpec((tm, d), lambda i: (i, 0)),
            pl.BlockSpec((1, d), lambda i: (0, 0)),
            pl.BlockSpec(w_in.shape, lambda i: (0, 0)),
            pl.BlockSpec(bd.shape, lambda i: (0, 0)),
        ],
        out_specs=[
            pl.BlockSpec((tm, 2 * df), lambda i: (i, 0)),
            pl.BlockSpec((ds // LANES, tm, LANES), lambda i: (0, i, 0)),
            pl.BlockSpec((tm, dg), lambda i: (i, 0)),
        ],
        out_shape=[
            jax.ShapeDtypeStruct((t, 2 * df), BF16),
            jax.ShapeDtypeStruct((ds // LANES, t, LANES), F32),
            jax.ShapeDtypeStruct((t, dg), BF16),
        ],
        compiler_params=_cparams("parallel"),
        name="inproj",
    )(x2, g, w_in, bd)


def _channel_dft_table(df):
    c = df // FOURIER_GROUPS
    k = jnp.arange(c, dtype=I32)
    ang = (2.0 * math.pi / c) * ((k[:, None] * k[None, :]) % c).astype(F32)
    eye = jnp.eye(FOURIER_GROUPS, dtype=F32)
    scale = 1.0 / math.sqrt(c)
    re = jnp.kron(eye, jnp.cos(ang)) * scale
    im = -jnp.kron(eye, jnp.sin(ang)) * scale
    return jnp.concatenate([re, im], axis=1).astype(BF16)


def _dft1_tables(n1):
    k = jnp.arange(n1, dtype=I32)
    ang = (2.0 * math.pi / n1) * ((k[:, None] * k[None, :]) % n1).astype(F32)
    row = jnp.arange(n1 * SUBLANES, dtype=I32)
    rep = (row[:, None] // SUBLANES == k[None, :]).astype(F32)
    same_slot = (row[:, None] % SUBLANES == row[None, :] % SUBLANES).astype(F32) * (1.0 / math.sqrt(n1))
    hp = lax.Precision.HIGHEST
    kron8 = lambda t: (jnp.dot(jnp.dot(rep, t, precision=hp), rep.T, precision=hp) * same_slot).astype(BF16)
    return kron8(jnp.cos(ang)), kron8(jnp.sin(ang))


def _dft3_tables(n1):
    s = n1 * DFT_N2
    nb = n1 // SUBLANES
    k1 = jnp.arange(n1, dtype=I32)[:, None, None]
    k2 = jnp.arange(DFT_N2, dtype=I32)[None, :, None]
    n2 = jnp.arange(DFT_N2, dtype=I32)[None, None, :]
    ang = (2.0 * math.pi / s) * ((n2 * (k1 + n1 * k2)) % s).astype(F32)
    eye = jnp.eye(SUBLANES, dtype=BF16)[None, None, :, :, None]
    rows = DFT_N2 * SUBLANES

    def expand(t):
        t = (t * (1.0 / math.sqrt(DFT_N2))).astype(BF16).reshape(nb, SUBLANES, DFT_N2, DFT_N2)
        t = jnp.transpose(t, (0, 2, 1, 3))[:, :, :, None, :]
        return (t * eye).reshape(nb, rows, rows)

    return expand(jnp.cos(ang)), expand(jnp.sin(ang))


def _dft1_kernel(z_ref, ck_ref, sk_ref, a_ref):
    n1, slab, c2 = z_ref.shape[1], z_ref.shape[2], z_ref.shape[3]
    c = c2 // 2
    z = z_ref[0].astype(F32)
    halves = []
    for h in range(slab // SUBLANES):
        zh = z[:, h * SUBLANES:(h + 1) * SUBLANES, :].reshape(n1 * SUBLANES, c2).astype(BF16)
        cz = _dot(ck_ref[...], zh)
        sz = _dot(sk_ref[...], zh)
        a = jnp.concatenate([cz[:, :c] + sz[:, c:], cz[:, c:] - sz[:, :c]], axis=1)
        halves.append(a.reshape(n1, SUBLANES, c2))
    a_ref[0] = jnp.concatenate(halves, axis=1).astype(BF16)


def _dft1(pq4, ck, sk):
    b, n1, n2, c2 = pq4.shape
    slab = PACKED_ROWS
    return pl.pallas_call(
        _dft1_kernel,
        grid=(b, n2 // slab),
        in_specs=[
            pl.BlockSpec((1, n1, slab, c2), lambda i, j: (i, 0, j, 0)),
            pl.BlockSpec(ck.shape, lambda i, j: (0, 0)),
            pl.BlockSpec(sk.shape, lambda i, j: (0, 0)),
        ],
        out_specs=pl.BlockSpec((1, n1, slab, c2), lambda i, j: (i, 0, j, 0)),
        out_shape=jax.ShapeDtypeStruct(pq4.shape, BF16),
        compiler_params=_cparams("parallel", "parallel"),
        name="dft1",
    )(pq4, ck, sk)


def _dft3_kernel(a_ref, tr_ref, ti_ref, o_ref):
    nblk, c2 = tr_ref.shape[0], a_ref.shape[3]
    c = c2 // 2
    outs = []
    for h in range(nblk):
        a = a_ref[0, h * SUBLANES:(h + 1) * SUBLANES].reshape(SUBLANES * DFT_N2, c2)
        out = _dot(tr_ref[h], a[:, :c]) + _dot(ti_ref[h], a[:, c:])
        outs.append(out.reshape(DFT_N2, SUBLANES, c))
    o_ref[0] = jnp.concatenate(outs, axis=1).astype(BF16)


def _dft3(a4, tr, ti):
    b, n1, n2, c2 = a4.shape
    c = c2 // 2
    rows = DFT_N2 * SUBLANES
    nblk = PACKED_ROWS // SUBLANES
    return pl.pallas_call(
        _dft3_kernel,
        grid=(n1 // PACKED_ROWS, b),
        in_specs=[
            pl.BlockSpec((1, PACKED_ROWS, n2, c2), lambda k, i: (i, k, 0, 0)),
            pl.BlockSpec((nblk, rows, rows), lambda k, i: (k, 0, 0)),
            pl.BlockSpec((nblk, rows, rows), lambda k, i: (k, 0, 0)),
        ],
        out_specs=pl.BlockSpec((1, DFT_N2, PACKED_ROWS, c), lambda k, i: (i, 0, k, 0)),
        out_shape=jax.ShapeDtypeStruct((b, DFT_N2, n1, c), BF16),
        compiler_params=_cparams("parallel", "parallel"),
        name="dft3",
    )(a4, tr, ti)


def _ssm_tables(a_re, a_im, log_dt, b_re, b_im, c_re, c_im, d):
    L = SSM_CHUNK
    _, g, n = a_re.shape
    gi = b_re.shape[-1]
    dt = jnp.exp(log_dt.astype(F32))[..., None]
    ar, ai = a_re.astype(F32), a_im.astype(F32)
    tau = jnp.arange(L + 1, dtype=F32)[:, None, None, None]
    mag = jnp.exp(tau * (ar * dt)[None])
    ang = tau * (ai * dt)[None]
    pr, pi = mag * jnp.cos(ang), mag * jnp.sin(ang)
    nr, ni = pr[1] - 1.0, pi[1]
    den = ar * ar + ai * ai
    qr, qi = (nr * ar + ni * ai) / den, (ni * ar - nr * ai) / den
    br, bi = b_re.astype(F32), b_im.astype(F32)
    bbr = qr[..., None] * br - qi[..., None] * bi
    bbi = qr[..., None] * bi + qi[..., None] * br
    cr, ci = c_re.astype(F32), c_im.astype(F32)

    cat = jnp.concatenate
    steps = lambda p: jnp.transpose(p, (1, 0, 2))
    pf_r, pf_i, pb_r, pb_i = steps(pr[1:, 0]), steps(pi[1:, 0]), steps(pr[1:, 1][::-1]), steps(pi[1:, 1][::-1])
    f_small = jnp.stack([cat([cr[0], cr[1], cr[0], cr[1]], -1), cat([-ci[0], -ci[1], -ci[0], -ci[1]], -1)], 1)
    f_steps = jnp.stack([cat([pf_r, pb_r, -pf_i, -pb_i], -1), cat([pf_i, pb_i, pf_r, pb_r], -1)], 1)
    last = lambda p: jnp.transpose(p, (1, 2, 0))
    ef_r, ef_i, eb_r, eb_i = last(pr[:L, 0][::-1]), last(pi[:L, 0][::-1]), last(pr[:L, 1]), last(pi[:L, 1])
    e_small = jnp.stack([cat([bbr[0], bbr[1], bbi[0], bbi[1]], 1), cat([-bbi[0], -bbi[1], bbr[0], bbr[1]], 1)], 1)
    e_steps = jnp.stack([cat([ef_r, eb_r, ef_r, eb_r], 1), cat([ef_i, eb_i, ef_i, eb_i], 1)], 1)
    zpad = lambda p, before: jnp.pad(p, ((0, 0), (0, 0), (before, L - before)))
    wf_r, wf_i, wb_r, wb_i = zpad(ef_r, 0), zpad(ef_i, 0), zpad(eb_r, L - 1), zpad(eb_i, L - 1)
    t_small = jnp.stack([cat([bbr[0], bbi[0], bbr[1], bbi[1]], 1), cat([-bbi[0], bbr[0], -bbi[1], bbr[1]], 1)], 1)
    t_steps = jnp.stack([cat([wf_r, wf_r, wb_r, wb_r], 1), cat([wf_i, wf_i, wb_i, wb_i], 1)], 1)
    t_left = cat([cr[0], -ci[0], cr[1], -ci[1]], -1)

    al = jnp.stack([cat([pr[L, 0], pr[L, 1]], -1), cat([pi[L, 0], pi[L, 1]], -1)], axis=1)
    dv = jnp.tile(d.astype(F32).reshape(g, 1, gi), (1, L, 1)).reshape(g, L * gi, 1)
    t_tab, e_tab, f_tab = _ssm_table_call(f_small, f_steps, e_small, e_steps, t_small, t_steps, t_left, L, gi)
    return t_tab, e_tab, f_tab, al, dv


def _ssm_table_kernel(fs_ref, fp_ref, es_ref, ep_ref, ts_ref, tp_ref, tl_ref, ri_ref, rj_ref, rit_ref, rjt_ref,
                      ri2_ref, rs_ref, t_ref, e_ref, f_ref, *, L, gi):
    def split(a):
        hi = a.astype(BF16)
        return hi, (a - hi.astype(F32)).astype(BF16)

    def spread_lanes(a, rep):
        hi, lo = split(a)
        return _dot(hi, rep) + _dot(lo, rep)

    def spread_rows(rep, a):
        hi, lo = split(a)
        return _dot(rep, hi) + _dot(rep, lo)

    ri, rj, rit, rjt, ri2, rs = ri_ref[...], rj_ref[...], rit_ref[...], rjt_ref[...], ri2_ref[...], rs_ref[...]
    f_ref[0] = (spread_rows(rit, fs_ref[0, 0]) * spread_rows(rjt, fp_ref[0, 0])
                + spread_rows(rit, fs_ref[0, 1]) * spread_rows(rjt, fp_ref[0, 1])).astype(BF16)
    e_ref[0] = (spread_lanes(es_ref[0, 0], ri) * spread_lanes(ep_ref[0, 0], rj)
                + spread_lanes(es_ref[0, 1], ri) * spread_lanes(ep_ref[0, 1], rj)).astype(BF16)
    ew = (spread_lanes(ts_ref[0, 0], ri2) * spread_lanes(tp_ref[0, 0], rs)
          + spread_lanes(ts_ref[0, 1], ri2) * spread_lanes(tp_ref[0, 1], rs))
    c_hi, c_lo = split(tl_ref[0])
    e_hi, e_lo = split(ew)
    kw = _dot(c_hi, e_hi) + _dot(c_hi, e_lo) + _dot(c_lo, e_hi)
    per = LANES // gi
    for r in range(per):
        shifted = kw if r == 0 else pltpu.roll(kw, 2 * L * gi - r * gi, axis=1)
        for a in range(L // per):
            jo = L - 1 - (a * per + r)
            t_ref[0, pl.ds(jo * gi, gi), :] = shifted[:, a * LANES:a * LANES + L * gi].astype(BF16)


def _ssm_table_call(f_small, f_steps, e_small, e_steps, t_small, t_steps, t_left, L, gi):
    g = f_small.shape[0]
    n4 = f_small.shape[3]
    k = L * gi
    lane = jnp.arange(2 * k, dtype=I32)
    ri2 = (lane[None, :] % gi == jnp.arange(gi, dtype=I32)[:, None]).astype(BF16)
    rs = (lane[None, :] // gi == jnp.arange(2 * L, dtype=I32)[:, None]).astype(BF16)
    ri, rj = ri2[:, :k], rs[:L, :k]
    per_group = lambda a: pl.BlockSpec((1,) + a.shape[1:], lambda i: (i,) + (0,) * (a.ndim - 1))
    full = lambda a: pl.BlockSpec(a.shape, lambda i: (0,) * a.ndim)
    ins = (f_small, f_steps, e_small, e_steps, t_small, t_steps, t_left)
    reps = (ri, rj, ri.T, rj.T, ri2, rs)
    return pl.pallas_call(
        functools.partial(_ssm_table_kernel, L=L, gi=gi),
        grid=(g,),
        in_specs=[per_group(a) for a in ins] + [full(a) for a in reps],
        out_specs=[pl.BlockSpec((1, k, k), lambda i: (i, 0, 0)), pl.BlockSpec((1, n4, k), lambda i: (i, 0, 0)),
                   pl.BlockSpec((1, k, n4), lambda i: (i, 0, 0))],
        out_shape=[jax.ShapeDtypeStruct((g, k, k), BF16), jax.ShapeDtypeStruct((g, n4, k), BF16),
                   jax.ShapeDtypeStruct((g, k, n4), BF16)],
        compiler_params=_cparams("parallel"),
        name="ssm_tables",
    )(*ins, *reps)


def _to_groups_kernel(u_ref, a_ref, *, L, groups, gi):
    ncl = a_ref.shape[2]
    gq = LANES // gi
    for q in range(u_ref.shape[0]):
        for j in range(L):
            zt = u_ref[q, pl.ds(j, ncl, stride=L), :].T
            a_ref[q * gq:(q + 1) * gq, pl.ds(j * gi, gi), :] = zt.reshape(gq, gi, ncl).astype(BF16)


def _to_groups(us, L, groups, gi):
    nq, t, _ = us.shape
    ncl = SSM_TILE_CHUNKS
    return pl.pallas_call(
        functools.partial(_to_groups_kernel, L=L, groups=groups, gi=gi),
        grid=(t // (ncl * L),),
        in_specs=[pl.BlockSpec((nq, ncl * L, LANES), lambda i: (0, i, 0))],
        out_specs=pl.BlockSpec((groups, L * gi, ncl), lambda i: (0, 0, i)),
        out_shape=jax.ShapeDtypeStruct((groups, L * gi, t // L), BF16),
        compiler_params=_cparams("parallel"),
        name="to_groups",
    )(us)


def _to_tokens_kernel(y_ref, o_ref, *, L, groups, gi):
    ncl = y_ref.shape[2]
    gq = LANES // gi
    for q in range(o_ref.shape[0]):
        for j in range(L):
            yj = y_ref[q * gq:(q + 1) * gq, pl.ds(j * gi, gi), :].astype(F32).reshape(LANES, ncl)
            o_ref[q, pl.ds(j, ncl, stride=L), :] = yj.T


def _to_tokens(yt, L, groups, gi):
    _, k, nchunks = yt.shape
    ncl = SSM_TILE_CHUNKS
    return pl.pallas_call(
        functools.partial(_to_tokens_kernel, L=L, groups=groups, gi=gi),
        grid=(nchunks // ncl,),
        in_specs=[pl.BlockSpec((groups, k, ncl), lambda i: (0, 0, i))],
        out_specs=pl.BlockSpec((groups * gi // LANES, ncl * L, LANES), lambda i: (0, i, 0)),
        out_shape=jax.ShapeDtypeStruct((groups * gi // LANES, nchunks * L, LANES), F32),
        compiler_params=_cparams("parallel"),
        name="to_tokens",
    )(yt)


def _ssm_kernel(a_ref, t_ref, e_ref, f_ref, al_ref, dv_ref, y_ref, s_scr, h_scr, *, nc, nb, rows, n):
    a = a_ref[0]
    y1 = _dot(t_ref[0], a) + dv_ref[0] * a.astype(F32)
    st = _dot(e_ref[0], a)
    n2 = 2 * n
    s_scr[...] = jnp.zeros_like(s_scr)
    for b in range(nb):
        sb_t = st[:, b * nc:(b + 1) * nc].T
        for q in range(2):
            s_scr[q, pl.ds(b, nc, stride=rows), :] = sb_t[:, q * n2:(q + 1) * n2]
    ar = al_ref[0, 0:1, :]
    ai = al_ref[0, 1:2, :]
    is_fwd = lax.broadcasted_iota(I32, (rows, n2), 1) < n

    def step(i, carry):
        hr, hi = carry
        rf = pl.ds(pl.multiple_of(i * rows, rows), rows)
        rb = pl.ds(pl.multiple_of((nc - 1 - i) * rows, rows), rows)
        h_scr[0, rf, 0:n] = hr[:, 0:n]
        h_scr[1, rf, 0:n] = hi[:, 0:n]
        h_scr[0, rb, n:n2] = hr[:, n:n2]
        h_scr[1, rb, n:n2] = hi[:, n:n2]
        sr = jnp.where(is_fwd, s_scr[0, rf, :], s_scr[0, rb, :])
        si = jnp.where(is_fwd, s_scr[1, rf, :], s_scr[1, rb, :])
        return ar * hr - ai * hi + sr, ar * hi + ai * hr + si

    zero = jnp.zeros((rows, n2), F32)
    lax.fori_loop(0, nc, step, (zero, zero))
    ht = jnp.concatenate(
        [jnp.concatenate([h_scr[q, pl.ds(b, nc, stride=rows), :].T for q in range(2)], axis=0) for b in range(nb)],
        axis=1)
    y = y1 + _dot(f_ref[0], ht.astype(BF16))
    y_ref[0] = jax.nn.gelu(y, approximate=True).astype(BF16)


def _ssm(ag, t_tab, e_tab, f_tab, al, dv, nc, nb):
    g, k, m = ag.shape
    n4 = e_tab.shape[1]
    rows = -(-nb // SUBLANES) * SUBLANES
    return pl.pallas_call(
        functools.partial(_ssm_kernel, nc=nc, nb=nb, rows=rows, n=n4 // 4),
        grid=(g,),
        in_specs=[
            pl.BlockSpec((1, k, m), lambda i: (i, 0, 0)),
            pl.BlockSpec((1, k, k), lambda i: (i, 0, 0)),
            pl.BlockSpec((1, n4, k), lambda i: (i, 0, 0)),
            pl.BlockSpec((1, k, n4), lambda i: (i, 0, 0)),
            pl.BlockSpec((1, 2, n4 // 2), lambda i: (i, 0, 0)),
            pl.BlockSpec((1, k, 1), lambda i: (i, 0, 0)),
        ],
        out_specs=pl.BlockSpec((1, k, m), lambda i: (i, 0, 0)),
        out_shape=jax.ShapeDtypeStruct((g, k, m), BF16),
        scratch_shapes=[pltpu.VMEM((2, nc * rows, n4 // 2), F32), pltpu.VMEM((2, nc * rows, n4 // 2), F32)],
        compiler_params=_cparams("parallel"),
        name="ssm",
    )(ag, t_tab, e_tab, f_tab, al, dv)


def _mixout_kernel(fre_ref, ys_ref, gate_ref, x_ref, wf_ref, wglu_ref, wout_ref, gffn_ref, wr_ref,
                   x1_ref, h2_ref, aff_ref, *, d, ne):
    for r in range(fre_ref.shape[0] // MIX_ROWS):
        rows = pl.ds(r * MIX_ROWS, MIX_ROWS)
        y_f = _dot(fre_ref[rows, :], wf_ref[...])
        ys = jnp.concatenate([ys_ref[q, rows, :] for q in range(ys_ref.shape[0])], axis=1)
        vg = _dot(ys.astype(BF16), wglu_ref[...])
        y_s = vg[:, :d] * jax.nn.sigmoid(vg[:, d:])
        gate = gate_ref[rows, :].astype(F32)
        m = gate[:, :d] * y_f + gate[:, d:] * y_s
        x1 = x_ref[rows, :] + _dot(m.astype(BF16), wout_ref[...])
        x1_ref[rows, :] = x1
        h2 = _rms(x1, gffn_ref[...])
        h2_ref[rows, :] = h2.astype(BF16)
        hi = h2.astype(BF16)
        lo = (h2 - hi.astype(F32)).astype(BF16)
        rl = _dot(jnp.concatenate([hi, lo], axis=1), wr_ref[...])
        logits = rl[:, :LANES] + rl[:, LANES:]
        logits = jnp.where(lax.broadcasted_iota(I32, logits.shape, 1) < ne, logits, -1e30)
        logits = logits - jnp.max(logits, axis=-1, keepdims=True)
        p = jnp.exp(logits)
        aff = p / jnp.sum(p, axis=-1, keepdims=True)
        aff_ref[:, rows] = aff.T[:ne, :]


def _mixout(fre, ys, gate, x2, wf, wglu, wout, gffn, wr2, ne, tm=512):
    t, d = x2.shape
    full = lambda a: pl.BlockSpec(a.shape, lambda i: (0,) * a.ndim)
    row = lambda a: pl.BlockSpec((tm, a.shape[1]), lambda i: (i, 0))
    return pl.pallas_call(
        functools.partial(_mixout_kernel, d=d, ne=ne),
        grid=(t // tm,),
        in_specs=[row(fre), pl.BlockSpec((ys.shape[0], tm, LANES), lambda i: (0, i, 0)), row(gate), row(x2),
                  full(wf), full(wglu), full(wout), full(gffn), full(wr2)],
        out_specs=[
            pl.BlockSpec((tm, d), lambda i: (i, 0)),
            pl.BlockSpec((tm, d), lambda i: (i, 0)),
            pl.BlockSpec((ne, tm), lambda i: (0, i)),
        ],
        out_shape=[
            jax.ShapeDtypeStruct((t, d), F32),
            jax.ShapeDtypeStruct((t, d), BF16),
            jax.ShapeDtypeStruct((ne, t), F32),
        ],
        compiler_params=_cparams("parallel"),
        name="mixout",
    )(fre, ys, gate, x2, wf, wglu, wout, gffn, wr2)


def _topk_kernel(aff_ref, pos_ref, st_ref, *, cap, blk):
    v = aff_ref[...]
    r, s = v.shape
    capf = float(cap)

    def bit_step(i, t):
        cand = t | (jnp.int32(1) << (30 - i))
        cnt = jnp.sum(jnp.where(v >= pltpu.bitcast(cand, F32), 1.0, 0.0), axis=1, keepdims=True)
        return jnp.where(cnt >= capf, cand, t)

    thr = lax.fori_loop(0, 31, bit_step, jnp.zeros((r, 1), I32))
    gt = jnp.where(v >= pltpu.bitcast(thr + 1, F32), 1.0, 0.0)
    eq = jnp.where(v >= pltpu.bitcast(thr, F32), 1.0, 0.0) - gt
    need = capf - jnp.sum(gt, axis=1, keepdims=True)
    ii = lax.broadcasted_iota(I32, (blk, blk), 0)
    jj = lax.broadcasted_iota(I32, (blk, blk), 1)
    tri = jnp.where(ii < jj, 1.0, 0.0).astype(BF16)
    run_eq = jnp.zeros((r, 1), F32)
    run = jnp.zeros((r, 1), F32)
    for k in range(s // blk):
        sl = slice(k * blk, (k + 1) * blk)
        eqb, gtb = eq[:, sl], gt[:, sl]
        rank_eq = _dot(eqb.astype(BF16), tri) + run_eq
        run_eq = run_eq + jnp.sum(eqb, axis=1, keepdims=True)
        mask = gtb + eqb * jnp.where(rank_eq < need, 1.0, 0.0)
        pos = _dot(mask.astype(BF16), tri) + run
        st_ref[:, k:k + 1] = run.astype(I32)
        run = run + jnp.sum(mask, axis=1, keepdims=True)
        pos_ref[:, sl] = jnp.where(mask > 0.0, pos, -1.0).astype(I32)
    st_ref[:, s // blk:s // blk + 1] = run.astype(I32)


def _topk(aff_rows, cap, blk=TOPK_BLOCK):
    r, s = aff_rows.shape
    nblk = s // blk
    return pl.pallas_call(
        functools.partial(_topk_kernel, cap=cap, blk=blk),
        grid=(1,),
        in_specs=[pl.BlockSpec((r, s), lambda i: (0, 0))],
        out_specs=[pl.BlockSpec((r, s), lambda i: (0, 0)), pl.BlockSpec((r, nblk + 1), lambda i: (0, 0))],
        out_shape=[jax.ShapeDtypeStruct((r, s), I32), jax.ShapeDtypeStruct((r, nblk + 1), I32)],
        compiler_params=_cparams("arbitrary"),
        name="topk",
    )(aff_rows)


def _floor_rows(x):
    return (x // PACKED_ROWS) * PACKED_ROWS


def _num_passes(starts, ends, win):
    need = ends[0] - _floor_rows(starts[0])
    for a, b in zip(starts[1:], ends[1:]):
        need = jnp.maximum(need, b - _floor_rows(a))
    return (need + win - 1) // win


def _pass_window(start, p, win, cap):
    first = _floor_rows(start) + p * win
    return first, pl.multiple_of(jnp.minimum(first, cap - win), PACKED_ROWS)


def _gather_kernel(st_ref, h_ref, pos_ref, aff_ref, x_ref, v_ref, *, nt, tt, win, cap, nbatch, ng):
    rows = [((pl.program_id(1) * ng + g) * nbatch + pl.program_id(0)) * (nt + 1) for g in range(ng)]
    x_ref[...] = jnp.zeros_like(x_ref)
    v_ref[...] = jnp.zeros_like(v_ref)
    riota = lax.broadcasted_iota(I32, (win, tt), 0)

    def bounds(t):
        return [st_ref[r + t] for r in rows], [st_ref[r + t + 1] for r in rows]

    def one_pass(t, p, starts):
        hrows = h_ref[0, pl.ds(pl.multiple_of(t * tt, tt), tt), :]
        hots, wins = [], []
        for g in range(ng):
            first, ws = _pass_window(starts[g], p, win, cap)
            pos = pos_ref[g, pl.ds(t, 1), :]
            hots.append((pos - ws == riota) & (pos >= first))
            wins.append(ws)
        onehot = jnp.concatenate([jnp.where(h, 1.0, 0.0).astype(BF16) for h in hots], axis=0)
        rows = _dot(onehot, hrows)
        for g in range(ng):
            x_ref[0, g, pl.ds(wins[g], win), :] += rows[g * win:(g + 1) * win].astype(BF16)
            vals = jnp.sum(jnp.where(hots[g], aff_ref[g, pl.ds(t, 1), :], 0.0), axis=1, keepdims=True)
            v_ref[0, g, pl.ds(wins[g], win), :] += vals

    def first_pass(t, most):
        starts, ends = bounds(t)
        one_pass(t, 0, starts)
        return jnp.maximum(most, _num_passes(starts, ends, win))

    def more_passes(t, carry):
        starts, ends = bounds(t)

        def body(p, c):
            one_pass(t, p, starts)
            return c

        lax.fori_loop(1, _num_passes(starts, ends, win), body, 0)
        return carry

    most = lax.fori_loop(0, nt, first_pass, jnp.int32(0), unroll=GATHER_UNROLL)

    @pl.when(most > 1)
    def _():
        lax.fori_loop(0, nt, more_passes, 0)


def _gather(starts, h3, pos3, aff3t, cap, ne):
    b, s, d = h3.shape
    tt, win, ng = GATHER_TILE, GATHER_WINDOW, GATHER_EXPERTS
    nt = s // tt
    grid_spec = pltpu.PrefetchScalarGridSpec(
        num_scalar_prefetch=1,
        grid=(b, ne // ng),
        in_specs=[
            pl.BlockSpec((1, s, d), lambda i, e, st: (i, 0, 0), pipeline_mode=pl.Buffered(1)),
            pl.BlockSpec((ng, nt, tt), lambda i, e, st: (e, i, 0)),
            pl.BlockSpec((ng, nt, tt), lambda i, e, st: (e, i, 0)),
        ],
        out_specs=[
            pl.BlockSpec((1, ng, cap, d), lambda i, e, st: (i, e, 0, 0)),
            pl.BlockSpec((1, ng, cap, 1), lambda i, e, st: (i, e, 0, 0)),
        ],
    )
    return pl.pallas_call(
        functools.partial(_gather_kernel, nt=nt, tt=tt, win=win, cap=cap, nbatch=b, ng=ng),
        grid_spec=grid_spec,
        out_shape=[jax.ShapeDtypeStruct((b, ne, cap, d), BF16), jax.ShapeDtypeStruct((b, ne, cap, 1), F32)],
        compiler_params=_cparams("arbitrary", "arbitrary"),
        name="gather",
    )(starts, h3, pos3, aff3t)


def _ffn_kernel(x_ref, v_ref, wg_ref, wu_ref, wd_ref, y_ref, acc_ref, *, nf):
    f = pl.program_id(2)
    nb, _, cap, d = x_ref.shape

    @pl.when(f == 0)
    def _():
        acc_ref[...] = jnp.zeros_like(acc_ref)

    wg = wg_ref[0].astype(BF16)
    wu = wu_ref[0].astype(BF16)
    wd = wd_ref[0].astype(BF16)
    mr = min(FFN_ROWS, cap)
    for r in range(nb * cap // mr):
        x = x_ref[r * mr // cap, 0, pl.ds(r * mr % cap, mr), :]
        g = _dot(x, wg)
        u = _dot(x, wu)
        hid = (g * jax.nn.sigmoid(g) * u).astype(BF16)
        acc_ref[pl.ds(r * mr, mr), :] += _dot(hid, wd)

    @pl.when(f == nf - 1)
    def _():
        y = acc_ref[...] * v_ref[...].reshape(nb * cap, 1)
        y_ref[...] = y.astype(BF16).reshape(nb, 1, cap, d)


def _ffn(xg, vals, wg, wu, wd, nb, tf):
    b, ne, cap, d = xg.shape
    dexp = wg.shape[2]
    nf = dexp // tf
    return pl.pallas_call(
        functools.partial(_ffn_kernel, nf=nf),
        grid=(ne, b // nb, nf),
        in_specs=[
            pl.BlockSpec((nb, 1, cap, d), lambda e, i, f: (i, e, 0, 0)),
            pl.BlockSpec((nb, 1, cap, 1), lambda e, i, f: (i, e, 0, 0)),
            pl.BlockSpec((1, d, tf), lambda e, i, f: (e, 0, f)),
            pl.BlockSpec((1, d, tf), lambda e, i, f: (e, 0, f)),
            pl.BlockSpec((1, tf, d), lambda e, i, f: (e, f, 0)),
        ],
        out_specs=pl.BlockSpec((nb, 1, cap, d), lambda e, i, f: (i, e, 0, 0)),
        out_shape=jax.ShapeDtypeStruct(xg.shape, BF16),
        scratch_shapes=[pltpu.VMEM((nb * cap, d), F32)],
        compiler_params=_cparams("parallel", "parallel", "arbitrary"),
        name="ffn",
    )(xg, vals, wg, wu, wd)


def _combine_kernel(st_ref, y_ref, pos_ref, spread_ref, lanes_ref, x1_ref, p_ref, wpp_ref, wpg_ref, gple_ref,
                    gout_ref, o_ref, *, nt, tt, win, cap, ne, nbatch, final):
    t = pl.program_id(1)
    rows = [(e * nbatch + pl.program_id(0)) * (nt + 1) + t for e in range(ne)]
    starts = [st_ref[r] for r in rows]
    ends = [st_ref[r + 1] for r in rows]
    k = ne * win
    lane_e = lanes_ref[0:1, :]
    lane_r = lanes_ref[1:2, :]
    pos = pos_ref[...].astype(F32)
    pos = jnp.concatenate([pos, jnp.zeros((LANES - ne, tt), F32)], axis=0).T
    hi = jnp.floor(pos * (1.0 / COMBINE_SPLIT))
    lo = pos - hi * COMBINE_SPLIT
    rank = _dot(jnp.concatenate([hi, lo], axis=1).astype(BF16), spread_ref[...])

    def expert_rows(p):
        firsts = jnp.zeros((1, k), F32)
        offs = jnp.zeros((1, k), F32)
        wins = []
        for e in range(ne):
            first, ws = _pass_window(starts[e], p, win, cap)
            firsts = jnp.where(lane_e == e, first.astype(F32), firsts)
            offs = jnp.where(lane_e == e, ws.astype(F32), offs)
            wins.append(y_ref[0, e, pl.ds(ws, win), :])
        onehot = jnp.where((rank - offs == lane_r) & (rank >= firsts), 1.0, 0.0).astype(BF16)
        return _dot(onehot, jnp.concatenate(wins, axis=0))

    def finish(acc):
        for r in range(tt // LANES):
            rows = pl.ds(r * LANES, LANES)
            x2 = acc[r * LANES:(r + 1) * LANES]
            emb = _dot(p_ref[0, rows, :].astype(BF16), wpp_ref[...])
            gate = jax.nn.sigmoid(_dot(_rms(x2, gple_ref[...]).astype(BF16), wpg_ref[...]))
            x3 = x2 + gate * emb
            o_ref[0, rows, :] = _rms(x3, gout_ref[...]) if final else x3

    first_pass = x1_ref[0] + expert_rows(0)
    finish(first_pass)
    npass = _num_passes(starts, ends, win)

    @pl.when(npass > 1)
    def _():
        finish(lax.fori_loop(1, npass, lambda p, acc: acc + expert_rows(p), x1_ref[0] + expert_rows(0)))


def _combine(starts, yg, post, x13, p3, wpp, wpg, gple, gout, cap, final):
    b, ne, _, d = yg.shape
    s = x13.shape[1]
    tt, win = COMBINE_TILE, COMBINE_WINDOW
    nt = s // tt
    lane = jnp.arange(ne * win, dtype=I32)
    part = jnp.arange(2 * LANES, dtype=I32)[:, None]
    spread = jnp.where(part == lane // win, float(COMBINE_SPLIT), jnp.where(part == lane // win + LANES, 1.0, 0.0))
    spread = spread.astype(BF16)
    lanes = jnp.stack([lane // win, lane % win]).astype(F32)
    tile = lambda a: pl.BlockSpec((1, tt, a.shape[2]), lambda i, t, st: (i, t, 0))
    full = lambda a: pl.BlockSpec(a.shape, lambda i, t, st: (0,) * a.ndim)
    grid_spec = pltpu.PrefetchScalarGridSpec(
        num_scalar_prefetch=1,
        grid=(b, nt),
        in_specs=[
            pl.BlockSpec((1, ne, cap, d), lambda i, t, st: (i, 0, 0, 0), pipeline_mode=pl.Buffered(1)),
            pl.BlockSpec((ne, tt), lambda i, t, st: (0, i * nt + t)),
            full(spread), full(lanes), tile(x13), tile(p3), full(wpp), full(wpg), full(gple), full(gout),
        ],
        out_specs=pl.BlockSpec((1, tt, d), lambda i, t, st: (i, t, 0)),
    )
    return pl.pallas_call(
        functools.partial(_combine_kernel, nt=nt, tt=tt, win=win, cap=cap, ne=ne, nbatch=b, final=final),
        grid_spec=grid_spec,
        out_shape=jax.ShapeDtypeStruct(x13.shape, F32),
        compiler_params=_cparams("arbitrary", "arbitrary"),
        name="combine",
    )(starts, yg, post, spread, lanes, x13, p3, wpp, wpg, gple, gout)


def kernel(x, p, g_mix, w_in, w_fourier, ssm_a_re, ssm_a_im, ssm_log_dt, ssm_b_re, ssm_b_im, ssm_c_re, ssm_c_im, ssm_d, w_glu, w_out, g_ffn, w_router, w_exp_gate, w_exp_up, w_exp_down, g_ple, w_ple_gate, w_ple_proj, g_final):
    b, s, d = x.shape
    depth = p.shape[0]
    df = w_fourier.shape[1]
    ds = w_glu.shape[1]
    ne = w_router.shape[2]
    dexp = w_exp_gate.shape[3]
    groups, gi = ssm_b_re.shape[2], ssm_b_re.shape[4]
    L = SSM_CHUNK
    n1 = s // DFT_N2
    nc = s // L
    cap = EC_CAPACITY * s // ne
    assert s % (DFT_N2 * PACKED_ROWS) == 0 and s % GATHER_TILE == 0 and cap % PACKED_ROWS == 0
    assert nc % SSM_TILE_CHUNKS == 0 and gi == PACKED_ROWS and cap // COMBINE_SPLIT < 256
    assert 2 * ssm_a_re.shape[3] == LANES and ds % LANES == 0
    assert cap >= COMBINE_WINDOW and ne % GATHER_EXPERTS == 0 and GATHER_TILE == COMBINE_TILE
    assert ds == groups * gi and df % FOURIER_GROUPS == 0

    bd = _channel_dft_table(df)
    ck, sk = _dft1_tables(n1)
    tr, ti = _dft3_tables(n1)
    tf = 256 if dexp % 256 == 0 else dexp
    nb_ffn = 2 if b % 2 == 0 else 1

    xcur = x.reshape(b * s, d)
    for i in range(depth):
        final = i == depth - 1
        row = lambda v: v.astype(F32).reshape(1, -1)
        pq, us, gate = _inproj(xcur, row(g_mix[i]), w_in[i].astype(BF16), bd, df, ds)
        a4 = _dft1(pq.reshape(b, n1, DFT_N2, 2 * df), ck, sk)
        fre = _dft3(a4, tr, ti).reshape(b * s, df)

        tabs = _ssm_tables(ssm_a_re[i], ssm_a_im[i], ssm_log_dt[i], ssm_b_re[i], ssm_b_im[i],
                           ssm_c_re[i], ssm_c_im[i], ssm_d[i])
        ys = _to_tokens(_ssm(_to_groups(us, L, groups, gi), *tabs, nc=nc, nb=b), L, groups, gi)

        wr = w_router[i].astype(F32)
        wr_hi = wr.astype(BF16)
        wr_lo = (wr - wr_hi.astype(F32)).astype(BF16)
        lane_pad = lambda w: jnp.pad(w, ((0, 0), (0, LANES - ne)))
        wr2 = jnp.concatenate([jnp.concatenate([lane_pad(wr_hi), lane_pad(wr_lo)], axis=1),
                               jnp.concatenate([lane_pad(wr_hi), jnp.zeros((d, LANES), BF16)], axis=1)], axis=0)
        x1, h2, aff = _mixout(fre, ys, gate, xcur, w_fourier[i].astype(BF16), w_glu[i].astype(BF16),
                              w_out[i].astype(BF16), row(g_ffn[i]), wr2, ne)

        aff_rows = aff.reshape(ne * b, s)
        posm, st_blk = _topk(aff_rows, cap)
        starts = st_blk[:, ::GATHER_TILE // TOPK_BLOCK].reshape(-1)
        tiles = (ne, b * (s // GATHER_TILE), GATHER_TILE)
        xg, vals = _gather(starts, h2.reshape(b, s, d), posm.reshape(tiles), aff_rows.reshape(tiles), cap, ne)
        yg = _ffn(xg, vals, w_exp_gate[i], w_exp_up[i], w_exp_down[i], nb_ffn, tf)
        xnext = _combine(starts, yg, posm.reshape(ne, b * s), x1.reshape(b, s, d), p[i],
                         w_ple_proj[i].astype(BF16), w_ple_gate[i].astype(BF16), row(g_ple[i]),
                         row(g_final) if final else row(g_ple[i]), cap, final)
        xcur = xnext.reshape(b * s, d)
    return xcur.reshape(b, s, d)
```

```python
import functools
import math

import jax
import jax.numpy as jnp
from jax import lax
from jax.experimental import pallas as pl
from jax.experimental.pallas import tpu as pltpu

F32 = jnp.float32
BF16 = jnp.bfloat16
I32 = jnp.int32

RMS_EPS = 1e-6
FOURIER_GROUPS = 4
EC_CAPACITY = 2
DFT_N2 = 128
SUBLANES = 8
LANES = 128
PACKED_ROWS = 16
SSM_CHUNK = 32
ROW_TILE = 512
SSM_TILE_CHUNKS = 128
TOPK_BLOCK = 128
GATHER_TILE = 256
MIX_ROWS = 256
FFN_ROWS = 512
GATHER_WINDOW = 64
GATHER_EXPERTS = 4
GATHER_UNROLL = 4
COMBINE_TILE = 256
COMBINE_WINDOW = 64
COMBINE_SPLIT = 32
VMEM_LIMIT = 56 * 1024 * 1024


def _cparams(*sem):
    return pltpu.CompilerParams(dimension_semantics=sem, vmem_limit_bytes=VMEM_LIMIT)


def _rms(x, g):
    return x * lax.rsqrt(jnp.mean(x * x, axis=-1, keepdims=True) + RMS_EPS) * g


def _dot(a, b):
    return jnp.dot(a, b, preferred_element_type=F32)


def _inproj_kernel(x_ref, g_ref, w_ref, bd_ref, pq_ref, us_ref, gate_ref, *, df, ds):
    h = _rms(x_ref[...], g_ref[...]).astype(BF16)
    z = _dot(h, w_ref[...])
    pq_ref[...] = _dot(z[:, :df].astype(BF16), bd_ref[...]).astype(BF16)
    nck = z.shape[0] // SSM_CHUNK
    for q in range(ds // LANES):
        for c in range(nck):
            us_ref[q, pl.ds(c, SSM_CHUNK, stride=nck), :] = z[c * SSM_CHUNK:(c + 1) * SSM_CHUNK,
                                                              df + q * LANES:df + (q + 1) * LANES]
    gate_ref[...] = jax.nn.sigmoid(z[:, df + ds:]).astype(BF16)


def _inproj(x2, g, w_in, bd, df, ds, tm=ROW_TILE):
    t, d = x2.shape
    dg = w_in.shape[1] - df - ds
    return pl.pallas_call(
        functools.partial(_inproj_kernel, df=df, ds=ds),
        grid=(t // tm,),
        in_specs=[
            pl.BlockSpec((tm, d), lambda i: (i, 0)),
            pl.BlockSpec((1, d), lambda i: (0, 0)),
            pl.BlockSpec(w_in.shape, lambda i: (0, 0)),
            pl.BlockSpec(bd.shape, lambda i: (0, 0)),
        ],
        out_specs=[
            pl.BlockSpec((tm, 2 * df), lambda i: (i, 0)),
            pl.BlockSpec((ds // LANES, tm, LANES), lambda i: (0, i, 0)),
            pl.BlockSpec((tm, dg), lambda i: (i, 0)),
        ],
        out_shape=[
            jax.ShapeDtypeStruct((t, 2 * df), BF16),
            jax.ShapeDtypeStruct((ds // LANES, t, LANES), F32),
            jax.ShapeDtypeStruct((t, dg), BF16),
        ],
        compiler_params=_cparams("parallel"),
        name="inproj",
    )(x2, g, w_in, bd)


def _channel_dft_table(df):
    c = df // FOURIER_GROUPS
    k = jnp.arange(c, dtype=I32)
    ang = (2.0 * math.pi / c) * ((k[:, None] * k[None, :]) % c).astype(F32)
    eye = jnp.eye(FOURIER_GROUPS, dtype=F32)
    scale = 1.0 / math.sqrt(c)
    re = jnp.kron(eye, jnp.cos(ang)) * scale
    im = -jnp.kron(eye, jnp.sin(ang)) * scale
    return jnp.concatenate([re, im], axis=1).astype(BF16)


def _dft1_tables(n1):
    k = jnp.arange(n1, dtype=I32)
    ang = (2.0 * math.pi / n1) * ((k[:, None] * k[None, :]) % n1).astype(F32)
    row = jnp.arange(n1 * SUBLANES, dtype=I32)
    rep = (row[:, None] // SUBLANES == k[None, :]).astype(F32)
    same_slot = (row[:, None] % SUBLANES == row[None, :] % SUBLANES).astype(F32) * (1.0 / math.sqrt(n1))
    hp = lax.Precision.HIGHEST
    kron8 = lambda t: (jnp.dot(jnp.dot(rep, t, precision=hp), rep.T, precision=hp) * same_slot).astype(BF16)
    return kron8(jnp.cos(ang)), kron8(jnp.sin(ang))


def _dft3_tables(n1):
    s = n1 * DFT_N2
    nb = n1 // SUBLANES
    k1 = jnp.arange(n1, dtype=I32)[:, None, None]
    k2 = jnp.arange(DFT_N2, dtype=I32)[None, :, None]
    n2 = jnp.arange(DFT_N2, dtype=I32)[None, None, :]
    ang = (2.0 * math.pi / s) * ((n2 * (k1 + n1 * k2)) % s).astype(F32)
    eye = jnp.eye(SUBLANES, dtype=BF16)[None, None, :, :, None]
    rows = DFT_N2 * SUBLANES

    def expand(t):
        t = (t * (1.0 / math.sqrt(DFT_N2))).astype(BF16).reshape(nb, SUBLANES, DFT_N2, DFT_N2)
        t = jnp.transpose(t, (0, 2, 1, 3))[:, :, :, None, :]
        return (t * eye).reshape(nb, rows, rows)

    return expand(jnp.cos(ang)), expand(jnp.sin(ang))


def _dft1_kernel(z_ref, ck_ref, sk_ref, a_ref):
    n1, slab, c2 = z_ref.shape[1], z_ref.shape[2], z_ref.shape[3]
    c = c2 // 2
    z = z_ref[0].astype(F32)
    halves = []
    for h in range(slab // SUBLANES):
        zh = z[:, h * SUBLANES:(h + 1) * SUBLANES, :].reshape(n1 * SUBLANES, c2).astype(BF16)
        cz = _dot(ck_ref[...], zh)
        sz = _dot(sk_ref[...], zh)
        a = jnp.concatenate([cz[:, :c] + sz[:, c:], cz[:, c:] - sz[:, :c]], axis=1)
        halves.append(a.reshape(n1, SUBLANES, c2))
    a_ref[0] = jnp.concatenate(halves, axis=1).astype(BF16)


def _dft1(pq4, ck, sk):
    b, n1, n2, c2 = pq4.shape
    slab = PACKED_ROWS
    return pl.pallas_call(
        _dft1_kernel,
        grid=(b, n2 // slab),
        in_specs=[
            pl.BlockSpec((1, n1, slab, c2), lambda i, j: (i, 0, j, 0)),
            pl.BlockSpec(ck.shape, lambda i, j: (0, 0)),
            pl.BlockSpec(sk.shape, lambda i, j: (0, 0)),
        ],
        out_specs=pl.BlockSpec((1, n1, slab, c2), lambda i, j: (i, 0, j, 0)),
        out_shape=jax.ShapeDtypeStruct(pq4.shape, BF16),
        compiler_params=_cparams("parallel", "parallel"),
        name="dft1",
    )(pq4, ck, sk)


def _dft3_kernel(a_ref, tr_ref, ti_ref, o_ref):
    nblk, c2 = tr_ref.shape[0], a_ref.shape[3]
    c = c2 // 2
    outs = []
    for h in range(nblk):
        a = a_ref[0, h * SUBLANES:(h + 1) * SUBLANES].reshape(SUBLANES * DFT_N2, c2)
        out = _dot(tr_ref[h], a[:, :c]) + _dot(ti_ref[h], a[:, c:])
        outs.append(out.reshape(DFT_N2, SUBLANES, c))
    o_ref[0] = jnp.concatenate(outs, axis=1).astype(BF16)


def _dft3(a4, tr, ti):
    b, n1, n2, c2 = a4.shape
    c = c2 // 2
    rows = DFT_N2 * SUBLANES
    nblk = PACKED_ROWS // SUBLANES
    return pl.pallas_call(
        _dft3_kernel,
        grid=(n1 // PACKED_ROWS, b),
        in_specs=[
            pl.BlockSpec((1, PACKED_ROWS, n2, c2), lambda k, i: (i, k, 0, 0)),
            pl.BlockSpec((nblk, rows, rows), lambda k, i: (k, 0, 0)),
            pl.BlockSpec((nblk, rows, rows), lambda k, i: (k, 0, 0)),
        ],
        out_specs=pl.BlockSpec((1, DFT_N2, PACKED_ROWS, c), lambda k, i: (i, 0, k, 0)),
        out_shape=jax.ShapeDtypeStruct((b, DFT_N2, n1, c), BF16),
        compiler_params=_cparams("parallel", "parallel"),
        name="dft3",
    )(a4, tr, ti)


def _ssm_tables(a_re, a_im, log_dt, b_re, b_im, c_re, c_im, d):
    L = SSM_CHUNK
    _, g, n = a_re.shape
    gi = b_re.shape[-1]
    dt = jnp.exp(log_dt.astype(F32))[..., None]
    ar, ai = a_re.astype(F32), a_im.astype(F32)
    tau = jnp.arange(L + 1, dtype=F32)[:, None, None, None]
    mag = jnp.exp(tau * (ar * dt)[None])
    ang = tau * (ai * dt)[None]
    pr, pi = mag * jnp.cos(ang), mag * jnp.sin(ang)
    nr, ni = pr[1] - 1.0, pi[1]
    den = ar * ar + ai * ai
    qr, qi = (nr * ar + ni * ai) / den, (ni * ar - nr * ai) / den
    br, bi = b_re.astype(F32), b_im.astype(F32)
    bbr = qr[..., None] * br - qi[..., None] * bi
    bbi = qr[..., None] * bi + qi[..., None] * br
    cr, ci = c_re.astype(F32), c_im.astype(F32)

    cat = jnp.concatenate
    steps = lambda p: jnp.transpose(p, (1, 0, 2))
    pf_r, pf_i, pb_r, pb_i = steps(pr[1:, 0]), steps(pi[1:, 0]), steps(pr[1:, 1][::-1]), steps(pi[1:, 1][::-1])
    f_small = jnp.stack([cat([cr[0], cr[1], cr[0], cr[1]], -1), cat([-ci[0], -ci[1], -ci[0], -ci[1]], -1)], 1)
    f_steps = jnp.stack([cat([pf_r, pb_r, -pf_i, -pb_i], -1), cat([pf_i, pb_i, pf_r, pb_r], -1)], 1)
    last = lambda p: jnp.transpose(p, (1, 2, 0))
    ef_r, ef_i, eb_r, eb_i = last(pr[:L, 0][::-1]), last(pi[:L, 0][::-1]), last(pr[:L, 1]), last(pi[:L, 1])
    e_small = jnp.stack([cat([bbr[0], bbr[1], bbi[0], bbi[1]], 1), cat([-bbi[0], -bbi[1], bbr[0], bbr[1]], 1)], 1)
    e_steps = jnp.stack([cat([ef_r, eb_r, ef_r, eb_r], 1), cat([ef_i, eb_i, ef_i, eb_i], 1)], 1)
    zpad = lambda p, before: jnp.pad(p, ((0, 0), (0, 0), (before, L - before)))
    wf_r, wf_i, wb_r, wb_i = zpad(ef_r, 0), zpad(ef_i, 0), zpad(eb_r, L - 1), zpad(eb_i, L - 1)
    t_small = jnp.stack([cat([bbr[0], bbi[0], bbr[1], bbi[1]], 1), cat([-bbi[0], bbr[0], -bbi[1], bbr[1]], 1)], 1)
    t_steps = jnp.stack([cat([wf_r, wf_r, wb_r, wb_r], 1), cat([wf_i, wf_i, wb_i, wb_i], 1)], 1)
    t_left = cat([cr[0], -ci[0], cr[1], -ci[1]], -1)

    al = jnp.stack([cat([pr[L, 0], pr[L, 1]], -1), cat([pi[L, 0], pi[L, 1]], -1)], axis=1)
    dv = jnp.tile(d.astype(F32).reshape(g, 1, gi), (1, L, 1)).reshape(g, L * gi, 1)
    t_tab, e_tab, f_tab = _ssm_table_call(f_small, f_steps, e_small, e_steps, t_small, t_steps, t_left, L, gi)
    return t_tab, e_tab, f_tab, al, dv


def _ssm_table_kernel(fs_ref, fp_ref, es_ref, ep_ref, ts_ref, tp_ref, tl_ref, ri_ref, rj_ref, rit_ref, rjt_ref,
                      ri2_ref, rs_ref, t_ref, e_ref, f_ref, *, L, gi):
    def split(a):
        hi = a.astype(BF16)
        return hi, (a - hi.astype(F32)).astype(BF16)

    def spread_lanes(a, rep):
        hi, lo = split(a)
        return _dot(hi, rep) + _dot(lo, rep)

    def spread_rows(rep, a):
        hi, lo = split(a)
        return _dot(rep, hi) + _dot(rep, lo)

    ri, rj, rit, rjt, ri2, rs = ri_ref[...], rj_ref[...], rit_ref[...], rjt_ref[...], ri2_ref[...], rs_ref[...]
    f_ref[0] = (spread_rows(rit, fs_ref[0, 0]) * spread_rows(rjt, fp_ref[0, 0])
                + spread_rows(rit, fs_ref[0, 1]) * spread_rows(rjt, fp_ref[0, 1])).astype(BF16)
    e_ref[0] = (spread_lanes(es_ref[0, 0], ri) * spread_lanes(ep_ref[0, 0], rj)
                + spread_lanes(es_ref[0, 1], ri) * spread_lanes(ep_ref[0, 1], rj)).astype(BF16)
    ew = (spread_lanes(ts_ref[0, 0], ri2) * spread_lanes(tp_ref[0, 0], rs)
          + spread_lanes(ts_ref[0, 1], ri2) * spread_lanes(tp_ref[0, 1], rs))
    c_hi, c_lo = split(tl_ref[0])
    e_hi, e_lo = split(ew)
    kw = _dot(c_hi, e_hi) + _dot(c_hi, e_lo) + _dot(c_lo, e_hi)
    per = LANES // gi
    for r in range(per):
        shifted = kw if r == 0 else pltpu.roll(kw, 2 * L * gi - r * gi, axis=1)
        for a in range(L // per):
            jo = L - 1 - (a * per + r)
            t_ref[0, pl.ds(jo * gi, gi), :] = shifted[:, a * LANES:a * LANES + L * gi].astype(BF16)


def _ssm_table_call(f_small, f_steps, e_small, e_steps, t_small, t_steps, t_left, L, gi):
    g = f_small.shape[0]
    n4 = f_small.shape[3]
    k = L * gi
    lane = jnp.arange(2 * k, dtype=I32)
    ri2 = (lane[None, :] % gi == jnp.arange(gi, dtype=I32)[:, None]).astype(BF16)
    rs = (lane[None, :] // gi == jnp.arange(2 * L, dtype=I32)[:, None]).astype(BF16)
    ri, rj = ri2[:, :k], rs[:L, :k]
    per_group = lambda a: pl.BlockSpec((1,) + a.shape[1:], lambda i: (i,) + (0,) * (a.ndim - 1))
    full = lambda a: pl.BlockSpec(a.shape, lambda i: (0,) * a.ndim)
    ins = (f_small, f_steps, e_small, e_steps, t_small, t_steps, t_left)
    reps = (ri, rj, ri.T, rj.T, ri2, rs)
    return pl.pallas_call(
        functools.partial(_ssm_table_kernel, L=L, gi=gi),
        grid=(g,),
        in_specs=[per_group(a) for a in ins] + [full(a) for a in reps],
        out_specs=[pl.BlockSpec((1, k, k), lambda i: (i, 0, 0)), pl.BlockSpec((1, n4, k), lambda i: (i, 0, 0)),
                   pl.BlockSpec((1, k, n4), lambda i: (i, 0, 0))],
        out_shape=[jax.ShapeDtypeStruct((g, k, k), BF16), jax.ShapeDtypeStruct((g, n4, k), BF16),
                   jax.ShapeDtypeStruct((g, k, n4), BF16)],
        compiler_params=_cparams("parallel"),
        name="ssm_tables",
    )(*ins, *reps)


def _to_groups_kernel(u_ref, a_ref, *, L, groups, gi):
    ncl = a_ref.shape[2]
    gq = LANES // gi
    for q in range(u_ref.shape[0]):
        for j in range(L):
            zt = u_ref[q, :, j].reshape(ncl, LANES).T
            a_ref[q * gq:(q + 1) * gq, pl.ds(j * gi, gi), :] = zt.reshape(gq, gi, ncl).astype(BF16)


def _to_groups(us, L, groups, gi):
    nq, t, _ = us.shape
    ncl, nck = SSM_TILE_CHUNKS, ROW_TILE // L
    return pl.pallas_call(
        functools.partial(_to_groups_kernel, L=L, groups=groups, gi=gi),
        grid=(t // (ncl * L),),
        in_specs=[pl.BlockSpec((nq, ncl // nck, L, nck, LANES), lambda i: (0, i, 0, 0, 0))],
        out_specs=pl.BlockSpec((groups, L * gi, ncl), lambda i: (0, 0, i)),
        out_shape=jax.ShapeDtypeStruct((groups, L * gi, t // L), BF16),
        compiler_params=_cparams("parallel"),
        name="to_groups",
    )(us.reshape(nq, t // ROW_TILE, L, nck, LANES))


def _to_tokens_kernel(y_ref, o_ref, *, L, groups, gi):
    ncl = y_ref.shape[2]
    gq = LANES // gi
    for q in range(o_ref.shape[0]):
        for j in range(L):
            yj = y_ref[q * gq:(q + 1) * gq, pl.ds(j * gi, gi), :].astype(F32).reshape(LANES, ncl)
            o_ref[q, :, j] = yj.T.reshape(o_ref.shape[1], o_ref.shape[3], LANES)


def _to_tokens(yt, L, groups, gi):
    _, k, nchunks = yt.shape
    ncl, nck, nq = SSM_TILE_CHUNKS, ROW_TILE // L, groups * gi // LANES
    return pl.pallas_call(
        functools.partial(_to_tokens_kernel, L=L, groups=groups, gi=gi),
        grid=(nchunks // ncl,),
        in_specs=[pl.BlockSpec((groups, k, ncl), lambda i: (0, 0, i))],
        out_specs=pl.BlockSpec((nq, ncl // nck, L, nck, LANES), lambda i: (0, i, 0, 0, 0)),
        out_shape=jax.ShapeDtypeStruct((nq, nchunks * L // ROW_TILE, L, nck, LANES), F32),
        compiler_params=_cparams("parallel"),
        name="to_tokens",
    )(yt).reshape(nq, nchunks * L, LANES)


def _ssm_kernel(a_ref, t_ref, e_ref, f_ref, al_ref, dv_ref, y_ref, s_scr, h_scr, *, nc, nb, rows, n):
    a = a_ref[0]
    y1 = _dot(t_ref[0], a) + dv_ref[0] * a.astype(F32)
    st = _dot(e_ref[0], a)
    n2 = 2 * n
    s_scr[...] = jnp.zeros_like(s_scr)
    for b in range(nb):
        sb_t = st[:, b * nc:(b + 1) * nc].T
        for q in range(2):
            s_scr[q, pl.ds(b, nc, stride=rows), :] = sb_t[:, q * n2:(q + 1) * n2]
    ar = al_ref[0, 0:1, :]
    ai = al_ref[0, 1:2, :]
    is_fwd = lax.broadcasted_iota(I32, (rows, n2), 1) < n

    def step(i, carry):
        hr, hi = carry
        rf = pl.ds(pl.multiple_of(i * rows, rows), rows)
        rb = pl.ds(pl.multiple_of((nc - 1 - i) * rows, rows), rows)
        h_scr[0, rf, 0:n] = hr[:, 0:n]
        h_scr[1, rf, 0:n] = hi[:, 0:n]
        h_scr[0, rb, n:n2] = hr[:, n:n2]
        h_scr[1, rb, n:n2] = hi[:, n:n2]
        sr = jnp.where(is_fwd, s_scr[0, rf, :], s_scr[0, rb, :])
        si = jnp.where(is_fwd, s_scr[1, rf, :], s_scr[1, rb, :])
        return ar * hr - ai * hi + sr, ar * hi + ai * hr + si

    zero = jnp.zeros((rows, n2), F32)
    lax.fori_loop(0, nc, step, (zero, zero))
    ht = jnp.concatenate(
        [jnp.concatenate([h_scr[q, pl.ds(b, nc, stride=rows), :].T for q in range(2)], axis=0) for b in range(nb)],
        axis=1)
    y = y1 + _dot(f_ref[0], ht.astype(BF16))
    y_ref[0] = jax.nn.gelu(y, approximate=True).astype(BF16)


def _ssm(ag, t_tab, e_tab, f_tab, al, dv, nc, nb):
    g, k, m = ag.shape
    n4 = e_tab.shape[1]
    rows = -(-nb // SUBLANES) * SUBLANES
    return pl.pallas_call(
        functools.partial(_ssm_kernel, nc=nc, nb=nb, rows=rows, n=n4 // 4),
        grid=(g,),
        in_specs=[
            pl.BlockSpec((1, k, m), lambda i: (i, 0, 0)),
            pl.BlockSpec((1, k, k), lambda i: (i, 0, 0)),
            pl.BlockSpec((1, n4, k), lambda i: (i, 0, 0)),
            pl.BlockSpec((1, k, n4), lambda i: (i, 0, 0)),
            pl.BlockSpec((1, 2, n4 // 2), lambda i: (i, 0, 0)),
            pl.BlockSpec((1, k, 1), lambda i: (i, 0, 0)),
        ],
        out_specs=pl.BlockSpec((1, k, m), lambda i: (i, 0, 0)),
        out_shape=jax.ShapeDtypeStruct((g, k, m), BF16),
        scratch_shapes=[pltpu.VMEM((2, nc * rows, n4 // 2), F32), pltpu.VMEM((2, nc * rows, n4 // 2), F32)],
        compiler_params=_cparams("parallel"),
        name="ssm",
    )(ag, t_tab, e_tab, f_tab, al, dv)


def _mixout_kernel(fre_ref, ys_ref, gate_ref, x_ref, wf_ref, wglu_ref, wout_ref, gffn_ref, wr_ref,
                   x1_ref, h2_ref, aff_ref, *, d, ne):
    for r in range(fre_ref.shape[0] // MIX_ROWS):
        rows = pl.ds(r * MIX_ROWS, MIX_ROWS)
        y_f = _dot(fre_ref[rows, :], wf_ref[...])
        nck = fre_ref.shape[0] // SSM_CHUNK
        chunks = range(r * MIX_ROWS // SSM_CHUNK, (r + 1) * MIX_ROWS // SSM_CHUNK)
        ys = jnp.concatenate([jnp.concatenate([ys_ref[q, pl.ds(c, SSM_CHUNK, stride=nck), :] for c in chunks], axis=0)
                              for q in range(ys_ref.shape[0])], axis=1)
        vg = _dot(ys.astype(BF16), wglu_ref[...])
        y_s = vg[:, :d] * jax.nn.sigmoid(vg[:, d:])
        gate = gate_ref[rows, :].astype(F32)
        m = gate[:, :d] * y_f + gate[:, d:] * y_s
        x1 = x_ref[rows, :] + _dot(m.astype(BF16), wout_ref[...])
        x1_ref[rows, :] = x1
        h2 = _rms(x1, gffn_ref[...])
        h2_ref[rows, :] = h2.astype(BF16)
        hi = h2.astype(BF16)
        lo = (h2 - hi.astype(F32)).astype(BF16)
        rl = _dot(jnp.concatenate([hi, lo], axis=1), wr_ref[...])
        logits = rl[:, :LANES] + rl[:, LANES:]
        logits = jnp.where(lax.broadcasted_iota(I32, logits.shape, 1) < ne, logits, -1e30)
        logits = logits - jnp.max(logits, axis=-1, keepdims=True)
        p = jnp.exp(logits)
        aff = p / jnp.sum(p, axis=-1, keepdims=True)
        aff_ref[:, rows] = aff.T[:ne, :]


def _mixout(fre, ys, gate, x2, wf, wglu, wout, gffn, wr2, ne, tm=ROW_TILE):
    t, d = x2.shape
    full = lambda a: pl.BlockSpec(a.shape, lambda i: (0,) * a.ndim)
    row = lambda a: pl.BlockSpec((tm, a.shape[1]), lambda i: (i, 0))
    return pl.pallas_call(
        functools.partial(_mixout_kernel, d=d, ne=ne),
        grid=(t // tm,),
        in_specs=[row(fre), pl.BlockSpec((ys.shape[0], tm, LANES), lambda i: (0, i, 0)), row(gate), row(x2),
                  full(wf), full(wglu), full(wout), full(gffn), full(wr2)],
        out_specs=[
            pl.BlockSpec((tm, d), lambda i: (i, 0)),
            pl.BlockSpec((tm, d), lambda i: (i, 0)),
            pl.BlockSpec((ne, tm), lambda i: (0, i)),
        ],
        out_shape=[
            jax.ShapeDtypeStruct((t, d), F32),
            jax.ShapeDtypeStruct((t, d), BF16),
            jax.ShapeDtypeStruct((ne, t), F32),
        ],
        compiler_params=_cparams("parallel"),
        name="mixout",
    )(fre, ys, gate, x2, wf, wglu, wout, gffn, wr2)


def _topk_kernel(aff_ref, pos_ref, st_ref, *, cap, blk):
    v = aff_ref[...]
    r, s = v.shape
    capf = float(cap)

    def bit_step(i, t):
        cand = t | (jnp.int32(1) << (30 - i))
        cnt = jnp.sum(jnp.where(v >= pltpu.bitcast(cand, F32), 1.0, 0.0), axis=1, keepdims=True)
        return jnp.where(cnt >= capf, cand, t)

    thr = lax.fori_loop(0, 31, bit_step, jnp.zeros((r, 1), I32))
    gt = jnp.where(v >= pltpu.bitcast(thr + 1, F32), 1.0, 0.0)
    eq = jnp.where(v >= pltpu.bitcast(thr, F32), 1.0, 0.0) - gt
    need = capf - jnp.sum(gt, axis=1, keepdims=True)
    ii = lax.broadcasted_iota(I32, (blk, blk), 0)
    jj = lax.broadcasted_iota(I32, (blk, blk), 1)
    tri = jnp.where(ii < jj, 1.0, 0.0).astype(BF16)
    run_eq = jnp.zeros((r, 1), F32)
    run = jnp.zeros((r, 1), F32)
    for k in range(s // blk):
        sl = slice(k * blk, (k + 1) * blk)
        eqb, gtb = eq[:, sl], gt[:, sl]
        rank_eq = _dot(eqb.astype(BF16), tri) + run_eq
        run_eq = run_eq + jnp.sum(eqb, axis=1, keepdims=True)
        mask = gtb + eqb * jnp.where(rank_eq < need, 1.0, 0.0)
        pos = _dot(mask.astype(BF16), tri) + run
        st_ref[:, k:k + 1] = run.astype(I32)
        run = run + jnp.sum(mask, axis=1, keepdims=True)
        pos_ref[:, sl] = jnp.where(mask > 0.0, pos, -1.0).astype(I32)
    st_ref[:, s // blk:s // blk + 1] = run.astype(I32)


def _topk(aff_rows, cap, blk=TOPK_BLOCK):
    r, s = aff_rows.shape
    nblk = s // blk
    return pl.pallas_call(
        functools.partial(_topk_kernel, cap=cap, blk=blk),
        grid=(1,),
        in_specs=[pl.BlockSpec((r, s), lambda i: (0, 0))],
        out_specs=[pl.BlockSpec((r, s), lambda i: (0, 0)), pl.BlockSpec((r, nblk + 1), lambda i: (0, 0))],
        out_shape=[jax.ShapeDtypeStruct((r, s), I32), jax.ShapeDtypeStruct((r, nblk + 1), I32)],
        compiler_params=_cparams("arbitrary"),
        name="topk",
    )(aff_rows)


def _floor_rows(x):
    return (x // PACKED_ROWS) * PACKED_ROWS


def _num_passes(starts, ends, win):
    need = ends[0] - _floor_rows(starts[0])
    for a, b in zip(starts[1:], ends[1:]):
        need = jnp.maximum(need, b - _floor_rows(a))
    return (need + win - 1) // win


def _pass_window(start, p, win, cap):
    first = _floor_rows(start) + p * win
    return first, pl.multiple_of(jnp.minimum(first, cap - win), PACKED_ROWS)


def _gather_kernel(st_ref, h_ref, pos_ref, aff_ref, x_ref, v_ref, *, nt, tt, win, cap, nbatch, ng):
    rows = [((pl.program_id(1) * ng + g) * nbatch + pl.program_id(0)) * (nt + 1) for g in range(ng)]
    x_ref[...] = jnp.zeros_like(x_ref)
    v_ref[...] = jnp.zeros_like(v_ref)
    riota = lax.broadcasted_iota(I32, (win, tt), 0)

    def bounds(t):
        return [st_ref[r + t] for r in rows], [st_ref[r + t + 1] for r in rows]

    def one_pass(t, p, starts):
        hrows = h_ref[0, pl.ds(pl.multiple_of(t * tt, tt), tt), :]
        hots, wins = [], []
        for g in range(ng):
            first, ws = _pass_window(starts[g], p, win, cap)
            pos = pos_ref[g, pl.ds(t, 1), :]
            hots.append((pos - ws == riota) & (pos >= first))
            wins.append(ws)
        onehot = jnp.concatenate([jnp.where(h, 1.0, 0.0).astype(BF16) for h in hots], axis=0)
        rows = _dot(onehot, hrows)
        for g in range(ng):
            x_ref[0, g, pl.ds(wins[g], win), :] += rows[g * win:(g + 1) * win].astype(BF16)
            vals = jnp.sum(jnp.where(hots[g], aff_ref[g, pl.ds(t, 1), :], 0.0), axis=1, keepdims=True)
            v_ref[0, g, pl.ds(wins[g], win), :] += vals

    def first_pass(t, most):
        starts, ends = bounds(t)
        one_pass(t, 0, starts)
        return jnp.maximum(most, _num_passes(starts, ends, win))

    def more_passes(t, carry):
        starts, ends = bounds(t)

        def body(p, c):
            one_pass(t, p, starts)
            return c

        lax.fori_loop(1, _num_passes(starts, ends, win), body, 0)
        return carry

    most = lax.fori_loop(0, nt, first_pass, jnp.int32(0), unroll=GATHER_UNROLL)

    @pl.when(most > 1)
    def _():
        lax.fori_loop(0, nt, more_passes, 0)


def _gather(starts, h3, pos3, aff3t, cap, ne):
    b, s, d = h3.shape
    tt, win, ng = GATHER_TILE, GATHER_WINDOW, GATHER_EXPERTS
    nt = s // tt
    grid_spec = pltpu.PrefetchScalarGridSpec(
        num_scalar_prefetch=1,
        grid=(b, ne // ng),
        in_specs=[
            pl.BlockSpec((1, s, d), lambda i, e, st: (i, 0, 0), pipeline_mode=pl.Buffered(1)),
            pl.BlockSpec((ng, nt, tt), lambda i, e, st: (e, i, 0)),
            pl.BlockSpec((ng, nt, tt), lambda i, e, st: (e, i, 0)),
        ],
        out_specs=[
            pl.BlockSpec((1, ng, cap, d), lambda i, e, st: (i, e, 0, 0)),
            pl.BlockSpec((1, ng, cap, 1), lambda i, e, st: (i, e, 0, 0)),
        ],
    )
    return pl.pallas_call(
        functools.partial(_gather_kernel, nt=nt, tt=tt, win=win, cap=cap, nbatch=b, ng=ng),
        grid_spec=grid_spec,
        out_shape=[jax.ShapeDtypeStruct((b, ne, cap, d), BF16), jax.ShapeDtypeStruct((b, ne, cap, 1), F32)],
        compiler_params=_cparams("arbitrary", "arbitrary"),
        name="gather",
    )(starts, h3, pos3, aff3t)


def _ffn_kernel(x_ref, v_ref, wg_ref, wu_ref, wd_ref, y_ref, acc_ref, *, nf):
    f = pl.program_id(2)
    nb, _, cap, d = x_ref.shape

    @pl.when(f == 0)
    def _():
        acc_ref[...] = jnp.zeros_like(acc_ref)

    wg = wg_ref[0].astype(BF16)
    wu = wu_ref[0].astype(BF16)
    wd = wd_ref[0].astype(BF16)
    mr = min(FFN_ROWS, cap)
    for r in range(nb * cap // mr):
        x = x_ref[r * mr // cap, 0, pl.ds(r * mr % cap, mr), :]
        g = _dot(x, wg)
        u = _dot(x, wu)
        hid = (g * jax.nn.sigmoid(g) * u).astype(BF16)
        acc_ref[pl.ds(r * mr, mr), :] += _dot(hid, wd)

    @pl.when(f == nf - 1)
    def _():
        y = acc_ref[...] * v_ref[...].reshape(nb * cap, 1)
        y_ref[...] = y.astype(BF16).reshape(nb, 1, cap, d)


def _ffn(xg, vals, wg, wu, wd, nb, tf):
    b, ne, cap, d = xg.shape
    dexp = wg.shape[2]
    nf = dexp // tf
    return pl.pallas_call(
        functools.partial(_ffn_kernel, nf=nf),
        grid=(ne, b // nb, nf),
        in_specs=[
            pl.BlockSpec((nb, 1, cap, d), lambda e, i, f: (i, e, 0, 0)),
            pl.BlockSpec((nb, 1, cap, 1), lambda e, i, f: (i, e, 0, 0)),
            pl.BlockSpec((1, d, tf), lambda e, i, f: (e, 0, f)),
            pl.BlockSpec((1, d, tf), lambda e, i, f: (e, 0, f)),
            pl.BlockSpec((1, tf, d), lambda e, i, f: (e, f, 0)),
        ],
        out_specs=pl.BlockSpec((nb, 1, cap, d), lambda e, i, f: (i, e, 0, 0)),
        out_shape=jax.ShapeDtypeStruct(xg.shape, BF16),
        scratch_shapes=[pltpu.VMEM((nb * cap, d), F32)],
        compiler_params=_cparams("parallel", "parallel", "arbitrary"),
        name="ffn",
    )(xg, vals, wg, wu, wd)


def _combine_kernel(st_ref, y_ref, pos_ref, spread_ref, lanes_ref, x1_ref, p_ref, wpp_ref, wpg_ref, gple_ref,
                    gout_ref, o_ref, *, nt, tt, win, cap, ne, nbatch, final):
    t = pl.program_id(1)
    rows = [(e * nbatch + pl.program_id(0)) * (nt + 1) + t for e in range(ne)]
    starts = [st_ref[r] for r in rows]
    ends = [st_ref[r + 1] for r in rows]
    k = ne * win
    lane_e = lanes_ref[0:1, :]
    lane_r = lanes_ref[1:2, :]
    pos = pos_ref[...].astype(F32)
    pos = jnp.concatenate([pos, jnp.zeros((LANES - ne, tt), F32)], axis=0).T
    hi = jnp.floor(pos * (1.0 / COMBINE_SPLIT))
    lo = pos - hi * COMBINE_SPLIT
    rank = _dot(jnp.concatenate([hi, lo], axis=1).astype(BF16), spread_ref[...])

    def expert_rows(p):
        firsts = jnp.zeros((1, k), F32)
        offs = jnp.zeros((1, k), F32)
        wins = []
        for e in range(ne):
            first, ws = _pass_window(starts[e], p, win, cap)
            firsts = jnp.where(lane_e == e, first.astype(F32), firsts)
            offs = jnp.where(lane_e == e, ws.astype(F32), offs)
            wins.append(y_ref[0, e, pl.ds(ws, win), :])
        onehot = jnp.where((rank - offs == lane_r) & (rank >= firsts), 1.0, 0.0).astype(BF16)
        return _dot(onehot, jnp.concatenate(wins, axis=0))

    def finish(acc):
        for r in range(tt // LANES):
            rows = pl.ds(r * LANES, LANES)
            x2 = acc[r * LANES:(r + 1) * LANES]
            emb = _dot(p_ref[0, rows, :].astype(BF16), wpp_ref[...])
            gate = jax.nn.sigmoid(_dot(_rms(x2, gple_ref[...]).astype(BF16), wpg_ref[...]))
            x3 = x2 + gate * emb
            o_ref[0, rows, :] = _rms(x3, gout_ref[...]) if final else x3

    first_pass = x1_ref[0] + expert_rows(0)
    finish(first_pass)
    npass = _num_passes(starts, ends, win)

    @pl.when(npass > 1)
    def _():
        finish(lax.fori_loop(1, npass, lambda p, acc: acc + expert_rows(p), x1_ref[0] + expert_rows(0)))


def _combine(starts, yg, post, x13, p3, wpp, wpg, gple, gout, cap, final):
    b, ne, _, d = yg.shape
    s = x13.shape[1]
    tt, win = COMBINE_TILE, COMBINE_WINDOW
    nt = s // tt
    lane = jnp.arange(ne * win, dtype=I32)
    part = jnp.arange(2 * LANES, dtype=I32)[:, None]
    spread = jnp.where(part == lane // win, float(COMBINE_SPLIT), jnp.where(part == lane // win + LANES, 1.0, 0.0))
    spread = spread.astype(BF16)
    lanes = jnp.stack([lane // win, lane % win]).astype(F32)
    tile = lambda a: pl.BlockSpec((1, tt, a.shape[2]), lambda i, t, st: (i, t, 0))
    full = lambda a: pl.BlockSpec(a.shape, lambda i, t, st: (0,) * a.ndim)
    grid_spec = pltpu.PrefetchScalarGridSpec(
        num_scalar_prefetch=1,
        grid=(b, nt),
        in_specs=[
            pl.BlockSpec((1, ne, cap, d), lambda i, t, st: (i, 0, 0, 0), pipeline_mode=pl.Buffered(1)),
            pl.BlockSpec((ne, tt), lambda i, t, st: (0, i * nt + t)),
            full(spread), full(lanes), tile(x13), tile(p3), full(wpp), full(wpg), full(gple), full(gout),
        ],
        out_specs=pl.BlockSpec((1, tt, d), lambda i, t, st: (i, t, 0)),
    )
    return pl.pallas_call(
        functools.partial(_combine_kernel, nt=nt, tt=tt, win=win, cap=cap, ne=ne, nbatch=b, final=final),
        grid_spec=grid_spec,
        out_shape=jax.ShapeDtypeStruct(x13.shape, F32),
        compiler_params=_cparams("arbitrary", "arbitrary"),
        name="combine",
    )(starts, yg, post, spread, lanes, x13, p3, wpp, wpg, gple, gout)


def kernel(x, p, g_mix, w_in, w_fourier, ssm_a_re, ssm_a_im, ssm_log_dt, ssm_b_re, ssm_b_im, ssm_c_re, ssm_c_im, ssm_d, w_glu, w_out, g_ffn, w_router, w_exp_gate, w_exp_up, w_exp_down, g_ple, w_ple_gate, w_ple_proj, g_final):
    b, s, d = x.shape
    depth = p.shape[0]
    df = w_fourier.shape[1]
    ds = w_glu.shape[1]
    ne = w_router.shape[2]
    dexp = w_exp_gate.shape[3]
    groups, gi = ssm_b_re.shape[2], ssm_b_re.shape[4]
    L = SSM_CHUNK
    n1 = s // DFT_N2
    nc = s // L
    cap = EC_CAPACITY * s // ne
    assert s % (DFT_N2 * PACKED_ROWS) == 0 and s % GATHER_TILE == 0 and cap % PACKED_ROWS == 0
    assert nc % SSM_TILE_CHUNKS == 0 and gi == PACKED_ROWS and cap // COMBINE_SPLIT < 256
    assert 2 * ssm_a_re.shape[3] == LANES and ds % LANES == 0
    assert cap >= COMBINE_WINDOW and ne % GATHER_EXPERTS == 0 and GATHER_TILE == COMBINE_TILE
    assert ds == groups * gi and df % FOURIER_GROUPS == 0

    bd = _channel_dft_table(df)
    ck, sk = _dft1_tables(n1)
    tr, ti = _dft3_tables(n1)
    tf = 256 if dexp % 256 == 0 else dexp
    nb_ffn = 2 if b % 2 == 0 else 1

    xcur = x.reshape(b * s, d)
    for i in range(depth):
        final = i == depth - 1
        row = lambda v: v.astype(F32).reshape(1, -1)
        pq, us, gate = _inproj(xcur, row(g_mix[i]), w_in[i].astype(BF16), bd, df, ds)
        a4 = _dft1(pq.reshape(b, n1, DFT_N2, 2 * df), ck, sk)
        fre = _dft3(a4, tr, ti).reshape(b * s, df)

        tabs = _ssm_tables(ssm_a_re[i], ssm_a_im[i], ssm_log_dt[i], ssm_b_re[i], ssm_b_im[i],
                           ssm_c_re[i], ssm_c_im[i], ssm_d[i])
        ys = _to_tokens(_ssm(_to_groups(us, L, groups, gi), *tabs, nc=nc, nb=b), L, groups, gi)

        wr = w_router[i].astype(F32)
        wr_hi = wr.astype(BF16)
        wr_lo = (wr - wr_hi.astype(F32)).astype(BF16)
        lane_pad = lambda w: jnp.pad(w, ((0, 0), (0, LANES - ne)))
        wr2 = jnp.concatenate([jnp.concatenate([lane_pad(wr_hi), lane_pad(wr_lo)], axis=1),
                               jnp.concatenate([lane_pad(wr_hi), jnp.zeros((d, LANES), BF16)], axis=1)], axis=0)
        x1, h2, aff = _mixout(fre, ys, gate, xcur, w_fourier[i].astype(BF16), w_glu[i].astype(BF16),
                              w_out[i].astype(BF16), row(g_ffn[i]), wr2, ne)

        aff_rows = aff.reshape(ne * b, s)
        posm, st_blk = _topk(aff_rows, cap)
        starts = st_blk[:, ::GATHER_TILE // TOPK_BLOCK].reshape(-1)
        tiles = (ne, b * (s // GATHER_TILE), GATHER_TILE)
        xg, vals = _gather(starts, h2.reshape(b, s, d), posm.reshape(tiles), aff_rows.reshape(tiles), cap, ne)
        yg = _ffn(xg, vals, w_exp_gate[i], w_exp_up[i], w_exp_down[i], nb_ffn, tf)
        xnext = _combine(starts, yg, posm.reshape(ne, b * s), x1.reshape(b, s, d), p[i],
                         w_ple_proj[i].astype(BF16), w_ple_gate[i].astype(BF16), row(g_ple[i]),
                         row(g_final) if final else row(g_ple[i]), cap, final)
        xcur = xnext.reshape(b * s, d)
    return xcur.reshape(b, s, d)
```

```python
import functools
import math

import jax
import jax.numpy as jnp
from jax import lax
from jax.experimental import pallas as pl
from jax.experimental.pallas import tpu as pltpu

F32 = jnp.float32
BF16 = jnp.bfloat16
I32 = jnp.int32

RMS_EPS = 1e-6
FOURIER_GROUPS = 4
EC_CAPACITY = 2
DFT_N2 = 128
SUBLANES = 8
LANES = 128
PACKED_ROWS = 16
SSM_CHUNK = 32
ROW_TILE = 1024
INPROJ_ROWS = 512
SSM_TILE_CHUNKS = 128
TOPK_BLOCK = 128
GATHER_TILE = 256
MIX_ROWS = 256
FFN_ROWS = 512
GATHER_WINDOW = 64
GATHER_EXPERTS = 4
GATHER_UNROLL = 8
COMBINE_TILE = 256
COMBINE_WINDOW = 64
COMBINE_SPLIT = 32
VMEM_LIMIT = 56 * 1024 * 1024


def _cparams(*sem):
    return pltpu.CompilerParams(dimension_semantics=sem, vmem_limit_bytes=VMEM_LIMIT)


def _rms(x, g):
    return x * lax.rsqrt(jnp.mean(x * x, axis=-1, keepdims=True) + RMS_EPS) * g


def _dot(a, b):
    return jnp.dot(a, b, preferred_element_type=F32)


def _inproj_kernel(x_ref, g_ref, w_ref, bd_ref, pq_ref, us_ref, gate_ref, *, df, ds):
    for r in range(x_ref.shape[0] // INPROJ_ROWS):
        rows = pl.ds(r * INPROJ_ROWS, INPROJ_ROWS)
        h = _rms(x_ref[rows, :], g_ref[...]).astype(BF16)
        z = _dot(h, w_ref[...])
        pq_ref[rows, :] = _dot(z[:, :df].astype(BF16), bd_ref[...]).astype(BF16)
        for q in range(ds // LANES):
            us_ref[q, rows, :] = z[:, df + q * LANES:df + (q + 1) * LANES]
        gate_ref[rows, :] = jax.nn.sigmoid(z[:, df + ds:]).astype(BF16)


def _inproj(x2, g, w_in, bd, df, ds, tm=ROW_TILE):
    t, d = x2.shape
    dg = w_in.shape[1] - df - ds
    return pl.pallas_call(
        functools.partial(_inproj_kernel, df=df, ds=ds),
        grid=(t // tm,),
        in_specs=[
            pl.BlockSpec((tm, d), lambda i: (i, 0)),
            pl.BlockSpec((1, d), lambda i: (0, 0)),
            pl.BlockSpec(w_in.shape, lambda i: (0, 0), pipeline_mode=pl.Buffered(1)),
            pl.BlockSpec(bd.shape, lambda i: (0, 0), pipeline_mode=pl.Buffered(1)),
        ],
        out_specs=[
            pl.BlockSpec((tm, 2 * df), lambda i: (i, 0)),
            pl.BlockSpec((ds // LANES, tm, LANES), lambda i: (0, i, 0)),
            pl.BlockSpec((tm, dg), lambda i: (i, 0)),
        ],
        out_shape=[
            jax.ShapeDtypeStruct((t, 2 * df), BF16),
            jax.ShapeDtypeStruct((ds // LANES, t, LANES), F32),
            jax.ShapeDtypeStruct((t, dg), BF16),
        ],
        compiler_params=_cparams("parallel"),
        name="inproj",
    )(x2, g, w_in, bd)


def _channel_dft_table(df):
    c = df // FOURIER_GROUPS
    k = jnp.arange(c, dtype=I32)
    ang = (2.0 * math.pi / c) * ((k[:, None] * k[None, :]) % c).astype(F32)
    eye = jnp.eye(FOURIER_GROUPS, dtype=F32)
    scale = 1.0 / math.sqrt(c)
    re = jnp.kron(eye, jnp.cos(ang)) * scale
    im = -jnp.kron(eye, jnp.sin(ang)) * scale
    return jnp.concatenate([re, im], axis=1).astype(BF16)


def _dft1_tables(n1):
    k = jnp.arange(n1, dtype=I32)
    ang = (2.0 * math.pi / n1) * ((k[:, None] * k[None, :]) % n1).astype(F32)
    row = jnp.arange(n1 * SUBLANES, dtype=I32)
    rep = (row[:, None] // SUBLANES == k[None, :]).astype(F32)
    same_slot = (row[:, None] % SUBLANES == row[None, :] % SUBLANES).astype(F32) * (1.0 / math.sqrt(n1))
    hp = lax.Precision.HIGHEST
    kron8 = lambda t: (jnp.dot(jnp.dot(rep, t, precision=hp), rep.T, precision=hp) * same_slot).astype(BF16)
    return kron8(jnp.cos(ang)), kron8(jnp.sin(ang))


def _dft3_tables(n1):
    s = n1 * DFT_N2
    nb = n1 // SUBLANES
    k1 = jnp.arange(n1, dtype=I32)[:, None, None]
    k2 = jnp.arange(DFT_N2, dtype=I32)[None, :, None]
    n2 = jnp.arange(DFT_N2, dtype=I32)[None, None, :]
    ang = (2.0 * math.pi / s) * ((n2 * (k1 + n1 * k2)) % s).astype(F32)
    eye = jnp.eye(SUBLANES, dtype=BF16)[None, None, :, :, None]
    rows = DFT_N2 * SUBLANES

    def expand(t):
        t = (t * (1.0 / math.sqrt(DFT_N2))).astype(BF16).reshape(nb, SUBLANES, DFT_N2, DFT_N2)
        t = jnp.transpose(t, (0, 2, 1, 3))[:, :, :, None, :]
        return (t * eye).reshape(nb, rows, rows)

    return expand(jnp.cos(ang)), expand(jnp.sin(ang))


def _dft1_kernel(z_ref, ck_ref, sk_ref, a_ref):
    n1, slab, c2 = z_ref.shape[1], z_ref.shape[2], z_ref.shape[3]
    c = c2 // 2
    z = z_ref[0].astype(F32)
    halves = []
    for h in range(slab // SUBLANES):
        zh = z[:, h * SUBLANES:(h + 1) * SUBLANES, :].reshape(n1 * SUBLANES, c2).astype(BF16)
        cz = _dot(ck_ref[...], zh)
        sz = _dot(sk_ref[...], zh)
        a = jnp.concatenate([cz[:, :c] + sz[:, c:], cz[:, c:] - sz[:, :c]], axis=1)
        halves.append(a.reshape(n1, SUBLANES, c2))
    a_ref[0] = jnp.concatenate(halves, axis=1).astype(BF16)


def _dft1(pq4, ck, sk):
    b, n1, n2, c2 = pq4.shape
    slab = PACKED_ROWS
    return pl.pallas_call(
        _dft1_kernel,
        grid=(b, n2 // slab),
        in_specs=[
            pl.BlockSpec((1, n1, slab, c2), lambda i, j: (i, 0, j, 0)),
            pl.BlockSpec(ck.shape, lambda i, j: (0, 0)),
            pl.BlockSpec(sk.shape, lambda i, j: (0, 0)),
        ],
        out_specs=pl.BlockSpec((1, n1, slab, c2), lambda i, j: (i, 0, j, 0)),
        out_shape=jax.ShapeDtypeStruct(pq4.shape, BF16),
        compiler_params=_cparams("parallel", "parallel"),
        name="dft1",
    )(pq4, ck, sk)


def _dft3_kernel(a_ref, tr_ref, ti_ref, o_ref):
    nblk, c2 = tr_ref.shape[0], a_ref.shape[3]
    c = c2 // 2
    outs = []
    for h in range(nblk):
        a = a_ref[0, h * SUBLANES:(h + 1) * SUBLANES].reshape(SUBLANES * DFT_N2, c2)
        out = _dot(tr_ref[h], a[:, :c]) + _dot(ti_ref[h], a[:, c:])
        outs.append(out.reshape(DFT_N2, SUBLANES, c))
    o_ref[0] = jnp.concatenate(outs, axis=1).astype(BF16)


def _dft3(a4, tr, ti):
    b, n1, n2, c2 = a4.shape
    c = c2 // 2
    rows = DFT_N2 * SUBLANES
    nblk = PACKED_ROWS // SUBLANES
    return pl.pallas_call(
        _dft3_kernel,
        grid=(n1 // PACKED_ROWS, b),
        in_specs=[
            pl.BlockSpec((1, PACKED_ROWS, n2, c2), lambda k, i: (i, k, 0, 0)),
            pl.BlockSpec((nblk, rows, rows), lambda k, i: (k, 0, 0)),
            pl.BlockSpec((nblk, rows, rows), lambda k, i: (k, 0, 0)),
        ],
        out_specs=pl.BlockSpec((1, DFT_N2, PACKED_ROWS, c), lambda k, i: (i, 0, k, 0)),
        out_shape=jax.ShapeDtypeStruct((b, DFT_N2, n1, c), BF16),
        compiler_params=_cparams("parallel", "parallel"),
        name="dft3",
    )(a4, tr, ti)


def _ssm_tables(a_re, a_im, log_dt, b_re, b_im, c_re, c_im, d):
    L = SSM_CHUNK
    _, g, n = a_re.shape
    gi = b_re.shape[-1]
    dt = jnp.exp(log_dt.astype(F32))[..., None]
    ar, ai = a_re.astype(F32), a_im.astype(F32)
    tau = jnp.arange(L + 1, dtype=F32)[:, None, None, None]
    mag = jnp.exp(tau * (ar * dt)[None])
    ang = tau * (ai * dt)[None]
    pr, pi = mag * jnp.cos(ang), mag * jnp.sin(ang)
    nr, ni = pr[1] - 1.0, pi[1]
    den = ar * ar + ai * ai
    qr, qi = (nr * ar + ni * ai) / den, (ni * ar - nr * ai) / den
    br, bi = b_re.astype(F32), b_im.astype(F32)
    bbr = qr[..., None] * br - qi[..., None] * bi
    bbi = qr[..., None] * bi + qi[..., None] * br
    cr, ci = c_re.astype(F32), c_im.astype(F32)

    cat = jnp.concatenate
    steps = lambda p: jnp.transpose(p, (1, 0, 2))
    pf_r, pf_i, pb_r, pb_i = steps(pr[1:, 0]), steps(pi[1:, 0]), steps(pr[1:, 1][::-1]), steps(pi[1:, 1][::-1])
    f_small = jnp.stack([cat([cr[0], cr[1], cr[0], cr[1]], -1), cat([-ci[0], -ci[1], -ci[0], -ci[1]], -1)], 1)
    f_steps = jnp.stack([cat([pf_r, pb_r, -pf_i, -pb_i], -1), cat([pf_i, pb_i, pf_r, pb_r], -1)], 1)
    last = lambda p: jnp.transpose(p, (1, 2, 0))
    ef_r, ef_i, eb_r, eb_i = last(pr[:L, 0][::-1]), last(pi[:L, 0][::-1]), last(pr[:L, 1]), last(pi[:L, 1])
    e_small = jnp.stack([cat([bbr[0], bbr[1], bbi[0], bbi[1]], 1), cat([-bbi[0], -bbi[1], bbr[0], bbr[1]], 1)], 1)
    e_steps = jnp.stack([cat([ef_r, eb_r, ef_r, eb_r], 1), cat([ef_i, eb_i, ef_i, eb_i], 1)], 1)
    zpad = lambda p, before: jnp.pad(p, ((0, 0), (0, 0), (before, L - before)))
    wf_r, wf_i, wb_r, wb_i = zpad(ef_r, 0), zpad(ef_i, 0), zpad(eb_r, L - 1), zpad(eb_i, L - 1)
    t_small = jnp.stack([cat([bbr[0], bbi[0], bbr[1], bbi[1]], 1), cat([-bbi[0], bbr[0], -bbi[1], bbr[1]], 1)], 1)
    t_steps = jnp.stack([cat([wf_r, wf_r, wb_r, wb_r], 1), cat([wf_i, wf_i, wb_i, wb_i], 1)], 1)
    t_left = cat([cr[0], -ci[0], cr[1], -ci[1]], -1)

    al = jnp.stack([cat([pr[L, 0], pr[L, 1]], -1), cat([pi[L, 0], pi[L, 1]], -1)], axis=1)
    dv = jnp.tile(d.astype(F32).reshape(g, 1, gi), (1, L, 1)).reshape(g, L * gi, 1)
    t_tab, e_tab, f_tab = _ssm_table_call(f_small, f_steps, e_small, e_steps, t_small, t_steps, t_left, L, gi)
    return t_tab, e_tab, f_tab, al, dv


def _ssm_table_kernel(fs_ref, fp_ref, es_ref, ep_ref, ts_ref, tp_ref, tl_ref, ri_ref, rj_ref, rit_ref, rjt_ref,
                      ri2_ref, rs_ref, t_ref, e_ref, f_ref, *, L, gi):
    def split(a):
        hi = a.astype(BF16)
        return hi, (a - hi.astype(F32)).astype(BF16)

    def spread_lanes(a, rep):
        hi, lo = split(a)
        return _dot(hi, rep) + _dot(lo, rep)

    def spread_rows(rep, a):
        hi, lo = split(a)
        return _dot(rep, hi) + _dot(rep, lo)

    ri, rj, rit, rjt, ri2, rs = ri_ref[...], rj_ref[...], rit_ref[...], rjt_ref[...], ri2_ref[...], rs_ref[...]
    f_ref[0] = (spread_rows(rit, fs_ref[0, 0]) * spread_rows(rjt, fp_ref[0, 0])
                + spread_rows(rit, fs_ref[0, 1]) * spread_rows(rjt, fp_ref[0, 1])).astype(BF16)
    e_ref[0] = (spread_lanes(es_ref[0, 0], ri) * spread_lanes(ep_ref[0, 0], rj)
                + spread_lanes(es_ref[0, 1], ri) * spread_lanes(ep_ref[0, 1], rj)).astype(BF16)
    ew = (spread_lanes(ts_ref[0, 0], ri2) * spread_lanes(tp_ref[0, 0], rs)
          + spread_lanes(ts_ref[0, 1], ri2) * spread_lanes(tp_ref[0, 1], rs))
    c_hi, c_lo = split(tl_ref[0])
    e_hi, e_lo = split(ew)
    kw = _dot(c_hi, e_hi) + _dot(c_hi, e_lo) + _dot(c_lo, e_hi)
    per = LANES // gi
    for r in range(per):
        shifted = kw if r == 0 else pltpu.roll(kw, 2 * L * gi - r * gi, axis=1)
        for a in range(L // per):
            jo = L - 1 - (a * per + r)
            t_ref[0, pl.ds(jo * gi, gi), :] = shifted[:, a * LANES:a * LANES + L * gi].astype(BF16)


def _ssm_table_call(f_small, f_steps, e_small, e_steps, t_small, t_steps, t_left, L, gi):
    g = f_small.shape[0]
    n4 = f_small.shape[3]
    k = L * gi
    lane = jnp.arange(2 * k, dtype=I32)
    ri2 = (lane[None, :] % gi == jnp.arange(gi, dtype=I32)[:, None]).astype(BF16)
    rs = (lane[None, :] // gi == jnp.arange(2 * L, dtype=I32)[:, None]).astype(BF16)
    ri, rj = ri2[:, :k], rs[:L, :k]
    per_group = lambda a: pl.BlockSpec((1,) + a.shape[1:], lambda i: (i,) + (0,) * (a.ndim - 1))
    full = lambda a: pl.BlockSpec(a.shape, lambda i: (0,) * a.ndim)
    ins = (f_small, f_steps, e_small, e_steps, t_small, t_steps, t_left)
    reps = (ri, rj, ri.T, rj.T, ri2, rs)
    return pl.pallas_call(
        functools.partial(_ssm_table_kernel, L=L, gi=gi),
        grid=(g,),
        in_specs=[per_group(a) for a in ins] + [full(a) for a in reps],
        out_specs=[pl.BlockSpec((1, k, k), lambda i: (i, 0, 0)), pl.BlockSpec((1, n4, k), lambda i: (i, 0, 0)),
                   pl.BlockSpec((1, k, n4), lambda i: (i, 0, 0))],
        out_shape=[jax.ShapeDtypeStruct((g, k, k), BF16), jax.ShapeDtypeStruct((g, n4, k), BF16),
                   jax.ShapeDtypeStruct((g, k, n4), BF16)],
        compiler_params=_cparams("parallel"),
        name="ssm_tables",
    )(*ins, *reps)


def _to_groups_kernel(u_ref, a_ref, *, L, groups, gi):
    ncl = a_ref.shape[2]
    gq = LANES // gi
    for q in range(u_ref.shape[0]):
        for j in range(L):
            zt = u_ref[q, pl.ds(j, ncl, stride=L), :].T
            a_ref[q * gq:(q + 1) * gq, pl.ds(j * gi, gi), :] = zt.reshape(gq, gi, ncl).astype(BF16)


def _to_groups(us, L, groups, gi):
    nq, t, _ = us.shape
    ncl = SSM_TILE_CHUNKS
    return pl.pallas_call(
        functools.partial(_to_groups_kernel, L=L, groups=groups, gi=gi),
        grid=(t // (ncl * L),),
        in_specs=[pl.BlockSpec((nq, ncl * L, LANES), lambda i: (0, i, 0))],
        out_specs=pl.BlockSpec((groups, L * gi, ncl), lambda i: (0, 0, i)),
        out_shape=jax.ShapeDtypeStruct((groups, L * gi, t // L), BF16),
        compiler_params=_cparams("parallel"),
        name="to_groups",
    )(us)


def _to_tokens_kernel(y_ref, o_ref, *, L, groups, gi):
    ncl = y_ref.shape[2]
    gq = LANES // gi
    for q in range(o_ref.shape[0]):
        for j in range(L):
            yj = y_ref[q * gq:(q + 1) * gq, pl.ds(j * gi, gi), :].astype(F32).reshape(LANES, ncl)
            o_ref[q, pl.ds(j, ncl, stride=L), :] = yj.T


def _to_tokens(yt, L, groups, gi):
    _, k, nchunks = yt.shape
    ncl = SSM_TILE_CHUNKS
    return pl.pallas_call(
        functools.partial(_to_tokens_kernel, L=L, groups=groups, gi=gi),
        grid=(nchunks // ncl,),
        in_specs=[pl.BlockSpec((groups, k, ncl), lambda i: (0, 0, i))],
        out_specs=pl.BlockSpec((groups * gi // LANES, ncl * L, LANES), lambda i: (0, i, 0)),
        out_shape=jax.ShapeDtypeStruct((groups * gi // LANES, nchunks * L, LANES), F32),
        compiler_params=_cparams("parallel"),
        name="to_tokens",
    )(yt)


def _ssm_kernel(a_ref, t_ref, e_ref, f_ref, al_ref, dv_ref, y_ref, s_scr, h_scr, *, nc, nb, rows, n):
    a = a_ref[0]
    y1 = _dot(t_ref[0], a) + dv_ref[0] * a.astype(F32)
    st = _dot(e_ref[0], a)
    n2 = 2 * n
    s_scr[...] = jnp.zeros_like(s_scr)
    for b in range(nb):
        sb_t = st[:, b * nc:(b + 1) * nc].T
        for q in range(2):
            s_scr[q, pl.ds(b, nc, stride=rows), :] = sb_t[:, q * n2:(q + 1) * n2]
    ar = al_ref[0, 0:1, :]
    ai = al_ref[0, 1:2, :]
    is_fwd = lax.broadcasted_iota(I32, (rows, n2), 1) < n

    def step(i, carry):
        hr, hi = carry
        rf = pl.ds(pl.multiple_of(i * rows, rows), rows)
        rb = pl.ds(pl.multiple_of((nc - 1 - i) * rows, rows), rows)
        h_scr[0, rf, 0:n] = hr[:, 0:n]
        h_scr[1, rf, 0:n] = hi[:, 0:n]
        h_scr[0, rb, n:n2] = hr[:, n:n2]
        h_scr[1, rb, n:n2] = hi[:, n:n2]
        sr = jnp.where(is_fwd, s_scr[0, rf, :], s_scr[0, rb, :])
        si = jnp.where(is_fwd, s_scr[1, rf, :], s_scr[1, rb, :])
        return ar * hr - ai * hi + sr, ar * hi + ai * hr + si

    zero = jnp.zeros((rows, n2), F32)
    lax.fori_loop(0, nc, step, (zero, zero))
    ht = jnp.concatenate(
        [jnp.concatenate([h_scr[q, pl.ds(b, nc, stride=rows), :].T for q in range(2)], axis=0) for b in range(nb)],
        axis=1)
    y = y1 + _dot(f_ref[0], ht.astype(BF16))
    y_ref[0] = jax.nn.gelu(y, approximate=True).astype(BF16)


def _ssm(ag, t_tab, e_tab, f_tab, al, dv, nc, nb):
    g, k, m = ag.shape
    n4 = e_tab.shape[1]
    rows = -(-nb // SUBLANES) * SUBLANES
    return pl.pallas_call(
        functools.partial(_ssm_kernel, nc=nc, nb=nb, rows=rows, n=n4 // 4),
        grid=(g,),
        in_specs=[
            pl.BlockSpec((1, k, m), lambda i: (i, 0, 0)),
            pl.BlockSpec((1, k, k), lambda i: (i, 0, 0)),
            pl.BlockSpec((1, n4, k), lambda i: (i, 0, 0)),
            pl.BlockSpec((1, k, n4), lambda i: (i, 0, 0)),
            pl.BlockSpec((1, 2, n4 // 2), lambda i: (i, 0, 0)),
            pl.BlockSpec((1, k, 1), lambda i: (i, 0, 0)),
        ],
        out_specs=pl.BlockSpec((1, k, m), lambda i: (i, 0, 0)),
        out_shape=jax.ShapeDtypeStruct((g, k, m), BF16),
        scratch_shapes=[pltpu.VMEM((2, nc * rows, n4 // 2), F32), pltpu.VMEM((2, nc * rows, n4 // 2), F32)],
        compiler_params=_cparams("parallel"),
        name="ssm",
    )(ag, t_tab, e_tab, f_tab, al, dv)


def _mixout_kernel(fre_ref, ys_ref, gate_ref, x_ref, wf_ref, wglu_ref, wout_ref, gffn_ref, wr_ref,
                   x1_ref, h2_ref, aff_ref, *, d, ne):
    for r in range(fre_ref.shape[0] // MIX_ROWS):
        rows = pl.ds(r * MIX_ROWS, MIX_ROWS)
        y_f = _dot(fre_ref[rows, :], wf_ref[...])
        ys = jnp.concatenate([ys_ref[q, rows, :] for q in range(ys_ref.shape[0])], axis=1)
        vg = _dot(ys.astype(BF16), wglu_ref[...])
        y_s = vg[:, :d] * jax.nn.sigmoid(vg[:, d:])
        gate = gate_ref[rows, :].astype(F32)
        m = gate[:, :d] * y_f + gate[:, d:] * y_s
        x1 = x_ref[rows, :] + _dot(m.astype(BF16), wout_ref[...])
        x1_ref[rows, :] = x1
        h2 = _rms(x1, gffn_ref[...])
        h2_ref[rows, :] = h2.astype(BF16)
        hi = h2.astype(BF16)
        lo = (h2 - hi.astype(F32)).astype(BF16)
        rl = _dot(jnp.concatenate([hi, lo], axis=1), wr_ref[...])
        logits = rl[:, :LANES] + rl[:, LANES:]
        logits = jnp.where(lax.broadcasted_iota(I32, logits.shape, 1) < ne, logits, -1e30)
        logits = logits - jnp.max(logits, axis=-1, keepdims=True)
        p = jnp.exp(logits)
        aff = p / jnp.sum(p, axis=-1, keepdims=True)
        aff_ref[:, rows] = aff.T[:ne, :]


def _mixout(fre, ys, gate, x2, wf, wglu, wout, gffn, wr2, ne, tm=ROW_TILE):
    t, d = x2.shape
    full = lambda a: pl.BlockSpec(a.shape, lambda i: (0,) * a.ndim, pipeline_mode=pl.Buffered(1))
    row = lambda a: pl.BlockSpec((tm, a.shape[1]), lambda i: (i, 0))
    return pl.pallas_call(
        functools.partial(_mixout_kernel, d=d, ne=ne),
        grid=(t // tm,),
        in_specs=[row(fre), pl.BlockSpec((ys.shape[0], tm, LANES), lambda i: (0, i, 0)), row(gate), row(x2),
                  full(wf), full(wglu), full(wout), full(gffn), full(wr2)],
        out_specs=[
            pl.BlockSpec((tm, d), lambda i: (i, 0)),
            pl.BlockSpec((tm, d), lambda i: (i, 0)),
            pl.BlockSpec((ne, tm), lambda i: (0, i)),
        ],
        out_shape=[
            jax.ShapeDtypeStruct((t, d), F32),
            jax.ShapeDtypeStruct((t, d), BF16),
            jax.ShapeDtypeStruct((ne, t), F32),
        ],
        compiler_params=_cparams("parallel"),
        name="mixout",
    )(fre, ys, gate, x2, wf, wglu, wout, gffn, wr2)


def _topk_kernel(aff_ref, pos_ref, st_ref, *, cap, blk):
    v = aff_ref[...]
    r, s = v.shape
    capf = float(cap)

    def bit_step(i, t):
        cand = t | (jnp.int32(1) << (30 - i))
        cnt = jnp.sum(jnp.where(v >= pltpu.bitcast(cand, F32), 1.0, 0.0), axis=1, keepdims=True)
        return jnp.where(cnt >= capf, cand, t)

    thr = lax.fori_loop(0, 31, bit_step, jnp.zeros((r, 1), I32))
    gt = jnp.where(v >= pltpu.bitcast(thr + 1, F32), 1.0, 0.0)
    eq = jnp.where(v >= pltpu.bitcast(thr, F32), 1.0, 0.0) - gt
    need = capf - jnp.sum(gt, axis=1, keepdims=True)
    ii = lax.broadcasted_iota(I32, (blk, blk), 0)
    jj = lax.broadcasted_iota(I32, (blk, blk), 1)
    tri = jnp.where(ii < jj, 1.0, 0.0).astype(BF16)
    run_eq = jnp.zeros((r, 1), F32)
    run = jnp.zeros((r, 1), F32)
    for k in range(s // blk):
        sl = slice(k * blk, (k + 1) * blk)
        eqb, gtb = eq[:, sl], gt[:, sl]
        rank_eq = _dot(eqb.astype(BF16), tri) + run_eq
        run_eq = run_eq + jnp.sum(eqb, axis=1, keepdims=True)
        mask = gtb + eqb * jnp.where(rank_eq < need, 1.0, 0.0)
        pos = _dot(mask.astype(BF16), tri) + run
        st_ref[:, k:k + 1] = run.astype(I32)
        run = run + jnp.sum(mask, axis=1, keepdims=True)
        pos_ref[:, sl] = jnp.where(mask > 0.0, pos, -1.0).astype(I32)
    st_ref[:, s // blk:s // blk + 1] = run.astype(I32)


def _topk(aff_rows, cap, blk=TOPK_BLOCK):
    r, s = aff_rows.shape
    nblk = s // blk
    return pl.pallas_call(
        functools.partial(_topk_kernel, cap=cap, blk=blk),
        grid=(1,),
        in_specs=[pl.BlockSpec((r, s), lambda i: (0, 0))],
        out_specs=[pl.BlockSpec((r, s), lambda i: (0, 0)), pl.BlockSpec((r, nblk + 1), lambda i: (0, 0))],
        out_shape=[jax.ShapeDtypeStruct((r, s), I32), jax.ShapeDtypeStruct((r, nblk + 1), I32)],
        compiler_params=_cparams("arbitrary"),
        name="topk",
    )(aff_rows)


def _floor_rows(x):
    return (x // PACKED_ROWS) * PACKED_ROWS


def _num_passes(starts, ends, win):
    need = ends[0] - _floor_rows(starts[0])
    for a, b in zip(starts[1:], ends[1:]):
        need = jnp.maximum(need, b - _floor_rows(a))
    return (need + win - 1) // win


def _pass_window(start, p, win, cap):
    first = _floor_rows(start) + p * win
    return first, pl.multiple_of(jnp.minimum(first, cap - win), PACKED_ROWS)


def _gather_kernel(st_ref, h_ref, pos_ref, aff_ref, x_ref, v_ref, *, nt, tt, win, cap, nbatch, ng):
    rows = [((pl.program_id(1) * ng + g) * nbatch + pl.program_id(0)) * (nt + 1) for g in range(ng)]
    x_ref[...] = jnp.zeros_like(x_ref)
    v_ref[...] = jnp.zeros_like(v_ref)
    riota = lax.broadcasted_iota(I32, (win, tt), 0)

    def bounds(t):
        return [st_ref[r + t] for r in rows], [st_ref[r + t + 1] for r in rows]

    def one_pass(t, p, starts):
        hrows = h_ref[0, pl.ds(pl.multiple_of(t * tt, tt), tt), :]
        hots, wins = [], []
        for g in range(ng):
            first, ws = _pass_window(starts[g], p, win, cap)
            pos = pos_ref[g, pl.ds(t, 1), :]
            hots.append((pos - ws == riota) & (pos >= first))
            wins.append(ws)
        onehot = jnp.concatenate([jnp.where(h, 1.0, 0.0).astype(BF16) for h in hots], axis=0)
        rows = _dot(onehot, hrows)
        for g in range(ng):
            x_ref[0, g, pl.ds(wins[g], win), :] += rows[g * win:(g + 1) * win].astype(BF16)
            vals = jnp.sum(jnp.where(hots[g], aff_ref[g, pl.ds(t, 1), :], 0.0), axis=1, keepdims=True)
            v_ref[0, g, pl.ds(wins[g], win), :] += vals

    def first_pass(t, most):
        starts, ends = bounds(t)
        one_pass(t, 0, starts)
        return jnp.maximum(most, _num_passes(starts, ends, win))

    def more_passes(t, carry):
        starts, ends = bounds(t)

        def body(p, c):
            one_pass(t, p, starts)
            return c

        lax.fori_loop(1, _num_passes(starts, ends, win), body, 0)
        return carry

    most = lax.fori_loop(0, nt, first_pass, jnp.int32(0), unroll=GATHER_UNROLL)

    @pl.when(most > 1)
    def _():
        lax.fori_loop(0, nt, more_passes, 0)


def _gather(starts, h3, pos3, aff3t, cap, ne):
    b, s, d = h3.shape
    tt, win, ng = GATHER_TILE, GATHER_WINDOW, GATHER_EXPERTS
    nt = s // tt
    grid_spec = pltpu.PrefetchScalarGridSpec(
        num_scalar_prefetch=1,
        grid=(b, ne // ng),
        in_specs=[
            pl.BlockSpec((1, s, d), lambda i, e, st: (i, 0, 0), pipeline_mode=pl.Buffered(1)),
            pl.BlockSpec((ng, nt, tt), lambda i, e, st: (e, i, 0)),
            pl.BlockSpec((ng, nt, tt), lambda i, e, st: (e, i, 0)),
        ],
        out_specs=[
            pl.BlockSpec((1, ng, cap, d), lambda i, e, st: (i, e, 0, 0)),
            pl.BlockSpec((1, ng, cap, 1), lambda i, e, st: (i, e, 0, 0)),
        ],
    )
    return pl.pallas_call(
        functools.partial(_gather_kernel, nt=nt, tt=tt, win=win, cap=cap, nbatch=b, ng=ng),
        grid_spec=grid_spec,
        out_shape=[jax.ShapeDtypeStruct((b, ne, cap, d), BF16), jax.ShapeDtypeStruct((b, ne, cap, 1), F32)],
        compiler_params=_cparams("arbitrary", "arbitrary"),
        name="gather",
    )(starts, h3, pos3, aff3t)


def _ffn_kernel(x_ref, v_ref, wg_ref, wu_ref, wd_ref, y_ref, acc_ref, *, nf):
    f = pl.program_id(2)
    nb, _, cap, d = x_ref.shape

    @pl.when(f == 0)
    def _():
        acc_ref[...] = jnp.zeros_like(acc_ref)

    wg = wg_ref[0].astype(BF16)
    wu = wu_ref[0].astype(BF16)
    wd = wd_ref[0].astype(BF16)
    mr = min(FFN_ROWS, cap)
    for r in range(nb * cap // mr):
        x = x_ref[r * mr // cap, 0, pl.ds(r * mr % cap, mr), :]
        g = _dot(x, wg)
        u = _dot(x, wu)
        hid = (g * jax.nn.sigmoid(g) * u).astype(BF16)
        acc_ref[pl.ds(r * mr, mr), :] += _dot(hid, wd)

    @pl.when(f == nf - 1)
    def _():
        y = acc_ref[...] * v_ref[...].reshape(nb * cap, 1)
        y_ref[...] = y.astype(BF16).reshape(nb, 1, cap, d)


def _ffn(xg, vals, wg, wu, wd, nb, tf):
    b, ne, cap, d = xg.shape
    dexp = wg.shape[2]
    nf = dexp // tf
    return pl.pallas_call(
        functools.partial(_ffn_kernel, nf=nf),
        grid=(ne, b // nb, nf),
        in_specs=[
            pl.BlockSpec((nb, 1, cap, d), lambda e, i, f: (i, e, 0, 0)),
            pl.BlockSpec((nb, 1, cap, 1), lambda e, i, f: (i, e, 0, 0)),
            pl.BlockSpec((1, d, tf), lambda e, i, f: (e, 0, f)),
            pl.BlockSpec((1, d, tf), lambda e, i, f: (e, 0, f)),
            pl.BlockSpec((1, tf, d), lambda e, i, f: (e, f, 0)),
        ],
        out_specs=pl.BlockSpec((nb, 1, cap, d), lambda e, i, f: (i, e, 0, 0)),
        out_shape=jax.ShapeDtypeStruct(xg.shape, BF16),
        scratch_shapes=[pltpu.VMEM((nb * cap, d), F32)],
        compiler_params=_cparams("parallel", "parallel", "arbitrary"),
        name="ffn",
    )(xg, vals, wg, wu, wd)


def _combine_kernel(st_ref, y_ref, pos_ref, spread_ref, lanes_ref, x1_ref, p_ref, wpp_ref, wpg_ref, gple_ref,
                    gout_ref, o_ref, *, nt, tt, win, cap, ne, nbatch, final):
    t = pl.program_id(1)
    rows = [(e * nbatch + pl.program_id(0)) * (nt + 1) + t for e in range(ne)]
    starts = [st_ref[r] for r in rows]
    ends = [st_ref[r + 1] for r in rows]
    k = ne * win
    lane_e = lanes_ref[0:1, :]
    lane_r = lanes_ref[1:2, :]
    pos = pos_ref[...].astype(F32)
    pos = jnp.concatenate([pos, jnp.zeros((LANES - ne, tt), F32)], axis=0).T
    hi = jnp.floor(pos * (1.0 / COMBINE_SPLIT))
    lo = pos - hi * COMBINE_SPLIT
    rank = _dot(jnp.concatenate([hi, lo], axis=1).astype(BF16), spread_ref[...])

    def expert_rows(p):
        firsts = jnp.zeros((1, k), F32)
        offs = jnp.zeros((1, k), F32)
        wins = []
        for e in range(ne):
            first, ws = _pass_window(starts[e], p, win, cap)
            firsts = jnp.where(lane_e == e, first.astype(F32), firsts)
            offs = jnp.where(lane_e == e, ws.astype(F32), offs)
            wins.append(y_ref[0, e, pl.ds(ws, win), :])
        onehot = jnp.where((rank - offs == lane_r) & (rank >= firsts), 1.0, 0.0).astype(BF16)
        return _dot(onehot, jnp.concatenate(wins, axis=0))

    def finish(acc):
        for r in range(tt // LANES):
            rows = pl.ds(r * LANES, LANES)
            x2 = acc[r * LANES:(r + 1) * LANES]
            emb = _dot(p_ref[0, rows, :].astype(BF16), wpp_ref[...])
            gate = jax.nn.sigmoid(_dot(_rms(x2, gple_ref[...]).astype(BF16), wpg_ref[...]))
            x3 = x2 + gate * emb
            o_ref[0, rows, :] = _rms(x3, gout_ref[...]) if final else x3

    first_pass = x1_ref[0] + expert_rows(0)
    finish(first_pass)
    npass = _num_passes(starts, ends, win)

    @pl.when(npass > 1)
    def _():
        finish(lax.fori_loop(1, npass, lambda p, acc: acc + expert_rows(p), x1_ref[0] + expert_rows(0)))


def _combine(starts, yg, post, x13, p3, wpp, wpg, gple, gout, cap, final):
    b, ne, _, d = yg.shape
    s = x13.shape[1]
    tt, win = COMBINE_TILE, COMBINE_WINDOW
    nt = s // tt
    lane = jnp.arange(ne * win, dtype=I32)
    part = jnp.arange(2 * LANES, dtype=I32)[:, None]
    spread = jnp.where(part == lane // win, float(COMBINE_SPLIT), jnp.where(part == lane // win + LANES, 1.0, 0.0))
    spread = spread.astype(BF16)
    lanes = jnp.stack([lane // win, lane % win]).astype(F32)
    tile = lambda a: pl.BlockSpec((1, tt, a.shape[2]), lambda i, t, st: (i, t, 0))
    full = lambda a: pl.BlockSpec(a.shape, lambda i, t, st: (0,) * a.ndim)
    grid_spec = pltpu.PrefetchScalarGridSpec(
        num_scalar_prefetch=1,
        grid=(b, nt),
        in_specs=[
            pl.BlockSpec((1, ne, cap, d), lambda i, t, st: (i, 0, 0, 0), pipeline_mode=pl.Buffered(1)),
            pl.BlockSpec((ne, tt), lambda i, t, st: (0, i * nt + t)),
            full(spread), full(lanes), tile(x13), tile(p3), full(wpp), full(wpg), full(gple), full(gout),
        ],
        out_specs=pl.BlockSpec((1, tt, d), lambda i, t, st: (i, t, 0)),
    )
    return pl.pallas_call(
        functools.partial(_combine_kernel, nt=nt, tt=tt, win=win, cap=cap, ne=ne, nbatch=b, final=final),
        grid_spec=grid_spec,
        out_shape=jax.ShapeDtypeStruct(x13.shape, F32),
        compiler_params=_cparams("arbitrary", "arbitrary"),
        name="combine",
    )(starts, yg, post, spread, lanes, x13, p3, wpp, wpg, gple, gout)


def kernel(x, p, g_mix, w_in, w_fourier, ssm_a_re, ssm_a_im, ssm_log_dt, ssm_b_re, ssm_b_im, ssm_c_re, ssm_c_im, ssm_d, w_glu, w_out, g_ffn, w_router, w_exp_gate, w_exp_up, w_exp_down, g_ple, w_ple_gate, w_ple_proj, g_final):
    b, s, d = x.shape
    depth = p.shape[0]
    df = w_fourier.shape[1]
    ds = w_glu.shape[1]
    ne = w_router.shape[2]
    dexp = w_exp_gate.shape[3]
    groups, gi = ssm_b_re.shape[2], ssm_b_re.shape[4]
    L = SSM_CHUNK
    n1 = s // DFT_N2
    nc = s // L
    cap = EC_CAPACITY * s // ne
    assert s % (DFT_N2 * PACKED_ROWS) == 0 and s % GATHER_TILE == 0 and cap % PACKED_ROWS == 0
    assert nc % SSM_TILE_CHUNKS == 0 and gi == PACKED_ROWS and cap // COMBINE_SPLIT < 256
    assert 2 * ssm_a_re.shape[3] == LANES and ds % LANES == 0
    assert cap >= COMBINE_WINDOW and ne % GATHER_EXPERTS == 0 and GATHER_TILE == COMBINE_TILE
    assert ds == groups * gi and df % FOURIER_GROUPS == 0

    bd = _channel_dft_table(df)
    ck, sk = _dft1_tables(n1)
    tr, ti = _dft3_tables(n1)
    tf = 256 if dexp % 256 == 0 else dexp
    nb_ffn = 2 if b % 2 == 0 else 1

    xcur = x.reshape(b * s, d)
    for i in range(depth):
        final = i == depth - 1
        row = lambda v: v.astype(F32).reshape(1, -1)
        pq, us, gate = _inproj(xcur, row(g_mix[i]), w_in[i].astype(BF16), bd, df, ds)
        a4 = _dft1(pq.reshape(b, n1, DFT_N2, 2 * df), ck, sk)
        fre = _dft3(a4, tr, ti).reshape(b * s, df)

        tabs = _ssm_tables(ssm_a_re[i], ssm_a_im[i], ssm_log_dt[i], ssm_b_re[i], ssm_b_im[i],
                           ssm_c_re[i], ssm_c_im[i], ssm_d[i])
        ys = _to_tokens(_ssm(_to_groups(us, L, groups, gi), *tabs, nc=nc, nb=b), L, groups, gi)

        wr = w_router[i].astype(F32)
        wr_hi = wr.astype(BF16)
        wr_lo = (wr - wr_hi.astype(F32)).astype(BF16)
        lane_pad = lambda w: jnp.pad(w, ((0, 0), (0, LANES - ne)))
        wr2 = jnp.concatenate([jnp.concatenate([lane_pad(wr_hi), lane_pad(wr_lo)], axis=1),
                               jnp.concatenate([lane_pad(wr_hi), jnp.zeros((d, LANES), BF16)], axis=1)], axis=0)
        x1, h2, aff = _mixout(fre, ys, gate, xcur, w_fourier[i].astype(BF16), w_glu[i].astype(BF16),
                              w_out[i].astype(BF16), row(g_ffn[i]), wr2, ne)

        aff_rows = aff.reshape(ne * b, s)
        posm, st_blk = _topk(aff_rows, cap)
        starts = st_blk[:, ::GATHER_TILE // TOPK_BLOCK].reshape(-1)
        tiles = (ne, b * (s // GATHER_TILE), GATHER_TILE)
        xg, vals = _gather(starts, h2.reshape(b, s, d), posm.reshape(tiles), aff_rows.reshape(tiles), cap, ne)
        yg = _ffn(xg, vals, w_exp_gate[i], w_exp_up[i], w_exp_down[i], nb_ffn, tf)
        xnext = _combine(starts, yg, posm.reshape(ne, b * s), x1.reshape(b, s, d), p[i],
                         w_ple_proj[i].astype(BF16), w_ple_gate[i].astype(BF16), row(g_ple[i]),
                         row(g_final) if final else row(g_ple[i]), cap, final)
        xcur = xnext.reshape(b * s, d)
    return xcur.reshape(b, s, d)
```

```python
import functools
import math

import jax
import jax.numpy as jnp
from jax import lax
from jax.experimental import pallas as pl
from jax.experimental.pallas import tpu as pltpu

F32 = jnp.float32
BF16 = jnp.bfloat16
I32 = jnp.int32

RMS_EPS = 1e-6
FOURIER_GROUPS = 4
EC_CAPACITY = 2
DFT_N2 = 128
SUBLANES = 8
LANES = 128
PACKED_ROWS = 16
SSM_CHUNK = 32
ROW_TILE = 1024
INPROJ_ROWS = 256
SSM_TILE_CHUNKS = 128
TOPK_BLOCK = 128
GATHER_TILE = 256
MIX_ROWS = 512
FFN_ROWS = 512
GATHER_WINDOW = 64
GATHER_EXPERTS = 4
GATHER_UNROLL = 8
COMBINE_TILE = 256
COMBINE_WINDOW = 64
COMBINE_STEP_TILES = 2
COMBINE_SPLIT = 32
VMEM_LIMIT = 56 * 1024 * 1024


def _cparams(*sem):
    return pltpu.CompilerParams(dimension_semantics=sem, vmem_limit_bytes=VMEM_LIMIT)


def _rms(x, g):
    return x * lax.rsqrt(jnp.mean(x * x, axis=-1, keepdims=True) + RMS_EPS) * g


def _dot(a, b):
    return jnp.dot(a, b, preferred_element_type=F32)


def _inproj_kernel(x_ref, g_ref, w_ref, bd_ref, pq_ref, us_ref, gate_ref, *, df, ds):
    for r in range(x_ref.shape[0] // INPROJ_ROWS):
        rows = pl.ds(r * INPROJ_ROWS, INPROJ_ROWS)
        h = _rms(x_ref[rows, :], g_ref[...]).astype(BF16)
        z = _dot(h, w_ref[...])
        pq_ref[rows, :] = _dot(z[:, :df].astype(BF16), bd_ref[...]).astype(BF16)
        for q in range(ds // LANES):
            us_ref[q, rows, :] = z[:, df + q * LANES:df + (q + 1) * LANES]
        gate_ref[rows, :] = jax.nn.sigmoid(z[:, df + ds:]).astype(BF16)


def _inproj(x2, g, w_in, bd, df, ds, tm=ROW_TILE):
    t, d = x2.shape
    dg = w_in.shape[1] - df - ds
    return pl.pallas_call(
        functools.partial(_inproj_kernel, df=df, ds=ds),
        grid=(t // tm,),
        in_specs=[
            pl.BlockSpec((tm, d), lambda i: (i, 0)),
            pl.BlockSpec((1, d), lambda i: (0, 0)),
            pl.BlockSpec(w_in.shape, lambda i: (0, 0), pipeline_mode=pl.Buffered(1)),
            pl.BlockSpec(bd.shape, lambda i: (0, 0), pipeline_mode=pl.Buffered(1)),
        ],
        out_specs=[
            pl.BlockSpec((tm, 2 * df), lambda i: (i, 0)),
            pl.BlockSpec((ds // LANES, tm, LANES), lambda i: (0, i, 0)),
            pl.BlockSpec((tm, dg), lambda i: (i, 0)),
        ],
        out_shape=[
            jax.ShapeDtypeStruct((t, 2 * df), BF16),
            jax.ShapeDtypeStruct((ds // LANES, t, LANES), F32),
            jax.ShapeDtypeStruct((t, dg), BF16),
        ],
        compiler_params=_cparams("parallel"),
        name="inproj",
    )(x2, g, w_in, bd)


def _channel_dft_table(df):
    c = df // FOURIER_GROUPS
    k = jnp.arange(c, dtype=I32)
    ang = (2.0 * math.pi / c) * ((k[:, None] * k[None, :]) % c).astype(F32)
    eye = jnp.eye(FOURIER_GROUPS, dtype=F32)
    scale = 1.0 / math.sqrt(c)
    re = jnp.kron(eye, jnp.cos(ang)) * scale
    im = -jnp.kron(eye, jnp.sin(ang)) * scale
    return jnp.concatenate([re, im], axis=1).astype(BF16)


def _dft1_tables(n1):
    k = jnp.arange(n1, dtype=I32)
    ang = (2.0 * math.pi / n1) * ((k[:, None] * k[None, :]) % n1).astype(F32)
    row = jnp.arange(n1 * SUBLANES, dtype=I32)
    rep = (row[:, None] // SUBLANES == k[None, :]).astype(F32)
    same_slot = (row[:, None] % SUBLANES == row[None, :] % SUBLANES).astype(F32) * (1.0 / math.sqrt(n1))
    hp = lax.Precision.HIGHEST
    kron8 = lambda t: (jnp.dot(jnp.dot(rep, t, precision=hp), rep.T, precision=hp) * same_slot).astype(BF16)
    return kron8(jnp.cos(ang)), kron8(jnp.sin(ang))


def _dft3_tables(n1):
    s = n1 * DFT_N2
    nb = n1 // SUBLANES
    k1 = jnp.arange(n1, dtype=I32)[:, None, None]
    k2 = jnp.arange(DFT_N2, dtype=I32)[None, :, None]
    n2 = jnp.arange(DFT_N2, dtype=I32)[None, None, :]
    ang = (2.0 * math.pi / s) * ((n2 * (k1 + n1 * k2)) % s).astype(F32)
    eye = jnp.eye(SUBLANES, dtype=BF16)[None, None, :, :, None]
    rows = DFT_N2 * SUBLANES

    def expand(t):
        t = (t * (1.0 / math.sqrt(DFT_N2))).astype(BF16).reshape(nb, SUBLANES, DFT_N2, DFT_N2)
        t = jnp.transpose(t, (0, 2, 1, 3))[:, :, :, None, :]
        return (t * eye).reshape(nb, rows, rows)

    return expand(jnp.cos(ang)), expand(jnp.sin(ang))


def _dft1_kernel(z_ref, ck_ref, sk_ref, a_ref):
    n1, slab, c2 = z_ref.shape[1], z_ref.shape[2], z_ref.shape[3]
    c = c2 // 2
    z = z_ref[0].astype(F32)
    halves = []
    for h in range(slab // SUBLANES):
        zh = z[:, h * SUBLANES:(h + 1) * SUBLANES, :].reshape(n1 * SUBLANES, c2).astype(BF16)
        cz = _dot(ck_ref[...], zh)
        sz = _dot(sk_ref[...], zh)
        a = jnp.concatenate([cz[:, :c] + sz[:, c:], cz[:, c:] - sz[:, :c]], axis=1)
        halves.append(a.reshape(n1, SUBLANES, c2))
    a_ref[0] = jnp.concatenate(halves, axis=1).astype(BF16)


def _dft1(pq4, ck, sk):
    b, n1, n2, c2 = pq4.shape
    slab = PACKED_ROWS
    return pl.pallas_call(
        _dft1_kernel,
        grid=(b, n2 // slab),
        in_specs=[
            pl.BlockSpec((1, n1, slab, c2), lambda i, j: (i, 0, j, 0)),
            pl.BlockSpec(ck.shape, lambda i, j: (0, 0)),
            pl.BlockSpec(sk.shape, lambda i, j: (0, 0)),
        ],
        out_specs=pl.BlockSpec((1, n1, slab, c2), lambda i, j: (i, 0, j, 0)),
        out_shape=jax.ShapeDtypeStruct(pq4.shape, BF16),
        compiler_params=_cparams("parallel", "parallel"),
        name="dft1",
    )(pq4, ck, sk)


def _dft3_kernel(a_ref, tr_ref, ti_ref, o_ref):
    nblk, c2 = tr_ref.shape[0], a_ref.shape[3]
    c = c2 // 2
    outs = []
    for h in range(nblk):
        a = a_ref[0, h * SUBLANES:(h + 1) * SUBLANES].reshape(SUBLANES * DFT_N2, c2)
        out = _dot(tr_ref[h], a[:, :c]) + _dot(ti_ref[h], a[:, c:])
        outs.append(out.reshape(DFT_N2, SUBLANES, c))
    o_ref[0] = jnp.concatenate(outs, axis=1).astype(BF16)


def _dft3(a4, tr, ti):
    b, n1, n2, c2 = a4.shape
    c = c2 // 2
    rows = DFT_N2 * SUBLANES
    nblk = PACKED_ROWS // SUBLANES
    return pl.pallas_call(
        _dft3_kernel,
        grid=(n1 // PACKED_ROWS, b),
        in_specs=[
            pl.BlockSpec((1, PACKED_ROWS, n2, c2), lambda k, i: (i, k, 0, 0)),
            pl.BlockSpec((nblk, rows, rows), lambda k, i: (k, 0, 0)),
            pl.BlockSpec((nblk, rows, rows), lambda k, i: (k, 0, 0)),
        ],
        out_specs=pl.BlockSpec((1, DFT_N2, PACKED_ROWS, c), lambda k, i: (i, 0, k, 0)),
        out_shape=jax.ShapeDtypeStruct((b, DFT_N2, n1, c), BF16),
        compiler_params=_cparams("parallel", "parallel"),
        name="dft3",
    )(a4, tr, ti)


def _ssm_tables(a_re, a_im, log_dt, b_re, b_im, c_re, c_im, d):
    L = SSM_CHUNK
    _, g, n = a_re.shape
    gi = b_re.shape[-1]
    dt = jnp.exp(log_dt.astype(F32))[..., None]
    ar, ai = a_re.astype(F32), a_im.astype(F32)
    tau = jnp.arange(L + 1, dtype=F32)[:, None, None, None]
    mag = jnp.exp(tau * (ar * dt)[None])
    ang = tau * (ai * dt)[None]
    pr, pi = mag * jnp.cos(ang), mag * jnp.sin(ang)
    nr, ni = pr[1] - 1.0, pi[1]
    den = ar * ar + ai * ai
    qr, qi = (nr * ar + ni * ai) / den, (ni * ar - nr * ai) / den
    br, bi = b_re.astype(F32), b_im.astype(F32)
    bbr = qr[..., None] * br - qi[..., None] * bi
    bbi = qr[..., None] * bi + qi[..., None] * br
    cr, ci = c_re.astype(F32), c_im.astype(F32)

    cat = jnp.concatenate
    steps = lambda p: jnp.transpose(p, (1, 0, 2))
    pf_r, pf_i, pb_r, pb_i = steps(pr[1:, 0]), steps(pi[1:, 0]), steps(pr[1:, 1][::-1]), steps(pi[1:, 1][::-1])
    f_small = jnp.stack([cat([cr[0], cr[1], cr[0], cr[1]], -1), cat([-ci[0], -ci[1], -ci[0], -ci[1]], -1)], 1)
    f_steps = jnp.stack([cat([pf_r, pb_r, -pf_i, -pb_i], -1), cat([pf_i, pb_i, pf_r, pb_r], -1)], 1)
    last = lambda p: jnp.transpose(p, (1, 2, 0))
    ef_r, ef_i, eb_r, eb_i = last(pr[:L, 0][::-1]), last(pi[:L, 0][::-1]), last(pr[:L, 1]), last(pi[:L, 1])
    e_small = jnp.stack([cat([bbr[0], bbr[1], bbi[0], bbi[1]], 1), cat([-bbi[0], -bbi[1], bbr[0], bbr[1]], 1)], 1)
    e_steps = jnp.stack([cat([ef_r, eb_r, ef_r, eb_r], 1), cat([ef_i, eb_i, ef_i, eb_i], 1)], 1)
    zpad = lambda p, before: jnp.pad(p, ((0, 0), (0, 0), (before, L - before)))
    wf_r, wf_i, wb_r, wb_i = zpad(ef_r, 0), zpad(ef_i, 0), zpad(eb_r, L - 1), zpad(eb_i, L - 1)
    t_small = jnp.stack([cat([bbr[0], bbi[0], bbr[1], bbi[1]], 1), cat([-bbi[0], bbr[0], -bbi[1], bbr[1]], 1)], 1)
    t_steps = jnp.stack([cat([wf_r, wf_r, wb_r, wb_r], 1), cat([wf_i, wf_i, wb_i, wb_i], 1)], 1)
    t_left = cat([cr[0], -ci[0], cr[1], -ci[1]], -1)

    al = jnp.stack([cat([pr[L, 0], pr[L, 1]], -1), cat([pi[L, 0], pi[L, 1]], -1)], axis=1)
    dv = jnp.tile(d.astype(F32).reshape(g, 1, gi), (1, L, 1)).reshape(g, L * gi, 1)
    t_tab, e_tab, f_tab = _ssm_table_call(f_small, f_steps, e_small, e_steps, t_small, t_steps, t_left, L, gi)
    return t_tab, e_tab, f_tab, al, dv


def _ssm_table_kernel(fs_ref, fp_ref, es_ref, ep_ref, ts_ref, tp_ref, tl_ref, ri_ref, rj_ref, rit_ref, rjt_ref,
                      ri2_ref, rs_ref, t_ref, e_ref, f_ref, *, L, gi):
    def split(a):
        hi = a.astype(BF16)
        return hi, (a - hi.astype(F32)).astype(BF16)

    def spread_lanes(a, rep):
        hi, lo = split(a)
        return _dot(hi, rep) + _dot(lo, rep)

    def spread_rows(rep, a):
        hi, lo = split(a)
        return _dot(rep, hi) + _dot(rep, lo)

    ri, rj, rit, rjt, ri2, rs = ri_ref[...], rj_ref[...], rit_ref[...], rjt_ref[...], ri2_ref[...], rs_ref[...]
    f_ref[0] = (spread_rows(rit, fs_ref[0, 0]) * spread_rows(rjt, fp_ref[0, 0])
                + spread_rows(rit, fs_ref[0, 1]) * spread_rows(rjt, fp_ref[0, 1])).astype(BF16)
    e_ref[0] = (spread_lanes(es_ref[0, 0], ri) * spread_lanes(ep_ref[0, 0], rj)
                + spread_lanes(es_ref[0, 1], ri) * spread_lanes(ep_ref[0, 1], rj)).astype(BF16)
    ew = (spread_lanes(ts_ref[0, 0], ri2) * spread_lanes(tp_ref[0, 0], rs)
          + spread_lanes(ts_ref[0, 1], ri2) * spread_lanes(tp_ref[0, 1], rs))
    c_hi, c_lo = split(tl_ref[0])
    e_hi, e_lo = split(ew)
    kw = _dot(c_hi, e_hi) + _dot(c_hi, e_lo) + _dot(c_lo, e_hi)
    per = LANES // gi
    for r in range(per):
        shifted = kw if r == 0 else pltpu.roll(kw, 2 * L * gi - r * gi, axis=1)
        for a in range(L // per):
            jo = L - 1 - (a * per + r)
            t_ref[0, pl.ds(jo * gi, gi), :] = shifted[:, a * LANES:a * LANES + L * gi].astype(BF16)


def _ssm_table_call(f_small, f_steps, e_small, e_steps, t_small, t_steps, t_left, L, gi):
    g = f_small.shape[0]
    n4 = f_small.shape[3]
    k = L * gi
    lane = jnp.arange(2 * k, dtype=I32)
    ri2 = (lane[None, :] % gi == jnp.arange(gi, dtype=I32)[:, None]).astype(BF16)
    rs = (lane[None, :] // gi == jnp.arange(2 * L, dtype=I32)[:, None]).astype(BF16)
    ri, rj = ri2[:, :k], rs[:L, :k]
    per_group = lambda a: pl.BlockSpec((1,) + a.shape[1:], lambda i: (i,) + (0,) * (a.ndim - 1))
    full = lambda a: pl.BlockSpec(a.shape, lambda i: (0,) * a.ndim)
    ins = (f_small, f_steps, e_small, e_steps, t_small, t_steps, t_left)
    reps = (ri, rj, ri.T, rj.T, ri2, rs)
    return pl.pallas_call(
        functools.partial(_ssm_table_kernel, L=L, gi=gi),
        grid=(g,),
        in_specs=[per_group(a) for a in ins] + [full(a) for a in reps],
        out_specs=[pl.BlockSpec((1, k, k), lambda i: (i, 0, 0)), pl.BlockSpec((1, n4, k), lambda i: (i, 0, 0)),
                   pl.BlockSpec((1, k, n4), lambda i: (i, 0, 0))],
        out_shape=[jax.ShapeDtypeStruct((g, k, k), BF16), jax.ShapeDtypeStruct((g, n4, k), BF16),
                   jax.ShapeDtypeStruct((g, k, n4), BF16)],
        compiler_params=_cparams("parallel"),
        name="ssm_tables",
    )(*ins, *reps)


def _to_groups_kernel(u_ref, a_ref, *, L, groups, gi):
    ncl = a_ref.shape[2]
    gq = LANES // gi
    for q in range(u_ref.shape[0]):
        for j in range(L):
            zt = u_ref[q, pl.ds(j, ncl, stride=L), :].T
            a_ref[q * gq:(q + 1) * gq, pl.ds(j * gi, gi), :] = zt.reshape(gq, gi, ncl).astype(BF16)


def _to_groups(us, L, groups, gi):
    nq, t, _ = us.shape
    ncl = SSM_TILE_CHUNKS
    return pl.pallas_call(
        functools.partial(_to_groups_kernel, L=L, groups=groups, gi=gi),
        grid=(t // (ncl * L),),
        in_specs=[pl.BlockSpec((nq, ncl * L, LANES), lambda i: (0, i, 0))],
        out_specs=pl.BlockSpec((groups, L * gi, ncl), lambda i: (0, 0, i)),
        out_shape=jax.ShapeDtypeStruct((groups, L * gi, t // L), BF16),
        compiler_params=_cparams("parallel"),
        name="to_groups",
    )(us)


def _to_tokens_kernel(y_ref, o_ref, *, L, groups, gi):
    ncl = y_ref.shape[2]
    gq = LANES // gi
    for q in range(o_ref.shape[0]):
        for j in range(L):
            yj = y_ref[q * gq:(q + 1) * gq, pl.ds(j * gi, gi), :].astype(F32).reshape(LANES, ncl)
            o_ref[q, pl.ds(j, ncl, stride=L), :] = yj.T


def _to_tokens(yt, L, groups, gi):
    _, k, nchunks = yt.shape
    ncl = SSM_TILE_CHUNKS
    return pl.pallas_call(
        functools.partial(_to_tokens_kernel, L=L, groups=groups, gi=gi),
        grid=(nchunks // ncl,),
        in_specs=[pl.BlockSpec((groups, k, ncl), lambda i: (0, 0, i))],
        out_specs=pl.BlockSpec((groups * gi // LANES, ncl * L, LANES), lambda i: (0, i, 0)),
        out_shape=jax.ShapeDtypeStruct((groups * gi // LANES, nchunks * L, LANES), F32),
        compiler_params=_cparams("parallel"),
        name="to_tokens",
    )(yt)


def _ssm_kernel(a_ref, t_ref, e_ref, f_ref, al_ref, dv_ref, y_ref, s_scr, h_scr, *, nc, nb, rows, n):
    a = a_ref[0]
    y1 = _dot(t_ref[0], a) + dv_ref[0] * a.astype(F32)
    st = _dot(e_ref[0], a)
    n2 = 2 * n
    s_scr[...] = jnp.zeros_like(s_scr)
    for b in range(nb):
        sb_t = st[:, b * nc:(b + 1) * nc].T
        for q in range(2):
            s_scr[q, pl.ds(b, nc, stride=rows), :] = sb_t[:, q * n2:(q + 1) * n2]
    ar = al_ref[0, 0:1, :]
    ai = al_ref[0, 1:2, :]
    is_fwd = lax.broadcasted_iota(I32, (rows, n2), 1) < n

    def step(i, carry):
        hr, hi = carry
        rf = pl.ds(pl.multiple_of(i * rows, rows), rows)
        rb = pl.ds(pl.multiple_of((nc - 1 - i) * rows, rows), rows)
        h_scr[0, rf, 0:n] = hr[:, 0:n]
        h_scr[1, rf, 0:n] = hi[:, 0:n]
        h_scr[0, rb, n:n2] = hr[:, n:n2]
        h_scr[1, rb, n:n2] = hi[:, n:n2]
        sr = jnp.where(is_fwd, s_scr[0, rf, :], s_scr[0, rb, :])
        si = jnp.where(is_fwd, s_scr[1, rf, :], s_scr[1, rb, :])
        return ar * hr - ai * hi + sr, ar * hi + ai * hr + si

    zero = jnp.zeros((rows, n2), F32)
    lax.fori_loop(0, nc, step, (zero, zero))
    ht = jnp.concatenate(
        [jnp.concatenate([h_scr[q, pl.ds(b, nc, stride=rows), :].T for q in range(2)], axis=0) for b in range(nb)],
        axis=1)
    y = y1 + _dot(f_ref[0], ht.astype(BF16))
    y_ref[0] = jax.nn.gelu(y, approximate=True).astype(BF16)


def _ssm(ag, t_tab, e_tab, f_tab, al, dv, nc, nb):
    g, k, m = ag.shape
    n4 = e_tab.shape[1]
    rows = -(-nb // SUBLANES) * SUBLANES
    return pl.pallas_call(
        functools.partial(_ssm_kernel, nc=nc, nb=nb, rows=rows, n=n4 // 4),
        grid=(g,),
        in_specs=[
            pl.BlockSpec((1, k, m), lambda i: (i, 0, 0)),
            pl.BlockSpec((1, k, k), lambda i: (i, 0, 0)),
            pl.BlockSpec((1, n4, k), lambda i: (i, 0, 0)),
            pl.BlockSpec((1, k, n4), lambda i: (i, 0, 0)),
            pl.BlockSpec((1, 2, n4 // 2), lambda i: (i, 0, 0)),
            pl.BlockSpec((1, k, 1), lambda i: (i, 0, 0)),
        ],
        out_specs=pl.BlockSpec((1, k, m), lambda i: (i, 0, 0)),
        out_shape=jax.ShapeDtypeStruct((g, k, m), BF16),
        scratch_shapes=[pltpu.VMEM((2, nc * rows, n4 // 2), F32), pltpu.VMEM((2, nc * rows, n4 // 2), F32)],
        compiler_params=_cparams("parallel"),
        name="ssm",
    )(ag, t_tab, e_tab, f_tab, al, dv)


def _mixout_kernel(fre_ref, ys_ref, gate_ref, x_ref, wf_ref, wglu_ref, wout_ref, gffn_ref, wr_ref,
                   x1_ref, h2_ref, aff_ref, *, d, ne):
    for r in range(fre_ref.shape[0] // MIX_ROWS):
        rows = pl.ds(r * MIX_ROWS, MIX_ROWS)
        y_f = _dot(fre_ref[rows, :], wf_ref[...])
        ys = jnp.concatenate([ys_ref[q, rows, :] for q in range(ys_ref.shape[0])], axis=1)
        vg = _dot(ys.astype(BF16), wglu_ref[...])
        y_s = vg[:, :d] * jax.nn.sigmoid(vg[:, d:])
        gate = gate_ref[rows, :].astype(F32)
        m = gate[:, :d] * y_f + gate[:, d:] * y_s
        x1 = x_ref[rows, :] + _dot(m.astype(BF16), wout_ref[...])
        x1_ref[rows, :] = x1
        h2 = _rms(x1, gffn_ref[...])
        h2_ref[rows, :] = h2.astype(BF16)
        hi = h2.astype(BF16)
        lo = (h2 - hi.astype(F32)).astype(BF16)
        rl = _dot(jnp.concatenate([hi, lo], axis=1), wr_ref[...])
        logits = rl[:, :LANES] + rl[:, LANES:]
        logits = jnp.where(lax.broadcasted_iota(I32, logits.shape, 1) < ne, logits, -1e30)
        logits = logits - jnp.max(logits, axis=-1, keepdims=True)
        p = jnp.exp(logits)
        aff = p / jnp.sum(p, axis=-1, keepdims=True)
        aff_ref[:, rows] = aff.T[:ne, :]


def _mixout(fre, ys, gate, x2, wf, wglu, wout, gffn, wr2, ne, tm=ROW_TILE):
    t, d = x2.shape
    full = lambda a: pl.BlockSpec(a.shape, lambda i: (0,) * a.ndim, pipeline_mode=pl.Buffered(1))
    row = lambda a: pl.BlockSpec((tm, a.shape[1]), lambda i: (i, 0))
    return pl.pallas_call(
        functools.partial(_mixout_kernel, d=d, ne=ne),
        grid=(t // tm,),
        in_specs=[row(fre), pl.BlockSpec((ys.shape[0], tm, LANES), lambda i: (0, i, 0)), row(gate), row(x2),
                  full(wf), full(wglu), full(wout), full(gffn), full(wr2)],
        out_specs=[
            pl.BlockSpec((tm, d), lambda i: (i, 0)),
            pl.BlockSpec((tm, d), lambda i: (i, 0)),
            pl.BlockSpec((ne, tm), lambda i: (0, i)),
        ],
        out_shape=[
            jax.ShapeDtypeStruct((t, d), F32),
            jax.ShapeDtypeStruct((t, d), BF16),
            jax.ShapeDtypeStruct((ne, t), F32),
        ],
        compiler_params=_cparams("parallel"),
        name="mixout",
    )(fre, ys, gate, x2, wf, wglu, wout, gffn, wr2)


def _topk_kernel(aff_ref, pos_ref, st_ref, *, cap, blk):
    v = aff_ref[...]
    r, s = v.shape
    capf = float(cap)

    def bit_step(i, t):
        cand = t | (jnp.int32(1) << (30 - i))
        cnt = jnp.sum(jnp.where(v >= pltpu.bitcast(cand, F32), 1.0, 0.0), axis=1, keepdims=True)
        return jnp.where(cnt >= capf, cand, t)

    thr = lax.fori_loop(0, 31, bit_step, jnp.zeros((r, 1), I32))
    gt = jnp.where(v >= pltpu.bitcast(thr + 1, F32), 1.0, 0.0)
    eq = jnp.where(v >= pltpu.bitcast(thr, F32), 1.0, 0.0) - gt
    need = capf - jnp.sum(gt, axis=1, keepdims=True)
    ii = lax.broadcasted_iota(I32, (blk, blk), 0)
    jj = lax.broadcasted_iota(I32, (blk, blk), 1)
    tri = jnp.where(ii < jj, 1.0, 0.0).astype(BF16)
    run_eq = jnp.zeros((r, 1), F32)
    run = jnp.zeros((r, 1), F32)
    for k in range(s // blk):
        sl = slice(k * blk, (k + 1) * blk)
        eqb, gtb = eq[:, sl], gt[:, sl]
        rank_eq = _dot(eqb.astype(BF16), tri) + run_eq
        run_eq = run_eq + jnp.sum(eqb, axis=1, keepdims=True)
        mask = gtb + eqb * jnp.where(rank_eq < need, 1.0, 0.0)
        pos = _dot(mask.astype(BF16), tri) + run
        st_ref[:, k:k + 1] = run.astype(I32)
        run = run + jnp.sum(mask, axis=1, keepdims=True)
        pos_ref[:, sl] = jnp.where(mask > 0.0, pos, -1.0).astype(I32)
    st_ref[:, s // blk:s // blk + 1] = run.astype(I32)


def _topk(aff_rows, cap, blk=TOPK_BLOCK):
    r, s = aff_rows.shape
    nblk = s // blk
    return pl.pallas_call(
        functools.partial(_topk_kernel, cap=cap, blk=blk),
        grid=(1,),
        in_specs=[pl.BlockSpec((r, s), lambda i: (0, 0))],
        out_specs=[pl.BlockSpec((r, s), lambda i: (0, 0)), pl.BlockSpec((r, nblk + 1), lambda i: (0, 0))],
        out_shape=[jax.ShapeDtypeStruct((r, s), I32), jax.ShapeDtypeStruct((r, nblk + 1), I32)],
        compiler_params=_cparams("arbitrary"),
        name="topk",
    )(aff_rows)


def _floor_rows(x):
    return (x // PACKED_ROWS) * PACKED_ROWS


def _num_passes(starts, ends, win):
    need = ends[0] - _floor_rows(starts[0])
    for a, b in zip(starts[1:], ends[1:]):
        need = jnp.maximum(need, b - _floor_rows(a))
    return (need + win - 1) // win


def _pass_window(start, p, win, cap):
    first = _floor_rows(start) + p * win
    return first, pl.multiple_of(jnp.minimum(first, cap - win), PACKED_ROWS)


def _gather_kernel(st_ref, h_ref, pos_ref, aff_ref, x_ref, v_ref, *, nt, tt, win, cap, nbatch, ng):
    rows = [((pl.program_id(1) * ng + g) * nbatch + pl.program_id(0)) * (nt + 1) for g in range(ng)]
    x_ref[...] = jnp.zeros_like(x_ref)
    v_ref[...] = jnp.zeros_like(v_ref)
    riota = lax.broadcasted_iota(I32, (win, tt), 0)

    def bounds(t):
        return [st_ref[r + t] for r in rows], [st_ref[r + t + 1] for r in rows]

    def one_pass(t, p, starts):
        hrows = h_ref[0, pl.ds(pl.multiple_of(t * tt, tt), tt), :]
        hots, wins = [], []
        for g in range(ng):
            first, ws = _pass_window(starts[g], p, win, cap)
            pos = pos_ref[g, pl.ds(t, 1), :]
            hots.append((pos - ws == riota) & (pos >= first))
            wins.append(ws)
        onehot = jnp.concatenate([jnp.where(h, 1.0, 0.0).astype(BF16) for h in hots], axis=0)
        rows = _dot(onehot, hrows)
        for g in range(ng):
            x_ref[0, g, pl.ds(wins[g], win), :] += rows[g * win:(g + 1) * win].astype(BF16)
            vals = jnp.sum(jnp.where(hots[g], aff_ref[g, pl.ds(t, 1), :], 0.0), axis=1, keepdims=True)
            v_ref[0, g, pl.ds(wins[g], win), :] += vals

    def first_pass(t, most):
        starts, ends = bounds(t)
        one_pass(t, 0, starts)
        return jnp.maximum(most, _num_passes(starts, ends, win))

    def more_passes(t, carry):
        starts, ends = bounds(t)

        def body(p, c):
            one_pass(t, p, starts)
            return c

        lax.fori_loop(1, _num_passes(starts, ends, win), body, 0)
        return carry

    most = lax.fori_loop(0, nt, first_pass, jnp.int32(0), unroll=GATHER_UNROLL)

    @pl.when(most > 1)
    def _():
        lax.fori_loop(0, nt, more_passes, 0)


def _gather(starts, h3, pos3, aff3t, cap, ne):
    b, s, d = h3.shape
    tt, win, ng = GATHER_TILE, GATHER_WINDOW, GATHER_EXPERTS
    nt = s // tt
    grid_spec = pltpu.PrefetchScalarGridSpec(
        num_scalar_prefetch=1,
        grid=(b, ne // ng),
        in_specs=[
            pl.BlockSpec((1, s, d), lambda i, e, st: (i, 0, 0), pipeline_mode=pl.Buffered(1)),
            pl.BlockSpec((ng, nt, tt), lambda i, e, st: (e, i, 0)),
            pl.BlockSpec((ng, nt, tt), lambda i, e, st: (e, i, 0)),
        ],
        out_specs=[
            pl.BlockSpec((1, ng, cap, d), lambda i, e, st: (i, e, 0, 0)),
            pl.BlockSpec((1, ng, cap, 1), lambda i, e, st: (i, e, 0, 0)),
        ],
    )
    return pl.pallas_call(
        functools.partial(_gather_kernel, nt=nt, tt=tt, win=win, cap=cap, nbatch=b, ng=ng),
        grid_spec=grid_spec,
        out_shape=[jax.ShapeDtypeStruct((b, ne, cap, d), BF16), jax.ShapeDtypeStruct((b, ne, cap, 1), F32)],
        compiler_params=_cparams("arbitrary", "arbitrary"),
        name="gather",
    )(starts, h3, pos3, aff3t)


def _ffn_kernel(x_ref, v_ref, wg_ref, wu_ref, wd_ref, y_ref, acc_ref, *, nf):
    f = pl.program_id(2)
    nb, _, cap, d = x_ref.shape

    @pl.when(f == 0)
    def _():
        acc_ref[...] = jnp.zeros_like(acc_ref)

    wg = wg_ref[0].astype(BF16)
    wu = wu_ref[0].astype(BF16)
    wd = wd_ref[0].astype(BF16)
    mr = min(FFN_ROWS, cap)
    for r in range(nb * cap // mr):
        x = x_ref[r * mr // cap, 0, pl.ds(r * mr % cap, mr), :]
        g = _dot(x, wg)
        u = _dot(x, wu)
        hid = (g * jax.nn.sigmoid(g) * u).astype(BF16)
        acc_ref[pl.ds(r * mr, mr), :] += _dot(hid, wd)

    @pl.when(f == nf - 1)
    def _():
        y = acc_ref[...] * v_ref[...].reshape(nb * cap, 1)
        y_ref[...] = y.astype(BF16).reshape(nb, 1, cap, d)


def _ffn(xg, vals, wg, wu, wd, nb, tf):
    b, ne, cap, d = xg.shape
    dexp = wg.shape[2]
    nf = dexp // tf
    return pl.pallas_call(
        functools.partial(_ffn_kernel, nf=nf),
        grid=(ne, b // nb, nf),
        in_specs=[
            pl.BlockSpec((nb, 1, cap, d), lambda e, i, f: (i, e, 0, 0)),
            pl.BlockSpec((nb, 1, cap, 1), lambda e, i, f: (i, e, 0, 0)),
            pl.BlockSpec((1, d, tf), lambda e, i, f: (e, 0, f)),
            pl.BlockSpec((1, d, tf), lambda e, i, f: (e, 0, f)),
            pl.BlockSpec((1, tf, d), lambda e, i, f: (e, f, 0)),
        ],
        out_specs=pl.BlockSpec((nb, 1, cap, d), lambda e, i, f: (i, e, 0, 0)),
        out_shape=jax.ShapeDtypeStruct(xg.shape, BF16),
        scratch_shapes=[pltpu.VMEM((nb * cap, d), F32)],
        compiler_params=_cparams("parallel", "parallel", "arbitrary"),
        name="ffn",
    )(xg, vals, wg, wu, wd)


def _combine_kernel(st_ref, y_ref, pos_ref, spread_ref, lanes_ref, x1_ref, p_ref, wpp_ref, wpg_ref, gple_ref,
                    gout_ref, o_ref, *, nt, tt, win, cap, ne, nbatch, nsub, final):
    k = ne * win
    lane_e = lanes_ref[0:1, :]
    lane_r = lanes_ref[1:2, :]

    def tile_work(u):
        t = pl.program_id(1) * nsub + u
        rows = [(e * nbatch + pl.program_id(0)) * (nt + 1) + t for e in range(ne)]
        starts = [st_ref[r] for r in rows]
        ends = [st_ref[r + 1] for r in rows]
        pos = pos_ref[:, u * tt:(u + 1) * tt].astype(F32)
        pos = jnp.concatenate([pos, jnp.zeros((LANES - ne, tt), F32)], axis=0).T
        hi = jnp.floor(pos * (1.0 / COMBINE_SPLIT))
        lo = pos - hi * COMBINE_SPLIT
        rank = _dot(jnp.concatenate([hi, lo], axis=1).astype(BF16), spread_ref[...])

        def expert_rows(p):
            firsts = jnp.zeros((1, k), F32)
            offs = jnp.zeros((1, k), F32)
            wins = []
            for e in range(ne):
                first, ws = _pass_window(starts[e], p, win, cap)
                firsts = jnp.where(lane_e == e, first.astype(F32), firsts)
                offs = jnp.where(lane_e == e, ws.astype(F32), offs)
                wins.append(y_ref[0, e, pl.ds(ws, win), :])
            onehot = jnp.where((rank - offs == lane_r) & (rank >= firsts), 1.0, 0.0).astype(BF16)
            return _dot(onehot, jnp.concatenate(wins, axis=0))

        def finish(acc):
            for r in range(tt // LANES):
                rows = pl.ds(u * tt + r * LANES, LANES)
                x2 = acc[r * LANES:(r + 1) * LANES]
                emb = _dot(p_ref[0, rows, :].astype(BF16), wpp_ref[...])
                gate = jax.nn.sigmoid(_dot(_rms(x2, gple_ref[...]).astype(BF16), wpg_ref[...]))
                x3 = x2 + gate * emb
                o_ref[0, rows, :] = _rms(x3, gout_ref[...]) if final else x3

        x1 = x1_ref[0, pl.ds(u * tt, tt), :]
        finish(x1 + expert_rows(0))
        npass = _num_passes(starts, ends, win)
        redo = lambda: finish(lax.fori_loop(1, npass, lambda p, acc: acc + expert_rows(p), x1 + expert_rows(0)))
        return npass, redo

    for npass, redo in [tile_work(u) for u in range(nsub)]:
        pl.when(npass > 1)(redo)


def _combine(starts, yg, post, x13, p3, wpp, wpg, gple, gout, cap, final):
    b, ne, _, d = yg.shape
    s = x13.shape[1]
    tt, win, nsub = COMBINE_TILE, COMBINE_WINDOW, COMBINE_STEP_TILES
    nt = s // tt
    lane = jnp.arange(ne * win, dtype=I32)
    part = jnp.arange(2 * LANES, dtype=I32)[:, None]
    spread = jnp.where(part == lane // win, float(COMBINE_SPLIT), jnp.where(part == lane // win + LANES, 1.0, 0.0))
    spread = spread.astype(BF16)
    lanes = jnp.stack([lane // win, lane % win]).astype(F32)
    tile = lambda a: pl.BlockSpec((1, nsub * tt, a.shape[2]), lambda i, t, st: (i, t, 0))
    full = lambda a: pl.BlockSpec(a.shape, lambda i, t, st: (0,) * a.ndim, pipeline_mode=pl.Buffered(1))
    grid_spec = pltpu.PrefetchScalarGridSpec(
        num_scalar_prefetch=1,
        grid=(b, nt // nsub),
        in_specs=[
            pl.BlockSpec((1, ne, cap, d), lambda i, t, st: (i, 0, 0, 0), pipeline_mode=pl.Buffered(1)),
            pl.BlockSpec((ne, nsub * tt), lambda i, t, st: (0, i * (nt // nsub) + t)),
            full(spread), full(lanes), tile(x13), tile(p3), full(wpp), full(wpg), full(gple), full(gout),
        ],
        out_specs=pl.BlockSpec((1, nsub * tt, d), lambda i, t, st: (i, t, 0)),
    )
    return pl.pallas_call(
        functools.partial(_combine_kernel, nt=nt, tt=tt, win=win, cap=cap, ne=ne, nbatch=b, nsub=nsub, final=final),
        grid_spec=grid_spec,
        out_shape=jax.ShapeDtypeStruct(x13.shape, F32),
        compiler_params=_cparams("arbitrary", "arbitrary"),
        name="combine",
    )(starts, yg, post, spread, lanes, x13, p3, wpp, wpg, gple, gout)


def kernel(x, p, g_mix, w_in, w_fourier, ssm_a_re, ssm_a_im, ssm_log_dt, ssm_b_re, ssm_b_im, ssm_c_re, ssm_c_im, ssm_d, w_glu, w_out, g_ffn, w_router, w_exp_gate, w_exp_up, w_exp_down, g_ple, w_ple_gate, w_ple_proj, g_final):
    b, s, d = x.shape
    depth = p.shape[0]
    df = w_fourier.shape[1]
    ds = w_glu.shape[1]
    ne = w_router.shape[2]
    dexp = w_exp_gate.shape[3]
    groups, gi = ssm_b_re.shape[2], ssm_b_re.shape[4]
    L = SSM_CHUNK
    n1 = s // DFT_N2
    nc = s // L
    cap = EC_CAPACITY * s // ne
    assert s % (DFT_N2 * PACKED_ROWS) == 0 and s % GATHER_TILE == 0 and cap % PACKED_ROWS == 0
    assert nc % SSM_TILE_CHUNKS == 0 and gi == PACKED_ROWS and cap // COMBINE_SPLIT < 256
    assert 2 * ssm_a_re.shape[3] == LANES and ds % LANES == 0
    assert cap >= COMBINE_WINDOW and ne % GATHER_EXPERTS == 0 and GATHER_TILE == COMBINE_TILE
    assert ds == groups * gi and df % FOURIER_GROUPS == 0

    bd = _channel_dft_table(df)
    ck, sk = _dft1_tables(n1)
    tr, ti = _dft3_tables(n1)
    tf = 256 if dexp % 256 == 0 else dexp
    nb_ffn = 2 if b % 2 == 0 else 1

    xcur = x.reshape(b * s, d)
    for i in range(depth):
        final = i == depth - 1
        row = lambda v: v.astype(F32).reshape(1, -1)
        pq, us, gate = _inproj(xcur, row(g_mix[i]), w_in[i].astype(BF16), bd, df, ds)
        a4 = _dft1(pq.reshape(b, n1, DFT_N2, 2 * df), ck, sk)
        fre = _dft3(a4, tr, ti).reshape(b * s, df)

        tabs = _ssm_tables(ssm_a_re[i], ssm_a_im[i], ssm_log_dt[i], ssm_b_re[i], ssm_b_im[i],
                           ssm_c_re[i], ssm_c_im[i], ssm_d[i])
        ys = _to_tokens(_ssm(_to_groups(us, L, groups, gi), *tabs, nc=nc, nb=b), L, groups, gi)

        wr = w_router[i].astype(F32)
        wr_hi = wr.astype(BF16)
        wr_lo = (wr - wr_hi.astype(F32)).astype(BF16)
        lane_pad = lambda w: jnp.pad(w, ((0, 0), (0, LANES - ne)))
        wr2 = jnp.concatenate([jnp.concatenate([lane_pad(wr_hi), lane_pad(wr_lo)], axis=1),
                               jnp.concatenate([lane_pad(wr_hi), jnp.zeros((d, LANES), BF16)], axis=1)], axis=0)
        x1, h2, aff = _mixout(fre, ys, gate, xcur, w_fourier[i].astype(BF16), w_glu[i].astype(BF16),
                              w_out[i].astype(BF16), row(g_ffn[i]), wr2, ne)

        aff_rows = aff.reshape(ne * b, s)
        posm, st_blk = _topk(aff_rows, cap)
        starts = st_blk[:, ::GATHER_TILE // TOPK_BLOCK].reshape(-1)
        tiles = (ne, b * (s // GATHER_TILE), GATHER_TILE)
        xg, vals = _gather(starts, h2.reshape(b, s, d), posm.reshape(tiles), aff_rows.reshape(tiles), cap, ne)
        yg = _ffn(xg, vals, w_exp_gate[i], w_exp_up[i], w_exp_down[i], nb_ffn, tf)
        xnext = _combine(starts, yg, posm.reshape(ne, b * s), x1.reshape(b, s, d), p[i],
                         w_ple_proj[i].astype(BF16), w_ple_gate[i].astype(BF16), row(g_ple[i]),
                         row(g_final) if final else row(g_ple[i]), cap, final)
        xcur = xnext.reshape(b * s, d)
    return xcur.reshape(b, s, d)
```

```python
import functools
import math

import jax
import jax.numpy as jnp
from jax import lax
from jax.experimental import pallas as pl
from jax.experimental.pallas import tpu as pltpu

F32 = jnp.float32
BF16 = jnp.bfloat16
I32 = jnp.int32

RMS_EPS = 1e-6
FOURIER_GROUPS = 4
EC_CAPACITY = 2
DFT_N2 = 128
SUBLANES = 8
LANES = 128
PACKED_ROWS = 16
SSM_CHUNK = 32
ROW_TILE = 1024
INPROJ_ROWS = 128
SSM_TILE_CHUNKS = 128
TOPK_BLOCK = 128
GATHER_TILE = 256
MIX_ROWS = 256
FFN_ROWS = 256
GATHER_WINDOW = 64
GATHER_EXPERTS = 4
GATHER_UNROLL = 8
COMBINE_TILE = 256
COMBINE_WINDOW = 64
COMBINE_STEP_TILES = 2
COMBINE_SPLIT = 32
VMEM_LIMIT = 56 * 1024 * 1024


def _cparams(*sem):
    return pltpu.CompilerParams(dimension_semantics=sem, vmem_limit_bytes=VMEM_LIMIT)


def _rms(x, g):
    return x * lax.rsqrt(jnp.mean(x * x, axis=-1, keepdims=True) + RMS_EPS) * g


def _dot(a, b):
    return jnp.dot(a, b, preferred_element_type=F32)


def _inproj_kernel(x_ref, g_ref, w_ref, bd_ref, pq_ref, us_ref, gate_ref, *, df, ds):
    for r in range(x_ref.shape[0] // INPROJ_ROWS):
        rows = pl.ds(r * INPROJ_ROWS, INPROJ_ROWS)
        h = _rms(x_ref[rows, :], g_ref[...]).astype(BF16)
        z = _dot(h, w_ref[...])
        pq_ref[rows, :] = _dot(z[:, :df].astype(BF16), bd_ref[...]).astype(BF16)
        for q in range(ds // LANES):
            us_ref[q, rows, :] = z[:, df + q * LANES:df + (q + 1) * LANES]
        gate_ref[rows, :] = jax.nn.sigmoid(z[:, df + ds:]).astype(BF16)


def _inproj(x2, g, w_in, bd, df, ds, tm=ROW_TILE):
    t, d = x2.shape
    dg = w_in.shape[1] - df - ds
    return pl.pallas_call(
        functools.partial(_inproj_kernel, df=df, ds=ds),
        grid=(t // tm,),
        in_specs=[
            pl.BlockSpec((tm, d), lambda i: (i, 0)),
            pl.BlockSpec((1, d), lambda i: (0, 0)),
            pl.BlockSpec(w_in.shape, lambda i: (0, 0), pipeline_mode=pl.Buffered(1)),
            pl.BlockSpec(bd.shape, lambda i: (0, 0), pipeline_mode=pl.Buffered(1)),
        ],
        out_specs=[
            pl.BlockSpec((tm, 2 * df), lambda i: (i, 0)),
            pl.BlockSpec((ds // LANES, tm, LANES), lambda i: (0, i, 0)),
            pl.BlockSpec((tm, dg), lambda i: (i, 0)),
        ],
        out_shape=[
            jax.ShapeDtypeStruct((t, 2 * df), BF16),
            jax.ShapeDtypeStruct((ds // LANES, t, LANES), F32),
            jax.ShapeDtypeStruct((t, dg), BF16),
        ],
        compiler_params=_cparams("parallel"),
        name="inproj",
    )(x2, g, w_in, bd)


def _channel_dft_table(df):
    c = df // FOURIER_GROUPS
    k = jnp.arange(c, dtype=I32)
    ang = (2.0 * math.pi / c) * ((k[:, None] * k[None, :]) % c).astype(F32)
    eye = jnp.eye(FOURIER_GROUPS, dtype=F32)
    scale = 1.0 / math.sqrt(c)
    re = jnp.kron(eye, jnp.cos(ang)) * scale
    im = -jnp.kron(eye, jnp.sin(ang)) * scale
    return jnp.concatenate([re, im], axis=1).astype(BF16)


def _dft1_tables(n1):
    k = jnp.arange(n1, dtype=I32)
    ang = (2.0 * math.pi / n1) * ((k[:, None] * k[None, :]) % n1).astype(F32)
    row = jnp.arange(n1 * SUBLANES, dtype=I32)
    rep = (row[:, None] // SUBLANES == k[None, :]).astype(F32)
    same_slot = (row[:, None] % SUBLANES == row[None, :] % SUBLANES).astype(F32) * (1.0 / math.sqrt(n1))
    hp = lax.Precision.HIGHEST
    kron8 = lambda t: (jnp.dot(jnp.dot(rep, t, precision=hp), rep.T, precision=hp) * same_slot).astype(BF16)
    return kron8(jnp.cos(ang)), kron8(jnp.sin(ang))


def _dft3_tables(n1):
    s = n1 * DFT_N2
    nb = n1 // SUBLANES
    k1 = jnp.arange(n1, dtype=I32)[:, None, None]
    k2 = jnp.arange(DFT_N2, dtype=I32)[None, :, None]
    n2 = jnp.arange(DFT_N2, dtype=I32)[None, None, :]
    ang = (2.0 * math.pi / s) * ((n2 * (k1 + n1 * k2)) % s).astype(F32)
    eye = jnp.eye(SUBLANES, dtype=BF16)[None, None, :, :, None]
    rows = DFT_N2 * SUBLANES

    def expand(t):
        t = (t * (1.0 / math.sqrt(DFT_N2))).astype(BF16).reshape(nb, SUBLANES, DFT_N2, DFT_N2)
        t = jnp.transpose(t, (0, 2, 1, 3))[:, :, :, None, :]
        return (t * eye).reshape(nb, rows, rows)

    return expand(jnp.cos(ang)), expand(jnp.sin(ang))


def _dft1_kernel(z_ref, ck_ref, sk_ref, a_ref):
    n1, slab, c2 = z_ref.shape[1], z_ref.shape[2], z_ref.shape[3]
    c = c2 // 2
    z = z_ref[0].astype(F32)
    halves = []
    for h in range(slab // SUBLANES):
        zh = z[:, h * SUBLANES:(h + 1) * SUBLANES, :].reshape(n1 * SUBLANES, c2).astype(BF16)
        cz = _dot(ck_ref[...], zh)
        sz = _dot(sk_ref[...], zh)
        a = jnp.concatenate([cz[:, :c] + sz[:, c:], cz[:, c:] - sz[:, :c]], axis=1)
        halves.append(a.reshape(n1, SUBLANES, c2))
    a_ref[0] = jnp.concatenate(halves, axis=1).astype(BF16)


def _dft1(pq4, ck, sk):
    b, n1, n2, c2 = pq4.shape
    slab = PACKED_ROWS
    return pl.pallas_call(
        _dft1_kernel,
        grid=(b, n2 // slab),
        in_specs=[
            pl.BlockSpec((1, n1, slab, c2), lambda i, j: (i, 0, j, 0)),
            pl.BlockSpec(ck.shape, lambda i, j: (0, 0)),
            pl.BlockSpec(sk.shape, lambda i, j: (0, 0)),
        ],
        out_specs=pl.BlockSpec((1, n1, slab, c2), lambda i, j: (i, 0, j, 0)),
        out_shape=jax.ShapeDtypeStruct(pq4.shape, BF16),
        compiler_params=_cparams("parallel", "parallel"),
        name="dft1",
    )(pq4, ck, sk)


def _dft3_kernel(a_ref, tr_ref, ti_ref, o_ref):
    nblk, c2 = tr_ref.shape[0], a_ref.shape[3]
    c = c2 // 2
    outs = []
    for h in range(nblk):
        a = a_ref[0, h * SUBLANES:(h + 1) * SUBLANES].reshape(SUBLANES * DFT_N2, c2)
        out = _dot(tr_ref[h], a[:, :c]) + _dot(ti_ref[h], a[:, c:])
        outs.append(out.reshape(DFT_N2, SUBLANES, c))
    o_ref[0] = jnp.concatenate(outs, axis=1).astype(BF16)


def _dft3(a4, tr, ti):
    b, n1, n2, c2 = a4.shape
    c = c2 // 2
    rows = DFT_N2 * SUBLANES
    nblk = PACKED_ROWS // SUBLANES
    return pl.pallas_call(
        _dft3_kernel,
        grid=(n1 // PACKED_ROWS, b),
        in_specs=[
            pl.BlockSpec((1, PACKED_ROWS, n2, c2), lambda k, i: (i, k, 0, 0)),
            pl.BlockSpec((nblk, rows, rows), lambda k, i: (k, 0, 0)),
            pl.BlockSpec((nblk, rows, rows), lambda k, i: (k, 0, 0)),
        ],
        out_specs=pl.BlockSpec((1, DFT_N2, PACKED_ROWS, c), lambda k, i: (i, 0, k, 0)),
        out_shape=jax.ShapeDtypeStruct((b, DFT_N2, n1, c), BF16),
        compiler_params=_cparams("parallel", "parallel"),
        name="dft3",
    )(a4, tr, ti)


def _ssm_tables(a_re, a_im, log_dt, b_re, b_im, c_re, c_im, d):
    L = SSM_CHUNK
    _, g, n = a_re.shape
    gi = b_re.shape[-1]
    dt = jnp.exp(log_dt.astype(F32))[..., None]
    ar, ai = a_re.astype(F32), a_im.astype(F32)
    tau = jnp.arange(L + 1, dtype=F32)[:, None, None, None]
    mag = jnp.exp(tau * (ar * dt)[None])
    ang = tau * (ai * dt)[None]
    pr, pi = mag * jnp.cos(ang), mag * jnp.sin(ang)
    nr, ni = pr[1] - 1.0, pi[1]
    den = ar * ar + ai * ai
    qr, qi = (nr * ar + ni * ai) / den, (ni * ar - nr * ai) / den
    br, bi = b_re.astype(F32), b_im.astype(F32)
    bbr = qr[..., None] * br - qi[..., None] * bi
    bbi = qr[..., None] * bi + qi[..., None] * br
    cr, ci = c_re.astype(F32), c_im.astype(F32)

    cat = jnp.concatenate
    steps = lambda p: jnp.transpose(p, (1, 0, 2))
    pf_r, pf_i, pb_r, pb_i = steps(pr[1:, 0]), steps(pi[1:, 0]), steps(pr[1:, 1][::-1]), steps(pi[1:, 1][::-1])
    f_small = jnp.stack([cat([cr[0], cr[1], cr[0], cr[1]], -1), cat([-ci[0], -ci[1], -ci[0], -ci[1]], -1)], 1)
    f_steps = jnp.stack([cat([pf_r, pb_r, -pf_i, -pb_i], -1), cat([pf_i, pb_i, pf_r, pb_r], -1)], 1)
    last = lambda p: jnp.transpose(p, (1, 2, 0))
    ef_r, ef_i, eb_r, eb_i = last(pr[:L, 0][::-1]), last(pi[:L, 0][::-1]), last(pr[:L, 1]), last(pi[:L, 1])
    e_small = jnp.stack([cat([bbr[0], bbr[1], bbi[0], bbi[1]], 1), cat([-bbi[0], -bbi[1], bbr[0], bbr[1]], 1)], 1)
    e_steps = jnp.stack([cat([ef_r, eb_r, ef_r, eb_r], 1), cat([ef_i, eb_i, ef_i, eb_i], 1)], 1)
    zpad = lambda p, before: jnp.pad(p, ((0, 0), (0, 0), (before, L - before)))
    wf_r, wf_i, wb_r, wb_i = zpad(ef_r, 0), zpad(ef_i, 0), zpad(eb_r, L - 1), zpad(eb_i, L - 1)
    t_small = jnp.stack([cat([bbr[0], bbi[0], bbr[1], bbi[1]], 1), cat([-bbi[0], bbr[0], -bbi[1], bbr[1]], 1)], 1)
    t_steps = jnp.stack([cat([wf_r, wf_r, wb_r, wb_r], 1), cat([wf_i, wf_i, wb_i, wb_i], 1)], 1)
    t_left = cat([cr[0], -ci[0], cr[1], -ci[1]], -1)

    al = jnp.stack([cat([pr[L, 0], pr[L, 1]], -1), cat([pi[L, 0], pi[L, 1]], -1)], axis=1)
    dv = jnp.tile(d.astype(F32).reshape(g, 1, gi), (1, L, 1)).reshape(g, L * gi, 1)
    t_tab, e_tab, f_tab = _ssm_table_call(f_small, f_steps, e_small, e_steps, t_small, t_steps, t_left, L, gi)
    return t_tab, e_tab, f_tab, al, dv


def _ssm_table_kernel(fs_ref, fp_ref, es_ref, ep_ref, ts_ref, tp_ref, tl_ref, ri_ref, rj_ref, rit_ref, rjt_ref,
                      ri2_ref, rs_ref, t_ref, e_ref, f_ref, *, L, gi):
    def split(a):
        hi = a.astype(BF16)
        return hi, (a - hi.astype(F32)).astype(BF16)

    def spread_lanes(a, rep):
        hi, lo = split(a)
        return _dot(hi, rep) + _dot(lo, rep)

    def spread_rows(rep, a):
        hi, lo = split(a)
        return _dot(rep, hi) + _dot(rep, lo)

    ri, rj, rit, rjt, ri2, rs = ri_ref[...], rj_ref[...], rit_ref[...], rjt_ref[...], ri2_ref[...], rs_ref[...]
    f_ref[0] = (spread_rows(rit, fs_ref[0, 0]) * spread_rows(rjt, fp_ref[0, 0])
                + spread_rows(rit, fs_ref[0, 1]) * spread_rows(rjt, fp_ref[0, 1])).astype(BF16)
    e_ref[0] = (spread_lanes(es_ref[0, 0], ri) * spread_lanes(ep_ref[0, 0], rj)
                + spread_lanes(es_ref[0, 1], ri) * spread_lanes(ep_ref[0, 1], rj)).astype(BF16)
    ew = (spread_lanes(ts_ref[0, 0], ri2) * spread_lanes(tp_ref[0, 0], rs)
          + spread_lanes(ts_ref[0, 1], ri2) * spread_lanes(tp_ref[0, 1], rs))
    c_hi, c_lo = split(tl_ref[0])
    e_hi, e_lo = split(ew)
    kw = _dot(c_hi, e_hi) + _dot(c_hi, e_lo) + _dot(c_lo, e_hi)
    per = LANES // gi
    for r in range(per):
        shifted = kw if r == 0 else pltpu.roll(kw, 2 * L * gi - r * gi, axis=1)
        for a in range(L // per):
            jo = L - 1 - (a * per + r)
            t_ref[0, pl.ds(jo * gi, gi), :] = shifted[:, a * LANES:a * LANES + L * gi].astype(BF16)


def _ssm_table_call(f_small, f_steps, e_small, e_steps, t_small, t_steps, t_left, L, gi):
    g = f_small.shape[0]
    n4 = f_small.shape[3]
    k = L * gi
    lane = jnp.arange(2 * k, dtype=I32)
    ri2 = (lane[None, :] % gi == jnp.arange(gi, dtype=I32)[:, None]).astype(BF16)
    rs = (lane[None, :] // gi == jnp.arange(2 * L, dtype=I32)[:, None]).astype(BF16)
    ri, rj = ri2[:, :k], rs[:L, :k]
    per_group = lambda a: pl.BlockSpec((1,) + a.shape[1:], lambda i: (i,) + (0,) * (a.ndim - 1))
    full = lambda a: pl.BlockSpec(a.shape, lambda i: (0,) * a.ndim)
    ins = (f_small, f_steps, e_small, e_steps, t_small, t_steps, t_left)
    reps = (ri, rj, ri.T, rj.T, ri2, rs)
    return pl.pallas_call(
        functools.partial(_ssm_table_kernel, L=L, gi=gi),
        grid=(g,),
        in_specs=[per_group(a) for a in ins] + [full(a) for a in reps],
        out_specs=[pl.BlockSpec((1, k, k), lambda i: (i, 0, 0)), pl.BlockSpec((1, n4, k), lambda i: (i, 0, 0)),
                   pl.BlockSpec((1, k, n4), lambda i: (i, 0, 0))],
        out_shape=[jax.ShapeDtypeStruct((g, k, k), BF16), jax.ShapeDtypeStruct((g, n4, k), BF16),
                   jax.ShapeDtypeStruct((g, k, n4), BF16)],
        compiler_params=_cparams("parallel"),
        name="ssm_tables",
    )(*ins, *reps)


def _to_groups_kernel(u_ref, a_ref, *, L, groups, gi):
    ncl = a_ref.shape[2]
    gq = LANES // gi
    for q in range(u_ref.shape[0]):
        for j in range(L):
            zt = u_ref[q, pl.ds(j, ncl, stride=L), :].T
            a_ref[q * gq:(q + 1) * gq, pl.ds(j * gi, gi), :] = zt.reshape(gq, gi, ncl).astype(BF16)


def _to_groups(us, L, groups, gi):
    nq, t, _ = us.shape
    ncl = SSM_TILE_CHUNKS
    return pl.pallas_call(
        functools.partial(_to_groups_kernel, L=L, groups=groups, gi=gi),
        grid=(t // (ncl * L),),
        in_specs=[pl.BlockSpec((nq, ncl * L, LANES), lambda i: (0, i, 0))],
        out_specs=pl.BlockSpec((groups, L * gi, ncl), lambda i: (0, 0, i)),
        out_shape=jax.ShapeDtypeStruct((groups, L * gi, t // L), BF16),
        compiler_params=_cparams("parallel"),
        name="to_groups",
    )(us)


def _to_tokens_kernel(y_ref, o_ref, *, L, groups, gi):
    ncl = y_ref.shape[2]
    gq = LANES // gi
    for q in range(o_ref.shape[0]):
        for j in range(L):
            yj = y_ref[q * gq:(q + 1) * gq, pl.ds(j * gi, gi), :].astype(F32).reshape(LANES, ncl)
            o_ref[q, pl.ds(j, ncl, stride=L), :] = yj.T


def _to_tokens(yt, L, groups, gi):
    _, k, nchunks = yt.shape
    ncl = SSM_TILE_CHUNKS
    return pl.pallas_call(
        functools.partial(_to_tokens_kernel, L=L, groups=groups, gi=gi),
        grid=(nchunks // ncl,),
        in_specs=[pl.BlockSpec((groups, k, ncl), lambda i: (0, 0, i))],
        out_specs=pl.BlockSpec((groups * gi // LANES, ncl * L, LANES), lambda i: (0, i, 0)),
        out_shape=jax.ShapeDtypeStruct((groups * gi // LANES, nchunks * L, LANES), F32),
        compiler_params=_cparams("parallel"),
        name="to_tokens",
    )(yt)


def _ssm_kernel(a_ref, t_ref, e_ref, f_ref, al_ref, dv_ref, y_ref, s_scr, h_scr, *, nc, nb, rows, n):
    a = a_ref[0]
    y1 = _dot(t_ref[0], a) + dv_ref[0] * a.astype(F32)
    st = _dot(e_ref[0], a)
    n2 = 2 * n
    s_scr[...] = jnp.zeros_like(s_scr)
    for b in range(nb):
        sb_t = st[:, b * nc:(b + 1) * nc].T
        for q in range(2):
            s_scr[q, pl.ds(b, nc, stride=rows), :] = sb_t[:, q * n2:(q + 1) * n2]
    ar = al_ref[0, 0:1, :]
    ai = al_ref[0, 1:2, :]
    is_fwd = lax.broadcasted_iota(I32, (rows, n2), 1) < n

    def step(i, carry):
        hr, hi = carry
        rf = pl.ds(pl.multiple_of(i * rows, rows), rows)
        rb = pl.ds(pl.multiple_of((nc - 1 - i) * rows, rows), rows)
        h_scr[0, rf, 0:n] = hr[:, 0:n]
        h_scr[1, rf, 0:n] = hi[:, 0:n]
        h_scr[0, rb, n:n2] = hr[:, n:n2]
        h_scr[1, rb, n:n2] = hi[:, n:n2]
        sr = jnp.where(is_fwd, s_scr[0, rf, :], s_scr[0, rb, :])
        si = jnp.where(is_fwd, s_scr[1, rf, :], s_scr[1, rb, :])
        return ar * hr - ai * hi + sr, ar * hi + ai * hr + si

    zero = jnp.zeros((rows, n2), F32)
    lax.fori_loop(0, nc, step, (zero, zero))
    ht = jnp.concatenate(
        [jnp.concatenate([h_scr[q, pl.ds(b, nc, stride=rows), :].T for q in range(2)], axis=0) for b in range(nb)],
        axis=1)
    y = y1 + _dot(f_ref[0], ht.astype(BF16))
    y_ref[0] = jax.nn.gelu(y, approximate=True).astype(BF16)


def _ssm(ag, t_tab, e_tab, f_tab, al, dv, nc, nb):
    g, k, m = ag.shape
    n4 = e_tab.shape[1]
    rows = -(-nb // SUBLANES) * SUBLANES
    return pl.pallas_call(
        functools.partial(_ssm_kernel, nc=nc, nb=nb, rows=rows, n=n4 // 4),
        grid=(g,),
        in_specs=[
            pl.BlockSpec((1, k, m), lambda i: (i, 0, 0)),
            pl.BlockSpec((1, k, k), lambda i: (i, 0, 0)),
            pl.BlockSpec((1, n4, k), lambda i: (i, 0, 0)),
            pl.BlockSpec((1, k, n4), lambda i: (i, 0, 0)),
            pl.BlockSpec((1, 2, n4 // 2), lambda i: (i, 0, 0)),
            pl.BlockSpec((1, k, 1), lambda i: (i, 0, 0)),
        ],
        out_specs=pl.BlockSpec((1, k, m), lambda i: (i, 0, 0)),
        out_shape=jax.ShapeDtypeStruct((g, k, m), BF16),
        scratch_shapes=[pltpu.VMEM((2, nc * rows, n4 // 2), F32), pltpu.VMEM((2, nc * rows, n4 // 2), F32)],
        compiler_params=_cparams("parallel"),
        name="ssm",
    )(ag, t_tab, e_tab, f_tab, al, dv)


def _mixout_kernel(fre_ref, ys_ref, gate_ref, x_ref, wf_ref, wglu_ref, wout_ref, gffn_ref, wr_ref,
                   x1_ref, h2_ref, aff_ref, *, d, ne):
    for r in range(fre_ref.shape[0] // MIX_ROWS):
        rows = pl.ds(r * MIX_ROWS, MIX_ROWS)
        y_f = _dot(fre_ref[rows, :], wf_ref[...])
        ys = jnp.concatenate([ys_ref[q, rows, :] for q in range(ys_ref.shape[0])], axis=1)
        vg = _dot(ys.astype(BF16), wglu_ref[...])
        y_s = vg[:, :d] * jax.nn.sigmoid(vg[:, d:])
        gate = gate_ref[rows, :].astype(F32)
        m = gate[:, :d] * y_f + gate[:, d:] * y_s
        x1 = x_ref[rows, :] + _dot(m.astype(BF16), wout_ref[...])
        x1_ref[rows, :] = x1
        h2 = _rms(x1, gffn_ref[...])
        h2_ref[rows, :] = h2.astype(BF16)
        hi = h2.astype(BF16)
        lo = (h2 - hi.astype(F32)).astype(BF16)
        rl = _dot(jnp.concatenate([hi, lo], axis=1), wr_ref[...])
        logits = rl[:, :LANES] + rl[:, LANES:]
        logits = jnp.where(lax.broadcasted_iota(I32, logits.shape, 1) < ne, logits, -1e30)
        logits = logits - jnp.max(logits, axis=-1, keepdims=True)
        p = jnp.exp(logits)
        aff = p / jnp.sum(p, axis=-1, keepdims=True)
        aff_ref[:, rows] = aff.T[:ne, :]


def _mixout(fre, ys, gate, x2, wf, wglu, wout, gffn, wr2, ne, tm=ROW_TILE):
    t, d = x2.shape
    full = lambda a: pl.BlockSpec(a.shape, lambda i: (0,) * a.ndim, pipeline_mode=pl.Buffered(1))
    row = lambda a: pl.BlockSpec((tm, a.shape[1]), lambda i: (i, 0))
    return pl.pallas_call(
        functools.partial(_mixout_kernel, d=d, ne=ne),
        grid=(t // tm,),
        in_specs=[row(fre), pl.BlockSpec((ys.shape[0], tm, LANES), lambda i: (0, i, 0)), row(gate), row(x2),
                  full(wf), full(wglu), full(wout), full(gffn), full(wr2)],
        out_specs=[
            pl.BlockSpec((tm, d), lambda i: (i, 0)),
            pl.BlockSpec((tm, d), lambda i: (i, 0)),
            pl.BlockSpec((ne, tm), lambda i: (0, i)),
        ],
        out_shape=[
            jax.ShapeDtypeStruct((t, d), F32),
            jax.ShapeDtypeStruct((t, d), BF16),
            jax.ShapeDtypeStruct((ne, t), F32),
        ],
        compiler_params=_cparams("parallel"),
        name="mixout",
    )(fre, ys, gate, x2, wf, wglu, wout, gffn, wr2)


def _topk_kernel(aff_ref, pos_ref, st_ref, *, cap, blk):
    v = aff_ref[...]
    r, s = v.shape
    capf = float(cap)

    def bit_step(i, t):
        cand = t | (jnp.int32(1) << (30 - i))
        cnt = jnp.sum(jnp.where(v >= pltpu.bitcast(cand, F32), 1.0, 0.0), axis=1, keepdims=True)
        return jnp.where(cnt >= capf, cand, t)

    thr = lax.fori_loop(0, 31, bit_step, jnp.zeros((r, 1), I32))
    gt = jnp.where(v >= pltpu.bitcast(thr + 1, F32), 1.0, 0.0)
    eq = jnp.where(v >= pltpu.bitcast(thr, F32), 1.0, 0.0) - gt
    need = capf - jnp.sum(gt, axis=1, keepdims=True)
    ii = lax.broadcasted_iota(I32, (blk, blk), 0)
    jj = lax.broadcasted_iota(I32, (blk, blk), 1)
    tri = jnp.where(ii < jj, 1.0, 0.0).astype(BF16)
    run_eq = jnp.zeros((r, 1), F32)
    run = jnp.zeros((r, 1), F32)
    for k in range(s // blk):
        sl = slice(k * blk, (k + 1) * blk)
        eqb, gtb = eq[:, sl], gt[:, sl]
        rank_eq = _dot(eqb.astype(BF16), tri) + run_eq
        run_eq = run_eq + jnp.sum(eqb, axis=1, keepdims=True)
        mask = gtb + eqb * jnp.where(rank_eq < need, 1.0, 0.0)
        pos = _dot(mask.astype(BF16), tri) + run
        st_ref[:, k:k + 1] = run.astype(I32)
        run = run + jnp.sum(mask, axis=1, keepdims=True)
        pos_ref[:, sl] = jnp.where(mask > 0.0, pos, -1.0).astype(I32)
    st_ref[:, s // blk:s // blk + 1] = run.astype(I32)


def _topk(aff_rows, cap, blk=TOPK_BLOCK):
    r, s = aff_rows.shape
    nblk = s // blk
    return pl.pallas_call(
        functools.partial(_topk_kernel, cap=cap, blk=blk),
        grid=(1,),
        in_specs=[pl.BlockSpec((r, s), lambda i: (0, 0))],
        out_specs=[pl.BlockSpec((r, s), lambda i: (0, 0)), pl.BlockSpec((r, nblk + 1), lambda i: (0, 0))],
        out_shape=[jax.ShapeDtypeStruct((r, s), I32), jax.ShapeDtypeStruct((r, nblk + 1), I32)],
        compiler_params=_cparams("arbitrary"),
        name="topk",
    )(aff_rows)


def _floor_rows(x):
    return (x // PACKED_ROWS) * PACKED_ROWS


def _num_passes(starts, ends, win):
    need = ends[0] - _floor_rows(starts[0])
    for a, b in zip(starts[1:], ends[1:]):
        need = jnp.maximum(need, b - _floor_rows(a))
    return (need + win - 1) // win


def _pass_window(start, p, win, cap):
    first = _floor_rows(start) + p * win
    return first, pl.multiple_of(jnp.minimum(first, cap - win), PACKED_ROWS)


def _gather_kernel(st_ref, h_ref, pos_ref, aff_ref, x_ref, v_ref, *, nt, tt, win, cap, nbatch, ng):
    rows = [((pl.program_id(1) * ng + g) * nbatch + pl.program_id(0)) * (nt + 1) for g in range(ng)]
    x_ref[...] = jnp.zeros_like(x_ref)
    v_ref[...] = jnp.zeros_like(v_ref)
    riota = lax.broadcasted_iota(I32, (win, tt), 0)

    def bounds(t):
        return [st_ref[r + t] for r in rows], [st_ref[r + t + 1] for r in rows]

    def one_pass(t, p, starts):
        hrows = h_ref[0, pl.ds(pl.multiple_of(t * tt, tt), tt), :]
        hots, wins = [], []
        for g in range(ng):
            first, ws = _pass_window(starts[g], p, win, cap)
            pos = pos_ref[g, pl.ds(t, 1), :]
            hots.append((pos - ws == riota) & (pos >= first))
            wins.append(ws)
        onehot = jnp.concatenate([jnp.where(h, 1.0, 0.0).astype(BF16) for h in hots], axis=0)
        rows = _dot(onehot, hrows)
        for g in range(ng):
            x_ref[0, g, pl.ds(wins[g], win), :] += rows[g * win:(g + 1) * win].astype(BF16)
            vals = jnp.sum(jnp.where(hots[g], aff_ref[g, pl.ds(t, 1), :], 0.0), axis=1, keepdims=True)
            v_ref[0, g, pl.ds(wins[g], win), :] += vals

    def first_pass(t, most):
        starts, ends = bounds(t)
        one_pass(t, 0, starts)
        return jnp.maximum(most, _num_passes(starts, ends, win))

    def more_passes(t, carry):
        starts, ends = bounds(t)

        def body(p, c):
            one_pass(t, p, starts)
            return c

        lax.fori_loop(1, _num_passes(starts, ends, win), body, 0)
        return carry

    most = lax.fori_loop(0, nt, first_pass, jnp.int32(0), unroll=GATHER_UNROLL)

    @pl.when(most > 1)
    def _():
        lax.fori_loop(0, nt, more_passes, 0)


def _gather(starts, h3, pos3, aff3t, cap, ne):
    b, s, d = h3.shape
    tt, win, ng = GATHER_TILE, GATHER_WINDOW, GATHER_EXPERTS
    nt = s // tt
    grid_spec = pltpu.PrefetchScalarGridSpec(
        num_scalar_prefetch=1,
        grid=(b, ne // ng),
        in_specs=[
            pl.BlockSpec((1, s, d), lambda i, e, st: (i, 0, 0), pipeline_mode=pl.Buffered(1)),
            pl.BlockSpec((ng, nt, tt), lambda i, e, st: (e, i, 0)),
            pl.BlockSpec((ng, nt, tt), lambda i, e, st: (e, i, 0)),
        ],
        out_specs=[
            pl.BlockSpec((1, ng, cap, d), lambda i, e, st: (i, e, 0, 0)),
            pl.BlockSpec((1, ng, cap, 1), lambda i, e, st: (i, e, 0, 0)),
        ],
    )
    return pl.pallas_call(
        functools.partial(_gather_kernel, nt=nt, tt=tt, win=win, cap=cap, nbatch=b, ng=ng),
        grid_spec=grid_spec,
        out_shape=[jax.ShapeDtypeStruct((b, ne, cap, d), BF16), jax.ShapeDtypeStruct((b, ne, cap, 1), F32)],
        compiler_params=_cparams("arbitrary", "arbitrary"),
        name="gather",
    )(starts, h3, pos3, aff3t)


def _ffn_kernel(x_ref, v_ref, wg_ref, wu_ref, wd_ref, y_ref, acc_ref, *, nf):
    f = pl.program_id(2)
    nb, _, cap, d = x_ref.shape

    @pl.when(f == 0)
    def _():
        acc_ref[...] = jnp.zeros_like(acc_ref)

    wg = wg_ref[0].astype(BF16)
    wu = wu_ref[0].astype(BF16)
    wd = wd_ref[0].astype(BF16)
    mr = min(FFN_ROWS, cap)
    for r in range(nb * cap // mr):
        x = x_ref[r * mr // cap, 0, pl.ds(r * mr % cap, mr), :]
        g = _dot(x, wg)
        u = _dot(x, wu)
        hid = (g * jax.nn.sigmoid(g) * u).astype(BF16)
        acc_ref[pl.ds(r * mr, mr), :] += _dot(hid, wd)

    @pl.when(f == nf - 1)
    def _():
        y = acc_ref[...] * v_ref[...].reshape(nb * cap, 1)
        y_ref[...] = y.astype(BF16).reshape(nb, 1, cap, d)


def _ffn(xg, vals, wg, wu, wd, nb, tf):
    b, ne, cap, d = xg.shape
    dexp = wg.shape[2]
    nf = dexp // tf
    return pl.pallas_call(
        functools.partial(_ffn_kernel, nf=nf),
        grid=(ne, b // nb, nf),
        in_specs=[
            pl.BlockSpec((nb, 1, cap, d), lambda e, i, f: (i, e, 0, 0)),
            pl.BlockSpec((nb, 1, cap, 1), lambda e, i, f: (i, e, 0, 0)),
            pl.BlockSpec((1, d, tf), lambda e, i, f: (e, 0, f)),
            pl.BlockSpec((1, d, tf), lambda e, i, f: (e, 0, f)),
            pl.BlockSpec((1, tf, d), lambda e, i, f: (e, f, 0)),
        ],
        out_specs=pl.BlockSpec((nb, 1, cap, d), lambda e, i, f: (i, e, 0, 0)),
        out_shape=jax.ShapeDtypeStruct(xg.shape, BF16),
        scratch_shapes=[pltpu.VMEM((nb * cap, d), F32)],
        compiler_params=_cparams("parallel", "parallel", "arbitrary"),
        name="ffn",
    )(xg, vals, wg, wu, wd)


def _combine_kernel(st_ref, y_ref, pos_ref, spread_ref, lanes_ref, x1_ref, p_ref, wpp_ref, wpg_ref, gple_ref,
                    gout_ref, o_ref, *, nt, tt, win, cap, ne, nbatch, nsub, final):
    k = ne * win
    lane_e = lanes_ref[0:1, :]
    lane_r = lanes_ref[1:2, :]

    def tile_work(u):
        t = pl.program_id(1) * nsub + u
        rows = [(e * nbatch + pl.program_id(0)) * (nt + 1) + t for e in range(ne)]
        starts = [st_ref[r] for r in rows]
        ends = [st_ref[r + 1] for r in rows]
        pos = pos_ref[:, u * tt:(u + 1) * tt].astype(F32)
        pos = jnp.concatenate([pos, jnp.zeros((LANES - ne, tt), F32)], axis=0).T
        hi = jnp.floor(pos * (1.0 / COMBINE_SPLIT))
        lo = pos - hi * COMBINE_SPLIT
        rank = _dot(jnp.concatenate([hi, lo], axis=1).astype(BF16), spread_ref[...])

        def expert_rows(p):
            firsts = jnp.zeros((1, k), F32)
            offs = jnp.zeros((1, k), F32)
            wins = []
            for e in range(ne):
                first, ws = _pass_window(starts[e], p, win, cap)
                firsts = jnp.where(lane_e == e, first.astype(F32), firsts)
                offs = jnp.where(lane_e == e, ws.astype(F32), offs)
                wins.append(y_ref[0, e, pl.ds(ws, win), :])
            onehot = jnp.where((rank - offs == lane_r) & (rank >= firsts), 1.0, 0.0).astype(BF16)
            return _dot(onehot, jnp.concatenate(wins, axis=0))

        def finish(acc):
            for r in range(tt // LANES):
                rows = pl.ds(u * tt + r * LANES, LANES)
                x2 = acc[r * LANES:(r + 1) * LANES]
                emb = _dot(p_ref[0, rows, :].astype(BF16), wpp_ref[...])
                gate = jax.nn.sigmoid(_dot(_rms(x2, gple_ref[...]).astype(BF16), wpg_ref[...]))
                x3 = x2 + gate * emb
                o_ref[0, rows, :] = _rms(x3, gout_ref[...]) if final else x3

        x1 = x1_ref[0, pl.ds(u * tt, tt), :]
        finish(x1 + expert_rows(0))
        npass = _num_passes(starts, ends, win)
        redo = lambda: finish(lax.fori_loop(1, npass, lambda p, acc: acc + expert_rows(p), x1 + expert_rows(0)))
        return npass, redo

    for npass, redo in [tile_work(u) for u in range(nsub)]:
        pl.when(npass > 1)(redo)


def _combine(starts, yg, post, x13, p3, wpp, wpg, gple, gout, cap, final):
    b, ne, _, d = yg.shape
    s = x13.shape[1]
    tt, win, nsub = COMBINE_TILE, COMBINE_WINDOW, COMBINE_STEP_TILES
    nt = s // tt
    lane = jnp.arange(ne * win, dtype=I32)
    part = jnp.arange(2 * LANES, dtype=I32)[:, None]
    spread = jnp.where(part == lane // win, float(COMBINE_SPLIT), jnp.where(part == lane // win + LANES, 1.0, 0.0))
    spread = spread.astype(BF16)
    lanes = jnp.stack([lane // win, lane % win]).astype(F32)
    tile = lambda a: pl.BlockSpec((1, nsub * tt, a.shape[2]), lambda i, t, st: (i, t, 0))
    full = lambda a: pl.BlockSpec(a.shape, lambda i, t, st: (0,) * a.ndim, pipeline_mode=pl.Buffered(1))
    grid_spec = pltpu.PrefetchScalarGridSpec(
        num_scalar_prefetch=1,
        grid=(b, nt // nsub),
        in_specs=[
            pl.BlockSpec((1, ne, cap, d), lambda i, t, st: (i, 0, 0, 0), pipeline_mode=pl.Buffered(1)),
            pl.BlockSpec((ne, nsub * tt), lambda i, t, st: (0, i * (nt // nsub) + t)),
            full(spread), full(lanes), tile(x13), tile(p3), full(wpp), full(wpg), full(gple), full(gout),
        ],
        out_specs=pl.BlockSpec((1, nsub * tt, d), lambda i, t, st: (i, t, 0)),
    )
    return pl.pallas_call(
        functools.partial(_combine_kernel, nt=nt, tt=tt, win=win, cap=cap, ne=ne, nbatch=b, nsub=nsub, final=final),
        grid_spec=grid_spec,
        out_shape=jax.ShapeDtypeStruct(x13.shape, F32),
        compiler_params=_cparams("arbitrary", "arbitrary"),
        name="combine",
    )(starts, yg, post, spread, lanes, x13, p3, wpp, wpg, gple, gout)


def kernel(x, p, g_mix, w_in, w_fourier, ssm_a_re, ssm_a_im, ssm_log_dt, ssm_b_re, ssm_b_im, ssm_c_re, ssm_c_im, ssm_d, w_glu, w_out, g_ffn, w_router, w_exp_gate, w_exp_up, w_exp_down, g_ple, w_ple_gate, w_ple_proj, g_final):
    b, s, d = x.shape
    depth = p.shape[0]
    df = w_fourier.shape[1]
    ds = w_glu.shape[1]
    ne = w_router.shape[2]
    dexp = w_exp_gate.shape[3]
    groups, gi = ssm_b_re.shape[2], ssm_b_re.shape[4]
    L = SSM_CHUNK
    n1 = s // DFT_N2
    nc = s // L
    cap = EC_CAPACITY * s // ne
    assert s % (DFT_N2 * PACKED_ROWS) == 0 and s % GATHER_TILE == 0 and cap % PACKED_ROWS == 0
    assert nc % SSM_TILE_CHUNKS == 0 and gi == PACKED_ROWS and cap // COMBINE_SPLIT < 256
    assert 2 * ssm_a_re.shape[3] == LANES and ds % LANES == 0
    assert cap >= COMBINE_WINDOW and ne % GATHER_EXPERTS == 0 and GATHER_TILE == COMBINE_TILE
    assert ds == groups * gi and df % FOURIER_GROUPS == 0

    bd = _channel_dft_table(df)
    ck, sk = _dft1_tables(n1)
    tr, ti = _dft3_tables(n1)
    tf = 256 if dexp % 256 == 0 else dexp
    nb_ffn = 2 if b % 2 == 0 else 1

    xcur = x.reshape(b * s, d)
    for i in range(depth):
        final = i == depth - 1
        row = lambda v: v.astype(F32).reshape(1, -1)
        pq, us, gate = _inproj(xcur, row(g_mix[i]), w_in[i].astype(BF16), bd, df, ds)
        a4 = _dft1(pq.reshape(b, n1, DFT_N2, 2 * df), ck, sk)
        fre = _dft3(a4, tr, ti).reshape(b * s, df)

        tabs = _ssm_tables(ssm_a_re[i], ssm_a_im[i], ssm_log_dt[i], ssm_b_re[i], ssm_b_im[i],
                           ssm_c_re[i], ssm_c_im[i], ssm_d[i])
        ys = _to_tokens(_ssm(_to_groups(us, L, groups, gi), *tabs, nc=nc, nb=b), L, groups, gi)

        wr = w_router[i].astype(F32)
        wr_hi = wr.astype(BF16)
        wr_lo = (wr - wr_hi.astype(F32)).astype(BF16)
        lane_pad = lambda w: jnp.pad(w, ((0, 0), (0, LANES - ne)))
        wr2 = jnp.concatenate([jnp.concatenate([lane_pad(wr_hi), lane_pad(wr_lo)], axis=1),
                               jnp.concatenate([lane_pad(wr_hi), jnp.zeros((d, LANES), BF16)], axis=1)], axis=0)
        x1, h2, aff = _mixout(fre, ys, gate, xcur, w_fourier[i].astype(BF16), w_glu[i].astype(BF16),
                              w_out[i].astype(BF16), row(g_ffn[i]), wr2, ne)

        aff_rows = aff.reshape(ne * b, s)
        posm, st_blk = _topk(aff_rows, cap)
        starts = st_blk[:, ::GATHER_TILE // TOPK_BLOCK].reshape(-1)
        tiles = (ne, b * (s // GATHER_TILE), GATHER_TILE)
        xg, vals = _gather(starts, h2.reshape(b, s, d), posm.reshape(tiles), aff_rows.reshape(tiles), cap, ne)
        yg = _ffn(xg, vals, w_exp_gate[i], w_exp_up[i], w_exp_down[i], nb_ffn, tf)
        xnext = _combine(starts, yg, posm.reshape(ne, b * s), x1.reshape(b, s, d), p[i],
                         w_ple_proj[i].astype(BF16), w_ple_gate[i].astype(BF16), row(g_ple[i]),
                         row(g_final) if final else row(g_ple[i]), cap, final)
        xcur = xnext.reshape(b * s, d)
    return xcur.reshape(b, s, d)
```

```python
import functools
import math

import jax
import jax.numpy as jnp
from jax import lax
from jax.experimental import pallas as pl
from jax.experimental.pallas import tpu as pltpu

F32 = jnp.float32
BF16 = jnp.bfloat16
I32 = jnp.int32

RMS_EPS = 1e-6
FOURIER_GROUPS = 4
EC_CAPACITY = 2
DFT_N2 = 128
SUBLANES = 8
LANES = 128
PACKED_ROWS = 16
SSM_CHUNK = 32
ROW_TILE = 1024
INPROJ_ROWS = 256
SSM_TILE_CHUNKS = 128
TOPK_BLOCK = 128
GATHER_TILE = 256
MIX_ROWS = 256
FFN_ROWS = 1024
GATHER_WINDOW = 64
GATHER_EXPERTS = 4
GATHER_UNROLL = 8
COMBINE_TILE = 256
COMBINE_WINDOW = 64
COMBINE_STEP_TILES = 2
COMBINE_SPLIT = 32
VMEM_LIMIT = 56 * 1024 * 1024


def _cparams(*sem):
    return pltpu.CompilerParams(dimension_semantics=sem, vmem_limit_bytes=VMEM_LIMIT)


def _rms(x, g):
    return x * lax.rsqrt(jnp.mean(x * x, axis=-1, keepdims=True) + RMS_EPS) * g


def _dot(a, b):
    return jnp.dot(a, b, preferred_element_type=F32)


def _inproj_kernel(x_ref, g_ref, w_ref, bd_ref, pq_ref, us_ref, gate_ref, *, df, ds):
    for r in range(x_ref.shape[0] // INPROJ_ROWS):
        rows = pl.ds(r * INPROJ_ROWS, INPROJ_ROWS)
        h = _rms(x_ref[rows, :], g_ref[...]).astype(BF16)
        z = _dot(h, w_ref[...])
        pq_ref[rows, :] = _dot(z[:, :df].astype(BF16), bd_ref[...]).astype(BF16)
        for q in range(ds // LANES):
            us_ref[q, rows, :] = z[:, df + q * LANES:df + (q + 1) * LANES]
        gate_ref[rows, :] = jax.nn.sigmoid(z[:, df + ds:]).astype(BF16)


def _inproj(x2, g, w_in, bd, df, ds, tm=ROW_TILE):
    t, d = x2.shape
    dg = w_in.shape[1] - df - ds
    return pl.pallas_call(
        functools.partial(_inproj_kernel, df=df, ds=ds),
        grid=(t // tm,),
        in_specs=[
            pl.BlockSpec((tm, d), lambda i: (i, 0)),
            pl.BlockSpec((1, d), lambda i: (0, 0)),
            pl.BlockSpec(w_in.shape, lambda i: (0, 0), pipeline_mode=pl.Buffered(1)),
            pl.BlockSpec(bd.shape, lambda i: (0, 0), pipeline_mode=pl.Buffered(1)),
        ],
        out_specs=[
            pl.BlockSpec((tm, 2 * df), lambda i: (i, 0)),
            pl.BlockSpec((ds // LANES, tm, LANES), lambda i: (0, i, 0)),
            pl.BlockSpec((tm, dg), lambda i: (i, 0)),
        ],
        out_shape=[
            jax.ShapeDtypeStruct((t, 2 * df), BF16),
            jax.ShapeDtypeStruct((ds // LANES, t, LANES), F32),
            jax.ShapeDtypeStruct((t, dg), BF16),
        ],
        compiler_params=_cparams("parallel"),
        name="inproj",
    )(x2, g, w_in, bd)


def _channel_dft_table(df):
    c = df // FOURIER_GROUPS
    k = jnp.arange(c, dtype=I32)
    ang = (2.0 * math.pi / c) * ((k[:, None] * k[None, :]) % c).astype(F32)
    eye = jnp.eye(FOURIER_GROUPS, dtype=F32)
    scale = 1.0 / math.sqrt(c)
    re = jnp.kron(eye, jnp.cos(ang)) * scale
    im = -jnp.kron(eye, jnp.sin(ang)) * scale
    return jnp.concatenate([re, im], axis=1).astype(BF16)


def _dft1_tables(n1):
    k = jnp.arange(n1, dtype=I32)
    ang = (2.0 * math.pi / n1) * ((k[:, None] * k[None, :]) % n1).astype(F32)
    row = jnp.arange(n1 * SUBLANES, dtype=I32)
    rep = (row[:, None] // SUBLANES == k[None, :]).astype(F32)
    same_slot = (row[:, None] % SUBLANES == row[None, :] % SUBLANES).astype(F32) * (1.0 / math.sqrt(n1))
    hp = lax.Precision.HIGHEST
    kron8 = lambda t: (jnp.dot(jnp.dot(rep, t, precision=hp), rep.T, precision=hp) * same_slot).astype(BF16)
    return kron8(jnp.cos(ang)), kron8(jnp.sin(ang))


def _dft3_tables(n1):
    s = n1 * DFT_N2
    nb = n1 // SUBLANES
    k1 = jnp.arange(n1, dtype=I32)[:, None, None]
    k2 = jnp.arange(DFT_N2, dtype=I32)[None, :, None]
    n2 = jnp.arange(DFT_N2, dtype=I32)[None, None, :]
    ang = (2.0 * math.pi / s) * ((n2 * (k1 + n1 * k2)) % s).astype(F32)
    eye = jnp.eye(SUBLANES, dtype=BF16)[None, None, :, :, None]
    rows = DFT_N2 * SUBLANES

    def expand(t):
        t = (t * (1.0 / math.sqrt(DFT_N2))).astype(BF16).reshape(nb, SUBLANES, DFT_N2, DFT_N2)
        t = jnp.transpose(t, (0, 2, 1, 3))[:, :, :, None, :]
        return (t * eye).reshape(nb, rows, rows)

    return expand(jnp.cos(ang)), expand(jnp.sin(ang))


def _dft1_kernel(z_ref, ck_ref, sk_ref, a_ref):
    n1, slab, c2 = z_ref.shape[1], z_ref.shape[2], z_ref.shape[3]
    c = c2 // 2
    z = z_ref[0].astype(F32)
    halves = []
    for h in range(slab // SUBLANES):
        zh = z[:, h * SUBLANES:(h + 1) * SUBLANES, :].reshape(n1 * SUBLANES, c2).astype(BF16)
        cz = _dot(ck_ref[...], zh)
        sz = _dot(sk_ref[...], zh)
        a = jnp.concatenate([cz[:, :c] + sz[:, c:], cz[:, c:] - sz[:, :c]], axis=1)
        halves.append(a.reshape(n1, SUBLANES, c2))
    a_ref[0] = jnp.concatenate(halves, axis=1).astype(BF16)


def _dft1(pq4, ck, sk):
    b, n1, n2, c2 = pq4.shape
    slab = PACKED_ROWS
    return pl.pallas_call(
        _dft1_kernel,
        grid=(b, n2 // slab),
        in_specs=[
            pl.BlockSpec((1, n1, slab, c2), lambda i, j: (i, 0, j, 0)),
            pl.BlockSpec(ck.shape, lambda i, j: (0, 0)),
            pl.BlockSpec(sk.shape, lambda i, j: (0, 0)),
        ],
        out_specs=pl.BlockSpec((1, n1, slab, c2), lambda i, j: (i, 0, j, 0)),
        out_shape=jax.ShapeDtypeStruct(pq4.shape, BF16),
        compiler_params=_cparams("parallel", "parallel"),
        name="dft1",
    )(pq4, ck, sk)


def _dft3_kernel(a_ref, tr_ref, ti_ref, o_ref):
    nblk, c2 = tr_ref.shape[0], a_ref.shape[3]
    c = c2 // 2
    outs = []
    for h in range(nblk):
        a = a_ref[0, h * SUBLANES:(h + 1) * SUBLANES].reshape(SUBLANES * DFT_N2, c2)
        out = _dot(tr_ref[h], a[:, :c]) + _dot(ti_ref[h], a[:, c:])
        outs.append(out.reshape(DFT_N2, SUBLANES, c))
    o_ref[0] = jnp.concatenate(outs, axis=1).astype(BF16)


def _dft3(a4, tr, ti):
    b, n1, n2, c2 = a4.shape
    c = c2 // 2
    rows = DFT_N2 * SUBLANES
    nblk = PACKED_ROWS // SUBLANES
    return pl.pallas_call(
        _dft3_kernel,
        grid=(n1 // PACKED_ROWS, b),
        in_specs=[
            pl.BlockSpec((1, PACKED_ROWS, n2, c2), lambda k, i: (i, k, 0, 0)),
            pl.BlockSpec((nblk, rows, rows), lambda k, i: (k, 0, 0)),
            pl.BlockSpec((nblk, rows, rows), lambda k, i: (k, 0, 0)),
        ],
        out_specs=pl.BlockSpec((1, DFT_N2, PACKED_ROWS, c), lambda k, i: (i, 0, k, 0)),
        out_shape=jax.ShapeDtypeStruct((b, DFT_N2, n1, c), BF16),
        compiler_params=_cparams("parallel", "parallel"),
        name="dft3",
    )(a4, tr, ti)


def _ssm_tables(a_re, a_im, log_dt, b_re, b_im, c_re, c_im, d):
    L = SSM_CHUNK
    _, g, n = a_re.shape
    gi = b_re.shape[-1]
    dt = jnp.exp(log_dt.astype(F32))[..., None]
    ar, ai = a_re.astype(F32), a_im.astype(F32)
    tau = jnp.arange(L + 1, dtype=F32)[:, None, None, None]
    mag = jnp.exp(tau * (ar * dt)[None])
    ang = tau * (ai * dt)[None]
    pr, pi = mag * jnp.cos(ang), mag * jnp.sin(ang)
    nr, ni = pr[1] - 1.0, pi[1]
    den = ar * ar + ai * ai
    qr, qi = (nr * ar + ni * ai) / den, (ni * ar - nr * ai) / den
    br, bi = b_re.astype(F32), b_im.astype(F32)
    bbr = qr[..., None] * br - qi[..., None] * bi
    bbi = qr[..., None] * bi + qi[..., None] * br
    cr, ci = c_re.astype(F32), c_im.astype(F32)

    cat = jnp.concatenate
    steps = lambda p: jnp.transpose(p, (1, 0, 2))
    pf_r, pf_i, pb_r, pb_i = steps(pr[1:, 0]), steps(pi[1:, 0]), steps(pr[1:, 1][::-1]), steps(pi[1:, 1][::-1])
    f_small = jnp.stack([cat([cr[0], cr[1], cr[0], cr[1]], -1), cat([-ci[0], -ci[1], -ci[0], -ci[1]], -1)], 1)
    f_steps = jnp.stack([cat([pf_r, pb_r, -pf_i, -pb_i], -1), cat([pf_i, pb_i, pf_r, pb_r], -1)], 1)
    last = lambda p: jnp.transpose(p, (1, 2, 0))
    ef_r, ef_i, eb_r, eb_i = last(pr[:L, 0][::-1]), last(pi[:L, 0][::-1]), last(pr[:L, 1]), last(pi[:L, 1])
    e_small = jnp.stack([cat([bbr[0], bbr[1], bbi[0], bbi[1]], 1), cat([-bbi[0], -bbi[1], bbr[0], bbr[1]], 1)], 1)
    e_steps = jnp.stack([cat([ef_r, eb_r, ef_r, eb_r], 1), cat([ef_i, eb_i, ef_i, eb_i], 1)], 1)
    zpad = lambda p, before: jnp.pad(p, ((0, 0), (0, 0), (before, L - before)))
    wf_r, wf_i, wb_r, wb_i = zpad(ef_r, 0), zpad(ef_i, 0), zpad(eb_r, L - 1), zpad(eb_i, L - 1)
    t_small = jnp.stack([cat([bbr[0], bbi[0], bbr[1], bbi[1]], 1), cat([-bbi[0], bbr[0], -bbi[1], bbr[1]], 1)], 1)
    t_steps = jnp.stack([cat([wf_r, wf_r, wb_r, wb_r], 1), cat([wf_i, wf_i, wb_i, wb_i], 1)], 1)
    t_left = cat([cr[0], -ci[0], cr[1], -ci[1]], -1)

    al = jnp.stack([cat([pr[L, 0], pr[L, 1]], -1), cat([pi[L, 0], pi[L, 1]], -1)], axis=1)
    dv = jnp.tile(d.astype(F32).reshape(g, 1, gi), (1, L, 1)).reshape(g, L * gi, 1)
    t_tab, e_tab, f_tab = _ssm_table_call(f_small, f_steps, e_small, e_steps, t_small, t_steps, t_left, L, gi)
    return t_tab, e_tab, f_tab, al, dv


def _ssm_table_kernel(fs_ref, fp_ref, es_ref, ep_ref, ts_ref, tp_ref, tl_ref, ri_ref, rj_ref, rit_ref, rjt_ref,
                      ri2_ref, rs_ref, t_ref, e_ref, f_ref, *, L, gi):
    def split(a):
        hi = a.astype(BF16)
        return hi, (a - hi.astype(F32)).astype(BF16)

    def spread_lanes(a, rep):
        hi, lo = split(a)
        return _dot(hi, rep) + _dot(lo, rep)

    def spread_rows(rep, a):
        hi, lo = split(a)
        return _dot(rep, hi) + _dot(rep, lo)

    ri, rj, rit, rjt, ri2, rs = ri_ref[...], rj_ref[...], rit_ref[...], rjt_ref[...], ri2_ref[...], rs_ref[...]
    f_ref[0] = (spread_rows(rit, fs_ref[0, 0]) * spread_rows(rjt, fp_ref[0, 0])
                + spread_rows(rit, fs_ref[0, 1]) * spread_rows(rjt, fp_ref[0, 1])).astype(BF16)
    e_ref[0] = (spread_lanes(es_ref[0, 0], ri) * spread_lanes(ep_ref[0, 0], rj)
                + spread_lanes(es_ref[0, 1], ri) * spread_lanes(ep_ref[0, 1], rj)).astype(BF16)
    ew = (spread_lanes(ts_ref[0, 0], ri2) * spread_lanes(tp_ref[0, 0], rs)
          + spread_lanes(ts_ref[0, 1], ri2) * spread_lanes(tp_ref[0, 1], rs))
    c_hi, c_lo = split(tl_ref[0])
    e_hi, e_lo = split(ew)
    kw = _dot(c_hi, e_hi) + _dot(c_hi, e_lo) + _dot(c_lo, e_hi)
    per = LANES // gi
    for r in range(per):
        shifted = kw if r == 0 else pltpu.roll(kw, 2 * L * gi - r * gi, axis=1)
        for a in range(L // per):
            jo = L - 1 - (a * per + r)
            t_ref[0, pl.ds(jo * gi, gi), :] = shifted[:, a * LANES:a * LANES + L * gi].astype(BF16)


def _ssm_table_call(f_small, f_steps, e_small, e_steps, t_small, t_steps, t_left, L, gi):
    g = f_small.shape[0]
    n4 = f_small.shape[3]
    k = L * gi
    lane = jnp.arange(2 * k, dtype=I32)
    ri2 = (lane[None, :] % gi == jnp.arange(gi, dtype=I32)[:, None]).astype(BF16)
    rs = (lane[None, :] // gi == jnp.arange(2 * L, dtype=I32)[:, None]).astype(BF16)
    ri, rj = ri2[:, :k], rs[:L, :k]
    per_group = lambda a: pl.BlockSpec((1,) + a.shape[1:], lambda i: (i,) + (0,) * (a.ndim - 1))
    full = lambda a: pl.BlockSpec(a.shape, lambda i: (0,) * a.ndim)
    ins = (f_small, f_steps, e_small, e_steps, t_small, t_steps, t_left)
    reps = (ri, rj, ri.T, rj.T, ri2, rs)
    return pl.pallas_call(
        functools.partial(_ssm_table_kernel, L=L, gi=gi),
        grid=(g,),
        in_specs=[per_group(a) for a in ins] + [full(a) for a in reps],
        out_specs=[pl.BlockSpec((1, k, k), lambda i: (i, 0, 0)), pl.BlockSpec((1, n4, k), lambda i: (i, 0, 0)),
                   pl.BlockSpec((1, k, n4), lambda i: (i, 0, 0))],
        out_shape=[jax.ShapeDtypeStruct((g, k, k), BF16), jax.ShapeDtypeStruct((g, n4, k), BF16),
                   jax.ShapeDtypeStruct((g, k, n4), BF16)],
        compiler_params=_cparams("parallel"),
        name="ssm_tables",
    )(*ins, *reps)


def _to_groups_kernel(u_ref, a_ref, *, L, groups, gi):
    ncl = a_ref.shape[2]
    gq = LANES // gi
    for q in range(u_ref.shape[0]):
        for j in range(L):
            zt = u_ref[q, pl.ds(j, ncl, stride=L), :].T
            a_ref[q * gq:(q + 1) * gq, pl.ds(j * gi, gi), :] = zt.reshape(gq, gi, ncl).astype(BF16)


def _to_groups(us, L, groups, gi):
    nq, t, _ = us.shape
    ncl = SSM_TILE_CHUNKS
    return pl.pallas_call(
        functools.partial(_to_groups_kernel, L=L, groups=groups, gi=gi),
        grid=(t // (ncl * L),),
        in_specs=[pl.BlockSpec((nq, ncl * L, LANES), lambda i: (0, i, 0))],
        out_specs=pl.BlockSpec((groups, L * gi, ncl), lambda i: (0, 0, i)),
        out_shape=jax.ShapeDtypeStruct((groups, L * gi, t // L), BF16),
        compiler_params=_cparams("parallel"),
        name="to_groups",
    )(us)


def _to_tokens_kernel(y_ref, o_ref, *, L, groups, gi):
    ncl = y_ref.shape[2]
    gq = LANES // gi
    for q in range(o_ref.shape[0]):
        for j in range(L):
            yj = y_ref[q * gq:(q + 1) * gq, pl.ds(j * gi, gi), :].astype(F32).reshape(LANES, ncl)
            o_ref[q, pl.ds(j, ncl, stride=L), :] = yj.T


def _to_tokens(yt, L, groups, gi):
    _, k, nchunks = yt.shape
    ncl = SSM_TILE_CHUNKS
    return pl.pallas_call(
        functools.partial(_to_tokens_kernel, L=L, groups=groups, gi=gi),
        grid=(nchunks // ncl,),
        in_specs=[pl.BlockSpec((groups, k, ncl), lambda i: (0, 0, i))],
        out_specs=pl.BlockSpec((groups * gi // LANES, ncl * L, LANES), lambda i: (0, i, 0)),
        out_shape=jax.ShapeDtypeStruct((groups * gi // LANES, nchunks * L, LANES), F32),
        compiler_params=_cparams("parallel"),
        name="to_tokens",
    )(yt)


def _ssm_kernel(a_ref, t_ref, e_ref, f_ref, al_ref, dv_ref, y_ref, s_scr, h_scr, *, nc, nb, rows, n):
    a = a_ref[0]
    y1 = _dot(t_ref[0], a) + dv_ref[0] * a.astype(F32)
    st = _dot(e_ref[0], a)
    n2 = 2 * n
    s_scr[...] = jnp.zeros_like(s_scr)
    for b in range(nb):
        sb_t = st[:, b * nc:(b + 1) * nc].T
        for q in range(2):
            s_scr[q, pl.ds(b, nc, stride=rows), :] = sb_t[:, q * n2:(q + 1) * n2]
    ar = al_ref[0, 0:1, :]
    ai = al_ref[0, 1:2, :]
    is_fwd = lax.broadcasted_iota(I32, (rows, n2), 1) < n

    def step(i, carry):
        hr, hi = carry
        rf = pl.ds(pl.multiple_of(i * rows, rows), rows)
        rb = pl.ds(pl.multiple_of((nc - 1 - i) * rows, rows), rows)
        h_scr[0, rf, 0:n] = hr[:, 0:n]
        h_scr[1, rf, 0:n] = hi[:, 0:n]
        h_scr[0, rb, n:n2] = hr[:, n:n2]
        h_scr[1, rb, n:n2] = hi[:, n:n2]
        sr = jnp.where(is_fwd, s_scr[0, rf, :], s_scr[0, rb, :])
        si = jnp.where(is_fwd, s_scr[1, rf, :], s_scr[1, rb, :])
        return ar * hr - ai * hi + sr, ar * hi + ai * hr + si

    zero = jnp.zeros((rows, n2), F32)
    lax.fori_loop(0, nc, step, (zero, zero))
    ht = jnp.concatenate(
        [jnp.concatenate([h_scr[q, pl.ds(b, nc, stride=rows), :].T for q in range(2)], axis=0) for b in range(nb)],
        axis=1)
    y = y1 + _dot(f_ref[0], ht.astype(BF16))
    y_ref[0] = jax.nn.gelu(y, approximate=True).astype(BF16)


def _ssm(ag, t_tab, e_tab, f_tab, al, dv, nc, nb):
    g, k, m = ag.shape
    n4 = e_tab.shape[1]
    rows = -(-nb // SUBLANES) * SUBLANES
    return pl.pallas_call(
        functools.partial(_ssm_kernel, nc=nc, nb=nb, rows=rows, n=n4 // 4),
        grid=(g,),
        in_specs=[
            pl.BlockSpec((1, k, m), lambda i: (i, 0, 0)),
            pl.BlockSpec((1, k, k), lambda i: (i, 0, 0)),
            pl.BlockSpec((1, n4, k), lambda i: (i, 0, 0)),
            pl.BlockSpec((1, k, n4), lambda i: (i, 0, 0)),
            pl.BlockSpec((1, 2, n4 // 2), lambda i: (i, 0, 0)),
            pl.BlockSpec((1, k, 1), lambda i: (i, 0, 0)),
        ],
        out_specs=pl.BlockSpec((1, k, m), lambda i: (i, 0, 0)),
        out_shape=jax.ShapeDtypeStruct((g, k, m), BF16),
        scratch_shapes=[pltpu.VMEM((2, nc * rows, n4 // 2), F32), pltpu.VMEM((2, nc * rows, n4 // 2), F32)],
        compiler_params=_cparams("parallel"),
        name="ssm",
    )(ag, t_tab, e_tab, f_tab, al, dv)


def _mixout_kernel(fre_ref, ys_ref, gate_ref, x_ref, wf_ref, wglu_ref, wout_ref, gffn_ref, wr_ref,
                   x1_ref, h2_ref, aff_ref, *, d, ne):
    for r in range(fre_ref.shape[0] // MIX_ROWS):
        rows = pl.ds(r * MIX_ROWS, MIX_ROWS)
        y_f = _dot(fre_ref[rows, :], wf_ref[...])
        ys = jnp.concatenate([ys_ref[q, rows, :] for q in range(ys_ref.shape[0])], axis=1)
        vg = _dot(ys.astype(BF16), wglu_ref[...])
        y_s = vg[:, :d] * jax.nn.sigmoid(vg[:, d:])
        gate = gate_ref[rows, :].astype(F32)
        m = gate[:, :d] * y_f + gate[:, d:] * y_s
        x1 = x_ref[rows, :] + _dot(m.astype(BF16), wout_ref[...])
        x1_ref[rows, :] = x1
        h2 = _rms(x1, gffn_ref[...])
        h2_ref[rows, :] = h2.astype(BF16)
        hi = h2.astype(BF16)
        lo = (h2 - hi.astype(F32)).astype(BF16)
        rl = _dot(jnp.concatenate([hi, lo], axis=1), wr_ref[...])
        logits = rl[:, :LANES] + rl[:, LANES:]
        logits = jnp.where(lax.broadcasted_iota(I32, logits.shape, 1) < ne, logits, -1e30)
        logits = logits - jnp.max(logits, axis=-1, keepdims=True)
        p = jnp.exp(logits)
        aff = p / jnp.sum(p, axis=-1, keepdims=True)
        aff_ref[:, rows] = aff.T[:ne, :]


def _mixout(fre, ys, gate, x2, wf, wglu, wout, gffn, wr2, ne, tm=ROW_TILE):
    t, d = x2.shape
    full = lambda a: pl.BlockSpec(a.shape, lambda i: (0,) * a.ndim, pipeline_mode=pl.Buffered(1))
    row = lambda a: pl.BlockSpec((tm, a.shape[1]), lambda i: (i, 0))
    return pl.pallas_call(
        functools.partial(_mixout_kernel, d=d, ne=ne),
        grid=(t // tm,),
        in_specs=[row(fre), pl.BlockSpec((ys.shape[0], tm, LANES), lambda i: (0, i, 0)), row(gate), row(x2),
                  full(wf), full(wglu), full(wout), full(gffn), full(wr2)],
        out_specs=[
            pl.BlockSpec((tm, d), lambda i: (i, 0)),
            pl.BlockSpec((tm, d), lambda i: (i, 0)),
            pl.BlockSpec((ne, tm), lambda i: (0, i)),
        ],
        out_shape=[
            jax.ShapeDtypeStruct((t, d), F32),
            jax.ShapeDtypeStruct((t, d), BF16),
            jax.ShapeDtypeStruct((ne, t), F32),
        ],
        compiler_params=_cparams("parallel"),
        name="mixout",
    )(fre, ys, gate, x2, wf, wglu, wout, gffn, wr2)


def _topk_kernel(aff_ref, pos_ref, st_ref, *, cap, blk):
    v = aff_ref[...]
    r, s = v.shape
    capf = float(cap)

    def bit_step(i, t):
        cand = t | (jnp.int32(1) << (30 - i))
        cnt = jnp.sum(jnp.where(v >= pltpu.bitcast(cand, F32), 1.0, 0.0), axis=1, keepdims=True)
        return jnp.where(cnt >= capf, cand, t)

    thr = lax.fori_loop(0, 31, bit_step, jnp.zeros((r, 1), I32))
    gt = jnp.where(v >= pltpu.bitcast(thr + 1, F32), 1.0, 0.0)
    eq = jnp.where(v >= pltpu.bitcast(thr, F32), 1.0, 0.0) - gt
    need = capf - jnp.sum(gt, axis=1, keepdims=True)
    ii = lax.broadcasted_iota(I32, (blk, blk), 0)
    jj = lax.broadcasted_iota(I32, (blk, blk), 1)
    tri = jnp.where(ii < jj, 1.0, 0.0).astype(BF16)
    run_eq = jnp.zeros((r, 1), F32)
    run = jnp.zeros((r, 1), F32)
    for k in range(s // blk):
        sl = slice(k * blk, (k + 1) * blk)
        eqb, gtb = eq[:, sl], gt[:, sl]
        rank_eq = _dot(eqb.astype(BF16), tri) + run_eq
        run_eq = run_eq + jnp.sum(eqb, axis=1, keepdims=True)
        mask = gtb + eqb * jnp.where(rank_eq < need, 1.0, 0.0)
        pos = _dot(mask.astype(BF16), tri) + run
        st_ref[:, k:k + 1] = run.astype(I32)
        run = run + jnp.sum(mask, axis=1, keepdims=True)
        pos_ref[:, sl] = jnp.where(mask > 0.0, pos, -1.0).astype(I32)
    st_ref[:, s // blk:s // blk + 1] = run.astype(I32)


def _topk(aff_rows, cap, blk=TOPK_BLOCK):
    r, s = aff_rows.shape
    nblk = s // blk
    return pl.pallas_call(
        functools.partial(_topk_kernel, cap=cap, blk=blk),
        grid=(1,),
        in_specs=[pl.BlockSpec((r, s), lambda i: (0, 0))],
        out_specs=[pl.BlockSpec((r, s), lambda i: (0, 0)), pl.BlockSpec((r, nblk + 1), lambda i: (0, 0))],
        out_shape=[jax.ShapeDtypeStruct((r, s), I32), jax.ShapeDtypeStruct((r, nblk + 1), I32)],
        compiler_params=_cparams("arbitrary"),
        name="topk",
    )(aff_rows)


def _floor_rows(x):
    return (x // PACKED_ROWS) * PACKED_ROWS


def _num_passes(starts, ends, win):
    need = ends[0] - _floor_rows(starts[0])
    for a, b in zip(starts[1:], ends[1:]):
        need = jnp.maximum(need, b - _floor_rows(a))
    return (need + win - 1) // win


def _pass_window(start, p, win, cap):
    first = _floor_rows(start) + p * win
    return first, pl.multiple_of(jnp.minimum(first, cap - win), PACKED_ROWS)


def _gather_kernel(st_ref, h_ref, pos_ref, aff_ref, x_ref, v_ref, *, nt, tt, win, cap, nbatch, ng):
    rows = [((pl.program_id(1) * ng + g) * nbatch + pl.program_id(0)) * (nt + 1) for g in range(ng)]
    x_ref[...] = jnp.zeros_like(x_ref)
    v_ref[...] = jnp.zeros_like(v_ref)
    riota = lax.broadcasted_iota(I32, (win, tt), 0)

    def bounds(t):
        return [st_ref[r + t] for r in rows], [st_ref[r + t + 1] for r in rows]

    def one_pass(t, p, starts):
        hrows = h_ref[0, pl.ds(pl.multiple_of(t * tt, tt), tt), :]
        hots, wins = [], []
        for g in range(ng):
            first, ws = _pass_window(starts[g], p, win, cap)
            pos = pos_ref[g, pl.ds(t, 1), :]
            hots.append((pos - ws == riota) & (pos >= first))
            wins.append(ws)
        onehot = jnp.concatenate([jnp.where(h, 1.0, 0.0).astype(BF16) for h in hots], axis=0)
        rows = _dot(onehot, hrows)
        for g in range(ng):
            x_ref[0, g, pl.ds(wins[g], win), :] += rows[g * win:(g + 1) * win].astype(BF16)
            vals = jnp.sum(jnp.where(hots[g], aff_ref[g, pl.ds(t, 1), :], 0.0), axis=1, keepdims=True)
            v_ref[0, g, pl.ds(wins[g], win), :] += vals

    def first_pass(t, most):
        starts, ends = bounds(t)
        one_pass(t, 0, starts)
        return jnp.maximum(most, _num_passes(starts, ends, win))

    def more_passes(t, carry):
        starts, ends = bounds(t)

        def body(p, c):
            one_pass(t, p, starts)
            return c

        lax.fori_loop(1, _num_passes(starts, ends, win), body, 0)
        return carry

    most = lax.fori_loop(0, nt, first_pass, jnp.int32(0), unroll=GATHER_UNROLL)

    @pl.when(most > 1)
    def _():
        lax.fori_loop(0, nt, more_passes, 0)


def _gather(starts, h3, pos3, aff3t, cap, ne):
    b, s, d = h3.shape
    tt, win, ng = GATHER_TILE, GATHER_WINDOW, GATHER_EXPERTS
    nt = s // tt
    grid_spec = pltpu.PrefetchScalarGridSpec(
        num_scalar_prefetch=1,
        grid=(b, ne // ng),
        in_specs=[
            pl.BlockSpec((1, s, d), lambda i, e, st: (i, 0, 0), pipeline_mode=pl.Buffered(1)),
            pl.BlockSpec((ng, nt, tt), lambda i, e, st: (e, i, 0)),
            pl.BlockSpec((ng, nt, tt), lambda i, e, st: (e, i, 0)),
        ],
        out_specs=[
            pl.BlockSpec((1, ng, cap, d), lambda i, e, st: (i, e, 0, 0)),
            pl.BlockSpec((1, ng, cap, 1), lambda i, e, st: (i, e, 0, 0)),
        ],
    )
    return pl.pallas_call(
        functools.partial(_gather_kernel, nt=nt, tt=tt, win=win, cap=cap, nbatch=b, ng=ng),
        grid_spec=grid_spec,
        out_shape=[jax.ShapeDtypeStruct((b, ne, cap, d), BF16), jax.ShapeDtypeStruct((b, ne, cap, 1), F32)],
        compiler_params=_cparams("arbitrary", "arbitrary"),
        name="gather",
    )(starts, h3, pos3, aff3t)


def _ffn_kernel(x_ref, v_ref, wg_ref, wu_ref, wd_ref, y_ref, acc_ref, *, nf):
    f = pl.program_id(2)
    nb, _, cap, d = x_ref.shape

    @pl.when(f == 0)
    def _():
        acc_ref[...] = jnp.zeros_like(acc_ref)

    wg = wg_ref[0].astype(BF16)
    wu = wu_ref[0].astype(BF16)
    wd = wd_ref[0].astype(BF16)
    mr = min(FFN_ROWS, cap)
    for r in range(nb * cap // mr):
        x = x_ref[r * mr // cap, 0, pl.ds(r * mr % cap, mr), :]
        g = _dot(x, wg)
        u = _dot(x, wu)
        hid = (g * jax.nn.sigmoid(g) * u).astype(BF16)
        acc_ref[pl.ds(r * mr, mr), :] += _dot(hid, wd)

    @pl.when(f == nf - 1)
    def _():
        y = acc_ref[...] * v_ref[...].reshape(nb * cap, 1)
        y_ref[...] = y.astype(BF16).reshape(nb, 1, cap, d)


def _ffn(xg, vals, wg, wu, wd, nb, tf):
    b, ne, cap, d = xg.shape
    dexp = wg.shape[2]
    nf = dexp // tf
    return pl.pallas_call(
        functools.partial(_ffn_kernel, nf=nf),
        grid=(ne, b // nb, nf),
        in_specs=[
            pl.BlockSpec((nb, 1, cap, d), lambda e, i, f: (i, e, 0, 0)),
            pl.BlockSpec((nb, 1, cap, 1), lambda e, i, f: (i, e, 0, 0)),
            pl.BlockSpec((1, d, tf), lambda e, i, f: (e, 0, f)),
            pl.BlockSpec((1, d, tf), lambda e, i, f: (e, 0, f)),
            pl.BlockSpec((1, tf, d), lambda e, i, f: (e, f, 0)),
        ],
        out_specs=pl.BlockSpec((nb, 1, cap, d), lambda e, i, f: (i, e, 0, 0)),
        out_shape=jax.ShapeDtypeStruct(xg.shape, BF16),
        scratch_shapes=[pltpu.VMEM((nb * cap, d), F32)],
        compiler_params=_cparams("parallel", "parallel", "arbitrary"),
        name="ffn",
    )(xg, vals, wg, wu, wd)


def _combine_kernel(st_ref, y_ref, pos_ref, spread_ref, lanes_ref, x1_ref, p_ref, wpp_ref, wpg_ref, gple_ref,
                    gout_ref, o_ref, *, nt, tt, win, cap, ne, nbatch, nsub, final):
    k = ne * win
    lane_e = lanes_ref[0:1, :]
    lane_r = lanes_ref[1:2, :]

    def tile_work(u):
        t = pl.program_id(1) * nsub + u
        rows = [(e * nbatch + pl.program_id(0)) * (nt + 1) + t for e in range(ne)]
        starts = [st_ref[r] for r in rows]
        ends = [st_ref[r + 1] for r in rows]
        pos = pos_ref[:, u * tt:(u + 1) * tt].astype(F32)
        pos = jnp.concatenate([pos, jnp.zeros((LANES - ne, tt), F32)], axis=0).T
        hi = jnp.floor(pos * (1.0 / COMBINE_SPLIT))
        lo = pos - hi * COMBINE_SPLIT
        rank = _dot(jnp.concatenate([hi, lo], axis=1).astype(BF16), spread_ref[...])

        def expert_rows(p):
            firsts = jnp.zeros((1, k), F32)
            offs = jnp.zeros((1, k), F32)
            wins = []
            for e in range(ne):
                first, ws = _pass_window(starts[e], p, win, cap)
                firsts = jnp.where(lane_e == e, first.astype(F32), firsts)
                offs = jnp.where(lane_e == e, ws.astype(F32), offs)
                wins.append(y_ref[0, e, pl.ds(ws, win), :])
            onehot = jnp.where((rank - offs == lane_r) & (rank >= firsts), 1.0, 0.0).astype(BF16)
            return _dot(onehot, jnp.concatenate(wins, axis=0))

        def finish(acc):
            for r in range(tt // LANES):
                rows = pl.ds(u * tt + r * LANES, LANES)
                x2 = acc[r * LANES:(r + 1) * LANES]
                emb = _dot(p_ref[0, rows, :].astype(BF16), wpp_ref[...])
                gate = jax.nn.sigmoid(_dot(_rms(x2, gple_ref[...]).astype(BF16), wpg_ref[...]))
                x3 = x2 + gate * emb
                o_ref[0, rows, :] = _rms(x3, gout_ref[...]) if final else x3

        x1 = x1_ref[0, pl.ds(u * tt, tt), :]
        finish(x1 + expert_rows(0))
        npass = _num_passes(starts, ends, win)
        redo = lambda: finish(lax.fori_loop(1, npass, lambda p, acc: acc + expert_rows(p), x1 + expert_rows(0)))
        return npass, redo

    for npass, redo in [tile_work(u) for u in range(nsub)]:
        pl.when(npass > 1)(redo)


def _combine(starts, yg, post, x13, p3, wpp, wpg, gple, gout, cap, final):
    b, ne, _, d = yg.shape
    s = x13.shape[1]
    tt, win, nsub = COMBINE_TILE, COMBINE_WINDOW, COMBINE_STEP_TILES
    nt = s // tt
    lane = jnp.arange(ne * win, dtype=I32)
    part = jnp.arange(2 * LANES, dtype=I32)[:, None]
    spread = jnp.where(part == lane // win, float(COMBINE_SPLIT), jnp.where(part == lane // win + LANES, 1.0, 0.0))
    spread = spread.astype(BF16)
    lanes = jnp.stack([lane // win, lane % win]).astype(F32)
    tile = lambda a: pl.BlockSpec((1, nsub * tt, a.shape[2]), lambda i, t, st: (i, t, 0))
    full = lambda a: pl.BlockSpec(a.shape, lambda i, t, st: (0,) * a.ndim, pipeline_mode=pl.Buffered(1))
    grid_spec = pltpu.PrefetchScalarGridSpec(
        num_scalar_prefetch=1,
        grid=(b, nt // nsub),
        in_specs=[
            pl.BlockSpec((1, ne, cap, d), lambda i, t, st: (i, 0, 0, 0), pipeline_mode=pl.Buffered(1)),
            pl.BlockSpec((ne, nsub * tt), lambda i, t, st: (0, i * (nt // nsub) + t)),
            full(spread), full(lanes), tile(x13), tile(p3), full(wpp), full(wpg), full(gple), full(gout),
        ],
        out_specs=pl.BlockSpec((1, nsub * tt, d), lambda i, t, st: (i, t, 0)),
    )
    return pl.pallas_call(
        functools.partial(_combine_kernel, nt=nt, tt=tt, win=win, cap=cap, ne=ne, nbatch=b, nsub=nsub, final=final),
        grid_spec=grid_spec,
        out_shape=jax.ShapeDtypeStruct(x13.shape, F32),
        compiler_params=_cparams("arbitrary", "arbitrary"),
        name="combine",
    )(starts, yg, post, spread, lanes, x13, p3, wpp, wpg, gple, gout)


def kernel(x, p, g_mix, w_in, w_fourier, ssm_a_re, ssm_a_im, ssm_log_dt, ssm_b_re, ssm_b_im, ssm_c_re, ssm_c_im, ssm_d, w_glu, w_out, g_ffn, w_router, w_exp_gate, w_exp_up, w_exp_down, g_ple, w_ple_gate, w_ple_proj, g_final):
    b, s, d = x.shape
    depth = p.shape[0]
    df = w_fourier.shape[1]
    ds = w_glu.shape[1]
    ne = w_router.shape[2]
    dexp = w_exp_gate.shape[3]
    groups, gi = ssm_b_re.shape[2], ssm_b_re.shape[4]
    L = SSM_CHUNK
    n1 = s // DFT_N2
    nc = s // L
    cap = EC_CAPACITY * s // ne
    assert s % (DFT_N2 * PACKED_ROWS) == 0 and s % GATHER_TILE == 0 and cap % PACKED_ROWS == 0
    assert nc % SSM_TILE_CHUNKS == 0 and gi == PACKED_ROWS and cap // COMBINE_SPLIT < 256
    assert 2 * ssm_a_re.shape[3] == LANES and ds % LANES == 0
    assert cap >= COMBINE_WINDOW and ne % GATHER_EXPERTS == 0 and GATHER_TILE == COMBINE_TILE
    assert ds == groups * gi and df % FOURIER_GROUPS == 0

    bd = _channel_dft_table(df)
    ck, sk = _dft1_tables(n1)
    tr, ti = _dft3_tables(n1)
    tf = 256 if dexp % 256 == 0 else dexp
    nb_ffn = 2 if b % 2 == 0 else 1

    xcur = x.reshape(b * s, d)
    for i in range(depth):
        final = i == depth - 1
        row = lambda v: v.astype(F32).reshape(1, -1)
        pq, us, gate = _inproj(xcur, row(g_mix[i]), w_in[i].astype(BF16), bd, df, ds)
        a4 = _dft1(pq.reshape(b, n1, DFT_N2, 2 * df), ck, sk)
        fre = _dft3(a4, tr, ti).reshape(b * s, df)

        tabs = _ssm_tables(ssm_a_re[i], ssm_a_im[i], ssm_log_dt[i], ssm_b_re[i], ssm_b_im[i],
                           ssm_c_re[i], ssm_c_im[i], ssm_d[i])
        ys = _to_tokens(_ssm(_to_groups(us, L, groups, gi), *tabs, nc=nc, nb=b), L, groups, gi)

        wr = w_router[i].astype(F32)
        wr_hi = wr.astype(BF16)
        wr_lo = (wr - wr_hi.astype(F32)).astype(BF16)
        lane_pad = lambda w: jnp.pad(w, ((0, 0), (0, LANES - ne)))
        wr2 = jnp.concatenate([jnp.concatenate([lane_pad(wr_hi), lane_pad(wr_lo)], axis=1),
                               jnp.concatenate([lane_pad(wr_hi), jnp.zeros((d, LANES), BF16)], axis=1)], axis=0)
        x1, h2, aff = _mixout(fre, ys, gate, xcur, w_fourier[i].astype(BF16), w_glu[i].astype(BF16),
                              w_out[i].astype(BF16), row(g_ffn[i]), wr2, ne)

        aff_rows = aff.reshape(ne * b, s)
        posm, st_blk = _topk(aff_rows, cap)
        starts = st_blk[:, ::GATHER_TILE // TOPK_BLOCK].reshape(-1)
        tiles = (ne, b * (s // GATHER_TILE), GATHER_TILE)
        xg, vals = _gather(starts, h2.reshape(b, s, d), posm.reshape(tiles), aff_rows.reshape(tiles), cap, ne)
        yg = _ffn(xg, vals, w_exp_gate[i], w_exp_up[i], w_exp_down[i], nb_ffn, tf)
        xnext = _combine(starts, yg, posm.reshape(ne, b * s), x1.reshape(b, s, d), p[i],
                         w_ple_proj[i].astype(BF16), w_ple_gate[i].astype(BF16), row(g_ple[i]),
                         row(g_final) if final else row(g_ple[i]), cap, final)
        xcur = xnext.reshape(b * s, d)
    return xcur.reshape(b, s, d)
```

```python
import functools
import math

import jax
import jax.numpy as jnp
from jax import lax
from jax.experimental import pallas as pl
from jax.experimental.pallas import tpu as pltpu

F32 = jnp.float32
BF16 = jnp.bfloat16
I32 = jnp.int32

RMS_EPS = 1e-6
FOURIER_GROUPS = 4
EC_CAPACITY = 2
DFT_N2 = 128
SUBLANES = 8
LANES = 128
PACKED_ROWS = 16
SSM_CHUNK = 32
ROW_TILE = 1024
INPROJ_ROWS = 256
SSM_TILE_CHUNKS = 128
SCAN_UNROLL = 4
TOPK_BLOCK = 128
GATHER_TILE = 256
MIX_ROWS = 256
FFN_ROWS = 1024
GATHER_WINDOW = 64
GATHER_EXPERTS = 4
GATHER_UNROLL = 8
COMBINE_TILE = 256
COMBINE_WINDOW = 64
COMBINE_STEP_TILES = 2
COMBINE_SPLIT = 32
VMEM_LIMIT = 56 * 1024 * 1024


def _cparams(*sem):
    return pltpu.CompilerParams(dimension_semantics=sem, vmem_limit_bytes=VMEM_LIMIT)


def _rms(x, g):
    return x * lax.rsqrt(jnp.mean(x * x, axis=-1, keepdims=True) + RMS_EPS) * g


def _dot(a, b):
    return jnp.dot(a, b, preferred_element_type=F32)


def _inproj_kernel(x_ref, g_ref, w_ref, bd_ref, pq_ref, us_ref, gate_ref, *, df, ds):
    for r in range(x_ref.shape[0] // INPROJ_ROWS):
        rows = pl.ds(r * INPROJ_ROWS, INPROJ_ROWS)
        h = _rms(x_ref[rows, :], g_ref[...]).astype(BF16)
        z = _dot(h, w_ref[...])
        pq_ref[rows, :] = _dot(z[:, :df].astype(BF16), bd_ref[...]).astype(BF16)
        for q in range(ds // LANES):
            us_ref[q, rows, :] = z[:, df + q * LANES:df + (q + 1) * LANES]
        gate_ref[rows, :] = jax.nn.sigmoid(z[:, df + ds:]).astype(BF16)


def _inproj(x2, g, w_in, bd, df, ds, tm=ROW_TILE):
    t, d = x2.shape
    dg = w_in.shape[1] - df - ds
    return pl.pallas_call(
        functools.partial(_inproj_kernel, df=df, ds=ds),
        grid=(t // tm,),
        in_specs=[
            pl.BlockSpec((tm, d), lambda i: (i, 0)),
            pl.BlockSpec((1, d), lambda i: (0, 0)),
            pl.BlockSpec(w_in.shape, lambda i: (0, 0), pipeline_mode=pl.Buffered(1)),
            pl.BlockSpec(bd.shape, lambda i: (0, 0), pipeline_mode=pl.Buffered(1)),
        ],
        out_specs=[
            pl.BlockSpec((tm, 2 * df), lambda i: (i, 0)),
            pl.BlockSpec((ds // LANES, tm, LANES), lambda i: (0, i, 0)),
            pl.BlockSpec((tm, dg), lambda i: (i, 0)),
        ],
        out_shape=[
            jax.ShapeDtypeStruct((t, 2 * df), BF16),
            jax.ShapeDtypeStruct((ds // LANES, t, LANES), F32),
            jax.ShapeDtypeStruct((t, dg), BF16),
        ],
        compiler_params=_cparams("parallel"),
        name="inproj",
    )(x2, g, w_in, bd)


def _channel_dft_table(df):
    c = df // FOURIER_GROUPS
    k = jnp.arange(c, dtype=I32)
    ang = (2.0 * math.pi / c) * ((k[:, None] * k[None, :]) % c).astype(F32)
    eye = jnp.eye(FOURIER_GROUPS, dtype=F32)
    scale = 1.0 / math.sqrt(c)
    re = jnp.kron(eye, jnp.cos(ang)) * scale
    im = -jnp.kron(eye, jnp.sin(ang)) * scale
    return jnp.concatenate([re, im], axis=1).astype(BF16)


def _dft1_tables(n1):
    k = jnp.arange(n1, dtype=I32)
    ang = (2.0 * math.pi / n1) * ((k[:, None] * k[None, :]) % n1).astype(F32)
    row = jnp.arange(n1 * SUBLANES, dtype=I32)
    rep = (row[:, None] // SUBLANES == k[None, :]).astype(F32)
    same_slot = (row[:, None] % SUBLANES == row[None, :] % SUBLANES).astype(F32) * (1.0 / math.sqrt(n1))
    hp = lax.Precision.HIGHEST
    kron8 = lambda t: (jnp.dot(jnp.dot(rep, t, precision=hp), rep.T, precision=hp) * same_slot).astype(BF16)
    return kron8(jnp.cos(ang)), kron8(jnp.sin(ang))


def _dft3_tables(n1):
    s = n1 * DFT_N2
    nb = n1 // SUBLANES
    k1 = jnp.arange(n1, dtype=I32)[:, None, None]
    k2 = jnp.arange(DFT_N2, dtype=I32)[None, :, None]
    n2 = jnp.arange(DFT_N2, dtype=I32)[None, None, :]
    ang = (2.0 * math.pi / s) * ((n2 * (k1 + n1 * k2)) % s).astype(F32)
    eye = jnp.eye(SUBLANES, dtype=BF16)[None, None, :, :, None]
    rows = DFT_N2 * SUBLANES

    def expand(t):
        t = (t * (1.0 / math.sqrt(DFT_N2))).astype(BF16).reshape(nb, SUBLANES, DFT_N2, DFT_N2)
        t = jnp.transpose(t, (0, 2, 1, 3))[:, :, :, None, :]
        return (t * eye).reshape(nb, rows, rows)

    return expand(jnp.cos(ang)), expand(jnp.sin(ang))


def _dft1_kernel(z_ref, ck_ref, sk_ref, a_ref):
    n1, slab, c2 = z_ref.shape[1], z_ref.shape[2], z_ref.shape[3]
    c = c2 // 2
    z = z_ref[0].astype(F32)
    halves = []
    for h in range(slab // SUBLANES):
        zh = z[:, h * SUBLANES:(h + 1) * SUBLANES, :].reshape(n1 * SUBLANES, c2).astype(BF16)
        cz = _dot(ck_ref[...], zh)
        sz = _dot(sk_ref[...], zh)
        a = jnp.concatenate([cz[:, :c] + sz[:, c:], cz[:, c:] - sz[:, :c]], axis=1)
        halves.append(a.reshape(n1, SUBLANES, c2))
    a_ref[0] = jnp.concatenate(halves, axis=1).astype(BF16)


def _dft1(pq4, ck, sk):
    b, n1, n2, c2 = pq4.shape
    slab = PACKED_ROWS
    return pl.pallas_call(
        _dft1_kernel,
        grid=(b, n2 // slab),
        in_specs=[
            pl.BlockSpec((1, n1, slab, c2), lambda i, j: (i, 0, j, 0)),
            pl.BlockSpec(ck.shape, lambda i, j: (0, 0)),
            pl.BlockSpec(sk.shape, lambda i, j: (0, 0)),
        ],
        out_specs=pl.BlockSpec((1, n1, slab, c2), lambda i, j: (i, 0, j, 0)),
        out_shape=jax.ShapeDtypeStruct(pq4.shape, BF16),
        compiler_params=_cparams("parallel", "parallel"),
        name="dft1",
    )(pq4, ck, sk)


def _dft3_kernel(a_ref, tr_ref, ti_ref, o_ref):
    nblk, c2 = tr_ref.shape[0], a_ref.shape[3]
    c = c2 // 2
    outs = []
    for h in range(nblk):
        a = a_ref[0, h * SUBLANES:(h + 1) * SUBLANES].reshape(SUBLANES * DFT_N2, c2)
        out = _dot(tr_ref[h], a[:, :c]) + _dot(ti_ref[h], a[:, c:])
        outs.append(out.reshape(DFT_N2, SUBLANES, c))
    o_ref[0] = jnp.concatenate(outs, axis=1).astype(BF16)


def _dft3(a4, tr, ti):
    b, n1, n2, c2 = a4.shape
    c = c2 // 2
    rows = DFT_N2 * SUBLANES
    nblk = PACKED_ROWS // SUBLANES
    return pl.pallas_call(
        _dft3_kernel,
        grid=(n1 // PACKED_ROWS, b),
        in_specs=[
            pl.BlockSpec((1, PACKED_ROWS, n2, c2), lambda k, i: (i, k, 0, 0)),
            pl.BlockSpec((nblk, rows, rows), lambda k, i: (k, 0, 0)),
            pl.BlockSpec((nblk, rows, rows), lambda k, i: (k, 0, 0)),
        ],
        out_specs=pl.BlockSpec((1, DFT_N2, PACKED_ROWS, c), lambda k, i: (i, 0, k, 0)),
        out_shape=jax.ShapeDtypeStruct((b, DFT_N2, n1, c), BF16),
        compiler_params=_cparams("parallel", "parallel"),
        name="dft3",
    )(a4, tr, ti)


def _ssm_tables(a_re, a_im, log_dt, b_re, b_im, c_re, c_im, d):
    L = SSM_CHUNK
    _, g, n = a_re.shape
    gi = b_re.shape[-1]
    dt = jnp.exp(log_dt.astype(F32))[..., None]
    ar, ai = a_re.astype(F32), a_im.astype(F32)
    tau = jnp.arange(L + 1, dtype=F32)[:, None, None, None]
    mag = jnp.exp(tau * (ar * dt)[None])
    ang = tau * (ai * dt)[None]
    pr, pi = mag * jnp.cos(ang), mag * jnp.sin(ang)
    nr, ni = pr[1] - 1.0, pi[1]
    den = ar * ar + ai * ai
    qr, qi = (nr * ar + ni * ai) / den, (ni * ar - nr * ai) / den
    br, bi = b_re.astype(F32), b_im.astype(F32)
    bbr = qr[..., None] * br - qi[..., None] * bi
    bbi = qr[..., None] * bi + qi[..., None] * br
    cr, ci = c_re.astype(F32), c_im.astype(F32)

    cat = jnp.concatenate
    steps = lambda p: jnp.transpose(p, (1, 0, 2))
    pf_r, pf_i, pb_r, pb_i = steps(pr[1:, 0]), steps(pi[1:, 0]), steps(pr[1:, 1][::-1]), steps(pi[1:, 1][::-1])
    f_small = jnp.stack([cat([cr[0], cr[1], cr[0], cr[1]], -1), cat([-ci[0], -ci[1], -ci[0], -ci[1]], -1)], 1)
    f_steps = jnp.stack([cat([pf_r, pb_r, -pf_i, -pb_i], -1), cat([pf_i, pb_i, pf_r, pb_r], -1)], 1)
    last = lambda p: jnp.transpose(p, (1, 2, 0))
    ef_r, ef_i, eb_r, eb_i = last(pr[:L, 0][::-1]), last(pi[:L, 0][::-1]), last(pr[:L, 1]), last(pi[:L, 1])
    e_small = jnp.stack([cat([bbr[0], bbr[1], bbi[0], bbi[1]], 1), cat([-bbi[0], -bbi[1], bbr[0], bbr[1]], 1)], 1)
    e_steps = jnp.stack([cat([ef_r, eb_r, ef_r, eb_r], 1), cat([ef_i, eb_i, ef_i, eb_i], 1)], 1)
    zpad = lambda p, before: jnp.pad(p, ((0, 0), (0, 0), (before, L - before)))
    wf_r, wf_i, wb_r, wb_i = zpad(ef_r, 0), zpad(ef_i, 0), zpad(eb_r, L - 1), zpad(eb_i, L - 1)
    t_small = jnp.stack([cat([bbr[0], bbi[0], bbr[1], bbi[1]], 1), cat([-bbi[0], bbr[0], -bbi[1], bbr[1]], 1)], 1)
    t_steps = jnp.stack([cat([wf_r, wf_r, wb_r, wb_r], 1), cat([wf_i, wf_i, wb_i, wb_i], 1)], 1)
    t_left = cat([cr[0], -ci[0], cr[1], -ci[1]], -1)

    al = jnp.stack([cat([pr[L, 0], pr[L, 1]], -1), cat([pi[L, 0], pi[L, 1]], -1)], axis=1)
    dv = jnp.tile(d.astype(F32).reshape(g, 1, gi), (1, L, 1)).reshape(g, L * gi, 1)
    t_tab, e_tab, f_tab = _ssm_table_call(f_small, f_steps, e_small, e_steps, t_small, t_steps, t_left, L, gi)
    return t_tab, e_tab, f_tab, al, dv


def _ssm_table_kernel(fs_ref, fp_ref, es_ref, ep_ref, ts_ref, tp_ref, tl_ref, ri_ref, rj_ref, rit_ref, rjt_ref,
                      ri2_ref, rs_ref, t_ref, e_ref, f_ref, *, L, gi):
    def split(a):
        hi = a.astype(BF16)
        return hi, (a - hi.astype(F32)).astype(BF16)

    def spread_lanes(a, rep):
        hi, lo = split(a)
        return _dot(hi, rep) + _dot(lo, rep)

    def spread_rows(rep, a):
        hi, lo = split(a)
        return _dot(rep, hi) + _dot(rep, lo)

    ri, rj, rit, rjt, ri2, rs = ri_ref[...], rj_ref[...], rit_ref[...], rjt_ref[...], ri2_ref[...], rs_ref[...]
    f_ref[0] = (spread_rows(rit, fs_ref[0, 0]) * spread_rows(rjt, fp_ref[0, 0])
                + spread_rows(rit, fs_ref[0, 1]) * spread_rows(rjt, fp_ref[0, 1])).astype(BF16)
    e_ref[0] = (spread_lanes(es_ref[0, 0], ri) * spread_lanes(ep_ref[0, 0], rj)
                + spread_lanes(es_ref[0, 1], ri) * spread_lanes(ep_ref[0, 1], rj)).astype(BF16)
    ew = (spread_lanes(ts_ref[0, 0], ri2) * spread_lanes(tp_ref[0, 0], rs)
          + spread_lanes(ts_ref[0, 1], ri2) * spread_lanes(tp_ref[0, 1], rs))
    c_hi, c_lo = split(tl_ref[0])
    e_hi, e_lo = split(ew)
    kw = _dot(c_hi, e_hi) + _dot(c_hi, e_lo) + _dot(c_lo, e_hi)
    per = LANES // gi
    for r in range(per):
        shifted = kw if r == 0 else pltpu.roll(kw, 2 * L * gi - r * gi, axis=1)
        for a in range(L // per):
            jo = L - 1 - (a * per + r)
            t_ref[0, pl.ds(jo * gi, gi), :] = shifted[:, a * LANES:a * LANES + L * gi].astype(BF16)


def _ssm_table_call(f_small, f_steps, e_small, e_steps, t_small, t_steps, t_left, L, gi):
    g = f_small.shape[0]
    n4 = f_small.shape[3]
    k = L * gi
    lane = jnp.arange(2 * k, dtype=I32)
    ri2 = (lane[None, :] % gi == jnp.arange(gi, dtype=I32)[:, None]).astype(BF16)
    rs = (lane[None, :] // gi == jnp.arange(2 * L, dtype=I32)[:, None]).astype(BF16)
    ri, rj = ri2[:, :k], rs[:L, :k]
    per_group = lambda a: pl.BlockSpec((1,) + a.shape[1:], lambda i: (i,) + (0,) * (a.ndim - 1))
    full = lambda a: pl.BlockSpec(a.shape, lambda i: (0,) * a.ndim)
    ins = (f_small, f_steps, e_small, e_steps, t_small, t_steps, t_left)
    reps = (ri, rj, ri.T, rj.T, ri2, rs)
    return pl.pallas_call(
        functools.partial(_ssm_table_kernel, L=L, gi=gi),
        grid=(g,),
        in_specs=[per_group(a) for a in ins] + [full(a) for a in reps],
        out_specs=[pl.BlockSpec((1, k, k), lambda i: (i, 0, 0)), pl.BlockSpec((1, n4, k), lambda i: (i, 0, 0)),
                   pl.BlockSpec((1, k, n4), lambda i: (i, 0, 0))],
        out_shape=[jax.ShapeDtypeStruct((g, k, k), BF16), jax.ShapeDtypeStruct((g, n4, k), BF16),
                   jax.ShapeDtypeStruct((g, k, n4), BF16)],
        compiler_params=_cparams("parallel"),
        name="ssm_tables",
    )(*ins, *reps)


def _to_groups_kernel(u_ref, a_ref, *, L, groups, gi):
    ncl = a_ref.shape[2]
    gq = LANES // gi
    for q in range(u_ref.shape[0]):
        for j in range(L):
            zt = u_ref[q, pl.ds(j, ncl, stride=L), :].T
            a_ref[q * gq:(q + 1) * gq, pl.ds(j * gi, gi), :] = zt.reshape(gq, gi, ncl).astype(BF16)


def _to_groups(us, L, groups, gi):
    nq, t, _ = us.shape
    ncl = SSM_TILE_CHUNKS
    return pl.pallas_call(
        functools.partial(_to_groups_kernel, L=L, groups=groups, gi=gi),
        grid=(t // (ncl * L),),
        in_specs=[pl.BlockSpec((nq, ncl * L, LANES), lambda i: (0, i, 0))],
        out_specs=pl.BlockSpec((groups, L * gi, ncl), lambda i: (0, 0, i)),
        out_shape=jax.ShapeDtypeStruct((groups, L * gi, t // L), BF16),
        compiler_params=_cparams("parallel"),
        name="to_groups",
    )(us)


def _to_tokens_kernel(y_ref, o_ref, *, L, groups, gi):
    ncl = y_ref.shape[2]
    gq = LANES // gi
    for q in range(o_ref.shape[0]):
        for j in range(L):
            yj = y_ref[q * gq:(q + 1) * gq, pl.ds(j * gi, gi), :].astype(F32).reshape(LANES, ncl)
            o_ref[q, pl.ds(j, ncl, stride=L), :] = yj.T


def _to_tokens(yt, L, groups, gi):
    _, k, nchunks = yt.shape
    ncl = SSM_TILE_CHUNKS
    return pl.pallas_call(
        functools.partial(_to_tokens_kernel, L=L, groups=groups, gi=gi),
        grid=(nchunks // ncl,),
        in_specs=[pl.BlockSpec((groups, k, ncl), lambda i: (0, 0, i))],
        out_specs=pl.BlockSpec((groups * gi // LANES, ncl * L, LANES), lambda i: (0, i, 0)),
        out_shape=jax.ShapeDtypeStruct((groups * gi // LANES, nchunks * L, LANES), F32),
        compiler_params=_cparams("parallel"),
        name="to_tokens",
    )(yt)


def _ssm_kernel(a_ref, t_ref, e_ref, f_ref, al_ref, dv_ref, y_ref, s_scr, h_scr, *, nc, nb, rows, n):
    a = a_ref[0]
    y1 = _dot(t_ref[0], a) + dv_ref[0] * a.astype(F32)
    st = _dot(e_ref[0], a)
    n2 = 2 * n
    s_scr[...] = jnp.zeros_like(s_scr)
    for b in range(nb):
        sb_t = st[:, b * nc:(b + 1) * nc].T
        for q in range(2):
            s_scr[q, pl.ds(b, nc, stride=rows), :] = sb_t[:, q * n2:(q + 1) * n2]
    ar = al_ref[0, 0:1, :]
    ai = al_ref[0, 1:2, :]
    is_fwd = lax.broadcasted_iota(I32, (rows, n2), 1) < n

    def step(i, carry):
        hr, hi = carry
        rf = pl.ds(pl.multiple_of(i * rows, rows), rows)
        rb = pl.ds(pl.multiple_of((nc - 1 - i) * rows, rows), rows)
        h_scr[0, rf, 0:n] = hr[:, 0:n]
        h_scr[1, rf, 0:n] = hi[:, 0:n]
        h_scr[0, rb, n:n2] = hr[:, n:n2]
        h_scr[1, rb, n:n2] = hi[:, n:n2]
        sr = jnp.where(is_fwd, s_scr[0, rf, :], s_scr[0, rb, :])
        si = jnp.where(is_fwd, s_scr[1, rf, :], s_scr[1, rb, :])
        return ar * hr - ai * hi + sr, ar * hi + ai * hr + si

    zero = jnp.zeros((rows, n2), F32)
    lax.fori_loop(0, nc, step, (zero, zero), unroll=SCAN_UNROLL)
    ht = jnp.concatenate(
        [jnp.concatenate([h_scr[q, pl.ds(b, nc, stride=rows), :].T for q in range(2)], axis=0) for b in range(nb)],
        axis=1)
    y = y1 + _dot(f_ref[0], ht.astype(BF16))
    y_ref[0] = jax.nn.gelu(y, approximate=True).astype(BF16)


def _ssm(ag, t_tab, e_tab, f_tab, al, dv, nc, nb):
    g, k, m = ag.shape
    n4 = e_tab.shape[1]
    rows = -(-nb // SUBLANES) * SUBLANES
    return pl.pallas_call(
        functools.partial(_ssm_kernel, nc=nc, nb=nb, rows=rows, n=n4 // 4),
        grid=(g,),
        in_specs=[
            pl.BlockSpec((1, k, m), lambda i: (i, 0, 0)),
            pl.BlockSpec((1, k, k), lambda i: (i, 0, 0)),
            pl.BlockSpec((1, n4, k), lambda i: (i, 0, 0)),
            pl.BlockSpec((1, k, n4), lambda i: (i, 0, 0)),
            pl.BlockSpec((1, 2, n4 // 2), lambda i: (i, 0, 0)),
            pl.BlockSpec((1, k, 1), lambda i: (i, 0, 0)),
        ],
        out_specs=pl.BlockSpec((1, k, m), lambda i: (i, 0, 0)),
        out_shape=jax.ShapeDtypeStruct((g, k, m), BF16),
        scratch_shapes=[pltpu.VMEM((2, nc * rows, n4 // 2), F32), pltpu.VMEM((2, nc * rows, n4 // 2), F32)],
        compiler_params=_cparams("parallel"),
        name="ssm",
    )(ag, t_tab, e_tab, f_tab, al, dv)


def _mixout_kernel(fre_ref, ys_ref, gate_ref, x_ref, wf_ref, wglu_ref, wout_ref, gffn_ref, wr_ref,
                   x1_ref, h2_ref, aff_ref, *, d, ne):
    for r in range(fre_ref.shape[0] // MIX_ROWS):
        rows = pl.ds(r * MIX_ROWS, MIX_ROWS)
        y_f = _dot(fre_ref[rows, :], wf_ref[...])
        ys = jnp.concatenate([ys_ref[q, rows, :] for q in range(ys_ref.shape[0])], axis=1)
        vg = _dot(ys.astype(BF16), wglu_ref[...])
        y_s = vg[:, :d] * jax.nn.sigmoid(vg[:, d:])
        gate = gate_ref[rows, :].astype(F32)
        m = gate[:, :d] * y_f + gate[:, d:] * y_s
        x1 = x_ref[rows, :] + _dot(m.astype(BF16), wout_ref[...])
        x1_ref[rows, :] = x1
        h2 = _rms(x1, gffn_ref[...])
        h2_ref[rows, :] = h2.astype(BF16)
        hi = h2.astype(BF16)
        lo = (h2 - hi.astype(F32)).astype(BF16)
        rl = _dot(jnp.concatenate([hi, lo], axis=1), wr_ref[...])
        logits = rl[:, :LANES] + rl[:, LANES:]
        logits = jnp.where(lax.broadcasted_iota(I32, logits.shape, 1) < ne, logits, -1e30)
        logits = logits - jnp.max(logits, axis=-1, keepdims=True)
        p = jnp.exp(logits)
        aff = p / jnp.sum(p, axis=-1, keepdims=True)
        aff_ref[:, rows] = aff.T[:ne, :]


def _mixout(fre, ys, gate, x2, wf, wglu, wout, gffn, wr2, ne, tm=ROW_TILE):
    t, d = x2.shape
    full = lambda a: pl.BlockSpec(a.shape, lambda i: (0,) * a.ndim, pipeline_mode=pl.Buffered(1))
    row = lambda a: pl.BlockSpec((tm, a.shape[1]), lambda i: (i, 0))
    return pl.pallas_call(
        functools.partial(_mixout_kernel, d=d, ne=ne),
        grid=(t // tm,),
        in_specs=[row(fre), pl.BlockSpec((ys.shape[0], tm, LANES), lambda i: (0, i, 0)), row(gate), row(x2),
                  full(wf), full(wglu), full(wout), full(gffn), full(wr2)],
        out_specs=[
            pl.BlockSpec((tm, d), lambda i: (i, 0)),
            pl.BlockSpec((tm, d), lambda i: (i, 0)),
            pl.BlockSpec((ne, tm), lambda i: (0, i)),
        ],
        out_shape=[
            jax.ShapeDtypeStruct((t, d), F32),
            jax.ShapeDtypeStruct((t, d), BF16),
            jax.ShapeDtypeStruct((ne, t), F32),
        ],
        compiler_params=_cparams("parallel"),
        name="mixout",
    )(fre, ys, gate, x2, wf, wglu, wout, gffn, wr2)


def _topk_kernel(aff_ref, pos_ref, st_ref, *, cap, blk):
    v = aff_ref[...]
    r, s = v.shape
    capf = float(cap)

    def bit_step(i, t):
        cand = t | (jnp.int32(1) << (30 - i))
        cnt = jnp.sum(jnp.where(v >= pltpu.bitcast(cand, F32), 1.0, 0.0), axis=1, keepdims=True)
        return jnp.where(cnt >= capf, cand, t)

    thr = lax.fori_loop(0, 31, bit_step, jnp.zeros((r, 1), I32))
    gt = jnp.where(v >= pltpu.bitcast(thr + 1, F32), 1.0, 0.0)
    eq = jnp.where(v >= pltpu.bitcast(thr, F32), 1.0, 0.0) - gt
    need = capf - jnp.sum(gt, axis=1, keepdims=True)
    ii = lax.broadcasted_iota(I32, (blk, blk), 0)
    jj = lax.broadcasted_iota(I32, (blk, blk), 1)
    tri = jnp.where(ii < jj, 1.0, 0.0).astype(BF16)
    run_eq = jnp.zeros((r, 1), F32)
    run = jnp.zeros((r, 1), F32)
    for k in range(s // blk):
        sl = slice(k * blk, (k + 1) * blk)
        eqb, gtb = eq[:, sl], gt[:, sl]
        rank_eq = _dot(eqb.astype(BF16), tri) + run_eq
        run_eq = run_eq + jnp.sum(eqb, axis=1, keepdims=True)
        mask = gtb + eqb * jnp.where(rank_eq < need, 1.0, 0.0)
        pos = _dot(mask.astype(BF16), tri) + run
        st_ref[:, k:k + 1] = run.astype(I32)
        run = run + jnp.sum(mask, axis=1, keepdims=True)
        pos_ref[:, sl] = jnp.where(mask > 0.0, pos, -1.0).astype(I32)
    st_ref[:, s // blk:s // blk + 1] = run.astype(I32)


def _topk(aff_rows, cap, blk=TOPK_BLOCK):
    r, s = aff_rows.shape
    nblk = s // blk
    return pl.pallas_call(
        functools.partial(_topk_kernel, cap=cap, blk=blk),
        grid=(1,),
        in_specs=[pl.BlockSpec((r, s), lambda i: (0, 0))],
        out_specs=[pl.BlockSpec((r, s), lambda i: (0, 0)), pl.BlockSpec((r, nblk + 1), lambda i: (0, 0))],
        out_shape=[jax.ShapeDtypeStruct((r, s), I32), jax.ShapeDtypeStruct((r, nblk + 1), I32)],
        compiler_params=_cparams("arbitrary"),
        name="topk",
    )(aff_rows)


def _floor_rows(x):
    return (x // PACKED_ROWS) * PACKED_ROWS


def _num_passes(starts, ends, win):
    need = ends[0] - _floor_rows(starts[0])
    for a, b in zip(starts[1:], ends[1:]):
        need = jnp.maximum(need, b - _floor_rows(a))
    return (need + win - 1) // win


def _pass_window(start, p, win, cap):
    first = _floor_rows(start) + p * win
    return first, pl.multiple_of(jnp.minimum(first, cap - win), PACKED_ROWS)


def _gather_kernel(st_ref, h_ref, pos_ref, aff_ref, x_ref, v_ref, *, nt, tt, win, cap, nbatch, ng):
    rows = [((pl.program_id(1) * ng + g) * nbatch + pl.program_id(0)) * (nt + 1) for g in range(ng)]
    x_ref[...] = jnp.zeros_like(x_ref)
    v_ref[...] = jnp.zeros_like(v_ref)
    riota = lax.broadcasted_iota(I32, (win, tt), 0)

    def bounds(t):
        return [st_ref[r + t] for r in rows], [st_ref[r + t + 1] for r in rows]

    def one_pass(t, p, starts):
        hrows = h_ref[0, pl.ds(pl.multiple_of(t * tt, tt), tt), :]
        hots, wins = [], []
        for g in range(ng):
            first, ws = _pass_window(starts[g], p, win, cap)
            pos = pos_ref[g, pl.ds(t, 1), :]
            hots.append((pos - ws == riota) & (pos >= first))
            wins.append(ws)
        onehot = jnp.concatenate([jnp.where(h, 1.0, 0.0).astype(BF16) for h in hots], axis=0)
        rows = _dot(onehot, hrows)
        for g in range(ng):
            x_ref[0, g, pl.ds(wins[g], win), :] += rows[g * win:(g + 1) * win].astype(BF16)
            vals = jnp.sum(jnp.where(hots[g], aff_ref[g, pl.ds(t, 1), :], 0.0), axis=1, keepdims=True)
            v_ref[0, g, pl.ds(wins[g], win), :] += vals

    def first_pass(t, most):
        starts, ends = bounds(t)
        one_pass(t, 0, starts)
        return jnp.maximum(most, _num_passes(starts, ends, win))

    def more_passes(t, carry):
        starts, ends = bounds(t)

        def body(p, c):
            one_pass(t, p, starts)
            return c

        lax.fori_loop(1, _num_passes(starts, ends, win), body, 0)
        return carry

    most = lax.fori_loop(0, nt, first_pass, jnp.int32(0), unroll=GATHER_UNROLL)

    @pl.when(most > 1)
    def _():
        lax.fori_loop(0, nt, more_passes, 0)


def _gather(starts, h3, pos3, aff3t, cap, ne):
    b, s, d = h3.shape
    tt, win, ng = GATHER_TILE, GATHER_WINDOW, GATHER_EXPERTS
    nt = s // tt
    grid_spec = pltpu.PrefetchScalarGridSpec(
        num_scalar_prefetch=1,
        grid=(b, ne // ng),
        in_specs=[
            pl.BlockSpec((1, s, d), lambda i, e, st: (i, 0, 0), pipeline_mode=pl.Buffered(1)),
            pl.BlockSpec((ng, nt, tt), lambda i, e, st: (e, i, 0)),
            pl.BlockSpec((ng, nt, tt), lambda i, e, st: (e, i, 0)),
        ],
        out_specs=[
            pl.BlockSpec((1, ng, cap, d), lambda i, e, st: (i, e, 0, 0)),
            pl.BlockSpec((1, ng, cap, 1), lambda i, e, st: (i, e, 0, 0)),
        ],
    )
    return pl.pallas_call(
        functools.partial(_gather_kernel, nt=nt, tt=tt, win=win, cap=cap, nbatch=b, ng=ng),
        grid_spec=grid_spec,
        out_shape=[jax.ShapeDtypeStruct((b, ne, cap, d), BF16), jax.ShapeDtypeStruct((b, ne, cap, 1), F32)],
        compiler_params=_cparams("arbitrary", "arbitrary"),
        name="gather",
    )(starts, h3, pos3, aff3t)


def _ffn_kernel(x_ref, v_ref, wg_ref, wu_ref, wd_ref, y_ref, acc_ref, *, nf):
    f = pl.program_id(2)
    nb, _, cap, d = x_ref.shape

    @pl.when(f == 0)
    def _():
        acc_ref[...] = jnp.zeros_like(acc_ref)

    wg = wg_ref[0].astype(BF16)
    wu = wu_ref[0].astype(BF16)
    wd = wd_ref[0].astype(BF16)
    mr = min(FFN_ROWS, cap)
    for r in range(nb * cap // mr):
        x = x_ref[r * mr // cap, 0, pl.ds(r * mr % cap, mr), :]
        g = _dot(x, wg)
        u = _dot(x, wu)
        hid = (g * jax.nn.sigmoid(g) * u).astype(BF16)
        acc_ref[pl.ds(r * mr, mr), :] += _dot(hid, wd)

    @pl.when(f == nf - 1)
    def _():
        y = acc_ref[...] * v_ref[...].reshape(nb * cap, 1)
        y_ref[...] = y.astype(BF16).reshape(nb, 1, cap, d)


def _ffn(xg, vals, wg, wu, wd, nb, tf):
    b, ne, cap, d = xg.shape
    dexp = wg.shape[2]
    nf = dexp // tf
    return pl.pallas_call(
        functools.partial(_ffn_kernel, nf=nf),
        grid=(ne, b // nb, nf),
        in_specs=[
            pl.BlockSpec((nb, 1, cap, d), lambda e, i, f: (i, e, 0, 0)),
            pl.BlockSpec((nb, 1, cap, 1), lambda e, i, f: (i, e, 0, 0)),
            pl.BlockSpec((1, d, tf), lambda e, i, f: (e, 0, f)),
            pl.BlockSpec((1, d, tf), lambda e, i, f: (e, 0, f)),
            pl.BlockSpec((1, tf, d), lambda e, i, f: (e, f, 0)),
        ],
        out_specs=pl.BlockSpec((nb, 1, cap, d), lambda e, i, f: (i, e, 0, 0)),
        out_shape=jax.ShapeDtypeStruct(xg.shape, BF16),
        scratch_shapes=[pltpu.VMEM((nb * cap, d), F32)],
        compiler_params=_cparams("parallel", "parallel", "arbitrary"),
        name="ffn",
    )(xg, vals, wg, wu, wd)


def _combine_kernel(st_ref, y_ref, pos_ref, spread_ref, lanes_ref, x1_ref, p_ref, wpp_ref, wpg_ref, gple_ref,
                    gout_ref, o_ref, *, nt, tt, win, cap, ne, nbatch, nsub, final):
    k = ne * win
    lane_e = lanes_ref[0:1, :]
    lane_r = lanes_ref[1:2, :]

    def tile_work(u):
        t = pl.program_id(1) * nsub + u
        rows = [(e * nbatch + pl.program_id(0)) * (nt + 1) + t for e in range(ne)]
        starts = [st_ref[r] for r in rows]
        ends = [st_ref[r + 1] for r in rows]
        pos = pos_ref[:, u * tt:(u + 1) * tt].astype(F32)
        pos = jnp.concatenate([pos, jnp.zeros((LANES - ne, tt), F32)], axis=0).T
        hi = jnp.floor(pos * (1.0 / COMBINE_SPLIT))
        lo = pos - hi * COMBINE_SPLIT
        rank = _dot(jnp.concatenate([hi, lo], axis=1).astype(BF16), spread_ref[...])

        def expert_rows(p):
            firsts = jnp.zeros((1, k), F32)
            offs = jnp.zeros((1, k), F32)
            wins = []
            for e in range(ne):
                first, ws = _pass_window(starts[e], p, win, cap)
                firsts = jnp.where(lane_e == e, first.astype(F32), firsts)
                offs = jnp.where(lane_e == e, ws.astype(F32), offs)
                wins.append(y_ref[0, e, pl.ds(ws, win), :])
            onehot = jnp.where((rank - offs == lane_r) & (rank >= firsts), 1.0, 0.0).astype(BF16)
            return _dot(onehot, jnp.concatenate(wins, axis=0))

        def finish(acc):
            for r in range(tt // LANES):
                rows = pl.ds(u * tt + r * LANES, LANES)
                x2 = acc[r * LANES:(r + 1) * LANES]
                emb = _dot(p_ref[0, rows, :].astype(BF16), wpp_ref[...])
                gate = jax.nn.sigmoid(_dot(_rms(x2, gple_ref[...]).astype(BF16), wpg_ref[...]))
                x3 = x2 + gate * emb
                o_ref[0, rows, :] = _rms(x3, gout_ref[...]) if final else x3

        x1 = x1_ref[0, pl.ds(u * tt, tt), :]
        finish(x1 + expert_rows(0))
        npass = _num_passes(starts, ends, win)
        redo = lambda: finish(lax.fori_loop(1, npass, lambda p, acc: acc + expert_rows(p), x1 + expert_rows(0)))
        return npass, redo

    for npass, redo in [tile_work(u) for u in range(nsub)]:
        pl.when(npass > 1)(redo)


def _combine(starts, yg, post, x13, p3, wpp, wpg, gple, gout, cap, final):
    b, ne, _, d = yg.shape
    s = x13.shape[1]
    tt, win, nsub = COMBINE_TILE, COMBINE_WINDOW, COMBINE_STEP_TILES
    nt = s // tt
    lane = jnp.arange(ne * win, dtype=I32)
    part = jnp.arange(2 * LANES, dtype=I32)[:, None]
    spread = jnp.where(part == lane // win, float(COMBINE_SPLIT), jnp.where(part == lane // win + LANES, 1.0, 0.0))
    spread = spread.astype(BF16)
    lanes = jnp.stack([lane // win, lane % win]).astype(F32)
    tile = lambda a: pl.BlockSpec((1, nsub * tt, a.shape[2]), lambda i, t, st: (i, t, 0))
    full = lambda a: pl.BlockSpec(a.shape, lambda i, t, st: (0,) * a.ndim, pipeline_mode=pl.Buffered(1))
    grid_spec = pltpu.PrefetchScalarGridSpec(
        num_scalar_prefetch=1,
        grid=(b, nt // nsub),
        in_specs=[
            pl.BlockSpec((1, ne, cap, d), lambda i, t, st: (i, 0, 0, 0), pipeline_mode=pl.Buffered(1)),
            pl.BlockSpec((ne, nsub * tt), lambda i, t, st: (0, i * (nt // nsub) + t)),
            full(spread), full(lanes), tile(x13), tile(p3), full(wpp), full(wpg), full(gple), full(gout),
        ],
        out_specs=pl.BlockSpec((1, nsub * tt, d), lambda i, t, st: (i, t, 0)),
    )
    return pl.pallas_call(
        functools.partial(_combine_kernel, nt=nt, tt=tt, win=win, cap=cap, ne=ne, nbatch=b, nsub=nsub, final=final),
        grid_spec=grid_spec,
        out_shape=jax.ShapeDtypeStruct(x13.shape, F32),
        compiler_params=_cparams("arbitrary", "arbitrary"),
        name="combine",
    )(starts, yg, post, spread, lanes, x13, p3, wpp, wpg, gple, gout)


def kernel(x, p, g_mix, w_in, w_fourier, ssm_a_re, ssm_a_im, ssm_log_dt, ssm_b_re, ssm_b_im, ssm_c_re, ssm_c_im, ssm_d, w_glu, w_out, g_ffn, w_router, w_exp_gate, w_exp_up, w_exp_down, g_ple, w_ple_gate, w_ple_proj, g_final):
    b, s, d = x.shape
    depth = p.shape[0]
    df = w_fourier.shape[1]
    ds = w_glu.shape[1]
    ne = w_router.shape[2]
    dexp = w_exp_gate.shape[3]
    groups, gi = ssm_b_re.shape[2], ssm_b_re.shape[4]
    L = SSM_CHUNK
    n1 = s // DFT_N2
    nc = s // L
    cap = EC_CAPACITY * s // ne
    assert s % (DFT_N2 * PACKED_ROWS) == 0 and s % GATHER_TILE == 0 and cap % PACKED_ROWS == 0
    assert nc % SSM_TILE_CHUNKS == 0 and gi == PACKED_ROWS and cap // COMBINE_SPLIT < 256
    assert 2 * ssm_a_re.shape[3] == LANES and ds % LANES == 0
    assert cap >= COMBINE_WINDOW and ne % GATHER_EXPERTS == 0 and GATHER_TILE == COMBINE_TILE
    assert ds == groups * gi and df % FOURIER_GROUPS == 0

    bd = _channel_dft_table(df)
    ck, sk = _dft1_tables(n1)
    tr, ti = _dft3_tables(n1)
    tf = 256 if dexp % 256 == 0 else dexp
    nb_ffn = 2 if b % 2 == 0 else 1

    xcur = x.reshape(b * s, d)
    for i in range(depth):
        final = i == depth - 1
        row = lambda v: v.astype(F32).reshape(1, -1)
        pq, us, gate = _inproj(xcur, row(g_mix[i]), w_in[i].astype(BF16), bd, df, ds)
        a4 = _dft1(pq.reshape(b, n1, DFT_N2, 2 * df), ck, sk)
        fre = _dft3(a4, tr, ti).reshape(b * s, df)

        tabs = _ssm_tables(ssm_a_re[i], ssm_a_im[i], ssm_log_dt[i], ssm_b_re[i], ssm_b_im[i],
                           ssm_c_re[i], ssm_c_im[i], ssm_d[i])
        ys = _to_tokens(_ssm(_to_groups(us, L, groups, gi), *tabs, nc=nc, nb=b), L, groups, gi)

        wr = w_router[i].astype(F32)
        wr_hi = wr.astype(BF16)
        wr_lo = (wr - wr_hi.astype(F32)).astype(BF16)
        lane_pad = lambda w: jnp.pad(w, ((0, 0), (0, LANES - ne)))
        wr2 = jnp.concatenate([jnp.concatenate([lane_pad(wr_hi), lane_pad(wr_lo)], axis=1),
                               jnp.concatenate([lane_pad(wr_hi), jnp.zeros((d, LANES), BF16)], axis=1)], axis=0)
        x1, h2, aff = _mixout(fre, ys, gate, xcur, w_fourier[i].astype(BF16), w_glu[i].astype(BF16),
                              w_out[i].astype(BF16), row(g_ffn[i]), wr2, ne)

        aff_rows = aff.reshape(ne * b, s)
        posm, st_blk = _topk(aff_rows, cap)
        starts = st_blk[:, ::GATHER_TILE // TOPK_BLOCK].reshape(-1)
        tiles = (ne, b * (s // GATHER_TILE), GATHER_TILE)
        xg, vals = _gather(starts, h2.reshape(b, s, d), posm.reshape(tiles), aff_rows.reshape(tiles), cap, ne)
        yg = _ffn(xg, vals, w_exp_gate[i], w_exp_up[i], w_exp_down[i], nb_ffn, tf)
        xnext = _combine(starts, yg, posm.reshape(ne, b * s), x1.reshape(b, s, d), p[i],
                         w_ple_proj[i].astype(BF16), w_ple_gate[i].astype(BF16), row(g_ple[i]),
                         row(g_final) if final else row(g_ple[i]), cap, final)
        xcur = xnext.reshape(b * s, d)
    return xcur.reshape(b, s, d)
```

```python
import functools
import math

import jax
import jax.numpy as jnp
from jax import lax
from jax.experimental import pallas as pl
from jax.experimental.pallas import tpu as pltpu

F32 = jnp.float32
BF16 = jnp.bfloat16
I32 = jnp.int32

RMS_EPS = 1e-6
FOURIER_GROUPS = 4
EC_CAPACITY = 2
DFT_N2 = 128
SUBLANES = 8
LANES = 128
PACKED_ROWS = 16
SSM_CHUNK = 32
ROW_TILE = 1024
INPROJ_ROWS = 256
SSM_TILE_CHUNKS = 128
SCAN_UNROLL = 16
TOPK_BLOCK = 128
GATHER_TILE = 256
MIX_ROWS = 256
FFN_ROWS = 1024
GATHER_WINDOW = 64
GATHER_EXPERTS = 4
GATHER_UNROLL = 8
COMBINE_TILE = 256
COMBINE_WINDOW = 64
COMBINE_STEP_TILES = 2
COMBINE_SPLIT = 32
VMEM_LIMIT = 56 * 1024 * 1024


def _cparams(*sem):
    return pltpu.CompilerParams(dimension_semantics=sem, vmem_limit_bytes=VMEM_LIMIT)


def _rms(x, g):
    return x * lax.rsqrt(jnp.mean(x * x, axis=-1, keepdims=True) + RMS_EPS) * g


def _dot(a, b):
    return jnp.dot(a, b, preferred_element_type=F32)


def _inproj_kernel(x_ref, g_ref, w_ref, bd_ref, pq_ref, us_ref, gate_ref, *, df, ds):
    for r in range(x_ref.shape[0] // INPROJ_ROWS):
        rows = pl.ds(r * INPROJ_ROWS, INPROJ_ROWS)
        h = _rms(x_ref[rows, :], g_ref[...]).astype(BF16)
        z = _dot(h, w_ref[...])
        pq_ref[rows, :] = _dot(z[:, :df].astype(BF16), bd_ref[...]).astype(BF16)
        for q in range(ds // LANES):
            us_ref[q, rows, :] = z[:, df + q * LANES:df + (q + 1) * LANES]
        gate_ref[rows, :] = jax.nn.sigmoid(z[:, df + ds:]).astype(BF16)


def _inproj(x2, g, w_in, bd, df, ds, tm=ROW_TILE):
    t, d = x2.shape
    dg = w_in.shape[1] - df - ds
    return pl.pallas_call(
        functools.partial(_inproj_kernel, df=df, ds=ds),
        grid=(t // tm,),
        in_specs=[
            pl.BlockSpec((tm, d), lambda i: (i, 0)),
            pl.BlockSpec((1, d), lambda i: (0, 0)),
            pl.BlockSpec(w_in.shape, lambda i: (0, 0), pipeline_mode=pl.Buffered(1)),
            pl.BlockSpec(bd.shape, lambda i: (0, 0), pipeline_mode=pl.Buffered(1)),
        ],
        out_specs=[
            pl.BlockSpec((tm, 2 * df), lambda i: (i, 0)),
            pl.BlockSpec((ds // LANES, tm, LANES), lambda i: (0, i, 0)),
            pl.BlockSpec((tm, dg), lambda i: (i, 0)),
        ],
        out_shape=[
            jax.ShapeDtypeStruct((t, 2 * df), BF16),
            jax.ShapeDtypeStruct((ds // LANES, t, LANES), F32),
            jax.ShapeDtypeStruct((t, dg), BF16),
        ],
        compiler_params=_cparams("parallel"),
        name="inproj",
    )(x2, g, w_in, bd)


def _channel_dft_table(df):
    c = df // FOURIER_GROUPS
    k = jnp.arange(c, dtype=I32)
    ang = (2.0 * math.pi / c) * ((k[:, None] * k[None, :]) % c).astype(F32)
    eye = jnp.eye(FOURIER_GROUPS, dtype=F32)
    scale = 1.0 / math.sqrt(c)
    re = jnp.kron(eye, jnp.cos(ang)) * scale
    im = -jnp.kron(eye, jnp.sin(ang)) * scale
    return jnp.concatenate([re, im], axis=1).astype(BF16)


def _dft1_tables(n1):
    k = jnp.arange(n1, dtype=I32)
    ang = (2.0 * math.pi / n1) * ((k[:, None] * k[None, :]) % n1).astype(F32)
    row = jnp.arange(n1 * SUBLANES, dtype=I32)
    rep = (row[:, None] // SUBLANES == k[None, :]).astype(F32)
    same_slot = (row[:, None] % SUBLANES == row[None, :] % SUBLANES).astype(F32) * (1.0 / math.sqrt(n1))
    hp = lax.Precision.HIGHEST
    kron8 = lambda t: (jnp.dot(jnp.dot(rep, t, precision=hp), rep.T, precision=hp) * same_slot).astype(BF16)
    return kron8(jnp.cos(ang)), kron8(jnp.sin(ang))


def _dft3_tables(n1):
    s = n1 * DFT_N2
    nb = n1 // SUBLANES
    k1 = jnp.arange(n1, dtype=I32)[:, None, None]
    k2 = jnp.arange(DFT_N2, dtype=I32)[None, :, None]
    n2 = jnp.arange(DFT_N2, dtype=I32)[None, None, :]
    ang = (2.0 * math.pi / s) * ((n2 * (k1 + n1 * k2)) % s).astype(F32)
    eye = jnp.eye(SUBLANES, dtype=BF16)[None, None, :, :, None]
    rows = DFT_N2 * SUBLANES

    def expand(t):
        t = (t * (1.0 / math.sqrt(DFT_N2))).astype(BF16).reshape(nb, SUBLANES, DFT_N2, DFT_N2)
        t = jnp.transpose(t, (0, 2, 1, 3))[:, :, :, None, :]
        return (t * eye).reshape(nb, rows, rows)

    return expand(jnp.cos(ang)), expand(jnp.sin(ang))


def _dft1_kernel(z_ref, ck_ref, sk_ref, a_ref):
    n1, slab, c2 = z_ref.shape[1], z_ref.shape[2], z_ref.shape[3]
    c = c2 // 2
    z = z_ref[0].astype(F32)
    halves = []
    for h in range(slab // SUBLANES):
        zh = z[:, h * SUBLANES:(h + 1) * SUBLANES, :].reshape(n1 * SUBLANES, c2).astype(BF16)
        cz = _dot(ck_ref[...], zh)
        sz = _dot(sk_ref[...], zh)
        a = jnp.concatenate([cz[:, :c] + sz[:, c:], cz[:, c:] - sz[:, :c]], axis=1)
        halves.append(a.reshape(n1, SUBLANES, c2))
    a_ref[0] = jnp.concatenate(halves, axis=1).astype(BF16)


def _dft1(pq4, ck, sk):
    b, n1, n2, c2 = pq4.shape
    slab = PACKED_ROWS
    return pl.pallas_call(
        _dft1_kernel,
        grid=(b, n2 // slab),
        in_specs=[
            pl.BlockSpec((1, n1, slab, c2), lambda i, j: (i, 0, j, 0)),
            pl.BlockSpec(ck.shape, lambda i, j: (0, 0)),
            pl.BlockSpec(sk.shape, lambda i, j: (0, 0)),
        ],
        out_specs=pl.BlockSpec((1, n1, slab, c2), lambda i, j: (i, 0, j, 0)),
        out_shape=jax.ShapeDtypeStruct(pq4.shape, BF16),
        compiler_params=_cparams("parallel", "parallel"),
        name="dft1",
    )(pq4, ck, sk)


def _dft3_kernel(a_ref, tr_ref, ti_ref, o_ref):
    nblk, c2 = tr_ref.shape[0], a_ref.shape[3]
    c = c2 // 2
    outs = []
    for h in range(nblk):
        a = a_ref[0, h * SUBLANES:(h + 1) * SUBLANES].reshape(SUBLANES * DFT_N2, c2)
        out = _dot(tr_ref[h], a[:, :c]) + _dot(ti_ref[h], a[:, c:])
        outs.append(out.reshape(DFT_N2, SUBLANES, c))
    o_ref[0] = jnp.concatenate(outs, axis=1).astype(BF16)


def _dft3(a4, tr, ti):
    b, n1, n2, c2 = a4.shape
    c = c2 // 2
    rows = DFT_N2 * SUBLANES
    nblk = PACKED_ROWS // SUBLANES
    return pl.pallas_call(
        _dft3_kernel,
        grid=(n1 // PACKED_ROWS, b),
        in_specs=[
            pl.BlockSpec((1, PACKED_ROWS, n2, c2), lambda k, i: (i, k, 0, 0)),
            pl.BlockSpec((nblk, rows, rows), lambda k, i: (k, 0, 0)),
            pl.BlockSpec((nblk, rows, rows), lambda k, i: (k, 0, 0)),
        ],
        out_specs=pl.BlockSpec((1, DFT_N2, PACKED_ROWS, c), lambda k, i: (i, 0, k, 0)),
        out_shape=jax.ShapeDtypeStruct((b, DFT_N2, n1, c), BF16),
        compiler_params=_cparams("parallel", "parallel"),
        name="dft3",
    )(a4, tr, ti)


def _ssm_tables(a_re, a_im, log_dt, b_re, b_im, c_re, c_im, d):
    L = SSM_CHUNK
    _, g, n = a_re.shape
    gi = b_re.shape[-1]
    dt = jnp.exp(log_dt.astype(F32))[..., None]
    ar, ai = a_re.astype(F32), a_im.astype(F32)
    tau = jnp.arange(L + 1, dtype=F32)[:, None, None, None]
    mag = jnp.exp(tau * (ar * dt)[None])
    ang = tau * (ai * dt)[None]
    pr, pi = mag * jnp.cos(ang), mag * jnp.sin(ang)
    nr, ni = pr[1] - 1.0, pi[1]
    den = ar * ar + ai * ai
    qr, qi = (nr * ar + ni * ai) / den, (ni * ar - nr * ai) / den
    br, bi = b_re.astype(F32), b_im.astype(F32)
    bbr = qr[..., None] * br - qi[..., None] * bi
    bbi = qr[..., None] * bi + qi[..., None] * br
    cr, ci = c_re.astype(F32), c_im.astype(F32)

    cat = jnp.concatenate
    steps = lambda p: jnp.transpose(p, (1, 0, 2))
    pf_r, pf_i, pb_r, pb_i = steps(pr[1:, 0]), steps(pi[1:, 0]), steps(pr[1:, 1][::-1]), steps(pi[1:, 1][::-1])
    f_small = jnp.stack([cat([cr[0], cr[1], cr[0], cr[1]], -1), cat([-ci[0], -ci[1], -ci[0], -ci[1]], -1)], 1)
    f_steps = jnp.stack([cat([pf_r, pb_r, -pf_i, -pb_i], -1), cat([pf_i, pb_i, pf_r, pb_r], -1)], 1)
    last = lambda p: jnp.transpose(p, (1, 2, 0))
    ef_r, ef_i, eb_r, eb_i = last(pr[:L, 0][::-1]), last(pi[:L, 0][::-1]), last(pr[:L, 1]), last(pi[:L, 1])
    e_small = jnp.stack([cat([bbr[0], bbr[1], bbi[0], bbi[1]], 1), cat([-bbi[0], -bbi[1], bbr[0], bbr[1]], 1)], 1)
    e_steps = jnp.stack([cat([ef_r, eb_r, ef_r, eb_r], 1), cat([ef_i, eb_i, ef_i, eb_i], 1)], 1)
    zpad = lambda p, before: jnp.pad(p, ((0, 0), (0, 0), (before, L - before)))
    wf_r, wf_i, wb_r, wb_i = zpad(ef_r, 0), zpad(ef_i, 0), zpad(eb_r, L - 1), zpad(eb_i, L - 1)
    t_small = jnp.stack([cat([bbr[0], bbi[0], bbr[1], bbi[1]], 1), cat([-bbi[0], bbr[0], -bbi[1], bbr[1]], 1)], 1)
    t_steps = jnp.stack([cat([wf_r, wf_r, wb_r, wb_r], 1), cat([wf_i, wf_i, wb_i, wb_i], 1)], 1)
    t_left = cat([cr[0], -ci[0], cr[1], -ci[1]], -1)

    al = jnp.stack([cat([pr[L, 0], pr[L, 1]], -1), cat([pi[L, 0], pi[L, 1]], -1)], axis=1)
    dv = jnp.tile(d.astype(F32).reshape(g, 1, gi), (1, L, 1)).reshape(g, L * gi, 1)
    t_tab, e_tab, f_tab = _ssm_table_call(f_small, f_steps, e_small, e_steps, t_small, t_steps, t_left, L, gi)
    return t_tab, e_tab, f_tab, al, dv


def _ssm_table_kernel(fs_ref, fp_ref, es_ref, ep_ref, ts_ref, tp_ref, tl_ref, ri_ref, rj_ref, rit_ref, rjt_ref,
                      ri2_ref, rs_ref, t_ref, e_ref, f_ref, *, L, gi):
    def split(a):
        hi = a.astype(BF16)
        return hi, (a - hi.astype(F32)).astype(BF16)

    def spread_lanes(a, rep):
        hi, lo = split(a)
        return _dot(hi, rep) + _dot(lo, rep)

    def spread_rows(rep, a):
        hi, lo = split(a)
        return _dot(rep, hi) + _dot(rep, lo)

    ri, rj, rit, rjt, ri2, rs = ri_ref[...], rj_ref[...], rit_ref[...], rjt_ref[...], ri2_ref[...], rs_ref[...]
    f_ref[0] = (spread_rows(rit, fs_ref[0, 0]) * spread_rows(rjt, fp_ref[0, 0])
                + spread_rows(rit, fs_ref[0, 1]) * spread_rows(rjt, fp_ref[0, 1])).astype(BF16)
    e_ref[0] = (spread_lanes(es_ref[0, 0], ri) * spread_lanes(ep_ref[0, 0], rj)
                + spread_lanes(es_ref[0, 1], ri) * spread_lanes(ep_ref[0, 1], rj)).astype(BF16)
    ew = (spread_lanes(ts_ref[0, 0], ri2) * spread_lanes(tp_ref[0, 0], rs)
          + spread_lanes(ts_ref[0, 1], ri2) * spread_lanes(tp_ref[0, 1], rs))
    c_hi, c_lo = split(tl_ref[0])
    e_hi, e_lo = split(ew)
    kw = _dot(c_hi, e_hi) + _dot(c_hi, e_lo) + _dot(c_lo, e_hi)
    per = LANES // gi
    for r in range(per):
        shifted = kw if r == 0 else pltpu.roll(kw, 2 * L * gi - r * gi, axis=1)
        for a in range(L // per):
            jo = L - 1 - (a * per + r)
            t_ref[0, pl.ds(jo * gi, gi), :] = shifted[:, a * LANES:a * LANES + L * gi].astype(BF16)


def _ssm_table_call(f_small, f_steps, e_small, e_steps, t_small, t_steps, t_left, L, gi):
    g = f_small.shape[0]
    n4 = f_small.shape[3]
    k = L * gi
    lane = jnp.arange(2 * k, dtype=I32)
    ri2 = (lane[None, :] % gi == jnp.arange(gi, dtype=I32)[:, None]).astype(BF16)
    rs = (lane[None, :] // gi == jnp.arange(2 * L, dtype=I32)[:, None]).astype(BF16)
    ri, rj = ri2[:, :k], rs[:L, :k]
    per_group = lambda a: pl.BlockSpec((1,) + a.shape[1:], lambda i: (i,) + (0,) * (a.ndim - 1))
    full = lambda a: pl.BlockSpec(a.shape, lambda i: (0,) * a.ndim)
    ins = (f_small, f_steps, e_small, e_steps, t_small, t_steps, t_left)
    reps = (ri, rj, ri.T, rj.T, ri2, rs)
    return pl.pallas_call(
        functools.partial(_ssm_table_kernel, L=L, gi=gi),
        grid=(g,),
        in_specs=[per_group(a) for a in ins] + [full(a) for a in reps],
        out_specs=[pl.BlockSpec((1, k, k), lambda i: (i, 0, 0)), pl.BlockSpec((1, n4, k), lambda i: (i, 0, 0)),
                   pl.BlockSpec((1, k, n4), lambda i: (i, 0, 0))],
        out_shape=[jax.ShapeDtypeStruct((g, k, k), BF16), jax.ShapeDtypeStruct((g, n4, k), BF16),
                   jax.ShapeDtypeStruct((g, k, n4), BF16)],
        compiler_params=_cparams("parallel"),
        name="ssm_tables",
    )(*ins, *reps)


def _to_groups_kernel(u_ref, a_ref, *, L, groups, gi):
    ncl = a_ref.shape[2]
    gq = LANES // gi
    for q in range(u_ref.shape[0]):
        for j in range(L):
            zt = u_ref[q, pl.ds(j, ncl, stride=L), :].T
            a_ref[q * gq:(q + 1) * gq, pl.ds(j * gi, gi), :] = zt.reshape(gq, gi, ncl).astype(BF16)


def _to_groups(us, L, groups, gi):
    nq, t, _ = us.shape
    ncl = SSM_TILE_CHUNKS
    return pl.pallas_call(
        functools.partial(_to_groups_kernel, L=L, groups=groups, gi=gi),
        grid=(t // (ncl * L),),
        in_specs=[pl.BlockSpec((nq, ncl * L, LANES), lambda i: (0, i, 0))],
        out_specs=pl.BlockSpec((groups, L * gi, ncl), lambda i: (0, 0, i)),
        out_shape=jax.ShapeDtypeStruct((groups, L * gi, t // L), BF16),
        compiler_params=_cparams("parallel"),
        name="to_groups",
    )(us)


def _to_tokens_kernel(y_ref, o_ref, *, L, groups, gi):
    ncl = y_ref.shape[2]
    gq = LANES // gi
    for q in range(o_ref.shape[0]):
        for j in range(L):
            yj = y_ref[q * gq:(q + 1) * gq, pl.ds(j * gi, gi), :].astype(F32).reshape(LANES, ncl)
            o_ref[q, pl.ds(j, ncl, stride=L), :] = yj.T


def _to_tokens(yt, L, groups, gi):
    _, k, nchunks = yt.shape
    ncl = SSM_TILE_CHUNKS
    return pl.pallas_call(
        functools.partial(_to_tokens_kernel, L=L, groups=groups, gi=gi),
        grid=(nchunks // ncl,),
        in_specs=[pl.BlockSpec((groups, k, ncl), lambda i: (0, 0, i))],
        out_specs=pl.BlockSpec((groups * gi // LANES, ncl * L, LANES), lambda i: (0, i, 0)),
        out_shape=jax.ShapeDtypeStruct((groups * gi // LANES, nchunks * L, LANES), F32),
        compiler_params=_cparams("parallel"),
        name="to_tokens",
    )(yt)


def _ssm_kernel(a_ref, t_ref, e_ref, f_ref, al_ref, dv_ref, y_ref, s_scr, h_scr, *, nc, nb, rows, n):
    a = a_ref[0]
    y1 = _dot(t_ref[0], a) + dv_ref[0] * a.astype(F32)
    st = _dot(e_ref[0], a)
    n2 = 2 * n
    s_scr[...] = jnp.zeros_like(s_scr)
    for b in range(nb):
        sb_t = st[:, b * nc:(b + 1) * nc].T
        for q in range(2):
            s_scr[q, pl.ds(b, nc, stride=rows), :] = sb_t[:, q * n2:(q + 1) * n2]
    ar = al_ref[0, 0:1, :]
    ai = al_ref[0, 1:2, :]
    is_fwd = lax.broadcasted_iota(I32, (rows, n2), 1) < n

    def step(i, carry):
        hr, hi = carry
        rf = pl.ds(pl.multiple_of(i * rows, rows), rows)
        rb = pl.ds(pl.multiple_of((nc - 1 - i) * rows, rows), rows)
        h_scr[0, rf, 0:n] = hr[:, 0:n]
        h_scr[1, rf, 0:n] = hi[:, 0:n]
        h_scr[0, rb, n:n2] = hr[:, n:n2]
        h_scr[1, rb, n:n2] = hi[:, n:n2]
        sr = jnp.where(is_fwd, s_scr[0, rf, :], s_scr[0, rb, :])
        si = jnp.where(is_fwd, s_scr[1, rf, :], s_scr[1, rb, :])
        return ar * hr - ai * hi + sr, ar * hi + ai * hr + si

    zero = jnp.zeros((rows, n2), F32)
    lax.fori_loop(0, nc, step, (zero, zero), unroll=SCAN_UNROLL)
    ht = jnp.concatenate(
        [jnp.concatenate([h_scr[q, pl.ds(b, nc, stride=rows), :].T for q in range(2)], axis=0) for b in range(nb)],
        axis=1)
    y = y1 + _dot(f_ref[0], ht.astype(BF16))
    y_ref[0] = jax.nn.gelu(y, approximate=True).astype(BF16)


def _ssm(ag, t_tab, e_tab, f_tab, al, dv, nc, nb):
    g, k, m = ag.shape
    n4 = e_tab.shape[1]
    rows = -(-nb // SUBLANES) * SUBLANES
    return pl.pallas_call(
        functools.partial(_ssm_kernel, nc=nc, nb=nb, rows=rows, n=n4 // 4),
        grid=(g,),
        in_specs=[
            pl.BlockSpec((1, k, m), lambda i: (i, 0, 0)),
            pl.BlockSpec((1, k, k), lambda i: (i, 0, 0)),
            pl.BlockSpec((1, n4, k), lambda i: (i, 0, 0)),
            pl.BlockSpec((1, k, n4), lambda i: (i, 0, 0)),
            pl.BlockSpec((1, 2, n4 // 2), lambda i: (i, 0, 0)),
            pl.BlockSpec((1, k, 1), lambda i: (i, 0, 0)),
        ],
        out_specs=pl.BlockSpec((1, k, m), lambda i: (i, 0, 0)),
        out_shape=jax.ShapeDtypeStruct((g, k, m), BF16),
        scratch_shapes=[pltpu.VMEM((2, nc * rows, n4 // 2), F32), pltpu.VMEM((2, nc * rows, n4 // 2), F32)],
        compiler_params=_cparams("parallel"),
        name="ssm",
    )(ag, t_tab, e_tab, f_tab, al, dv)


def _mixout_kernel(fre_ref, ys_ref, gate_ref, x_ref, wf_ref, wglu_ref, wout_ref, gffn_ref, wr_ref,
                   x1_ref, h2_ref, aff_ref, *, d, ne):
    for r in range(fre_ref.shape[0] // MIX_ROWS):
        rows = pl.ds(r * MIX_ROWS, MIX_ROWS)
        y_f = _dot(fre_ref[rows, :], wf_ref[...])
        ys = jnp.concatenate([ys_ref[q, rows, :] for q in range(ys_ref.shape[0])], axis=1)
        vg = _dot(ys.astype(BF16), wglu_ref[...])
        y_s = vg[:, :d] * jax.nn.sigmoid(vg[:, d:])
        gate = gate_ref[rows, :].astype(F32)
        m = gate[:, :d] * y_f + gate[:, d:] * y_s
        x1 = x_ref[rows, :] + _dot(m.astype(BF16), wout_ref[...])
        x1_ref[rows, :] = x1
        h2 = _rms(x1, gffn_ref[...])
        h2_ref[rows, :] = h2.astype(BF16)
        hi = h2.astype(BF16)
        lo = (h2 - hi.astype(F32)).astype(BF16)
        rl = _dot(jnp.concatenate([hi, lo], axis=1), wr_ref[...])
        logits = rl[:, :LANES] + rl[:, LANES:]
        logits = jnp.where(lax.broadcasted_iota(I32, logits.shape, 1) < ne, logits, -1e30)
        logits = logits - jnp.max(logits, axis=-1, keepdims=True)
        p = jnp.exp(logits)
        aff = p / jnp.sum(p, axis=-1, keepdims=True)
        aff_ref[:, rows] = aff.T[:ne, :]


def _mixout(fre, ys, gate, x2, wf, wglu, wout, gffn, wr2, ne, tm=ROW_TILE):
    t, d = x2.shape
    full = lambda a: pl.BlockSpec(a.shape, lambda i: (0,) * a.ndim, pipeline_mode=pl.Buffered(1))
    row = lambda a: pl.BlockSpec((tm, a.shape[1]), lambda i: (i, 0))
    return pl.pallas_call(
        functools.partial(_mixout_kernel, d=d, ne=ne),
        grid=(t // tm,),
        in_specs=[row(fre), pl.BlockSpec((ys.shape[0], tm, LANES), lambda i: (0, i, 0)), row(gate), row(x2),
                  full(wf), full(wglu), full(wout), full(gffn), full(wr2)],
        out_specs=[
            pl.BlockSpec((tm, d), lambda i: (i, 0)),
            pl.BlockSpec((tm, d), lambda i: (i, 0)),
            pl.BlockSpec((ne, tm), lambda i: (0, i)),
        ],
        out_shape=[
            jax.ShapeDtypeStruct((t, d), F32),
            jax.ShapeDtypeStruct((t, d), BF16),
            jax.ShapeDtypeStruct((ne, t), F32),
        ],
        compiler_params=_cparams("parallel"),
        name="mixout",
    )(fre, ys, gate, x2, wf, wglu, wout, gffn, wr2)


def _topk_kernel(aff_ref, pos_ref, st_ref, *, cap, blk):
    v = aff_ref[...]
    r, s = v.shape
    capf = float(cap)

    def bit_step(i, t):
        cand = t | (jnp.int32(1) << (30 - i))
        cnt = jnp.sum(jnp.where(v >= pltpu.bitcast(cand, F32), 1.0, 0.0), axis=1, keepdims=True)
        return jnp.where(cnt >= capf, cand, t)

    thr = lax.fori_loop(0, 31, bit_step, jnp.zeros((r, 1), I32))
    gt = jnp.where(v >= pltpu.bitcast(thr + 1, F32), 1.0, 0.0)
    eq = jnp.where(v >= pltpu.bitcast(thr, F32), 1.0, 0.0) - gt
    need = capf - jnp.sum(gt, axis=1, keepdims=True)
    ii = lax.broadcasted_iota(I32, (blk, blk), 0)
    jj = lax.broadcasted_iota(I32, (blk, blk), 1)
    tri = jnp.where(ii < jj, 1.0, 0.0).astype(BF16)
    run_eq = jnp.zeros((r, 1), F32)
    run = jnp.zeros((r, 1), F32)
    for k in range(s // blk):
        sl = slice(k * blk, (k + 1) * blk)
        eqb, gtb = eq[:, sl], gt[:, sl]
        rank_eq = _dot(eqb.astype(BF16), tri) + run_eq
        run_eq = run_eq + jnp.sum(eqb, axis=1, keepdims=True)
        mask = gtb + eqb * jnp.where(rank_eq < need, 1.0, 0.0)
        pos = _dot(mask.astype(BF16), tri) + run
        st_ref[:, k:k + 1] = run.astype(I32)
        run = run + jnp.sum(mask, axis=1, keepdims=True)
        pos_ref[:, sl] = jnp.where(mask > 0.0, pos, -1.0).astype(I32)
    st_ref[:, s // blk:s // blk + 1] = run.astype(I32)


def _topk(aff_rows, cap, blk=TOPK_BLOCK):
    r, s = aff_rows.shape
    nblk = s // blk
    return pl.pallas_call(
        functools.partial(_topk_kernel, cap=cap, blk=blk),
        grid=(1,),
        in_specs=[pl.BlockSpec((r, s), lambda i: (0, 0))],
        out_specs=[pl.BlockSpec((r, s), lambda i: (0, 0)), pl.BlockSpec((r, nblk + 1), lambda i: (0, 0))],
        out_shape=[jax.ShapeDtypeStruct((r, s), I32), jax.ShapeDtypeStruct((r, nblk + 1), I32)],
        compiler_params=_cparams("arbitrary"),
        name="topk",
    )(aff_rows)


def _floor_rows(x):
    return (x // PACKED_ROWS) * PACKED_ROWS


def _num_passes(starts, ends, win):
    need = ends[0] - _floor_rows(starts[0])
    for a, b in zip(starts[1:], ends[1:]):
        need = jnp.maximum(need, b - _floor_rows(a))
    return (need + win - 1) // win


def _pass_window(start, p, win, cap):
    first = _floor_rows(start) + p * win
    return first, pl.multiple_of(jnp.minimum(first, cap - win), PACKED_ROWS)


def _gather_kernel(st_ref, h_ref, pos_ref, aff_ref, x_ref, v_ref, *, nt, tt, win, cap, nbatch, ng):
    rows = [((pl.program_id(1) * ng + g) * nbatch + pl.program_id(0)) * (nt + 1) for g in range(ng)]
    x_ref[...] = jnp.zeros_like(x_ref)
    v_ref[...] = jnp.zeros_like(v_ref)
    riota = lax.broadcasted_iota(I32, (win, tt), 0)

    def bounds(t):
        return [st_ref[r + t] for r in rows], [st_ref[r + t + 1] for r in rows]

    def one_pass(t, p, starts):
        hrows = h_ref[0, pl.ds(pl.multiple_of(t * tt, tt), tt), :]
        hots, wins = [], []
        for g in range(ng):
            first, ws = _pass_window(starts[g], p, win, cap)
            pos = pos_ref[g, pl.ds(t, 1), :]
            hots.append((pos - ws == riota) & (pos >= first))
            wins.append(ws)
        onehot = jnp.concatenate([jnp.where(h, 1.0, 0.0).astype(BF16) for h in hots], axis=0)
        rows = _dot(onehot, hrows)
        for g in range(ng):
            x_ref[0, g, pl.ds(wins[g], win), :] += rows[g * win:(g + 1) * win].astype(BF16)
            vals = jnp.sum(jnp.where(hots[g], aff_ref[g, pl.ds(t, 1), :], 0.0), axis=1, keepdims=True)
            v_ref[0, g, pl.ds(wins[g], win), :] += vals

    def first_pass(t, most):
        starts, ends = bounds(t)
        one_pass(t, 0, starts)
        return jnp.maximum(most, _num_passes(starts, ends, win))

    def more_passes(t, carry):
        starts, ends = bounds(t)

        def body(p, c):
            one_pass(t, p, starts)
            return c

        lax.fori_loop(1, _num_passes(starts, ends, win), body, 0)
        return carry

    most = lax.fori_loop(0, nt, first_pass, jnp.int32(0), unroll=GATHER_UNROLL)

    @pl.when(most > 1)
    def _():
        lax.fori_loop(0, nt, more_passes, 0)


def _gather(starts, h3, pos3, aff3t, cap, ne):
    b, s, d = h3.shape
    tt, win, ng = GATHER_TILE, GATHER_WINDOW, GATHER_EXPERTS
    nt = s // tt
    grid_spec = pltpu.PrefetchScalarGridSpec(
        num_scalar_prefetch=1,
        grid=(b, ne // ng),
        in_specs=[
            pl.BlockSpec((1, s, d), lambda i, e, st: (i, 0, 0), pipeline_mode=pl.Buffered(1)),
            pl.BlockSpec((ng, nt, tt), lambda i, e, st: (e, i, 0)),
            pl.BlockSpec((ng, nt, tt), lambda i, e, st: (e, i, 0)),
        ],
        out_specs=[
            pl.BlockSpec((1, ng, cap, d), lambda i, e, st: (i, e, 0, 0)),
            pl.BlockSpec((1, ng, cap, 1), lambda i, e, st: (i, e, 0, 0)),
        ],
    )
    return pl.pallas_call(
        functools.partial(_gather_kernel, nt=nt, tt=tt, win=win, cap=cap, nbatch=b, ng=ng),
        grid_spec=grid_spec,
        out_shape=[jax.ShapeDtypeStruct((b, ne, cap, d), BF16), jax.ShapeDtypeStruct((b, ne, cap, 1), F32)],
        compiler_params=_cparams("arbitrary", "arbitrary"),
        name="gather",
    )(starts, h3, pos3, aff3t)


def _ffn_kernel(x_ref, v_ref, wg_ref, wu_ref, wd_ref, y_ref, acc_ref, *, nf):
    f = pl.program_id(2)
    nb, _, cap, d = x_ref.shape

    @pl.when(f == 0)
    def _():
        acc_ref[...] = jnp.zeros_like(acc_ref)

    wg = wg_ref[0].astype(BF16)
    wu = wu_ref[0].astype(BF16)
    wd = wd_ref[0].astype(BF16)
    mr = min(FFN_ROWS, cap)
    for r in range(nb * cap // mr):
        x = x_ref[r * mr // cap, 0, pl.ds(r * mr % cap, mr), :]
        g = _dot(x, wg)
        u = _dot(x, wu)
        hid = (g * jax.nn.sigmoid(g) * u).astype(BF16)
        acc_ref[pl.ds(r * mr, mr), :] += _dot(hid, wd)

    @pl.when(f == nf - 1)
    def _():
        y = acc_ref[...] * v_ref[...].reshape(nb * cap, 1)
        y_ref[...] = y.astype(BF16).reshape(nb, 1, cap, d)


def _ffn(xg, vals, wg, wu, wd, nb, tf):
    b, ne, cap, d = xg.shape
    dexp = wg.shape[2]
    nf = dexp // tf
    return pl.pallas_call(
        functools.partial(_ffn_kernel, nf=nf),
        grid=(ne, b // nb, nf),
        in_specs=[
            pl.BlockSpec((nb, 1, cap, d), lambda e, i, f: (i, e, 0, 0)),
            pl.BlockSpec((nb, 1, cap, 1), lambda e, i, f: (i, e, 0, 0)),
            pl.BlockSpec((1, d, tf), lambda e, i, f: (e, 0, f)),
            pl.BlockSpec((1, d, tf), lambda e, i, f: (e, 0, f)),
            pl.BlockSpec((1, tf, d), lambda e, i, f: (e, f, 0)),
        ],
        out_specs=pl.BlockSpec((nb, 1, cap, d), lambda e, i, f: (i, e, 0, 0)),
        out_shape=jax.ShapeDtypeStruct(xg.shape, BF16),
        scratch_shapes=[pltpu.VMEM((nb * cap, d), F32)],
        compiler_params=_cparams("parallel", "parallel", "arbitrary"),
        name="ffn",
    )(xg, vals, wg, wu, wd)


def _combine_kernel(st_ref, y_ref, pos_ref, spread_ref, lanes_ref, x1_ref, p_ref, wpp_ref, wpg_ref, gple_ref,
                    gout_ref, o_ref, *, nt, tt, win, cap, ne, nbatch, nsub, final):
    k = ne * win
    lane_e = lanes_ref[0:1, :]
    lane_r = lanes_ref[1:2, :]

    def tile_work(u):
        t = pl.program_id(1) * nsub + u
        rows = [(e * nbatch + pl.program_id(0)) * (nt + 1) + t for e in range(ne)]
        starts = [st_ref[r] for r in rows]
        ends = [st_ref[r + 1] for r in rows]
        pos = pos_ref[:, u * tt:(u + 1) * tt].astype(F32)
        pos = jnp.concatenate([pos, jnp.zeros((LANES - ne, tt), F32)], axis=0).T
        hi = jnp.floor(pos * (1.0 / COMBINE_SPLIT))
        lo = pos - hi * COMBINE_SPLIT
        rank = _dot(jnp.concatenate([hi, lo], axis=1).astype(BF16), spread_ref[...])

        def expert_rows(p):
            firsts = jnp.zeros((1, k), F32)
            offs = jnp.zeros((1, k), F32)
            wins = []
            for e in range(ne):
                first, ws = _pass_window(starts[e], p, win, cap)
                firsts = jnp.where(lane_e == e, first.astype(F32), firsts)
                offs = jnp.where(lane_e == e, ws.astype(F32), offs)
                wins.append(y_ref[0, e, pl.ds(ws, win), :])
            onehot = jnp.where((rank - offs == lane_r) & (rank >= firsts), 1.0, 0.0).astype(BF16)
            return _dot(onehot, jnp.concatenate(wins, axis=0))

        def finish(acc):
            for r in range(tt // LANES):
                rows = pl.ds(u * tt + r * LANES, LANES)
                x2 = acc[r * LANES:(r + 1) * LANES]
                emb = _dot(p_ref[0, rows, :].astype(BF16), wpp_ref[...])
                gate = jax.nn.sigmoid(_dot(_rms(x2, gple_ref[...]).astype(BF16), wpg_ref[...]))
                x3 = x2 + gate * emb
                o_ref[0, rows, :] = _rms(x3, gout_ref[...]) if final else x3

        x1 = x1_ref[0, pl.ds(u * tt, tt), :]
        finish(x1 + expert_rows(0))
        npass = _num_passes(starts, ends, win)
        redo = lambda: finish(lax.fori_loop(1, npass, lambda p, acc: acc + expert_rows(p), x1 + expert_rows(0)))
        return npass, redo

    for npass, redo in [tile_work(u) for u in range(nsub)]:
        pl.when(npass > 1)(redo)


def _combine(starts, yg, post, x13, p3, wpp, wpg, gple, gout, cap, final):
    b, ne, _, d = yg.shape
    s = x13.shape[1]
    tt, win, nsub = COMBINE_TILE, COMBINE_WINDOW, COMBINE_STEP_TILES
    nt = s // tt
    lane = jnp.arange(ne * win, dtype=I32)
    part = jnp.arange(2 * LANES, dtype=I32)[:, None]
    spread = jnp.where(part == lane // win, float(COMBINE_SPLIT), jnp.where(part == lane // win + LANES, 1.0, 0.0))
    spread = spread.astype(BF16)
    lanes = jnp.stack([lane // win, lane % win]).astype(F32)
    tile = lambda a: pl.BlockSpec((1, nsub * tt, a.shape[2]), lambda i, t, st: (i, t, 0))
    full = lambda a: pl.BlockSpec(a.shape, lambda i, t, st: (0,) * a.ndim, pipeline_mode=pl.Buffered(1))
    grid_spec = pltpu.PrefetchScalarGridSpec(
        num_scalar_prefetch=1,
        grid=(b, nt // nsub),
        in_specs=[
            pl.BlockSpec((1, ne, cap, d), lambda i, t, st: (i, 0, 0, 0), pipeline_mode=pl.Buffered(1)),
            pl.BlockSpec((ne, nsub * tt), lambda i, t, st: (0, i * (nt // nsub) + t)),
            full(spread), full(lanes), tile(x13), tile(p3), full(wpp), full(wpg), full(gple), full(gout),
        ],
        out_specs=pl.BlockSpec((1, nsub * tt, d), lambda i, t, st: (i, t, 0)),
    )
    return pl.pallas_call(
        functools.partial(_combine_kernel, nt=nt, tt=tt, win=win, cap=cap, ne=ne, nbatch=b, nsub=nsub, final=final),
        grid_spec=grid_spec,
        out_shape=jax.ShapeDtypeStruct(x13.shape, F32),
        compiler_params=_cparams("arbitrary", "arbitrary"),
        name="combine",
    )(starts, yg, post, spread, lanes, x13, p3, wpp, wpg, gple, gout)


def kernel(x, p, g_mix, w_in, w_fourier, ssm_a_re, ssm_a_im, ssm_log_dt, ssm_b_re, ssm_b_im, ssm_c_re, ssm_c_im, ssm_d, w_glu, w_out, g_ffn, w_router, w_exp_gate, w_exp_up, w_exp_down, g_ple, w_ple_gate, w_ple_proj, g_final):
    b, s, d = x.shape
    depth = p.shape[0]
    df = w_fourier.shape[1]
    ds = w_glu.shape[1]
    ne = w_router.shape[2]
    dexp = w_exp_gate.shape[3]
    groups, gi = ssm_b_re.shape[2], ssm_b_re.shape[4]
    L = SSM_CHUNK
    n1 = s // DFT_N2
    nc = s // L
    cap = EC_CAPACITY * s // ne
    assert s % (DFT_N2 * PACKED_ROWS) == 0 and s % GATHER_TILE == 0 and cap % PACKED_ROWS == 0
    assert nc % SSM_TILE_CHUNKS == 0 and gi == PACKED_ROWS and cap // COMBINE_SPLIT < 256
    assert 2 * ssm_a_re.shape[3] == LANES and ds % LANES == 0
    assert cap >= COMBINE_WINDOW and ne % GATHER_EXPERTS == 0 and GATHER_TILE == COMBINE_TILE
    assert ds == groups * gi and df % FOURIER_GROUPS == 0

    bd = _channel_dft_table(df)
    ck, sk = _dft1_tables(n1)
    tr, ti = _dft3_tables(n1)
    tf = 256 if dexp % 256 == 0 else dexp
    nb_ffn = 2 if b % 2 == 0 else 1

    xcur = x.reshape(b * s, d)
    for i in range(depth):
        final = i == depth - 1
        row = lambda v: v.astype(F32).reshape(1, -1)
        pq, us, gate = _inproj(xcur, row(g_mix[i]), w_in[i].astype(BF16), bd, df, ds)
        a4 = _dft1(pq.reshape(b, n1, DFT_N2, 2 * df), ck, sk)
        fre = _dft3(a4, tr, ti).reshape(b * s, df)

        tabs = _ssm_tables(ssm_a_re[i], ssm_a_im[i], ssm_log_dt[i], ssm_b_re[i], ssm_b_im[i],
                           ssm_c_re[i], ssm_c_im[i], ssm_d[i])
        ys = _to_tokens(_ssm(_to_groups(us, L, groups, gi), *tabs, nc=nc, nb=b), L, groups, gi)

        wr = w_router[i].astype(F32)
        wr_hi = wr.astype(BF16)
        wr_lo = (wr - wr_hi.astype(F32)).astype(BF16)
        lane_pad = lambda w: jnp.pad(w, ((0, 0), (0, LANES - ne)))
        wr2 = jnp.concatenate([jnp.concatenate([lane_pad(wr_hi), lane_pad(wr_lo)], axis=1),
                               jnp.concatenate([lane_pad(wr_hi), jnp.zeros((d, LANES), BF16)], axis=1)], axis=0)
        x1, h2, aff = _mixout(fre, ys, gate, xcur, w_fourier[i].astype(BF16), w_glu[i].astype(BF16),
                              w_out[i].astype(BF16), row(g_ffn[i]), wr2, ne)

        aff_rows = aff.reshape(ne * b, s)
        posm, st_blk = _topk(aff_rows, cap)
        starts = st_blk[:, ::GATHER_TILE // TOPK_BLOCK].reshape(-1)
        tiles = (ne, b * (s // GATHER_TILE), GATHER_TILE)
        xg, vals = _gather(starts, h2.reshape(b, s, d), posm.reshape(tiles), aff_rows.reshape(tiles), cap, ne)
        yg = _ffn(xg, vals, w_exp_gate[i], w_exp_up[i], w_exp_down[i], nb_ffn, tf)
        xnext = _combine(starts, yg, posm.reshape(ne, b * s), x1.reshape(b, s, d), p[i],
                         w_ple_proj[i].astype(BF16), w_ple_gate[i].astype(BF16), row(g_ple[i]),
                         row(g_final) if final else row(g_ple[i]), cap, final)
        xcur = xnext.reshape(b * s, d)
    return xcur.reshape(b, s, d)
```

```python
import functools
import math

import jax
import jax.numpy as jnp
from jax import lax
from jax.experimental import pallas as pl
from jax.experimental.pallas import tpu as pltpu

F32 = jnp.float32
BF16 = jnp.bfloat16
I32 = jnp.int32

RMS_EPS = 1e-6
FOURIER_GROUPS = 4
EC_CAPACITY = 2
DFT_N2 = 128
SUBLANES = 8
LANES = 128
PACKED_ROWS = 16
SSM_CHUNK = 32
ROW_TILE = 1024
INPROJ_ROWS = 256
SSM_TILE_CHUNKS = 128
SCAN_UNROLL = 16
TOPK_BLOCK = 128
GATHER_TILE = 256
MIX_ROWS = 256
FFN_ROWS = 1024
GATHER_WINDOW = 64
GATHER_EXPERTS = 4
GATHER_UNROLL = 16
COMBINE_TILE = 256
COMBINE_WINDOW = 64
COMBINE_STEP_TILES = 2
COMBINE_ROWS = 128
COMBINE_SPLIT = 32
VMEM_LIMIT = 56 * 1024 * 1024


def _cparams(*sem):
    return pltpu.CompilerParams(dimension_semantics=sem, vmem_limit_bytes=VMEM_LIMIT)


def _rms(x, g):
    return x * lax.rsqrt(jnp.mean(x * x, axis=-1, keepdims=True) + RMS_EPS) * g


def _dot(a, b):
    return jnp.dot(a, b, preferred_element_type=F32)


def _inproj_kernel(x_ref, g_ref, w_ref, bd_ref, pq_ref, us_ref, gate_ref, *, df, ds):
    for r in range(x_ref.shape[0] // INPROJ_ROWS):
        rows = pl.ds(r * INPROJ_ROWS, INPROJ_ROWS)
        h = _rms(x_ref[rows, :], g_ref[...]).astype(BF16)
        z = _dot(h, w_ref[...])
        pq_ref[rows, :] = _dot(z[:, :df].astype(BF16), bd_ref[...]).astype(BF16)
        for q in range(ds // LANES):
            us_ref[q, rows, :] = z[:, df + q * LANES:df + (q + 1) * LANES]
        gate_ref[rows, :] = jax.nn.sigmoid(z[:, df + ds:]).astype(BF16)


def _inproj(x2, g, w_in, bd, df, ds, tm=ROW_TILE):
    t, d = x2.shape
    dg = w_in.shape[1] - df - ds
    return pl.pallas_call(
        functools.partial(_inproj_kernel, df=df, ds=ds),
        grid=(t // tm,),
        in_specs=[
            pl.BlockSpec((tm, d), lambda i: (i, 0)),
            pl.BlockSpec((1, d), lambda i: (0, 0)),
            pl.BlockSpec(w_in.shape, lambda i: (0, 0), pipeline_mode=pl.Buffered(1)),
            pl.BlockSpec(bd.shape, lambda i: (0, 0), pipeline_mode=pl.Buffered(1)),
        ],
        out_specs=[
            pl.BlockSpec((tm, 2 * df), lambda i: (i, 0)),
            pl.BlockSpec((ds // LANES, tm, LANES), lambda i: (0, i, 0)),
            pl.BlockSpec((tm, dg), lambda i: (i, 0)),
        ],
        out_shape=[
            jax.ShapeDtypeStruct((t, 2 * df), BF16),
            jax.ShapeDtypeStruct((ds // LANES, t, LANES), F32),
            jax.ShapeDtypeStruct((t, dg), BF16),
        ],
        compiler_params=_cparams("parallel"),
        name="inproj",
    )(x2, g, w_in, bd)


def _channel_dft_table(df):
    c = df // FOURIER_GROUPS
    k = jnp.arange(c, dtype=I32)
    ang = (2.0 * math.pi / c) * ((k[:, None] * k[None, :]) % c).astype(F32)
    eye = jnp.eye(FOURIER_GROUPS, dtype=F32)
    scale = 1.0 / math.sqrt(c)
    re = jnp.kron(eye, jnp.cos(ang)) * scale
    im = -jnp.kron(eye, jnp.sin(ang)) * scale
    return jnp.concatenate([re, im], axis=1).astype(BF16)


def _dft1_tables(n1):
    k = jnp.arange(n1, dtype=I32)
    ang = (2.0 * math.pi / n1) * ((k[:, None] * k[None, :]) % n1).astype(F32)
    row = jnp.arange(n1 * SUBLANES, dtype=I32)
    rep = (row[:, None] // SUBLANES == k[None, :]).astype(F32)
    same_slot = (row[:, None] % SUBLANES == row[None, :] % SUBLANES).astype(F32) * (1.0 / math.sqrt(n1))
    hp = lax.Precision.HIGHEST
    kron8 = lambda t: (jnp.dot(jnp.dot(rep, t, precision=hp), rep.T, precision=hp) * same_slot).astype(BF16)
    return kron8(jnp.cos(ang)), kron8(jnp.sin(ang))


def _dft3_tables(n1):
    s = n1 * DFT_N2
    nb = n1 // SUBLANES
    k1 = jnp.arange(n1, dtype=I32)[:, None, None]
    k2 = jnp.arange(DFT_N2, dtype=I32)[None, :, None]
    n2 = jnp.arange(DFT_N2, dtype=I32)[None, None, :]
    ang = (2.0 * math.pi / s) * ((n2 * (k1 + n1 * k2)) % s).astype(F32)
    eye = jnp.eye(SUBLANES, dtype=BF16)[None, None, :, :, None]
    rows = DFT_N2 * SUBLANES

    def expand(t):
        t = (t * (1.0 / math.sqrt(DFT_N2))).astype(BF16).reshape(nb, SUBLANES, DFT_N2, DFT_N2)
        t = jnp.transpose(t, (0, 2, 1, 3))[:, :, :, None, :]
        return (t * eye).reshape(nb, rows, rows)

    return expand(jnp.cos(ang)), expand(jnp.sin(ang))


def _dft1_kernel(z_ref, ck_ref, sk_ref, a_ref):
    n1, slab, c2 = z_ref.shape[1], z_ref.shape[2], z_ref.shape[3]
    c = c2 // 2
    z = z_ref[0].astype(F32)
    halves = []
    for h in range(slab // SUBLANES):
        zh = z[:, h * SUBLANES:(h + 1) * SUBLANES, :].reshape(n1 * SUBLANES, c2).astype(BF16)
        cz = _dot(ck_ref[...], zh)
        sz = _dot(sk_ref[...], zh)
        a = jnp.concatenate([cz[:, :c] + sz[:, c:], cz[:, c:] - sz[:, :c]], axis=1)
        halves.append(a.reshape(n1, SUBLANES, c2))
    a_ref[0] = jnp.concatenate(halves, axis=1).astype(BF16)


def _dft1(pq4, ck, sk):
    b, n1, n2, c2 = pq4.shape
    slab = PACKED_ROWS
    return pl.pallas_call(
        _dft1_kernel,
        grid=(b, n2 // slab),
        in_specs=[
            pl.BlockSpec((1, n1, slab, c2), lambda i, j: (i, 0, j, 0)),
            pl.BlockSpec(ck.shape, lambda i, j: (0, 0)),
            pl.BlockSpec(sk.shape, lambda i, j: (0, 0)),
        ],
        out_specs=pl.BlockSpec((1, n1, slab, c2), lambda i, j: (i, 0, j, 0)),
        out_shape=jax.ShapeDtypeStruct(pq4.shape, BF16),
        compiler_params=_cparams("parallel", "parallel"),
        name="dft1",
    )(pq4, ck, sk)


def _dft3_kernel(a_ref, tr_ref, ti_ref, o_ref):
    nblk, c2 = tr_ref.shape[0], a_ref.shape[3]
    c = c2 // 2
    outs = []
    for h in range(nblk):
        a = a_ref[0, h * SUBLANES:(h + 1) * SUBLANES].reshape(SUBLANES * DFT_N2, c2)
        out = _dot(tr_ref[h], a[:, :c]) + _dot(ti_ref[h], a[:, c:])
        outs.append(out.reshape(DFT_N2, SUBLANES, c))
    o_ref[0] = jnp.concatenate(outs, axis=1).astype(BF16)


def _dft3(a4, tr, ti):
    b, n1, n2, c2 = a4.shape
    c = c2 // 2
    rows = DFT_N2 * SUBLANES
    nblk = PACKED_ROWS // SUBLANES
    return pl.pallas_call(
        _dft3_kernel,
        grid=(n1 // PACKED_ROWS, b),
        in_specs=[
            pl.BlockSpec((1, PACKED_ROWS, n2, c2), lambda k, i: (i, k, 0, 0)),
            pl.BlockSpec((nblk, rows, rows), lambda k, i: (k, 0, 0)),
            pl.BlockSpec((nblk, rows, rows), lambda k, i: (k, 0, 0)),
        ],
        out_specs=pl.BlockSpec((1, DFT_N2, PACKED_ROWS, c), lambda k, i: (i, 0, k, 0)),
        out_shape=jax.ShapeDtypeStruct((b, DFT_N2, n1, c), BF16),
        compiler_params=_cparams("parallel", "parallel"),
        name="dft3",
    )(a4, tr, ti)


def _ssm_tables(a_re, a_im, log_dt, b_re, b_im, c_re, c_im, d):
    L = SSM_CHUNK
    _, g, n = a_re.shape
    gi = b_re.shape[-1]
    dt = jnp.exp(log_dt.astype(F32))[..., None]
    ar, ai = a_re.astype(F32), a_im.astype(F32)
    tau = jnp.arange(L + 1, dtype=F32)[:, None, None, None]
    mag = jnp.exp(tau * (ar * dt)[None])
    ang = tau * (ai * dt)[None]
    pr, pi = mag * jnp.cos(ang), mag * jnp.sin(ang)
    nr, ni = pr[1] - 1.0, pi[1]
    den = ar * ar + ai * ai
    qr, qi = (nr * ar + ni * ai) / den, (ni * ar - nr * ai) / den
    br, bi = b_re.astype(F32), b_im.astype(F32)
    bbr = qr[..., None] * br - qi[..., None] * bi
    bbi = qr[..., None] * bi + qi[..., None] * br
    cr, ci = c_re.astype(F32), c_im.astype(F32)

    cat = jnp.concatenate
    steps = lambda p: jnp.transpose(p, (1, 0, 2))
    pf_r, pf_i, pb_r, pb_i = steps(pr[1:, 0]), steps(pi[1:, 0]), steps(pr[1:, 1][::-1]), steps(pi[1:, 1][::-1])
    f_small = jnp.stack([cat([cr[0], cr[1], cr[0], cr[1]], -1), cat([-ci[0], -ci[1], -ci[0], -ci[1]], -1)], 1)
    f_steps = jnp.stack([cat([pf_r, pb_r, -pf_i, -pb_i], -1), cat([pf_i, pb_i, pf_r, pb_r], -1)], 1)
    last = lambda p: jnp.transpose(p, (1, 2, 0))
    ef_r, ef_i, eb_r, eb_i = last(pr[:L, 0][::-1]), last(pi[:L, 0][::-1]), last(pr[:L, 1]), last(pi[:L, 1])
    e_small = jnp.stack([cat([bbr[0], bbr[1], bbi[0], bbi[1]], 1), cat([-bbi[0], -bbi[1], bbr[0], bbr[1]], 1)], 1)
    e_steps = jnp.stack([cat([ef_r, eb_r, ef_r, eb_r], 1), cat([ef_i, eb_i, ef_i, eb_i], 1)], 1)
    zpad = lambda p, before: jnp.pad(p, ((0, 0), (0, 0), (before, L - before)))
    wf_r, wf_i, wb_r, wb_i = zpad(ef_r, 0), zpad(ef_i, 0), zpad(eb_r, L - 1), zpad(eb_i, L - 1)
    t_small = jnp.stack([cat([bbr[0], bbi[0], bbr[1], bbi[1]], 1), cat([-bbi[0], bbr[0], -bbi[1], bbr[1]], 1)], 1)
    t_steps = jnp.stack([cat([wf_r, wf_r, wb_r, wb_r], 1), cat([wf_i, wf_i, wb_i, wb_i], 1)], 1)
    t_left = cat([cr[0], -ci[0], cr[1], -ci[1]], -1)

    al = jnp.stack([cat([pr[L, 0], pr[L, 1]], -1), cat([pi[L, 0], pi[L, 1]], -1)], axis=1)
    dv = jnp.tile(d.astype(F32).reshape(g, 1, gi), (1, L, 1)).reshape(g, L * gi, 1)
    t_tab, e_tab, f_tab = _ssm_table_call(f_small, f_steps, e_small, e_steps, t_small, t_steps, t_left, L, gi)
    return t_tab, e_tab, f_tab, al, dv


def _ssm_table_kernel(fs_ref, fp_ref, es_ref, ep_ref, ts_ref, tp_ref, tl_ref, ri_ref, rj_ref, rit_ref, rjt_ref,
                      ri2_ref, rs_ref, t_ref, e_ref, f_ref, *, L, gi):
    def split(a):
        hi = a.astype(BF16)
        return hi, (a - hi.astype(F32)).astype(BF16)

    def spread_lanes(a, rep):
        hi, lo = split(a)
        return _dot(hi, rep) + _dot(lo, rep)

    def spread_rows(rep, a):
        hi, lo = split(a)
        return _dot(rep, hi) + _dot(rep, lo)

    ri, rj, rit, rjt, ri2, rs = ri_ref[...], rj_ref[...], rit_ref[...], rjt_ref[...], ri2_ref[...], rs_ref[...]
    f_ref[0] = (spread_rows(rit, fs_ref[0, 0]) * spread_rows(rjt, fp_ref[0, 0])
                + spread_rows(rit, fs_ref[0, 1]) * spread_rows(rjt, fp_ref[0, 1])).astype(BF16)
    e_ref[0] = (spread_lanes(es_ref[0, 0], ri) * spread_lanes(ep_ref[0, 0], rj)
                + spread_lanes(es_ref[0, 1], ri) * spread_lanes(ep_ref[0, 1], rj)).astype(BF16)
    ew = (spread_lanes(ts_ref[0, 0], ri2) * spread_lanes(tp_ref[0, 0], rs)
          + spread_lanes(ts_ref[0, 1], ri2) * spread_lanes(tp_ref[0, 1], rs))
    c_hi, c_lo = split(tl_ref[0])
    e_hi, e_lo = split(ew)
    kw = _dot(c_hi, e_hi) + _dot(c_hi, e_lo) + _dot(c_lo, e_hi)
    per = LANES // gi
    for r in range(per):
        shifted = kw if r == 0 else pltpu.roll(kw, 2 * L * gi - r * gi, axis=1)
        for a in range(L // per):
            jo = L - 1 - (a * per + r)
            t_ref[0, pl.ds(jo * gi, gi), :] = shifted[:, a * LANES:a * LANES + L * gi].astype(BF16)


def _ssm_table_call(f_small, f_steps, e_small, e_steps, t_small, t_steps, t_left, L, gi):
    g = f_small.shape[0]
    n4 = f_small.shape[3]
    k = L * gi
    lane = jnp.arange(2 * k, dtype=I32)
    ri2 = (lane[None, :] % gi == jnp.arange(gi, dtype=I32)[:, None]).astype(BF16)
    rs = (lane[None, :] // gi == jnp.arange(2 * L, dtype=I32)[:, None]).astype(BF16)
    ri, rj = ri2[:, :k], rs[:L, :k]
    per_group = lambda a: pl.BlockSpec((1,) + a.shape[1:], lambda i: (i,) + (0,) * (a.ndim - 1))
    full = lambda a: pl.BlockSpec(a.shape, lambda i: (0,) * a.ndim)
    ins = (f_small, f_steps, e_small, e_steps, t_small, t_steps, t_left)
    reps = (ri, rj, ri.T, rj.T, ri2, rs)
    return pl.pallas_call(
        functools.partial(_ssm_table_kernel, L=L, gi=gi),
        grid=(g,),
        in_specs=[per_group(a) for a in ins] + [full(a) for a in reps],
        out_specs=[pl.BlockSpec((1, k, k), lambda i: (i, 0, 0)), pl.BlockSpec((1, n4, k), lambda i: (i, 0, 0)),
                   pl.BlockSpec((1, k, n4), lambda i: (i, 0, 0))],
        out_shape=[jax.ShapeDtypeStruct((g, k, k), BF16), jax.ShapeDtypeStruct((g, n4, k), BF16),
                   jax.ShapeDtypeStruct((g, k, n4), BF16)],
        compiler_params=_cparams("parallel"),
        name="ssm_tables",
    )(*ins, *reps)


def _to_groups_kernel(u_ref, a_ref, *, L, groups, gi):
    ncl = a_ref.shape[2]
    gq = LANES // gi
    for q in range(u_ref.shape[0]):
        for j in range(L):
            zt = u_ref[q, pl.ds(j, ncl, stride=L), :].T
            a_ref[q * gq:(q + 1) * gq, pl.ds(j * gi, gi), :] = zt.reshape(gq, gi, ncl).astype(BF16)


def _to_groups(us, L, groups, gi):
    nq, t, _ = us.shape
    ncl = SSM_TILE_CHUNKS
    return pl.pallas_call(
        functools.partial(_to_groups_kernel, L=L, groups=groups, gi=gi),
        grid=(t // (ncl * L),),
        in_specs=[pl.BlockSpec((nq, ncl * L, LANES), lambda i: (0, i, 0))],
        out_specs=pl.BlockSpec((groups, L * gi, ncl), lambda i: (0, 0, i)),
        out_shape=jax.ShapeDtypeStruct((groups, L * gi, t // L), BF16),
        compiler_params=_cparams("parallel"),
        name="to_groups",
    )(us)


def _to_tokens_kernel(y_ref, o_ref, *, L, groups, gi):
    ncl = y_ref.shape[2]
    gq = LANES // gi
    for q in range(o_ref.shape[0]):
        for j in range(L):
            yj = y_ref[q * gq:(q + 1) * gq, pl.ds(j * gi, gi), :].astype(F32).reshape(LANES, ncl)
            o_ref[q, pl.ds(j, ncl, stride=L), :] = yj.T


def _to_tokens(yt, L, groups, gi):
    _, k, nchunks = yt.shape
    ncl = SSM_TILE_CHUNKS
    return pl.pallas_call(
        functools.partial(_to_tokens_kernel, L=L, groups=groups, gi=gi),
        grid=(nchunks // ncl,),
        in_specs=[pl.BlockSpec((groups, k, ncl), lambda i: (0, 0, i))],
        out_specs=pl.BlockSpec((groups * gi // LANES, ncl * L, LANES), lambda i: (0, i, 0)),
        out_shape=jax.ShapeDtypeStruct((groups * gi // LANES, nchunks * L, LANES), F32),
        compiler_params=_cparams("parallel"),
        name="to_tokens",
    )(yt)


def _ssm_kernel(a_ref, t_ref, e_ref, f_ref, al_ref, dv_ref, y_ref, s_scr, h_scr, *, nc, nb, rows, n):
    a = a_ref[0]
    y1 = _dot(t_ref[0], a) + dv_ref[0] * a.astype(F32)
    st = _dot(e_ref[0], a)
    n2 = 2 * n
    s_scr[...] = jnp.zeros_like(s_scr)
    for b in range(nb):
        sb_t = st[:, b * nc:(b + 1) * nc].T
        for q in range(2):
            s_scr[q, pl.ds(b, nc, stride=rows), :] = sb_t[:, q * n2:(q + 1) * n2]
    ar = al_ref[0, 0:1, :]
    ai = al_ref[0, 1:2, :]
    is_fwd = lax.broadcasted_iota(I32, (rows, n2), 1) < n

    def step(i, carry):
        hr, hi = carry
        rf = pl.ds(pl.multiple_of(i * rows, rows), rows)
        rb = pl.ds(pl.multiple_of((nc - 1 - i) * rows, rows), rows)
        h_scr[0, rf, 0:n] = hr[:, 0:n]
        h_scr[1, rf, 0:n] = hi[:, 0:n]
        h_scr[0, rb, n:n2] = hr[:, n:n2]
        h_scr[1, rb, n:n2] = hi[:, n:n2]
        sr = jnp.where(is_fwd, s_scr[0, rf, :], s_scr[0, rb, :])
        si = jnp.where(is_fwd, s_scr[1, rf, :], s_scr[1, rb, :])
        return ar * hr - ai * hi + sr, ar * hi + ai * hr + si

    zero = jnp.zeros((rows, n2), F32)
    lax.fori_loop(0, nc, step, (zero, zero), unroll=SCAN_UNROLL)
    ht = jnp.concatenate(
        [jnp.concatenate([h_scr[q, pl.ds(b, nc, stride=rows), :].T for q in range(2)], axis=0) for b in range(nb)],
        axis=1)
    y = y1 + _dot(f_ref[0], ht.astype(BF16))
    y_ref[0] = jax.nn.gelu(y, approximate=True).astype(BF16)


def _ssm(ag, t_tab, e_tab, f_tab, al, dv, nc, nb):
    g, k, m = ag.shape
    n4 = e_tab.shape[1]
    rows = -(-nb // SUBLANES) * SUBLANES
    return pl.pallas_call(
        functools.partial(_ssm_kernel, nc=nc, nb=nb, rows=rows, n=n4 // 4),
        grid=(g,),
        in_specs=[
            pl.BlockSpec((1, k, m), lambda i: (i, 0, 0)),
            pl.BlockSpec((1, k, k), lambda i: (i, 0, 0)),
            pl.BlockSpec((1, n4, k), lambda i: (i, 0, 0)),
            pl.BlockSpec((1, k, n4), lambda i: (i, 0, 0)),
            pl.BlockSpec((1, 2, n4 // 2), lambda i: (i, 0, 0)),
            pl.BlockSpec((1, k, 1), lambda i: (i, 0, 0)),
        ],
        out_specs=pl.BlockSpec((1, k, m), lambda i: (i, 0, 0)),
        out_shape=jax.ShapeDtypeStruct((g, k, m), BF16),
        scratch_shapes=[pltpu.VMEM((2, nc * rows, n4 // 2), F32), pltpu.VMEM((2, nc * rows, n4 // 2), F32)],
        compiler_params=_cparams("parallel"),
        name="ssm",
    )(ag, t_tab, e_tab, f_tab, al, dv)


def _mixout_kernel(fre_ref, ys_ref, gate_ref, x_ref, wf_ref, wglu_ref, wout_ref, gffn_ref, wr_ref,
                   x1_ref, h2_ref, aff_ref, *, d, ne):
    for r in range(fre_ref.shape[0] // MIX_ROWS):
        rows = pl.ds(r * MIX_ROWS, MIX_ROWS)
        y_f = _dot(fre_ref[rows, :], wf_ref[...])
        ys = jnp.concatenate([ys_ref[q, rows, :] for q in range(ys_ref.shape[0])], axis=1)
        vg = _dot(ys.astype(BF16), wglu_ref[...])
        y_s = vg[:, :d] * jax.nn.sigmoid(vg[:, d:])
        gate = gate_ref[rows, :].astype(F32)
        m = gate[:, :d] * y_f + gate[:, d:] * y_s
        x1 = x_ref[rows, :] + _dot(m.astype(BF16), wout_ref[...])
        x1_ref[rows, :] = x1
        h2 = _rms(x1, gffn_ref[...])
        h2_ref[rows, :] = h2.astype(BF16)
        hi = h2.astype(BF16)
        lo = (h2 - hi.astype(F32)).astype(BF16)
        rl = _dot(jnp.concatenate([hi, lo], axis=1), wr_ref[...])
        logits = rl[:, :LANES] + rl[:, LANES:]
        logits = jnp.where(lax.broadcasted_iota(I32, logits.shape, 1) < ne, logits, -1e30)
        logits = logits - jnp.max(logits, axis=-1, keepdims=True)
        p = jnp.exp(logits)
        aff = p / jnp.sum(p, axis=-1, keepdims=True)
        aff_ref[:, rows] = aff.T[:ne, :]


def _mixout(fre, ys, gate, x2, wf, wglu, wout, gffn, wr2, ne, tm=ROW_TILE):
    t, d = x2.shape
    full = lambda a: pl.BlockSpec(a.shape, lambda i: (0,) * a.ndim, pipeline_mode=pl.Buffered(1))
    row = lambda a: pl.BlockSpec((tm, a.shape[1]), lambda i: (i, 0))
    return pl.pallas_call(
        functools.partial(_mixout_kernel, d=d, ne=ne),
        grid=(t // tm,),
        in_specs=[row(fre), pl.BlockSpec((ys.shape[0], tm, LANES), lambda i: (0, i, 0)), row(gate), row(x2),
                  full(wf), full(wglu), full(wout), full(gffn), full(wr2)],
        out_specs=[
            pl.BlockSpec((tm, d), lambda i: (i, 0)),
            pl.BlockSpec((tm, d), lambda i: (i, 0)),
            pl.BlockSpec((ne, tm), lambda i: (0, i)),
        ],
        out_shape=[
            jax.ShapeDtypeStruct((t, d), F32),
            jax.ShapeDtypeStruct((t, d), BF16),
            jax.ShapeDtypeStruct((ne, t), F32),
        ],
        compiler_params=_cparams("parallel"),
        name="mixout",
    )(fre, ys, gate, x2, wf, wglu, wout, gffn, wr2)


def _topk_kernel(aff_ref, pos_ref, st_ref, *, cap, blk):
    v = aff_ref[...]
    r, s = v.shape
    capf = float(cap)

    def bit_step(i, t):
        cand = t | (jnp.int32(1) << (30 - i))
        cnt = jnp.sum(jnp.where(v >= pltpu.bitcast(cand, F32), 1.0, 0.0), axis=1, keepdims=True)
        return jnp.where(cnt >= capf, cand, t)

    thr = lax.fori_loop(0, 31, bit_step, jnp.zeros((r, 1), I32))
    gt = jnp.where(v >= pltpu.bitcast(thr + 1, F32), 1.0, 0.0)
    eq = jnp.where(v >= pltpu.bitcast(thr, F32), 1.0, 0.0) - gt
    need = capf - jnp.sum(gt, axis=1, keepdims=True)
    ii = lax.broadcasted_iota(I32, (blk, blk), 0)
    jj = lax.broadcasted_iota(I32, (blk, blk), 1)
    tri = jnp.where(ii < jj, 1.0, 0.0).astype(BF16)
    run_eq = jnp.zeros((r, 1), F32)
    run = jnp.zeros((r, 1), F32)
    for k in range(s // blk):
        sl = slice(k * blk, (k + 1) * blk)
        eqb, gtb = eq[:, sl], gt[:, sl]
        rank_eq = _dot(eqb.astype(BF16), tri) + run_eq
        run_eq = run_eq + jnp.sum(eqb, axis=1, keepdims=True)
        mask = gtb + eqb * jnp.where(rank_eq < need, 1.0, 0.0)
        pos = _dot(mask.astype(BF16), tri) + run
        st_ref[:, k:k + 1] = run.astype(I32)
        run = run + jnp.sum(mask, axis=1, keepdims=True)
        pos_ref[:, sl] = jnp.where(mask > 0.0, pos, -1.0).astype(I32)
    st_ref[:, s // blk:s // blk + 1] = run.astype(I32)


def _topk(aff_rows, cap, blk=TOPK_BLOCK):
    r, s = aff_rows.shape
    nblk = s // blk
    return pl.pallas_call(
        functools.partial(_topk_kernel, cap=cap, blk=blk),
        grid=(1,),
        in_specs=[pl.BlockSpec((r, s), lambda i: (0, 0))],
        out_specs=[pl.BlockSpec((r, s), lambda i: (0, 0)), pl.BlockSpec((r, nblk + 1), lambda i: (0, 0))],
        out_shape=[jax.ShapeDtypeStruct((r, s), I32), jax.ShapeDtypeStruct((r, nblk + 1), I32)],
        compiler_params=_cparams("arbitrary"),
        name="topk",
    )(aff_rows)


def _floor_rows(x):
    return (x // PACKED_ROWS) * PACKED_ROWS


def _num_passes(starts, ends, win):
    need = ends[0] - _floor_rows(starts[0])
    for a, b in zip(starts[1:], ends[1:]):
        need = jnp.maximum(need, b - _floor_rows(a))
    return (need + win - 1) // win


def _pass_window(start, p, win, cap):
    first = _floor_rows(start) + p * win
    return first, pl.multiple_of(jnp.minimum(first, cap - win), PACKED_ROWS)


def _gather_kernel(st_ref, h_ref, pos_ref, aff_ref, x_ref, v_ref, *, nt, tt, win, cap, nbatch, ng):
    rows = [((pl.program_id(1) * ng + g) * nbatch + pl.program_id(0)) * (nt + 1) for g in range(ng)]
    x_ref[...] = jnp.zeros_like(x_ref)
    v_ref[...] = jnp.zeros_like(v_ref)
    riota = lax.broadcasted_iota(I32, (win, tt), 0)

    def bounds(t):
        return [st_ref[r + t] for r in rows], [st_ref[r + t + 1] for r in rows]

    def one_pass(t, p, starts):
        hrows = h_ref[0, pl.ds(pl.multiple_of(t * tt, tt), tt), :]
        hots, wins = [], []
        for g in range(ng):
            first, ws = _pass_window(starts[g], p, win, cap)
            pos = pos_ref[g, pl.ds(t, 1), :]
            hots.append((pos - ws == riota) & (pos >= first))
            wins.append(ws)
        onehot = jnp.concatenate([jnp.where(h, 1.0, 0.0).astype(BF16) for h in hots], axis=0)
        rows = _dot(onehot, hrows)
        for g in range(ng):
            x_ref[0, g, pl.ds(wins[g], win), :] += rows[g * win:(g + 1) * win].astype(BF16)
            vals = jnp.sum(jnp.where(hots[g], aff_ref[g, pl.ds(t, 1), :], 0.0), axis=1, keepdims=True)
            v_ref[0, g, pl.ds(wins[g], win), :] += vals

    def first_pass(t, most):
        starts, ends = bounds(t)
        one_pass(t, 0, starts)
        return jnp.maximum(most, _num_passes(starts, ends, win))

    def more_passes(t, carry):
        starts, ends = bounds(t)

        def body(p, c):
            one_pass(t, p, starts)
            return c

        lax.fori_loop(1, _num_passes(starts, ends, win), body, 0)
        return carry

    most = lax.fori_loop(0, nt, first_pass, jnp.int32(0), unroll=GATHER_UNROLL)

    @pl.when(most > 1)
    def _():
        lax.fori_loop(0, nt, more_passes, 0)


def _gather(starts, h3, pos3, aff3t, cap, ne):
    b, s, d = h3.shape
    tt, win, ng = GATHER_TILE, GATHER_WINDOW, GATHER_EXPERTS
    nt = s // tt
    grid_spec = pltpu.PrefetchScalarGridSpec(
        num_scalar_prefetch=1,
        grid=(b, ne // ng),
        in_specs=[
            pl.BlockSpec((1, s, d), lambda i, e, st: (i, 0, 0), pipeline_mode=pl.Buffered(1)),
            pl.BlockSpec((ng, nt, tt), lambda i, e, st: (e, i, 0)),
            pl.BlockSpec((ng, nt, tt), lambda i, e, st: (e, i, 0)),
        ],
        out_specs=[
            pl.BlockSpec((1, ng, cap, d), lambda i, e, st: (i, e, 0, 0)),
            pl.BlockSpec((1, ng, cap, 1), lambda i, e, st: (i, e, 0, 0)),
        ],
    )
    return pl.pallas_call(
        functools.partial(_gather_kernel, nt=nt, tt=tt, win=win, cap=cap, nbatch=b, ng=ng),
        grid_spec=grid_spec,
        out_shape=[jax.ShapeDtypeStruct((b, ne, cap, d), BF16), jax.ShapeDtypeStruct((b, ne, cap, 1), F32)],
        compiler_params=_cparams("arbitrary", "arbitrary"),
        name="gather",
    )(starts, h3, pos3, aff3t)


def _ffn_kernel(x_ref, v_ref, wg_ref, wu_ref, wd_ref, y_ref, acc_ref, *, nf):
    f = pl.program_id(2)
    nb, _, cap, d = x_ref.shape

    @pl.when(f == 0)
    def _():
        acc_ref[...] = jnp.zeros_like(acc_ref)

    wg = wg_ref[0].astype(BF16)
    wu = wu_ref[0].astype(BF16)
    wd = wd_ref[0].astype(BF16)
    mr = min(FFN_ROWS, cap)
    for r in range(nb * cap // mr):
        x = x_ref[r * mr // cap, 0, pl.ds(r * mr % cap, mr), :]
        g = _dot(x, wg)
        u = _dot(x, wu)
        hid = (g * jax.nn.sigmoid(g) * u).astype(BF16)
        acc_ref[pl.ds(r * mr, mr), :] += _dot(hid, wd)

    @pl.when(f == nf - 1)
    def _():
        y = acc_ref[...] * v_ref[...].reshape(nb * cap, 1)
        y_ref[...] = y.astype(BF16).reshape(nb, 1, cap, d)


def _ffn(xg, vals, wg, wu, wd, nb, tf):
    b, ne, cap, d = xg.shape
    dexp = wg.shape[2]
    nf = dexp // tf
    return pl.pallas_call(
        functools.partial(_ffn_kernel, nf=nf),
        grid=(ne, b // nb, nf),
        in_specs=[
            pl.BlockSpec((nb, 1, cap, d), lambda e, i, f: (i, e, 0, 0)),
            pl.BlockSpec((nb, 1, cap, 1), lambda e, i, f: (i, e, 0, 0)),
            pl.BlockSpec((1, d, tf), lambda e, i, f: (e, 0, f)),
            pl.BlockSpec((1, d, tf), lambda e, i, f: (e, 0, f)),
            pl.BlockSpec((1, tf, d), lambda e, i, f: (e, f, 0)),
        ],
        out_specs=pl.BlockSpec((nb, 1, cap, d), lambda e, i, f: (i, e, 0, 0)),
        out_shape=jax.ShapeDtypeStruct(xg.shape, BF16),
        scratch_shapes=[pltpu.VMEM((nb * cap, d), F32)],
        compiler_params=_cparams("parallel", "parallel", "arbitrary"),
        name="ffn",
    )(xg, vals, wg, wu, wd)


def _combine_kernel(st_ref, y_ref, pos_ref, spread_ref, lanes_ref, x1_ref, p_ref, wpp_ref, wpg_ref, gple_ref,
                    gout_ref, o_ref, *, nt, tt, win, cap, ne, nbatch, nsub, final):
    k = ne * win
    lane_e = lanes_ref[0:1, :]
    lane_r = lanes_ref[1:2, :]

    def tile_work(u):
        t = pl.program_id(1) * nsub + u
        rows = [(e * nbatch + pl.program_id(0)) * (nt + 1) + t for e in range(ne)]
        starts = [st_ref[r] for r in rows]
        ends = [st_ref[r + 1] for r in rows]
        pos = pos_ref[:, u * tt:(u + 1) * tt].astype(F32)
        pos = jnp.concatenate([pos, jnp.zeros((LANES - ne, tt), F32)], axis=0).T
        hi = jnp.floor(pos * (1.0 / COMBINE_SPLIT))
        lo = pos - hi * COMBINE_SPLIT
        rank = _dot(jnp.concatenate([hi, lo], axis=1).astype(BF16), spread_ref[...])

        def expert_rows(p):
            firsts = jnp.zeros((1, k), F32)
            offs = jnp.zeros((1, k), F32)
            wins = []
            for e in range(ne):
                first, ws = _pass_window(starts[e], p, win, cap)
                firsts = jnp.where(lane_e == e, first.astype(F32), firsts)
                offs = jnp.where(lane_e == e, ws.astype(F32), offs)
                wins.append(y_ref[0, e, pl.ds(ws, win), :])
            onehot = jnp.where((rank - offs == lane_r) & (rank >= firsts), 1.0, 0.0).astype(BF16)
            return _dot(onehot, jnp.concatenate(wins, axis=0))

        def finish(acc):
            for r in range(tt // COMBINE_ROWS):
                rows = pl.ds(u * tt + r * COMBINE_ROWS, COMBINE_ROWS)
                x2 = acc[r * COMBINE_ROWS:(r + 1) * COMBINE_ROWS]
                emb = _dot(p_ref[0, rows, :].astype(BF16), wpp_ref[...])
                gate = jax.nn.sigmoid(_dot(_rms(x2, gple_ref[...]).astype(BF16), wpg_ref[...]))
                x3 = x2 + gate * emb
                o_ref[0, rows, :] = _rms(x3, gout_ref[...]) if final else x3

        x1 = x1_ref[0, pl.ds(u * tt, tt), :]
        finish(x1 + expert_rows(0))
        npass = _num_passes(starts, ends, win)
        redo = lambda: finish(lax.fori_loop(1, npass, lambda p, acc: acc + expert_rows(p), x1 + expert_rows(0)))
        return npass, redo

    for npass, redo in [tile_work(u) for u in range(nsub)]:
        pl.when(npass > 1)(redo)


def _combine(starts, yg, post, x13, p3, wpp, wpg, gple, gout, cap, final):
    b, ne, _, d = yg.shape
    s = x13.shape[1]
    tt, win, nsub = COMBINE_TILE, COMBINE_WINDOW, COMBINE_STEP_TILES
    nt = s // tt
    lane = jnp.arange(ne * win, dtype=I32)
    part = jnp.arange(2 * LANES, dtype=I32)[:, None]
    spread = jnp.where(part == lane // win, float(COMBINE_SPLIT), jnp.where(part == lane // win + LANES, 1.0, 0.0))
    spread = spread.astype(BF16)
    lanes = jnp.stack([lane // win, lane % win]).astype(F32)
    tile = lambda a: pl.BlockSpec((1, nsub * tt, a.shape[2]), lambda i, t, st: (i, t, 0))
    full = lambda a: pl.BlockSpec(a.shape, lambda i, t, st: (0,) * a.ndim, pipeline_mode=pl.Buffered(1))
    grid_spec = pltpu.PrefetchScalarGridSpec(
        num_scalar_prefetch=1,
        grid=(b, nt // nsub),
        in_specs=[
            pl.BlockSpec((1, ne, cap, d), lambda i, t, st: (i, 0, 0, 0), pipeline_mode=pl.Buffered(1)),
            pl.BlockSpec((ne, nsub * tt), lambda i, t, st: (0, i * (nt // nsub) + t)),
            full(spread), full(lanes), tile(x13), tile(p3), full(wpp), full(wpg), full(gple), full(gout),
        ],
        out_specs=pl.BlockSpec((1, nsub * tt, d), lambda i, t, st: (i, t, 0)),
    )
    return pl.pallas_call(
        functools.partial(_combine_kernel, nt=nt, tt=tt, win=win, cap=cap, ne=ne, nbatch=b, nsub=nsub, final=final),
        grid_spec=grid_spec,
        out_shape=jax.ShapeDtypeStruct(x13.shape, F32),
        compiler_params=_cparams("arbitrary", "arbitrary"),
        name="combine",
    )(starts, yg, post, spread, lanes, x13, p3, wpp, wpg, gple, gout)


def kernel(x, p, g_mix, w_in, w_fourier, ssm_a_re, ssm_a_im, ssm_log_dt, ssm_b_re, ssm_b_im, ssm_c_re, ssm_c_im, ssm_d, w_glu, w_out, g_ffn, w_router, w_exp_gate, w_exp_up, w_exp_down, g_ple, w_ple_gate, w_ple_proj, g_final):
    b, s, d = x.shape
    depth = p.shape[0]
    df = w_fourier.shape[1]
    ds = w_glu.shape[1]
    ne = w_router.shape[2]
    dexp = w_exp_gate.shape[3]
    groups, gi = ssm_b_re.shape[2], ssm_b_re.shape[4]
    L = SSM_CHUNK
    n1 = s // DFT_N2
    nc = s // L
    cap = EC_CAPACITY * s // ne
    assert s % (DFT_N2 * PACKED_ROWS) == 0 and s % GATHER_TILE == 0 and cap % PACKED_ROWS == 0
    assert nc % SSM_TILE_CHUNKS == 0 and gi == PACKED_ROWS and cap // COMBINE_SPLIT < 256
    assert 2 * ssm_a_re.shape[3] == LANES and ds % LANES == 0
    assert cap >= COMBINE_WINDOW and ne % GATHER_EXPERTS == 0 and GATHER_TILE == COMBINE_TILE
    assert ds == groups * gi and df % FOURIER_GROUPS == 0

    bd = _channel_dft_table(df)
    ck, sk = _dft1_tables(n1)
    tr, ti = _dft3_tables(n1)
    tf = 256 if dexp % 256 == 0 else dexp
    nb_ffn = 2 if b % 2 == 0 else 1

    xcur = x.reshape(b * s, d)
    for i in range(depth):
        final = i == depth - 1
        row = lambda v: v.astype(F32).reshape(1, -1)
        pq, us, gate = _inproj(xcur, row(g_mix[i]), w_in[i].astype(BF16), bd, df, ds)
        a4 = _dft1(pq.reshape(b, n1, DFT_N2, 2 * df), ck, sk)
        fre = _dft3(a4, tr, ti).reshape(b * s, df)

        tabs = _ssm_tables(ssm_a_re[i], ssm_a_im[i], ssm_log_dt[i], ssm_b_re[i], ssm_b_im[i],
                           ssm_c_re[i], ssm_c_im[i], ssm_d[i])
        ys = _to_tokens(_ssm(_to_groups(us, L, groups, gi), *tabs, nc=nc, nb=b), L, groups, gi)

        wr = w_router[i].astype(F32)
        wr_hi = wr.astype(BF16)
        wr_lo = (wr - wr_hi.astype(F32)).astype(BF16)
        lane_pad = lambda w: jnp.pad(w, ((0, 0), (0, LANES - ne)))
        wr2 = jnp.concatenate([jnp.concatenate([lane_pad(wr_hi), lane_pad(wr_lo)], axis=1),
                               jnp.concatenate([lane_pad(wr_hi), jnp.zeros((d, LANES), BF16)], axis=1)], axis=0)
        x1, h2, aff = _mixout(fre, ys, gate, xcur, w_fourier[i].astype(BF16), w_glu[i].astype(BF16),
                              w_out[i].astype(BF16), row(g_ffn[i]), wr2, ne)

        aff_rows = aff.reshape(ne * b, s)
        posm, st_blk = _topk(aff_rows, cap)
        starts = st_blk[:, ::GATHER_TILE // TOPK_BLOCK].reshape(-1)
        tiles = (ne, b * (s // GATHER_TILE), GATHER_TILE)
        xg, vals = _gather(starts, h2.reshape(b, s, d), posm.reshape(tiles), aff_rows.reshape(tiles), cap, ne)
        yg = _ffn(xg, vals, w_exp_gate[i], w_exp_up[i], w_exp_down[i], nb_ffn, tf)
        xnext = _combine(starts, yg, posm.reshape(ne, b * s), x1.reshape(b, s, d), p[i],
                         w_ple_proj[i].astype(BF16), w_ple_gate[i].astype(BF16), row(g_ple[i]),
                         row(g_final) if final else row(g_ple[i]), cap, final)
        xcur = xnext.reshape(b * s, d)
    return xcur.reshape(b, s, d)
```
